```python
import math
import jax, jax.numpy as jnp
from jax import lax
import numpy as np

D_MODEL = 2048
BATCH = 4
SEQ = 4096
DEPTH = 2
DEC_BATCH = 32
DEC_SEQ = 32
PAST_LEN = 2048

CHUNK = 64
Q_BLOCK = 128
ROPE_THETA = 10000.0
EPS = 1e-6
A_HEADS = 4
A_HD = 128
A_VD = 2 * A_HD
B_HEADS = 8
B_KV = 2
B_HD = 128
I_HEADS = 8
I_HD = 64
TOPK_MAX = 256
C_HEADS = 8
C_HD = 128
C_CONV = 4
C_QKV = 3 * C_HEADS * C_HD
BRANCH_W = 1024
N_BRANCH = 3
D_FF = 5632
FFN_CONV = 3
IN_SPLITS = (A_HEADS * 2 * A_HD, A_HEADS * 2 * A_HD, A_HEADS * A_VD,
             B_HEADS * B_HD, B_KV * B_HD, B_KV * B_HD, I_HEADS * I_HD, I_HD, I_HEADS,
             C_QKV, C_HEADS, C_HEADS, C_HEADS * C_HD,
             N_BRANCH * D_MODEL)
D_IN = sum(IN_SPLITS)

kernel_name = 'hybrid_streaming_encoder_step'


def _rmsnorm(x, g):
    xf = x.astype(jnp.float32)
    y = xf * lax.rsqrt(jnp.mean(xf * xf, axis=-1, keepdims=True) + EPS)
    return (y * g.astype(jnp.float32)).astype(x.dtype)


def _l2norm(x):
    xf = x.astype(jnp.float32)
    return xf * lax.rsqrt(jnp.sum(xf * xf, axis=-1, keepdims=True) + EPS)


def _modulate(h, shift, scale):
    return h * (1.0 + scale[:, None, :]) + shift[:, None, :]


def _rope(x, pos):
    d = x.shape[-1]
    half = d // 2
    inv = jnp.power(ROPE_THETA, -jnp.arange(half, dtype=jnp.float32) / half)
    ang = pos.astype(jnp.float32)[:, None] * inv[None, :]
    shp = (pos.shape[0],) + (1,) * (x.ndim - 3) + (half,)
    cos = jnp.cos(ang).reshape(shp)
    sin = jnp.sin(ang).reshape(shp)
    xf = x.astype(jnp.float32)
    x1, x2 = xf[..., :half], xf[..., half:]
    return jnp.concatenate([x1 * cos - x2 * sin, x2 * cos + x1 * sin], axis=-1).astype(x.dtype)


def _chunk_visible(q_pos, k_pos):
    return (k_pos[None, :] // CHUNK) <= (q_pos[:, None] // CHUNK)


def _causal_dwconv(x, buf, w):
    W = w.shape[0]
    T = x.shape[1]
    xp = jnp.concatenate([buf.astype(x.dtype), x], axis=1)
    y = xp[:, 0:T] * w[0]
    for j in range(1, W):
        y = y + xp[:, j:j + T] * w[j]
    return y, xp[:, T:]


def _sweep(fn, q_arrays, q_pos):
    T = q_pos.shape[0]
    qb = min(Q_BLOCK, T)
    nb = T // qb
    blocks = tuple(jnp.moveaxis(a.reshape((a.shape[0], nb, qb) + a.shape[2:]), 1, 0) for a in q_arrays)
    out = lax.map(lambda args: fn(*args), blocks + (q_pos.reshape(nb, qb),))
    out = jnp.moveaxis(out, 0, 1)
    return out.reshape((out.shape[0], T) + out.shape[3:])


def _gdn_chunk(S, inp):
    q, k, v, g, beta = inp
    C = q.shape[2]
    G = jnp.cumsum(g, axis=-1)
    i = jnp.arange(C)
    incl = i[:, None] >= i[None, :]
    strict = i[:, None] > i[None, :]
    gam = jnp.exp(jnp.where(incl, G[..., :, None] - G[..., None, :], -jnp.inf))
    a = jnp.where(strict, beta[..., :, None] * jnp.einsum('bhik,bhjk->bhij', k, k) * gam, 0.0)
    m = a + jnp.eye(C, dtype=a.dtype)
    rhs = jnp.concatenate([beta[..., None] * v, (beta * jnp.exp(G))[..., None] * k], axis=-1)
    sol = lax.linalg.triangular_solve(m, rhs, left_side=True, lower=True, unit_diagonal=True)
    dv = v.shape[-1]
    u, w = sol[..., :dv], sol[..., dv:]
    v_new = u - jnp.einsum('bhck,bhkv->bhcv', w, S)
    qk = jnp.einsum('bhik,bhjk->bhij', q, k) * gam
    o = (jnp.einsum('bhck,bhkv->bhcv', q * jnp.exp(G)[..., None], S)
         + jnp.einsum('bhij,bhjv->bhiv', qk, v_new))
    gl = G[..., -1]
    S_new = (jnp.exp(gl)[..., None, None] * S
             + jnp.einsum('bhck,bhcv->bhkv', k * jnp.exp(gl[..., None] - G)[..., None], v_new))
    return S_new, o


def _layer(x, c, past, lp, layer_idx):
    kA_p, vA_p, kB_p, vB_p, kI_p, cbuf, S0, fbuf = past
    B, T, D = x.shape
    P = kA_p.shape[1]
    L = P + T
    q_pos = P + jnp.arange(T, dtype=jnp.int32)
    k_pos = jnp.arange(L, dtype=jnp.int32)
    f32 = jnp.float32

    mod = jnp.einsum('bd,de->be', jax.nn.silu(c), lp['w_ada']) + lp['b_ada']
    sh1, sc1, g1, sh2, sc2, g2 = jnp.split(mod, 6, axis=-1)

    h = _modulate(_rmsnorm(x, lp['norm_mix']), sh1, sc1)
    proj = jnp.einsum('btd,de->bte', h, lp['w_in'])
    split_pts = np.cumsum(IN_SPLITS)[:-1].tolist()
    (aq, ak, av, bq, bk, bv, bqi, bki, bwi, cqkv, ca, cb, cz, gt) = jnp.split(proj, split_pts, axis=-1)

    qa = _rope(_rmsnorm(aq.reshape(B, T, A_HEADS, 2, A_HD), lp['a_q_norm']), q_pos)
    ka = _rope(_rmsnorm(ak.reshape(B, T, A_HEADS, 2, A_HD), lp['a_k_norm']), q_pos)
    va = av.reshape(B, T, A_HEADS, A_VD)
    ka_all = jnp.concatenate([kA_p, ka], axis=1)
    va_all = jnp.concatenate([vA_p, va], axis=1)
    lam_init = 0.8 - 0.6 * math.exp(-0.3 * layer_idx)
    lam_p = lp['a_lambda'].astype(f32)
    lam = jnp.exp(jnp.sum(lam_p[0] * lam_p[1])) - jnp.exp(jnp.sum(lam_p[2] * lam_p[3])) + lam_init

    def diff_block(q_blk, qp_blk):
        s = jnp.einsum('bqhmd,bkhmd->bhmqk', q_blk, ka_all, preferred_element_type=f32) * (A_HD ** -0.5)
        vis = _chunk_visible(qp_blk, k_pos)
        p = jax.nn.softmax(jnp.where(vis, s, -jnp.inf), axis=-1)
        attn = p[:, :, 0] - lam * p[:, :, 1]
        return jnp.einsum('bhqk,bkhe->bqhe', attn, va_all).astype(x.dtype)

    oa = _sweep(diff_block, (qa,), q_pos)
    oa = _rmsnorm(oa, lp['a_subln']) * (1.0 - lam_init)

    qb = _rope(_rmsnorm(bq.reshape(B, T, B_HEADS, B_HD), lp['b_q_norm']), q_pos)
    kb = _rope(_rmsnorm(bk.reshape(B, T, B_KV, B_HD), lp['b_k_norm']), q_pos)
    vb = bv.reshape(B, T, B_KV, B_HD)
    qi = _rope(bqi.reshape(B, T, I_HEADS, I_HD), q_pos)
    ki = _rope(bki, q_pos)
    wi = bwi * (I_HEADS ** -0.5)
    kb_all = jnp.concatenate([kB_p, kb], axis=1)
    vb_all = jnp.concatenate([vB_p, vb], axis=1)
    ki_all = jnp.concatenate([kI_p, ki], axis=1)
    topk = min(TOPK_MAX, L // 4)
    grp = B_HEADS // B_KV

    def dsa_block(q_blk, qi_blk, w_blk, qp_blk):
        nb_, nq = q_blk.shape[0], qp_blk.shape[0]
        isc = jnp.einsum('bqhd,bkd->bqhk', qi_blk, ki_all, preferred_element_type=f32) * (I_HD ** -0.5)
        score = jnp.einsum('bqhk,bqh->bqk', jax.nn.relu(isc), w_blk.astype(f32))
        vis = _chunk_visible(qp_blk, k_pos)
        score = jnp.where(vis[None], score, -jnp.inf)
        _, sel = lax.top_k(score, topk)
        valid = (sel // CHUNK) <= (qp_blk // CHUNK)[None, :, None]
        k_sel = jax.vmap(lambda kk, ii: kk[ii])(kb_all, sel)
        v_sel = jax.vmap(lambda vv, ii: vv[ii])(vb_all, sel)
        qg = q_blk.reshape(nb_, nq, B_KV, grp, B_HD)
        s = jnp.einsum('bqngd,bqknd->bqngk', qg, k_sel, preferred_element_type=f32) * (B_HD ** -0.5)
        p = jax.nn.softmax(jnp.where(valid[:, :, None, None, :], s, -jnp.inf), axis=-1)
        o = jnp.einsum('bqngk,bqknd->bqngd', p, v_sel)
        return o.reshape(nb_, nq, B_HEADS, B_HD).astype(x.dtype)

    ob = _sweep(dsa_block, (qb, qi, wi), q_pos)

    cqkv_c, cbuf_new = _causal_dwconv(cqkv, cbuf, lp['c_conv'])
    cq, ck, cv = jnp.split(jax.nn.silu(cqkv_c), 3, axis=-1)
    cq = _l2norm(cq.reshape(B, T, C_HEADS, C_HD)) * (C_HD ** -0.5)
    ck = _l2norm(ck.reshape(B, T, C_HEADS, C_HD))
    cv = cv.reshape(B, T, C_HEADS, C_HD)
    g = -jnp.exp(lp['c_a_log'].astype(f32)) * jax.nn.softplus(ca.astype(f32) + lp['c_dt_bias'].astype(f32))
    beta = jax.nn.sigmoid(cb.astype(f32))
    cc = min(CHUNK, T)
    n = T // cc

    def to_blocks(a):
        a = a.astype(f32).reshape((B, n, cc) + a.shape[2:])
        return jnp.moveaxis(jnp.moveaxis(a, 1, 0), 2, 3)

    S_fin, oc = lax.scan(_gdn_chunk, S0.astype(f32),
                         (to_blocks(cq), to_blocks(ck), to_blocks(cv), to_blocks(g), to_blocks(beta)))
    oc = jnp.moveaxis(jnp.moveaxis(oc, 3, 2), 0, 1).reshape(B, T, C_HEADS, C_HD).astype(x.dtype)
    oc = _rmsnorm(oc, lp['c_out_norm']) * jax.nn.silu(cz.reshape(B, T, C_HEADS, C_HD))

    branches = jnp.stack([oa.reshape(B, T, BRANCH_W), ob.reshape(B, T, BRANCH_W),
                          oc.reshape(B, T, BRANCH_W)], axis=2)
    bproj = jnp.einsum('btnw,nwd->btnd', branches, lp['w_branch'])
    gates = jax.nn.sigmoid(gt.reshape(B, T, N_BRANCH, D))
    mixed = jnp.sum(gates * bproj, axis=2)
    y = jnp.einsum('btd,de->bte', mixed, lp['w_out'])
    x = x + g1[:, None, :] * y

    h2 = _modulate(_rmsnorm(x, lp['norm_ffn']), sh2, sc2)
    u = jnp.einsum('btd,df->btf', h2, lp['w_up'])
    u, fbuf_new = _causal_dwconv(u, fbuf, lp['ffn_conv'])
    ug, uu = jnp.split(u, 2, axis=-1)
    fo = jnp.einsum('btf,fd->btd', jax.nn.silu(ug) * uu, lp['w_down'])
    x = x + g2[:, None, :] * fo

    new_state = (ka, va, kb, vb, ki, cbuf_new, S_fin.astype(S0.dtype), fbuf_new)
    return x, new_state


def setup_inputs(seed: int = 0) -> dict:
    key = jax.random.key(seed)
    keys = jax.random.split(key, 40)
    counter = [0]

    def nxt():
        counter[0] += 1
        return keys[counter[0] - 1]

    def nrm(shape, scale=1.0):
        return jax.random.normal(nxt(), shape, jnp.float32) * scale

    def gain(shape):
        return 1.0 + 0.02 * nrm(shape)

    d = D_MODEL
    x_prompt = nrm((BATCH, SEQ, d))
    x_sample = nrm((DEC_BATCH, DEC_SEQ, d))
    c_prompt = nrm((BATCH, d))
    c_sample = nrm((DEC_BATCH, d))
    cache_diff_k = nrm((DEPTH, DEC_BATCH, PAST_LEN, A_HEADS, 2, A_HD))
    cache_diff_v = nrm((DEPTH, DEC_BATCH, PAST_LEN, A_HEADS, A_VD))
    cache_dsa_k = nrm((DEPTH, DEC_BATCH, PAST_LEN, B_KV, B_HD))
    cache_dsa_v = nrm((DEPTH, DEC_BATCH, PAST_LEN, B_KV, B_HD))
    cache_dsa_kidx = nrm((DEPTH, DEC_BATCH, PAST_LEN, I_HD))
    state_gdn_conv = nrm((DEPTH, DEC_BATCH, C_CONV - 1, C_QKV))
    state_gdn = nrm((DEPTH, DEC_BATCH, C_HEADS, C_HD, C_HD), 0.1)
    state_ffn_conv = nrm((DEPTH, DEC_BATCH, FFN_CONV - 1, 2 * D_FF))
    w_ada = nrm((DEPTH, d, 6 * d), 0.5 * d ** -0.5)
    b_ada = nrm((DEPTH, 6 * d), 0.01)
    norm_mix = gain((DEPTH, d))
    w_in = nrm((DEPTH, d, D_IN), d ** -0.5)
    a_q_norm = gain((DEPTH, A_HD))
    a_k_norm = gain((DEPTH, A_HD))
    a_lambda = nrm((DEPTH, 4, A_HD), 0.1)
    a_subln = gain((DEPTH, A_VD))
    b_q_norm = gain((DEPTH, B_HD))
    b_k_norm = gain((DEPTH, B_HD))
    c_conv = nrm((DEPTH, C_CONV, C_QKV), C_CONV ** -0.5)
    c_a_log = jnp.log(jax.random.uniform(nxt(), (DEPTH, C_HEADS), jnp.float32, 1.0, 16.0))
    dt = jnp.exp(jax.random.uniform(nxt(), (DEPTH, C_HEADS), jnp.float32, math.log(1e-3), math.log(1e-1)))
    c_dt_bias = dt + jnp.log(-jnp.expm1(-dt))
    c_out_norm = gain((DEPTH, C_HD))
    w_branch = nrm((DEPTH, N_BRANCH, BRANCH_W, d), BRANCH_W ** -0.5)
    w_out = nrm((DEPTH, d, d), d ** -0.5)
    norm_ffn = gain((DEPTH, d))
    w_up = nrm((DEPTH, d, 2 * D_FF), d ** -0.5)
    ffn_conv = nrm((DEPTH, FFN_CONV, 2 * D_FF), FFN_CONV ** -0.5)
    w_down = nrm((DEPTH, D_FF, d), D_FF ** -0.5)
    return {'x_prompt': x_prompt, 'x_sample': x_sample, 'c_prompt': c_prompt, 'c_sample': c_sample,
            'cache_diff_k': cache_diff_k, 'cache_diff_v': cache_diff_v, 'cache_dsa_k': cache_dsa_k,
            'cache_dsa_v': cache_dsa_v, 'cache_dsa_kidx': cache_dsa_kidx, 'state_gdn_conv': state_gdn_conv,
            'state_gdn': state_gdn, 'state_ffn_conv': state_ffn_conv,
            'w_ada': w_ada, 'b_ada': b_ada, 'norm_mix': norm_mix, 'w_in': w_in,
            'a_q_norm': a_q_norm, 'a_k_norm': a_k_norm, 'a_lambda': a_lambda, 'a_subln': a_subln,
            'b_q_norm': b_q_norm, 'b_k_norm': b_k_norm, 'c_conv': c_conv, 'c_a_log': c_a_log,
            'c_dt_bias': c_dt_bias, 'c_out_norm': c_out_norm, 'w_branch': w_branch, 'w_out': w_out,
            'norm_ffn': norm_ffn, 'w_up': w_up, 'ffn_conv': ffn_conv, 'w_down': w_down}


def reference(x_prompt, x_sample, c_prompt, c_sample, cache_diff_k, cache_diff_v, cache_dsa_k,
              cache_dsa_v, cache_dsa_kidx, state_gdn_conv, state_gdn, state_ffn_conv,
              w_ada, b_ada, norm_mix, w_in, a_q_norm, a_k_norm, a_lambda, a_subln,
              b_q_norm, b_k_norm, c_conv, c_a_log, c_dt_bias, c_out_norm, w_branch, w_out,
              norm_ffn, w_up, ffn_conv, w_down):
    dt_ = x_prompt.dtype
    bp = x_prompt.shape[0]
    prompt_past = (jnp.zeros((bp, 0, A_HEADS, 2, A_HD), dt_), jnp.zeros((bp, 0, A_HEADS, A_VD), dt_),
                   jnp.zeros((bp, 0, B_KV, B_HD), dt_), jnp.zeros((bp, 0, B_KV, B_HD), dt_),
                   jnp.zeros((bp, 0, I_HD), dt_), jnp.zeros((bp, C_CONV - 1, C_QKV), dt_),
                   jnp.zeros((bp, C_HEADS, C_HD, C_HD), dt_), jnp.zeros((bp, FFN_CONV - 1, 2 * D_FF), dt_))
    xp, xs = x_prompt, x_sample
    prompt_states, sample_states = [], []
    for l in range(DEPTH):
        lp = dict(w_ada=w_ada[l], b_ada=b_ada[l], norm_mix=norm_mix[l], w_in=w_in[l],
                  a_q_norm=a_q_norm[l], a_k_norm=a_k_norm[l], a_lambda=a_lambda[l], a_subln=a_subln[l],
                  b_q_norm=b_q_norm[l], b_k_norm=b_k_norm[l], c_conv=c_conv[l], c_a_log=c_a_log[l],
                  c_dt_bias=c_dt_bias[l], c_out_norm=c_out_norm[l], w_branch=w_branch[l], w_out=w_out[l],
                  norm_ffn=norm_ffn[l], w_up=w_up[l], ffn_conv=ffn_conv[l], w_down=w_down[l])
        xp, st_p = _layer(xp, c_prompt, prompt_past, lp, l)
        sample_past = (cache_diff_k[l], cache_diff_v[l], cache_dsa_k[l], cache_dsa_v[l], cache_dsa_kidx[l],
                       state_gdn_conv[l], state_gdn[l], state_ffn_conv[l])
        xs, st_s = _layer(xs, c_sample, sample_past, lp, l)
        prompt_states.append(st_p)
        sample_states.append(st_s)
    (p_diff_k, p_diff_v, p_dsa_k, p_dsa_v, p_dsa_kidx, p_gdn_conv, p_gdn_state,
     p_ffn_conv) = [jnp.stack(s, axis=0) for s in zip(*prompt_states)]
    (s_diff_k, s_diff_v, s_dsa_k, s_dsa_v, s_dsa_kidx, s_gdn_conv, s_gdn_state,
     s_ffn_conv) = [jnp.stack(s, axis=0) for s in zip(*sample_states)]
    return (xp, xs, p_diff_k, p_diff_v, p_dsa_k, p_dsa_v, p_dsa_kidx, p_gdn_conv, p_gdn_state, p_ffn_conv,
            s_diff_k, s_diff_v, s_dsa_k, s_dsa_v, s_dsa_kidx, s_gdn_conv, s_gdn_state, s_ffn_conv)
```

```python
import functools
import math

import jax
import jax.numpy as jnp
import numpy as np
from jax import lax
from jax.experimental import pallas as pl
from jax.experimental.pallas import tpu as pltpu

F32 = jnp.float32
BF16 = jnp.bfloat16

CHUNK = 64
CHUNK_SHIFT = 6
ROPE_THETA = 10000.0
EPS = 1e-6
A_HEADS, A_HD, A_VD = 4, 128, 256
B_HEADS, B_KV, B_HD = 8, 2, 128
I_HEADS, I_HD = 8, 64
I_HD_SHIFT = 6
TOPK_MAX = 256
C_HEADS, C_HD, C_CONV = 8, 128, 4
C_QKV = 3 * C_HEADS * C_HD
BRANCH_W = 1024
N_BRANCH = 3
FFN_CONV = 3

LANES = 128
SUBLANES = 8
VMEM_LIMIT = 56 * 1024 * 1024
ROW_TILE = 1024
COL_TILE = 512
SOLVE_BLOCK = 16
NEG = -1e30
KEY_MIN = -(2 ** 31)

OFF_CQKV = 0
OFF_CZ = 3072
OFF_AQ = 4096
OFF_AK = 5120
OFF_AV = 6144
OFF_BQ = 7168
OFF_BK = 8192
OFF_BV = 8448
OFF_BQI = 8704
OFF_BKI = 9216
OFF_SMALL = 9344
CA_LANE = 8
CB_LANE = 16
OFF_GT = 9728
N_PROJ = 15872


def _cparams(sem):
    return pltpu.CompilerParams(dimension_semantics=sem, vmem_limit_bytes=VMEM_LIMIT)


def _row_tiles(B, T, target=ROW_TILE):
    tt = min(T, target)
    assert T % tt == 0 and tt % SUBLANES == 0
    bb = max(1, min(B, target // tt))
    while B % bb:
        bb -= 1
    return bb, tt


def _dot(a, b):
    return jnp.dot(a, b, preferred_element_type=F32)


def _dot_nt(a, b):
    return lax.dot_general(a, b, (((1,), (1,)), ((), ())), preferred_element_type=F32)


def _bdot(a, b):
    return _dot(a.astype(BF16), b.astype(BF16))


def _ada_kernel(c_ref, w_ref, b_ref, o_ref):
    c = c_ref[...]
    s = (c * jax.nn.sigmoid(c)).astype(BF16)
    o_ref[...] = _dot(s, w_ref[...].astype(BF16)) + b_ref[...]


def _ada(c, w, b):
    M, D = c.shape
    N = w.shape[1]
    tn = 1024
    return pl.pallas_call(
        _ada_kernel,
        grid=(N // tn,),
        in_specs=[pl.BlockSpec((M, D), lambda j: (0, 0)),
                  pl.BlockSpec((D, tn), lambda j: (0, j)),
                  pl.BlockSpec((1, tn), lambda j: (0, j))],
        out_specs=pl.BlockSpec((M, tn), lambda j: (0, j)),
        out_shape=jax.ShapeDtypeStruct((M, N), F32),
        compiler_params=_cparams(("parallel",)),
        name="ada",
    )(c, w, b.reshape(1, N))


def _mm_norm_kernel(x_ref, g_ref, sh_ref, sc_ref, w_ref, o_ref, h_ref):
    bb, tt, D = x_ref.shape
    tn = w_ref.shape[1]

    @pl.when(pl.program_id(2) == 0)
    def _():
        x = x_ref[...]
        y = x * lax.rsqrt(jnp.mean(x * x, axis=-1, keepdims=True) + EPS) * g_ref[...]
        h = y * (1.0 + sc_ref[...]) + sh_ref[...]
        h_ref[...] = h.reshape(bb * tt, D).astype(BF16)

    acc = _dot(h_ref[...], w_ref[...])
    o_ref[...] = acc.reshape(bb, tt, tn).astype(o_ref.dtype)


def _mm_norm(x, g, shift, scale, w, out_dtype=F32):
    B, T, D = x.shape
    N = w.shape[1]
    bb, tt = _row_tiles(B, T)
    tn = COL_TILE
    assert N % tn == 0
    return pl.pallas_call(
        _mm_norm_kernel,
        grid=(B // bb, T // tt, N // tn),
        in_specs=[pl.BlockSpec((bb, tt, D), lambda b, i, j: (b, i, 0)),
                  pl.BlockSpec((1, 1, D), lambda b, i, j: (0, 0, 0)),
                  pl.BlockSpec((bb, 1, D), lambda b, i, j: (b, 0, 0)),
                  pl.BlockSpec((bb, 1, D), lambda b, i, j: (b, 0, 0)),
                  pl.BlockSpec((D, tn), lambda b, i, j: (0, j))],
        out_specs=pl.BlockSpec((bb, tt, tn), lambda b, i, j: (b, i, j)),
        out_shape=jax.ShapeDtypeStruct((B, T, N), out_dtype),
        scratch_shapes=[pltpu.VMEM((bb * tt, D), BF16)],
        compiler_params=_cparams(("parallel", "parallel", "arbitrary")),
        name="mm_norm",
    )(x, g.reshape(1, 1, D), shift, scale, w)


def _mm_res_kernel(a_ref, w_ref, res_ref, g_ref, o_ref):
    bb, tt, K = a_ref.shape
    tn = w_ref.shape[1]
    acc = _dot(a_ref[...].reshape(bb * tt, K), w_ref[...])
    o_ref[...] = res_ref[...] + g_ref[...] * acc.reshape(bb, tt, tn)


def _mm_res(a, w, res, gate, row_target=ROW_TILE):
    B, T, K = a.shape
    N = w.shape[1]
    bb, tt = _row_tiles(B, T, row_target)
    tn = COL_TILE
    assert N % tn == 0
    return pl.pallas_call(
        _mm_res_kernel,
        grid=(B // bb, T // tt, N // tn),
        in_specs=[pl.BlockSpec((bb, tt, K), lambda b, i, j: (b, i, 0)),
                  pl.BlockSpec((K, tn), lambda b, i, j: (0, j)),
                  pl.BlockSpec((bb, tt, tn), lambda b, i, j: (b, i, j)),
                  pl.BlockSpec((bb, 1, tn), lambda b, i, j: (b, 0, j))],
        out_specs=pl.BlockSpec((bb, tt, tn), lambda b, i, j: (b, i, j)),
        out_shape=jax.ShapeDtypeStruct((B, T, N), F32),
        compiler_params=_cparams(("parallel", "parallel", "arbitrary")),
        name="mm_res",
    )(a, w, res, gate)


def _merge_kernel(b0_ref, b1_ref, b2_ref, w_ref, g0_ref, g1_ref, g2_ref, o_ref):
    bb, tt, W = b0_ref.shape
    tn = w_ref.shape[2]
    acc = None
    for n, (br, gr) in enumerate(((b0_ref, g0_ref), (b1_ref, g1_ref), (b2_ref, g2_ref))):
        y = _dot(br[...].reshape(bb * tt, W), w_ref[n])
        t = jax.nn.sigmoid(gr[...].reshape(bb * tt, tn)) * y
        acc = t if acc is None else acc + t
    o_ref[...] = acc.reshape(bb, tt, tn).astype(o_ref.dtype)


def _merge(oa, ob, oc, w_branch, proj, D):
    B, T, W = oa.shape
    bb, tt = _row_tiles(B, T)
    tn = COL_TILE
    gblk = OFF_GT // tn
    nper = D // tn
    br_spec = pl.BlockSpec((bb, tt, W), lambda b, i, j: (b, i, 0))

    def gate_spec(n):
        return pl.BlockSpec((bb, tt, tn), lambda b, i, j: (b, i, gblk + n * nper + j))

    return pl.pallas_call(
        _merge_kernel,
        grid=(B // bb, T // tt, D // tn),
        in_specs=[br_spec, br_spec, br_spec,
                  pl.BlockSpec((N_BRANCH, W, tn), lambda b, i, j: (0, 0, j)),
                  gate_spec(0), gate_spec(1), gate_spec(2)],
        out_specs=pl.BlockSpec((bb, tt, tn), lambda b, i, j: (b, i, j)),
        out_shape=jax.ShapeDtypeStruct((B, T, D), BF16),
        compiler_params=_cparams(("parallel", "parallel", "arbitrary")),
        name="merge",
    )(oa, ob, oc, w_branch, proj, proj, proj)


def _prep_kernel(aqk_ref, av_ref, bq_ref, bkv_ref, bqi_ref, bki_ref, rope_ref,
                 gaq_ref, gak_ref, gbq_ref, gbk_ref,
                 qa_o, ka_o, kab_o, va_o, vab_o, qb_o, kb_o, kbb_o, vb_o, vbb_o,
                 qi_o, ki_o, ki2_o):
    c1 = rope_ref[:, 0:LANES]
    s1 = rope_ref[:, LANES:2 * LANES]
    c2 = rope_ref[:, 2 * LANES:3 * LANES]
    s2 = rope_ref[:, 3 * LANES:4 * LANES]
    tt = c1.shape[0]
    lane = lax.broadcasted_iota(jnp.int32, (tt, LANES), 1)
    low_half = (lane & (I_HD - 1)) < (I_HD // 2)

    def norm_rope(x, g):
        y = x * lax.rsqrt(jnp.mean(x * x, axis=-1, keepdims=True) + EPS) * g
        return y * c1 + pltpu.roll(y, A_HD // 2, 1) * s1

    def rope64(x):
        r = jnp.where(low_half, pltpu.roll(x, LANES - I_HD // 2, 1), pltpu.roll(x, I_HD // 2, 1))
        return x * c2 + r * s2

    gaq, gak, gbq, gbk = gaq_ref[...], gak_ref[...], gbq_ref[...], gbk_ref[...]
    for hm in range(2 * A_HEADS):
        sl = slice(hm * LANES, (hm + 1) * LANES)
        q = norm_rope(aqk_ref[0, :, sl], gaq) * (A_HD ** -0.5)
        qa_o[0, :, sl] = q.astype(BF16)
        k = norm_rope(aqk_ref[0, :, 2 * A_HEADS * LANES + hm * LANES:2 * A_HEADS * LANES + (hm + 1) * LANES], gak)
        ka_o[0, :, sl] = k
        kab_o[0, :, sl] = k.astype(BF16)
    va = av_ref[0]
    va_o[0] = va
    vab_o[0] = va.astype(BF16)
    for h in range(B_HEADS):
        sl = slice(h * LANES, (h + 1) * LANES)
        q = norm_rope(bq_ref[0, :, sl], gbq) * (B_HD ** -0.5)
        qb_o[0, :, sl] = q.astype(BF16)
    for n in range(B_KV):
        sl = slice(n * LANES, (n + 1) * LANES)
        k = norm_rope(bkv_ref[0, :, sl], gbk)
        kb_o[0, :, sl] = k
        kbb_o[0, :, sl] = k.astype(BF16)
    vb = bkv_ref[0, :, B_KV * LANES:2 * B_KV * LANES]
    vb_o[0] = vb
    vbb_o[0] = vb.astype(BF16)
    for p in range(I_HEADS * I_HD // LANES):
        sl = slice(p * LANES, (p + 1) * LANES)
        qi_o[0, :, sl] = (rope64(bqi_ref[0, :, sl]) * (I_HD ** -0.5)).astype(BF16)
    ki = rope64(bki_ref[0, :, 0:LANES])
    ki_o[0] = ki[:, 0:I_HD]
    ki2_o[0] = (ki + pltpu.roll(ki, I_HD, 1)).astype(BF16)


def _rope_table(pos):
    def tab(half):
        inv = jnp.power(ROPE_THETA, -jnp.arange(half, dtype=F32) / half)
        ang = pos.astype(F32)[:, None] * inv[None, :]
        return jnp.cos(ang), jnp.sin(ang)

    c, s = tab(A_HD // 2)
    ci, si = tab(I_HD // 2)
    return jnp.concatenate([c, c, -s, s, ci, ci, ci, ci, -si, si, -si, si], axis=1)


def _prep(proj, rope, gaq, gak, gbq, gbk):
    B, T, _ = proj.shape
    tt = min(T, 256)
    assert T % tt == 0

    def pspec(width, off):
        assert off % width == 0
        return pl.BlockSpec((1, tt, width), lambda b, i: (b, i, off // width))

    def ospec(width):
        return pl.BlockSpec((1, tt, width), lambda b, i: (b, i, 0))

    gspec = pl.BlockSpec((1, LANES), lambda b, i: (0, 0))
    outs = [(1024, BF16), (1024, F32), (1024, BF16), (1024, F32), (1024, BF16),
            (1024, BF16), (256, F32), (256, BF16), (256, F32), (256, BF16),
            (512, BF16), (I_HD, F32), (LANES, BF16)]
    return pl.pallas_call(
        _prep_kernel,
        grid=(B, T // tt),
        in_specs=[pspec(2048, OFF_AQ), pspec(1024, OFF_AV), pspec(1024, OFF_BQ), pspec(512, OFF_BK),
                  pspec(512, OFF_BQI), pspec(256, OFF_BKI),
                  pl.BlockSpec((tt, 4 * LANES), lambda b, i: (i, 0)),
                  gspec, gspec, gspec, gspec],
        out_specs=[ospec(w) for w, _ in outs],
        out_shape=[jax.ShapeDtypeStruct((B, T, w), d) for w, d in outs],
        compiler_params=_cparams(("parallel", "parallel")),
        name="prep",
    )(proj, proj, proj, proj, proj, proj, rope,
      gaq.reshape(1, LANES), gak.reshape(1, LANES), gbq.reshape(1, LANES), gbk.reshape(1, LANES))


def _diff_kernel(*refs, has_past, P, tq, tkp, tkn, nP, nN, lam_init):
    if has_past:
        q_ref, kn_ref, vn_ref, kp_ref, vp_ref, lam_ref, sub_ref, o_ref, m_s, l_s, acc_s = refs
    else:
        q_ref, kn_ref, vn_ref, lam_ref, sub_ref, o_ref, m_s, l_s, acc_s = refs
    i = pl.program_id(2)
    j = pl.program_id(3)
    q_pos0 = P + i * tq

    @pl.when(j == 0)
    def _():
        m_s[...] = jnp.full(m_s.shape, NEG, F32)
        l_s[...] = jnp.zeros(l_s.shape, F32)
        acc_s[...] = jnp.zeros(acc_s.shape, F32)

    def process(k, v, kpos0, tk):
        qpos = q_pos0 + lax.broadcasted_iota(jnp.int32, (tq, 1), 0)
        kpos = kpos0 + lax.broadcasted_iota(jnp.int32, (1, tk), 1)
        vis = (kpos >> CHUNK_SHIFT) <= (qpos >> CHUNK_SHIFT)
        for m in range(2):
            sl = slice(m * A_HD, (m + 1) * A_HD)
            s = _dot_nt(q_ref[0, :, sl], k[:, sl])
            s = jnp.where(vis, s, NEG)
            m_prev = m_s[m]
            m_new = jnp.maximum(m_prev, jnp.max(s, axis=-1, keepdims=True))
            alpha = jnp.exp(m_prev - m_new)
            p = jnp.exp(s - m_new)
            l_s[m] = alpha * l_s[m] + jnp.sum(p, axis=-1, keepdims=True)
            acc_s[m] = alpha * acc_s[m] + _dot(p.astype(BF16), v)
            m_s[m] = m_new

    last_q_chunk = (q_pos0 + tq - 1) // CHUNK
    if has_past:
        @pl.when(j < nP)
        def _():
            process(kp_ref[0].astype(BF16), vp_ref[0].astype(BF16), j * tkp, tkp)

    @pl.when((j >= nP) & ((P + (j - nP) * tkn) // CHUNK <= last_q_chunk))
    def _():
        process(kn_ref[0], vn_ref[0], P + (j - nP) * tkn, tkn)

    @pl.when(j == nP + nN - 1)
    def _():
        lp = lam_ref[...]
        lam = (jnp.exp(jnp.sum(lp[0:1] * lp[1:2], axis=-1, keepdims=True))
               - jnp.exp(jnp.sum(lp[2:3] * lp[3:4], axis=-1, keepdims=True)) + lam_init)
        o = acc_s[0] / l_s[0] - lam * (acc_s[1] / l_s[1])
        o = o * lax.rsqrt(jnp.mean(o * o, axis=-1, keepdims=True) + EPS) * sub_ref[...]
        o_ref[0] = (o * (1.0 - lam_init)).astype(o_ref.dtype)


def _diff_attn(qa, kn, vn, kp, vp, lam_p, subln, P, lam_init):
    B, T, _ = qa.shape
    has_past = kp is not None
    tq = min(T, 256)
    tkn = min(T, 512)
    assert T % tq == 0 and T % tkn == 0
    nN = T // tkn
    if has_past:
        tkp = min(P, 1024)
        assert P % tkp == 0 and P % CHUNK == 0
        nP = P // tkp
    else:
        assert P == 0
        tkp, nP = 0, 0

    def jn_of(i, j):
        last_pos = ((P + (i + 1) * tq - 1) // CHUNK + 1) * CHUNK - 1 - P
        return jnp.minimum(jnp.maximum(j - nP, 0), jnp.minimum(nN - 1, last_pos // tkn))

    W = 2 * A_HD
    in_specs = [pl.BlockSpec((1, tq, W), lambda b, h, i, j: (b, i, h)),
                pl.BlockSpec((1, tkn, W), lambda b, h, i, j: (b, jn_of(i, j), h)),
                pl.BlockSpec((1, tkn, W), lambda b, h, i, j: (b, jn_of(i, j), h))]
    args = [qa, kn, vn]
    if has_past:
        in_specs += [pl.BlockSpec((1, tkp, W), lambda b, h, i, j: (b, jnp.minimum(j, nP - 1), h)),
                     pl.BlockSpec((1, tkp, W), lambda b, h, i, j: (b, jnp.minimum(j, nP - 1), h))]
        args += [kp, vp]
    in_specs += [pl.BlockSpec((4, A_HD), lambda b, h, i, j: (0, 0)),
                 pl.BlockSpec((1, A_VD), lambda b, h, i, j: (0, 0))]
    args += [lam_p, subln.reshape(1, A_VD)]
    kern = functools.partial(_diff_kernel, has_past=has_past, P=P, tq=tq, tkp=tkp, tkn=tkn,
                             nP=nP, nN=nN, lam_init=lam_init)
    return pl.pallas_call(
        kern,
        grid=(B, A_HEADS, T // tq, nP + nN),
        in_specs=in_specs,
        out_specs=pl.BlockSpec((1, tq, W), lambda b, h, i, j: (b, i, h)),
        out_shape=jax.ShapeDtypeStruct((B, T, A_HEADS * A_VD), BF16),
        scratch_shapes=[pltpu.VMEM((2, tq, 1), F32), pltpu.VMEM((2, tq, 1), F32),
                        pltpu.VMEM((2, tq, A_VD), F32)],
        compiler_params=_cparams(("parallel", "parallel", "parallel", "arbitrary")),
        name="diff_attn",
    )(*args)


def _dsa_kernel(qi_ref, wi_ref, qb_ref, ki_ref, kb_ref, vb_ref, o_ref,
                key_s, bias_s, m_s, l_s, acc_s, *, P, L, tq, tk, topk):
    i = pl.program_id(1)
    q_pos0 = P + i * tq
    Lp = ki_ref.shape[1]
    ncol = jnp.minimum(L, ((q_pos0 + tq - 1) // CHUNK + 1) * CHUNK)
    nt = (ncol + tk - 1) // tk
    qpos = q_pos0 + lax.broadcasted_iota(jnp.int32, (tq, 1), 0)
    grp = B_HEADS // B_KV
    lane = lax.broadcasted_iota(jnp.int32, (tq, LANES), 1)
    wi = wi_ref[0] * (I_HEADS ** -0.5)

    def vis_of(t):
        kpos = t * tk + lax.broadcasted_iota(jnp.int32, (1, tk), 1)
        return ((kpos >> CHUNK_SHIFT) <= (qpos >> CHUNK_SHIFT)) & (kpos < L)

    def score_tile(t, carry):
        c0 = pl.multiple_of(t * tk, tk)
        kit = ki_ref[0, pl.ds(c0, tk), :]
        sc = jnp.zeros((tq, tk), F32)
        for pr in range(I_HEADS * I_HD // LANES):
            qpair = qi_ref[0, :, pr * LANES:(pr + 1) * LANES]
            for half in range(LANES // I_HD):
                h = pr * (LANES // I_HD) + half
                qh = jnp.where((lane >> I_HD_SHIFT) == half, qpair, jnp.zeros_like(qpair))
                isc = _dot_nt(qh, kit)
                sc = sc + jnp.maximum(isc, 0.0) * wi[:, h:h + 1]
        sc = jnp.where(vis_of(t), sc, -jnp.inf)
        bits = pltpu.bitcast(sc, jnp.int32)
        bits = jnp.where(bits == KEY_MIN, 0, bits)
        key_s[t] = bits ^ ((bits >> 31) & 0x7FFFFFFF)
        return carry

    lax.fori_loop(0, nt, score_tile, 0)

    def count(pred_fn):
        def body(t, acc):
            hit = jnp.where(pred_fn(key_s[t]), 1, 0)
            part = hit[:, 0:LANES]
            for c in range(1, tk // LANES):
                part = part + hit[:, c * LANES:(c + 1) * LANES]
            return acc + part
        acc = lax.fori_loop(0, nt, body, jnp.zeros((tq, LANES), jnp.int32))
        return jnp.sum(acc, axis=-1, keepdims=True)

    def select(_):
        def bit_body(bi, thr):
            cand = thr + jnp.left_shift(jnp.int32(1), 31 - bi)
            cnt = count(lambda kk: kk >= cand)
            return jnp.where(cnt >= topk, cand, thr)
        return lax.fori_loop(0, 32, bit_body, jnp.full((tq, 1), KEY_MIN, jnp.int32))

    thr = lax.cond(ncol > topk, select, lambda _: jnp.full((tq, 1), KEY_MIN, jnp.int32), 0)
    n_gt = count(lambda kk: kk > thr)
    n_eq = count(lambda kk: kk == thr)
    need = topk - n_gt
    neg_inf_key = jnp.int32(KEY_MIN + 0x7FFFFF)
    tie_break = jnp.max(jnp.where((n_eq > need) & (thr != neg_inf_key), 1.0, 0.0)) > 0.5

    def bias_fast(_):
        def body(t, c):
            sel = (key_s[t] >= thr) & vis_of(t)
            bias_s[t] = jnp.where(sel, 0.0, NEG)
            return c
        lax.fori_loop(0, nt, body, 0)
        return 0

    def bias_ties(_):
        r = lax.broadcasted_iota(jnp.int32, (LANES, LANES), 0)
        c = lax.broadcasted_iota(jnp.int32, (LANES, LANES), 1)
        before = jnp.where(r < c, 1.0, 0.0).astype(BF16)

        def body(t, seen):
            kk = key_s[t]
            vis = vis_of(t)
            for cb in range(tk // LANES):
                sl = slice(cb * LANES, (cb + 1) * LANES)
                eq = kk[:, sl] == thr
                eqf = jnp.where(eq, 1.0, 0.0)
                rank = seen + _dot(eqf.astype(BF16), before)
                sel = ((kk[:, sl] > thr) | (eq & (rank < need.astype(F32)))) & vis[:, sl]
                bias_s[t, :, sl] = jnp.where(sel, 0.0, NEG)
                seen = seen + jnp.sum(eqf, axis=-1, keepdims=True)
            return seen
        lax.fori_loop(0, nt, body, jnp.zeros((tq, 1), F32))
        return 0

    lax.cond(tie_break, bias_ties, bias_fast, 0)

    m_s[...] = jnp.full(m_s.shape, NEG, F32)
    l_s[...] = jnp.zeros(l_s.shape, F32)
    acc_s[...] = jnp.zeros(acc_s.shape, F32)

    def attn_tile(t, carry):
        c0 = pl.multiple_of(t * tk, tk)
        bias = bias_s[t]
        for n in range(B_KV):
            k = kb_ref[0, pl.ds(c0, tk), n * B_HD:(n + 1) * B_HD]
            v = vb_ref[0, pl.ds(c0, tk), n * B_HD:(n + 1) * B_HD]
            for g in range(grp):
                h = n * grp + g
                s = _dot_nt(qb_ref[0, :, h * B_HD:(h + 1) * B_HD], k) + bias
                m_prev = m_s[h]
                m_new = jnp.maximum(m_prev, jnp.max(s, axis=-1, keepdims=True))
                alpha = jnp.exp(m_prev - m_new)
                p = jnp.exp(s - m_new)
                l_s[h] = alpha * l_s[h] + jnp.sum(p, axis=-1, keepdims=True)
                acc_s[h] = alpha * acc_s[h] + _dot(p.astype(BF16), v)
                m_s[h] = m_new
        return carry

    lax.fori_loop(0, nt, attn_tile, 0)
    for h in range(B_HEADS):
        o_ref[0, :, h * B_HD:(h + 1) * B_HD] = (acc_s[h] / l_s[h]).astype(o_ref.dtype)


def _dsa(qi, proj, qb, ki2, kb, vb, P, L, tk):
    B, T, _ = qb.shape
    Lp = ki2.shape[1]
    tq = min(T, 128)
    assert T % tq == 0 and Lp % tk == 0 and tk % LANES == 0
    topk = min(TOPK_MAX, L // 4)
    kern = functools.partial(_dsa_kernel, P=P, L=L, tq=tq, tk=tk, topk=topk)
    return pl.pallas_call(
        kern,
        grid=(B, T // tq),
        in_specs=[pl.BlockSpec((1, tq, I_HEADS * I_HD), lambda b, i: (b, i, 0)),
                  pl.BlockSpec((1, tq, LANES), lambda b, i: (b, i, OFF_SMALL // LANES)),
                  pl.BlockSpec((1, tq, B_HEADS * B_HD), lambda b, i: (b, i, 0)),
                  pl.BlockSpec((1, Lp, LANES), lambda b, i: (b, 0, 0)),
                  pl.BlockSpec((1, Lp, B_KV * B_HD), lambda b, i: (b, 0, 0)),
                  pl.BlockSpec((1, Lp, B_KV * B_HD), lambda b, i: (b, 0, 0))],
        out_specs=pl.BlockSpec((1, tq, B_HEADS * B_HD), lambda b, i: (b, i, 0)),
        out_shape=jax.ShapeDtypeStruct((B, T, B_HEADS * B_HD), BF16),
        scratch_shapes=[pltpu.VMEM((Lp // tk, tq, tk), jnp.int32), pltpu.VMEM((Lp // tk, tq, tk), F32),
                        pltpu.VMEM((B_HEADS, tq, 1), F32), pltpu.VMEM((B_HEADS, tq, 1), F32),
                        pltpu.VMEM((B_HEADS, tq, B_HD), F32)],
        compiler_params=_cparams(("parallel", "arbitrary")),
        name="dsa",
    )(qi, proj, qb, ki2, kb, vb)


def _gdn_kernel(x_ref, z_ref, sm_ref, cw_ref, cbuf_ref, s0_ref, alog_ref, dtb_ref, gn_ref,
                o_ref, sfin_ref, buf_s, st_s, *, cc):
    c = pl.program_id(1)
    nc = pl.num_programs(1)
    HW = C_HEADS * C_HD

    @pl.when(c == 0)
    def _():
        buf_s[0:SUBLANES, :] = jnp.zeros((SUBLANES, C_QKV), F32)
        buf_s[SUBLANES - (C_CONV - 1):SUBLANES, :] = cbuf_ref[0]
        st_s[...] = s0_ref[0]

    buf_s[SUBLANES:SUBLANES + cc, :] = x_ref[0]
    y = None
    for jw in range(C_CONV):
        off = SUBLANES - (C_CONV - 1) + jw
        term = buf_s[off:off + cc, :] * cw_ref[jw:jw + 1, :]
        y = term if y is None else y + term
    buf_s[0:SUBLANES, :] = buf_s[cc:cc + SUBLANES, :]
    y = y * jax.nn.sigmoid(y)

    sm = sm_ref[0]
    xg = sm + dtb_ref[...]
    softplus = jnp.maximum(xg, 0.0) + jnp.log(1.0 + jnp.exp(-jnp.abs(xg)))
    g = -jnp.exp(alog_ref[...]) * softplus
    beta = jax.nn.sigmoid(sm)

    ri = lax.broadcasted_iota(jnp.int32, (cc, cc), 0)
    ci = lax.broadcasted_iota(jnp.int32, (cc, cc), 1)
    incl = ri >= ci
    strict = ri > ci
    eye = ri == ci
    bdiag = (ri // SOLVE_BLOCK) == (ci // SOLVE_BLOCK)
    G = jnp.dot(jnp.where(incl, 1.0, 0.0), g, preferred_element_type=F32,
                precision=lax.Precision.HIGHEST)

    for h in range(C_HEADS):
        sl = slice(h * C_HD, (h + 1) * C_HD)
        qh = y[:, sl]
        kh = y[:, HW + h * C_HD:HW + (h + 1) * C_HD]
        vh = y[:, 2 * HW + h * C_HD:2 * HW + (h + 1) * C_HD]
        qh = qh * lax.rsqrt(jnp.sum(qh * qh, axis=-1, keepdims=True) + EPS) * (C_HD ** -0.5)
        kh = kh * lax.rsqrt(jnp.sum(kh * kh, axis=-1, keepdims=True) + EPS)
        Gc = G[:, CA_LANE + h:CA_LANE + h + 1]
        bc = beta[:, CB_LANE + h:CB_LANE + h + 1]
        Gr = jnp.sum(jnp.where(eye, Gc, 0.0), axis=0, keepdims=True)
        gam = jnp.where(incl, jnp.exp(jnp.where(incl, Gc - Gr, 0.0)), 0.0)
        kb16 = kh.astype(BF16)
        A = jnp.where(strict, bc * _dot_nt(kb16, kb16) * gam, 0.0)
        Nd = jnp.where(bdiag, -A, 0.0)
        E = jnp.where(bdiag, 0.0, A)
        N2 = _bdot(Nd, Nd)
        N4 = _bdot(N2, N2)
        N8 = _bdot(N4, N4)
        Q = Nd + N2 + _bdot(Nd, N2)
        Q = Q + N4 + _bdot(Q, N4)
        Q = Q + N8 + _bdot(Q, N8)
        Fm = E + _bdot(Q, E)
        F2 = _bdot(Fm, Fm)
        R = F2 - Fm - _bdot(Fm, F2)
        Wm = R + Q + _bdot(R, Q)
        eG = jnp.exp(Gc)
        rhs = jnp.concatenate([bc * vh, (bc * eG) * kh], axis=-1)
        sol = rhs + _bdot(Wm, rhs)
        u = sol[:, 0:C_HD]
        w = sol[:, C_HD:2 * C_HD]
        S = st_s[h]
        S16 = S.astype(BF16)
        v_new = u - _dot(w.astype(BF16), S16)
        qk = _dot_nt(qh.astype(BF16), kb16) * gam
        o = _dot((qh * eG).astype(BF16), S16) + _bdot(qk, v_new)
        gl = Gc[cc - 1:cc, :]
        kdec = kh * jnp.exp(gl - Gc)
        st_s[h] = jnp.exp(gl) * S + _bdot(kdec.T, v_new)
        on = o * lax.rsqrt(jnp.mean(o * o, axis=-1, keepdims=True) + EPS) * gn_ref[...]
        z = z_ref[0, :, sl]
        o_ref[0, :, sl] = (on * (z * jax.nn.sigmoid(z))).astype(o_ref.dtype)

    @pl.when(c == nc - 1)
    def _():
        sfin_ref[0] = st_s[...]


def _gdn(proj, conv_w, cbuf, S0, a_log, dt_bias, out_norm):
    B, T, _ = proj.shape
    cc = min(CHUNK, T)
    assert T % cc == 0 and cc % SOLVE_BLOCK == 0 and cc // SOLVE_BLOCK <= 4 and cc >= SUBLANES
    HW = C_HEADS * C_HD
    kern = functools.partial(_gdn_kernel, cc=cc)

    def at_ca(p):
        return jnp.zeros((1, LANES), F32).at[0, CA_LANE:CA_LANE + C_HEADS].set(p.astype(F32))

    return pl.pallas_call(
        kern,
        grid=(B, T // cc),
        in_specs=[pl.BlockSpec((1, cc, C_QKV), lambda b, c: (b, c, OFF_CQKV // C_QKV)),
                  pl.BlockSpec((1, cc, HW), lambda b, c: (b, c, OFF_CZ // HW)),
                  pl.BlockSpec((1, cc, LANES), lambda b, c: (b, c, OFF_SMALL // LANES)),
                  pl.BlockSpec((C_CONV, C_QKV), lambda b, c: (0, 0)),
                  pl.BlockSpec((1, C_CONV - 1, C_QKV), lambda b, c: (b, 0, 0)),
                  pl.BlockSpec((1, C_HEADS, C_HD, C_HD), lambda b, c: (b, 0, 0, 0)),
                  pl.BlockSpec((1, LANES), lambda b, c: (0, 0)),
                  pl.BlockSpec((1, LANES), lambda b, c: (0, 0)),
                  pl.BlockSpec((1, C_HD), lambda b, c: (0, 0))],
        out_specs=[pl.BlockSpec((1, cc, HW), lambda b, c: (b, c, 0)),
                   pl.BlockSpec((1, C_HEADS, C_HD, C_HD), lambda b, c: (b, 0, 0, 0))],
        out_shape=[jax.ShapeDtypeStruct((B, T, HW), BF16),
                   jax.ShapeDtypeStruct((B, C_HEADS, C_HD, C_HD), F32)],
        scratch_shapes=[pltpu.VMEM((cc + SUBLANES, C_QKV), F32),
                        pltpu.VMEM((C_HEADS, C_HD, C_HD), F32)],
        compiler_params=_cparams(("parallel", "arbitrary")),
        name="gdn",
    )(proj, proj, proj, conv_w, cbuf, S0, at_ca(a_log), at_ca(dt_bias), out_norm.reshape(1, C_HD))


def _ffn_act_kernel(ug_ref, hg_ref, uu_ref, hu_ref, fg_ref, fu_ref, wg_ref, wu_ref, o_ref, bg_s, bu_s):
    i = pl.program_id(1)
    tt = ug_ref.shape[1]

    def conv(u_ref, h_ref, f_ref, w_ref, buf):
        @pl.when(i == 0)
        def _():
            buf[0:SUBLANES, :] = jnp.zeros((SUBLANES, buf.shape[1]), F32)
            buf[SUBLANES - (FFN_CONV - 1):SUBLANES, :] = f_ref[0]

        @pl.when(i > 0)
        def _():
            buf[0:SUBLANES, :] = h_ref[0]

        buf[SUBLANES:SUBLANES + tt, :] = u_ref[0]
        y = None
        for jw in range(FFN_CONV):
            off = SUBLANES - (FFN_CONV - 1) + jw
            term = buf[off:off + tt, :] * w_ref[jw:jw + 1, :]
            y = term if y is None else y + term
        return y

    yg = conv(ug_ref, hg_ref, fg_ref, wg_ref, bg_s)
    yu = conv(uu_ref, hu_ref, fu_ref, wu_ref, bu_s)
    o_ref[0] = (yg * jax.nn.sigmoid(yg) * yu).astype(o_ref.dtype)


def _ffn_act(u, fbuf, conv_w):
    B, T, F2 = u.shape
    FF = F2 // 2
    tt = min(T, 512)
    tc = 512
    assert T % tt == 0 and FF % tc == 0 and tt % SUBLANES == 0
    nco = FF // tc
    hb = tt // SUBLANES

    def main(off):
        return pl.BlockSpec((1, tt, tc), lambda b, i, c: (b, i, off + c))

    def halo(off):
        return pl.BlockSpec((1, SUBLANES, tc), lambda b, i, c: (b, jnp.maximum(i * hb - 1, 0), off + c))

    def fb(off):
        return pl.BlockSpec((1, FFN_CONV - 1, tc), lambda b, i, c: (b, 0, off + c))

    def wspec(off):
        return pl.BlockSpec((FFN_CONV, tc), lambda b, i, c: (0, off + c))

    return pl.pallas_call(
        _ffn_act_kernel,
        grid=(B, T // tt, nco),
        in_specs=[main(0), halo(0), main(nco), halo(nco), fb(0), fb(nco), wspec(0), wspec(nco)],
        out_specs=pl.BlockSpec((1, tt, tc), lambda b, i, c: (b, i, c)),
        out_shape=jax.ShapeDtypeStruct((B, T, FF), BF16),
        scratch_shapes=[pltpu.VMEM((tt + SUBLANES, tc), F32), pltpu.VMEM((tt + SUBLANES, tc), F32)],
        compiler_params=_cparams(("parallel", "parallel", "parallel")),
        name="ffn_act",
    )(u, u, u, u, fbuf, fbuf, conv_w, conv_w)


def _prep_w_in(w_in):
    D = w_in.shape[0]
    sizes = (1024, 1024, 1024, 1024, 256, 256, 512, 64, 8, C_QKV, 8, 8, 1024, N_BRANCH * D)
    offs = np.concatenate([[0], np.cumsum(sizes)])
    (aq, ak, av, bq, bk, bv, bqi, bki, bwi, cqkv, ca, cb, cz, gt) = [
        w_in[:, int(offs[k]):int(offs[k + 1])] for k in range(len(sizes))]

    def z(n):
        return jnp.zeros((D, n), w_in.dtype)

    cols = [cqkv, cz, aq, ak, av, bq, bk, bv, bqi, bki, z(LANES - I_HD), bwi, ca, cb, z(LANES - 24),
            z(OFF_GT - OFF_SMALL - LANES), gt]
    w = jnp.concatenate(cols, axis=1).astype(BF16)
    assert w.shape[1] == N_PROJ
    return w


def _pad_keys(past, new, Lp):
    B, P, W = past.shape
    T = new.shape[1]
    parts = [past.astype(BF16), new]
    if Lp > P + T:
        parts.append(jnp.zeros((B, Lp - P - T, W), BF16))
    return jnp.concatenate(parts, axis=1)


def _layer(x, mod, past, lw, layer_idx):
    kA_p, vA_p, kB_p, vB_p, kI_p, cbuf, S0, fbuf = past
    B, T, D = x.shape
    P = 0 if kA_p is None else kA_p.shape[1]
    L = P + T
    sh1, sc1, g1, sh2, sc2, g2 = [m.reshape(B, 1, D) for m in jnp.split(mod, 6, axis=-1)]

    proj = _mm_norm(x, lw['norm_mix'], sh1, sc1, lw['w_in'])
    rope = _rope_table(P + jnp.arange(T, dtype=jnp.int32))
    (qa, ka, kab, va, vab, qb, kb, kbb, vb, vbb, qi, ki, ki2) = _prep(
        proj, rope, lw['a_q_norm'], lw['a_k_norm'], lw['b_q_norm'], lw['b_k_norm'])

    lam_init = 0.8 - 0.6 * math.exp(-0.3 * layer_idx)
    if P:
        kp = kA_p.reshape(B, P, -1)
        vp = vA_p.reshape(B, P, -1)
    else:
        kp = vp = None
    oa = _diff_attn(qa, kab, vab, kp, vp, lw['a_lambda'], lw['a_subln'], P, lam_init)

    if P:
        Lp = -(-L // LANES) * LANES
        tk = Lp
        ki2_all = _pad_keys(jnp.tile(kI_p, (1, 1, LANES // I_HD)), ki2, Lp)
        kb_all = _pad_keys(kB_p.reshape(B, P, -1), kbb, Lp)
        vb_all = _pad_keys(vB_p.reshape(B, P, -1), vbb, Lp)
    else:
        tk = min(T, 512)
        ki2_all, kb_all, vb_all = ki2, kbb, vbb
    ob = _dsa(qi, proj, qb, ki2_all, kb_all, vb_all, P, L, tk)

    oc, S_fin = _gdn(proj, lw['c_conv'], cbuf, S0, lw['c_a_log'], lw['c_dt_bias'], lw['c_out_norm'])
    assert T >= C_CONV - 1 and T >= FFN_CONV - 1
    cbuf_new = proj[:, T - (C_CONV - 1):, OFF_CQKV:OFF_CQKV + C_QKV]

    mixed = _merge(oa, ob, oc, lw['w_branch'], proj, D)
    x = _mm_res(mixed, lw['w_out'], x, g1)

    u = _mm_norm(x, lw['norm_ffn'], sh2, sc2, lw['w_up'])
    act = _ffn_act(u, fbuf, lw['ffn_conv'])
    fbuf_new = u[:, T - (FFN_CONV - 1):, :]
    x = _mm_res(act, lw['w_down'], x, g2, row_target=512)

    new_state = (ka.reshape(B, T, A_HEADS, 2, A_HD), va.reshape(B, T, A_HEADS, A_VD),
                 kb.reshape(B, T, B_KV, B_HD), vb.reshape(B, T, B_KV, B_HD), ki,
                 cbuf_new, S_fin, fbuf_new)
    return x, new_state


def kernel(x_prompt, x_sample, c_prompt, c_sample, cache_diff_k, cache_diff_v, cache_dsa_k, cache_dsa_v, cache_dsa_kidx, state_gdn_conv, state_gdn, state_ffn_conv, w_ada, b_ada, norm_mix, w_in, a_q_norm, a_k_norm, a_lambda, a_subln, b_q_norm, b_k_norm, c_conv, c_a_log, c_dt_bias, c_out_norm, w_branch, w_out, norm_ffn, w_up, ffn_conv, w_down):
    depth = w_in.shape[0]
    bp, bs = x_prompt.shape[0], x_sample.shape[0]
    d_ff2 = w_up.shape[2]
    dt_ = x_prompt.dtype
    prompt_past = (None, None, None, None, None, jnp.zeros((bp, C_CONV - 1, C_QKV), dt_),
                   jnp.zeros((bp, C_HEADS, C_HD, C_HD), dt_), jnp.zeros((bp, FFN_CONV - 1, d_ff2), dt_))
    nrow = -(-(bp + bs) // SUBLANES) * SUBLANES
    c_all = jnp.concatenate([c_prompt, c_sample, jnp.zeros((nrow - bp - bs, c_prompt.shape[1]), dt_)], axis=0)
    xp, xs = x_prompt, x_sample
    prompt_states, sample_states = [], []
    for l in range(depth):
        lw = dict(norm_mix=norm_mix[l], w_in=_prep_w_in(w_in[l]),
                  a_q_norm=a_q_norm[l], a_k_norm=a_k_norm[l], a_lambda=a_lambda[l], a_subln=a_subln[l],
                  b_q_norm=b_q_norm[l], b_k_norm=b_k_norm[l], c_conv=c_conv[l], c_a_log=c_a_log[l],
                  c_dt_bias=c_dt_bias[l], c_out_norm=c_out_norm[l], w_branch=w_branch[l].astype(BF16),
                  w_out=w_out[l].astype(BF16), norm_ffn=norm_ffn[l], w_up=w_up[l].astype(BF16),
                  ffn_conv=ffn_conv[l], w_down=w_down[l].astype(BF16))
        mod = _ada(c_all, w_ada[l], b_ada[l])
        xp, st_p = _layer(xp, mod[:bp], prompt_past, lw, l)
        sample_past = (cache_diff_k[l], cache_diff_v[l], cache_dsa_k[l], cache_dsa_v[l], cache_dsa_kidx[l],
                       state_gdn_conv[l], state_gdn[l], state_ffn_conv[l])
        xs, st_s = _layer(xs, mod[bp:bp + bs], sample_past, lw, l)
        prompt_states.append(st_p)
        sample_states.append(st_s)
    p_out = [jnp.stack(s, axis=0) for s in zip(*prompt_states)]
    s_out = [jnp.stack(s, axis=0) for s in zip(*sample_states)]
    return (xp, xs, *p_out, *s_out)
```

```python
import functools
import math

import jax
import jax.numpy as jnp
import numpy as np
from jax import lax
from jax.experimental import pallas as pl
from jax.experimental.pallas import tpu as pltpu

F32 = jnp.float32
BF16 = jnp.bfloat16

CHUNK = 64
CHUNK_SHIFT = 6
ROPE_THETA = 10000.0
EPS = 1e-6
A_HEADS, A_HD, A_VD = 4, 128, 256
B_HEADS, B_KV, B_HD = 8, 2, 128
I_HEADS, I_HD = 8, 64
I_HD_SHIFT = 6
TOPK_MAX = 256
C_HEADS, C_HD, C_CONV = 8, 128, 4
C_QKV = 3 * C_HEADS * C_HD
BRANCH_W = 1024
N_BRANCH = 3
FFN_CONV = 3

LANES = 128
SUBLANES = 8
VMEM_LIMIT = 56 * 1024 * 1024
ROW_TILE = 1024
COL_TILE = 512
SOLVE_BLOCK = 16
TILE_UNROLL = 4
NEG = -1e30
KEY_MIN = -(2 ** 31)

OFF_CQKV = 0
OFF_CZ = 3072
OFF_AQ = 4096
OFF_AK = 5120
OFF_AV = 6144
OFF_BQ = 7168
OFF_BK = 8192
OFF_BV = 8448
OFF_BQI = 8704
OFF_BKI = 9216
OFF_SMALL = 9344
CA_LANE = 8
CB_LANE = 16
OFF_GT = 9728
N_PROJ = 15872


def _cparams(sem):
    return pltpu.CompilerParams(dimension_semantics=sem, vmem_limit_bytes=VMEM_LIMIT)


def _row_tiles(B, T, target=ROW_TILE):
    tt = min(T, target)
    assert T % tt == 0 and tt % SUBLANES == 0
    bb = max(1, min(B, target // tt))
    while B % bb:
        bb -= 1
    return bb, tt


def _dot(a, b):
    return jnp.dot(a, b, preferred_element_type=F32)


def _dot_nt(a, b):
    return lax.dot_general(a, b, (((1,), (1,)), ((), ())), preferred_element_type=F32)


def _bdot(a, b):
    return _dot(a.astype(BF16), b.astype(BF16))


def _tile_loop(n, body, init, unroll):
    nb = n // unroll

    def trip(tb, c):
        for u in range(unroll):
            c = body(tb * unroll + u, c)
        return c

    c = lax.fori_loop(0, nb, trip, init)
    return lax.fori_loop(nb * unroll, n, body, c)


def _ada_kernel(c_ref, w_ref, b_ref, o_ref):
    c = c_ref[...]
    s = (c * jax.nn.sigmoid(c)).astype(BF16)
    o_ref[...] = _dot(s, w_ref[...].astype(BF16)) + b_ref[...]


def _ada(c, w, b):
    M, D = c.shape
    N = w.shape[1]
    tn = 1024
    return pl.pallas_call(
        _ada_kernel,
        grid=(N // tn,),
        in_specs=[pl.BlockSpec((M, D), lambda j: (0, 0)),
                  pl.BlockSpec((D, tn), lambda j: (0, j)),
                  pl.BlockSpec((1, tn), lambda j: (0, j))],
        out_specs=pl.BlockSpec((M, tn), lambda j: (0, j)),
        out_shape=jax.ShapeDtypeStruct((M, N), F32),
        compiler_params=_cparams(("parallel",)),
        name="ada",
    )(c, w, b.reshape(1, N))


def _mm_norm_kernel(x_ref, g_ref, sh_ref, sc_ref, w_ref, o_ref, h_ref):
    bb, tt, D = x_ref.shape
    tn = w_ref.shape[1]

    @pl.when(pl.program_id(2) == 0)
    def _():
        x = x_ref[...]
        y = x * lax.rsqrt(jnp.mean(x * x, axis=-1, keepdims=True) + EPS) * g_ref[...]
        h = y * (1.0 + sc_ref[...]) + sh_ref[...]
        h_ref[...] = h.reshape(bb * tt, D).astype(BF16)

    acc = _dot(h_ref[...], w_ref[...])
    o_ref[...] = acc.reshape(bb, tt, tn).astype(o_ref.dtype)


def _mm_norm(x, g, shift, scale, w, out_dtype=F32):
    B, T, D = x.shape
    N = w.shape[1]
    bb, tt = _row_tiles(B, T)
    tn = COL_TILE
    assert N % tn == 0
    return pl.pallas_call(
        _mm_norm_kernel,
        grid=(B // bb, T // tt, N // tn),
        in_specs=[pl.BlockSpec((bb, tt, D), lambda b, i, j: (b, i, 0)),
                  pl.BlockSpec((1, 1, D), lambda b, i, j: (0, 0, 0)),
                  pl.BlockSpec((bb, 1, D), lambda b, i, j: (b, 0, 0)),
                  pl.BlockSpec((bb, 1, D), lambda b, i, j: (b, 0, 0)),
                  pl.BlockSpec((D, tn), lambda b, i, j: (0, j))],
        out_specs=pl.BlockSpec((bb, tt, tn), lambda b, i, j: (b, i, j)),
        out_shape=jax.ShapeDtypeStruct((B, T, N), out_dtype),
        scratch_shapes=[pltpu.VMEM((bb * tt, D), BF16)],
        compiler_params=_cparams(("parallel", "parallel", "arbitrary")),
        name="mm_norm",
    )(x, g.reshape(1, 1, D), shift, scale, w)


def _mm_res_kernel(a_ref, w_ref, res_ref, g_ref, o_ref):
    bb, tt, K = a_ref.shape
    tn = w_ref.shape[1]
    acc = _dot(a_ref[...].reshape(bb * tt, K), w_ref[...])
    o_ref[...] = res_ref[...] + g_ref[...] * acc.reshape(bb, tt, tn)


def _mm_res(a, w, res, gate, row_target=ROW_TILE):
    B, T, K = a.shape
    N = w.shape[1]
    bb, tt = _row_tiles(B, T, row_target)
    tn = COL_TILE
    assert N % tn == 0
    return pl.pallas_call(
        _mm_res_kernel,
        grid=(B // bb, T // tt, N // tn),
        in_specs=[pl.BlockSpec((bb, tt, K), lambda b, i, j: (b, i, 0)),
                  pl.BlockSpec((K, tn), lambda b, i, j: (0, j)),
                  pl.BlockSpec((bb, tt, tn), lambda b, i, j: (b, i, j)),
                  pl.BlockSpec((bb, 1, tn), lambda b, i, j: (b, 0, j))],
        out_specs=pl.BlockSpec((bb, tt, tn), lambda b, i, j: (b, i, j)),
        out_shape=jax.ShapeDtypeStruct((B, T, N), F32),
        compiler_params=_cparams(("parallel", "parallel", "arbitrary")),
        name="mm_res",
    )(a, w, res, gate)


def _merge_kernel(b0_ref, b1_ref, b2_ref, w_ref, g0_ref, g1_ref, g2_ref, o_ref):
    bb, tt, W = b0_ref.shape
    tn = w_ref.shape[2]
    acc = None
    for n, (br, gr) in enumerate(((b0_ref, g0_ref), (b1_ref, g1_ref), (b2_ref, g2_ref))):
        y = _dot(br[...].reshape(bb * tt, W), w_ref[n])
        t = jax.nn.sigmoid(gr[...].reshape(bb * tt, tn)) * y
        acc = t if acc is None else acc + t
    o_ref[...] = acc.reshape(bb, tt, tn).astype(o_ref.dtype)


def _merge(oa, ob, oc, w_branch, proj, D):
    B, T, W = oa.shape
    bb, tt = _row_tiles(B, T)
    tn = COL_TILE
    gblk = OFF_GT // tn
    nper = D // tn
    br_spec = pl.BlockSpec((bb, tt, W), lambda b, i, j: (b, i, 0))

    def gate_spec(n):
        return pl.BlockSpec((bb, tt, tn), lambda b, i, j: (b, i, gblk + n * nper + j))

    return pl.pallas_call(
        _merge_kernel,
        grid=(B // bb, T // tt, D // tn),
        in_specs=[br_spec, br_spec, br_spec,
                  pl.BlockSpec((N_BRANCH, W, tn), lambda b, i, j: (0, 0, j)),
                  gate_spec(0), gate_spec(1), gate_spec(2)],
        out_specs=pl.BlockSpec((bb, tt, tn), lambda b, i, j: (b, i, j)),
        out_shape=jax.ShapeDtypeStruct((B, T, D), BF16),
        compiler_params=_cparams(("parallel", "parallel", "arbitrary")),
        name="merge",
    )(oa, ob, oc, w_branch, proj, proj, proj)


def _prep_kernel(aqk_ref, av_ref, bq_ref, bkv_ref, bqi_ref, bki_ref, rope_ref,
                 gaq_ref, gak_ref, gbq_ref, gbk_ref,
                 qa_o, ka_o, kab_o, va_o, vab_o, qb_o, kb_o, kbb_o, vb_o, vbb_o,
                 qi_o, ki_o, ki2_o):
    c1 = rope_ref[:, 0:LANES]
    s1 = rope_ref[:, LANES:2 * LANES]
    c2 = rope_ref[:, 2 * LANES:3 * LANES]
    s2 = rope_ref[:, 3 * LANES:4 * LANES]
    tt = c1.shape[0]
    lane = lax.broadcasted_iota(jnp.int32, (tt, LANES), 1)
    low_half = (lane & (I_HD - 1)) < (I_HD // 2)

    def norm_rope(x, g):
        y = x * lax.rsqrt(jnp.mean(x * x, axis=-1, keepdims=True) + EPS) * g
        return y * c1 + pltpu.roll(y, A_HD // 2, 1) * s1

    def rope64(x):
        r = jnp.where(low_half, pltpu.roll(x, LANES - I_HD // 2, 1), pltpu.roll(x, I_HD // 2, 1))
        return x * c2 + r * s2

    gaq, gak, gbq, gbk = gaq_ref[...], gak_ref[...], gbq_ref[...], gbk_ref[...]
    for hm in range(2 * A_HEADS):
        sl = slice(hm * LANES, (hm + 1) * LANES)
        q = norm_rope(aqk_ref[0, :, sl], gaq) * (A_HD ** -0.5)
        qa_o[0, :, sl] = q.astype(BF16)
        k = norm_rope(aqk_ref[0, :, 2 * A_HEADS * LANES + hm * LANES:2 * A_HEADS * LANES + (hm + 1) * LANES], gak)
        ka_o[0, :, sl] = k
        kab_o[0, :, sl] = k.astype(BF16)
    va = av_ref[0]
    va_o[0] = va
    vab_o[0] = va.astype(BF16)
    for h in range(B_HEADS):
        sl = slice(h * LANES, (h + 1) * LANES)
        q = norm_rope(bq_ref[0, :, sl], gbq) * (B_HD ** -0.5)
        qb_o[0, :, sl] = q.astype(BF16)
    for n in range(B_KV):
        sl = slice(n * LANES, (n + 1) * LANES)
        k = norm_rope(bkv_ref[0, :, sl], gbk)
        kb_o[0, :, sl] = k
        kbb_o[0, :, sl] = k.astype(BF16)
    vb = bkv_ref[0, :, B_KV * LANES:2 * B_KV * LANES]
    vb_o[0] = vb
    vbb_o[0] = vb.astype(BF16)
    for p in range(I_HEADS * I_HD // LANES):
        sl = slice(p * LANES, (p + 1) * LANES)
        qi_o[0, :, sl] = (rope64(bqi_ref[0, :, sl]) * (I_HD ** -0.5)).astype(BF16)
    ki = rope64(bki_ref[0, :, 0:LANES])
    ki_o[0] = ki[:, 0:I_HD]
    ki2_o[0] = (ki + pltpu.roll(ki, I_HD, 1)).astype(BF16)


def _rope_table(pos):
    def tab(half):
        inv = jnp.power(ROPE_THETA, -jnp.arange(half, dtype=F32) / half)
        ang = pos.astype(F32)[:, None] * inv[None, :]
        return jnp.cos(ang), jnp.sin(ang)

    c, s = tab(A_HD // 2)
    ci, si = tab(I_HD // 2)
    return jnp.concatenate([c, c, -s, s, ci, ci, ci, ci, -si, si, -si, si], axis=1)


def _prep(proj, rope, gaq, gak, gbq, gbk):
    B, T, _ = proj.shape
    tt = min(T, 256)
    assert T % tt == 0

    def pspec(width, off):
        assert off % width == 0
        return pl.BlockSpec((1, tt, width), lambda b, i: (b, i, off // width))

    def ospec(width):
        return pl.BlockSpec((1, tt, width), lambda b, i: (b, i, 0))

    gspec = pl.BlockSpec((1, LANES), lambda b, i: (0, 0))
    outs = [(1024, BF16), (1024, F32), (1024, BF16), (1024, F32), (1024, BF16),
            (1024, BF16), (256, F32), (256, BF16), (256, F32), (256, BF16),
            (512, BF16), (I_HD, F32), (LANES, BF16)]
    return pl.pallas_call(
        _prep_kernel,
        grid=(B, T // tt),
        in_specs=[pspec(2048, OFF_AQ), pspec(1024, OFF_AV), pspec(1024, OFF_BQ), pspec(512, OFF_BK),
                  pspec(512, OFF_BQI), pspec(256, OFF_BKI),
                  pl.BlockSpec((tt, 4 * LANES), lambda b, i: (i, 0)),
                  gspec, gspec, gspec, gspec],
        out_specs=[ospec(w) for w, _ in outs],
        out_shape=[jax.ShapeDtypeStruct((B, T, w), d) for w, d in outs],
        compiler_params=_cparams(("parallel", "parallel")),
        name="prep",
    )(proj, proj, proj, proj, proj, proj, rope,
      gaq.reshape(1, LANES), gak.reshape(1, LANES), gbq.reshape(1, LANES), gbk.reshape(1, LANES))


def _diff_kernel(*refs, has_past, P, tq, tkp, lam_init):
    if has_past:
        q_ref, kn_ref, vn_ref, kp_ref, vp_ref, lam_ref, sub_ref, o_ref = refs[:8]
    else:
        q_ref, kn_ref, vn_ref, lam_ref, sub_ref, o_ref = refs[:6]
    nscr = 4 if has_past else 3
    scr = refs[-nscr:]
    acc = scr[0:2]
    sn_s = scr[2]
    sp_s = scr[3] if has_past else None
    i = pl.program_id(2)
    q_pos0 = P + i * tq
    q = [q_ref[0, :, m * A_HD:(m + 1) * A_HD] for m in range(2)]
    msl = [slice(m * A_HD, (m + 1) * A_HD) for m in range(2)]
    wide = tq % LANES == 0

    def fold(x, op):
        f = x[:, 0:LANES]
        for c in range(1, x.shape[1] // LANES):
            f = op(f, x[:, c * LANES:(c + 1) * LANES])
        return f

    neg = tuple(jnp.full((tq, LANES), NEG, F32) for _ in range(2))
    zero = tuple(jnp.zeros((tq, LANES), F32) for _ in range(2))

    mrow = [jnp.full((tq, 1), NEG, F32) for _ in range(2)]
    if has_past:
        def past_scores(t, mr):
            r0 = pl.multiple_of(t * tkp, tkp)
            k = kp_ref[0, pl.ds(r0, tkp), :].astype(BF16)
            out = []
            for m in range(2):
                s = _dot_nt(q[m], k[:, msl[m]])
                sp_s[m, t] = s
                out.append(jnp.maximum(mr[m], fold(s, jnp.maximum)))
            return tuple(out)
        mp = _tile_loop(P // tkp, past_scores, neg, 2)
        mrow = [jnp.maximum(mrow[m], jnp.max(mp[m], axis=-1, keepdims=True)) for m in range(2)]

    def new_scores(t, mr):
        r0 = pl.multiple_of(t * tq, tq)
        k = kn_ref[0, pl.ds(r0, tq), :]
        out = []
        for m in range(2):
            s = _dot_nt(q[m], k[:, msl[m]])
            sn_s[m, t] = s
            out.append(jnp.maximum(mr[m], fold(s, jnp.maximum)) if wide else mr[m])
        return tuple(out)
    mn = _tile_loop(i, new_scores, neg, TILE_UNROLL)

    r0 = pl.multiple_of(i * tq, tq)
    qpos = q_pos0 + lax.broadcasted_iota(jnp.int32, (tq, 1), 0)
    kpos = q_pos0 + lax.broadcasted_iota(jnp.int32, (1, tq), 1)
    vis = (kpos >> CHUNK_SHIFT) <= (qpos >> CHUNK_SHIFT)
    kd = kn_ref[0, pl.ds(r0, tq), :]
    for m in range(2):
        s = jnp.where(vis, _dot_nt(q[m], kd[:, msl[m]]), NEG)
        sn_s[m, i] = s
        mrow[m] = jnp.maximum(mrow[m], jnp.max(s, axis=-1, keepdims=True))
        if wide:
            mrow[m] = jnp.maximum(mrow[m], jnp.max(mn[m], axis=-1, keepdims=True))

    for m in range(2):
        acc[m][...] = jnp.zeros(acc[m].shape, F32)
    lrow = [jnp.zeros((tq, 1), F32) for _ in range(2)]
    if has_past:
        def past_weigh(t, lr):
            r0 = pl.multiple_of(t * tkp, tkp)
            v = vp_ref[0, pl.ds(r0, tkp), :].astype(BF16)
            out = []
            for m in range(2):
                p = jnp.exp(sp_s[m, t] - mrow[m])
                out.append(lr[m] + fold(p, jnp.add))
                acc[m][...] += _dot(p.astype(BF16), v)
            return tuple(out)
        lp_ = _tile_loop(P // tkp, past_weigh, zero, 2)
        lrow = [lrow[m] + jnp.sum(lp_[m], axis=-1, keepdims=True) for m in range(2)]

    def new_weigh(t, lr):
        r0 = pl.multiple_of(t * tq, tq)
        v = vn_ref[0, pl.ds(r0, tq), :]
        out = []
        for m in range(2):
            p = jnp.exp(sn_s[m, t] - mrow[m])
            if wide:
                out.append(lr[m] + fold(p, jnp.add))
            else:
                out.append(lr[m] + jnp.sum(p, axis=-1, keepdims=True))
            acc[m][...] += _dot(p.astype(BF16), v)
        return tuple(out)
    ln = _tile_loop(i + 1, new_weigh, zero if wide else tuple(lrow), TILE_UNROLL)
    if wide:
        lrow = [lrow[m] + jnp.sum(ln[m], axis=-1, keepdims=True) for m in range(2)]
    else:
        lrow = list(ln)

    lp = lam_ref[...]
    lam = (jnp.exp(jnp.sum(lp[0:1] * lp[1:2], axis=-1, keepdims=True))
           - jnp.exp(jnp.sum(lp[2:3] * lp[3:4], axis=-1, keepdims=True)) + lam_init)
    o = acc[0][...] / lrow[0] - lam * (acc[1][...] / lrow[1])
    o = o * lax.rsqrt(jnp.mean(o * o, axis=-1, keepdims=True) + EPS) * sub_ref[...]
    o_ref[0] = (o * (1.0 - lam_init)).astype(o_ref.dtype)


def _diff_attn(qa, kn, vn, kp, vp, lam_p, subln, P, lam_init):
    B, T, _ = qa.shape
    has_past = kp is not None
    tq = min(T, 256)
    assert T % tq == 0 and (tq % CHUNK == 0 or tq == T)
    if has_past:
        tkp = min(P, 512)
        assert P % tkp == 0 and P % CHUNK == 0
    else:
        assert P == 0
        tkp = 0
    W = 2 * A_HD
    in_specs = [pl.BlockSpec((1, tq, W), lambda b, h, i: (b, i, h)),
                pl.BlockSpec((1, T, W), lambda b, h, i: (b, 0, h)),
                pl.BlockSpec((1, T, W), lambda b, h, i: (b, 0, h))]
    args = [qa, kn, vn]
    if has_past:
        in_specs += [pl.BlockSpec((1, P, W), lambda b, h, i: (b, 0, h)),
                     pl.BlockSpec((1, P, W), lambda b, h, i: (b, 0, h))]
        args += [kp, vp]
    in_specs += [pl.BlockSpec((4, A_HD), lambda b, h, i: (0, 0)),
                 pl.BlockSpec((1, A_VD), lambda b, h, i: (0, 0))]
    args += [lam_p, subln.reshape(1, A_VD)]
    kern = functools.partial(_diff_kernel, has_past=has_past, P=P, tq=tq, tkp=tkp, lam_init=lam_init)
    scratch = [pltpu.VMEM((tq, A_VD), F32), pltpu.VMEM((tq, A_VD), F32),
               pltpu.VMEM((2, T // tq, tq, tq), F32)]
    if has_past:
        scratch.append(pltpu.VMEM((2, P // tkp, tq, tkp), F32))
    return pl.pallas_call(
        kern,
        grid=(B, A_HEADS, T // tq),
        in_specs=in_specs,
        out_specs=pl.BlockSpec((1, tq, W), lambda b, h, i: (b, i, h)),
        out_shape=jax.ShapeDtypeStruct((B, T, A_HEADS * A_VD), BF16),
        scratch_shapes=scratch,
        compiler_params=_cparams(("parallel", "parallel", "arbitrary")),
        name="diff_attn",
    )(*args)


def _dsa_kernel(qi_ref, wi_ref, qb_ref, ki_ref, kb_ref, vb_ref, o_ref,
                key_s, bias_s, s_s, *, P, L, tq, tk, topk):
    i = pl.program_id(1)
    q_pos0 = P + i * tq
    Lp = ki_ref.shape[1]
    ncol = jnp.minimum(L, ((q_pos0 + tq - 1) // CHUNK + 1) * CHUNK)
    nt = (ncol + tk - 1) // tk
    qpos = q_pos0 + lax.broadcasted_iota(jnp.int32, (tq, 1), 0)
    grp = B_HEADS // B_KV
    lane = lax.broadcasted_iota(jnp.int32, (tq, LANES), 1)
    wi = wi_ref[0] * (I_HEADS ** -0.5)

    def vis_of(t):
        kpos = t * tk + lax.broadcasted_iota(jnp.int32, (1, tk), 1)
        return ((kpos >> CHUNK_SHIFT) <= (qpos >> CHUNK_SHIFT)) & (kpos < L)

    def score_tile(t, carry):
        c0 = pl.multiple_of(t * tk, tk)
        kit = ki_ref[0, pl.ds(c0, tk), :]
        sc = jnp.zeros((tq, tk), F32)
        for pr in range(I_HEADS * I_HD // LANES):
            qpair = qi_ref[0, :, pr * LANES:(pr + 1) * LANES]
            for half in range(LANES // I_HD):
                h = pr * (LANES // I_HD) + half
                qh = jnp.where((lane >> I_HD_SHIFT) == half, qpair, jnp.zeros_like(qpair))
                isc = _dot_nt(qh, kit)
                sc = sc + jnp.maximum(isc, 0.0) * wi[:, h:h + 1]
        sc = jnp.where(vis_of(t), sc, -jnp.inf)
        bits = pltpu.bitcast(sc, jnp.int32)
        bits = jnp.where(bits == KEY_MIN, 0, bits)
        key_s[t] = bits ^ ((bits >> 31) & 0x7FFFFFFF)
        return carry

    lax.fori_loop(0, nt, score_tile, 0)

    def count(pred_fn):
        def body(t, acc):
            hit = jnp.where(pred_fn(key_s[t]), 1, 0)
            part = hit[:, 0:LANES]
            for c in range(1, tk // LANES):
                part = part + hit[:, c * LANES:(c + 1) * LANES]
            return acc + part
        acc = lax.fori_loop(0, nt, body, jnp.zeros((tq, LANES), jnp.int32))
        return jnp.sum(acc, axis=-1, keepdims=True)

    def select(_):
        def bit_body(bi, thr):
            cand = thr + jnp.left_shift(jnp.int32(1), 31 - bi)
            cnt = count(lambda kk: kk >= cand)
            return jnp.where(cnt >= topk, cand, thr)
        return lax.fori_loop(0, 32, bit_body, jnp.full((tq, 1), KEY_MIN, jnp.int32))

    thr = lax.cond(ncol > topk, select, lambda _: jnp.full((tq, 1), KEY_MIN, jnp.int32), 0)
    n_gt = count(lambda kk: kk > thr)
    n_eq = count(lambda kk: kk == thr)
    need = topk - n_gt
    neg_inf_key = jnp.int32(KEY_MIN + 0x7FFFFF)
    tie_break = jnp.max(jnp.where((n_eq > need) & (thr != neg_inf_key), 1.0, 0.0)) > 0.5

    def bias_fast(_):
        def body(t, c):
            sel = (key_s[t] >= thr) & vis_of(t)
            bias_s[t] = jnp.where(sel, 0.0, NEG)
            return c
        lax.fori_loop(0, nt, body, 0)
        return 0

    def bias_ties(_):
        r = lax.broadcasted_iota(jnp.int32, (LANES, LANES), 0)
        c = lax.broadcasted_iota(jnp.int32, (LANES, LANES), 1)
        before = jnp.where(r < c, 1.0, 0.0).astype(BF16)

        def body(t, seen):
            kk = key_s[t]
            vis = vis_of(t)
            for cb in range(tk // LANES):
                sl = slice(cb * LANES, (cb + 1) * LANES)
                eq = kk[:, sl] == thr
                eqf = jnp.where(eq, 1.0, 0.0)
                rank = seen + _dot(eqf.astype(BF16), before)
                sel = ((kk[:, sl] > thr) | (eq & (rank < need.astype(F32)))) & vis[:, sl]
                bias_s[t, :, sl] = jnp.where(sel, 0.0, NEG)
                seen = seen + jnp.sum(eqf, axis=-1, keepdims=True)
            return seen
        lax.fori_loop(0, nt, body, jnp.zeros((tq, 1), F32))
        return 0

    lax.cond(tie_break, bias_ties, bias_fast, 0)

    def fold(x, op):
        f = x[:, 0:LANES]
        for c in range(1, tk // LANES):
            f = op(f, x[:, c * LANES:(c + 1) * LANES])
        return f

    for n in range(B_KV):
        kv = slice(n * B_HD, (n + 1) * B_HD)

        def scores(t, mrun, kv=kv, n=n):
            c0 = pl.multiple_of(t * tk, tk)
            bias = bias_s[t]
            k = kb_ref[0, pl.ds(c0, tk), kv]
            out = []
            for g in range(grp):
                h = n * grp + g
                s = _dot_nt(qb_ref[0, :, h * B_HD:(h + 1) * B_HD], k) + bias
                s_s[g, t] = s
                out.append(jnp.maximum(mrun[g], fold(s, jnp.maximum)))
            return tuple(out)

        mrun = _tile_loop(nt, scores, tuple(jnp.full((tq, LANES), NEG, F32) for _ in range(grp)), 2)
        mrow = [jnp.max(m, axis=-1, keepdims=True) for m in mrun]

        def weigh(t, carry, kv=kv, mrow=mrow):
            lrun, acc = carry
            c0 = pl.multiple_of(t * tk, tk)
            v = vb_ref[0, pl.ds(c0, tk), kv]
            lnew, anew = [], []
            for g in range(grp):
                p = jnp.exp(s_s[g, t] - mrow[g])
                lnew.append(lrun[g] + fold(p, jnp.add))
                anew.append(acc[g] + _dot(p.astype(BF16), v))
            return tuple(lnew), tuple(anew)

        lrun, acc = _tile_loop(
            nt, weigh,
            (tuple(jnp.zeros((tq, LANES), F32) for _ in range(grp)),
             tuple(jnp.zeros((tq, B_HD), F32) for _ in range(grp))), 2)
        for g in range(grp):
            h = n * grp + g
            l = jnp.sum(lrun[g], axis=-1, keepdims=True)
            o_ref[0, :, h * B_HD:(h + 1) * B_HD] = (acc[g] / l).astype(o_ref.dtype)


def _dsa(qi, proj, qb, ki2, kb, vb, P, L, tk):
    B, T, _ = qb.shape
    Lp = ki2.shape[1]
    tq = min(T, 128)
    assert T % tq == 0 and Lp % tk == 0 and tk % LANES == 0
    topk = min(TOPK_MAX, L // 4)
    kern = functools.partial(_dsa_kernel, P=P, L=L, tq=tq, tk=tk, topk=topk)
    return pl.pallas_call(
        kern,
        grid=(B, T // tq),
        in_specs=[pl.BlockSpec((1, tq, I_HEADS * I_HD), lambda b, i: (b, i, 0)),
                  pl.BlockSpec((1, tq, LANES), lambda b, i: (b, i, OFF_SMALL // LANES)),
                  pl.BlockSpec((1, tq, B_HEADS * B_HD), lambda b, i: (b, i, 0)),
                  pl.BlockSpec((1, Lp, LANES), lambda b, i: (b, 0, 0)),
                  pl.BlockSpec((1, Lp, B_KV * B_HD), lambda b, i: (b, 0, 0)),
                  pl.BlockSpec((1, Lp, B_KV * B_HD), lambda b, i: (b, 0, 0))],
        out_specs=pl.BlockSpec((1, tq, B_HEADS * B_HD), lambda b, i: (b, i, 0)),
        out_shape=jax.ShapeDtypeStruct((B, T, B_HEADS * B_HD), BF16),
        scratch_shapes=[pltpu.VMEM((Lp // tk, tq, tk), jnp.int32), pltpu.VMEM((Lp // tk, tq, tk), F32),
                        pltpu.VMEM((B_HEADS // B_KV, Lp // tk, tq, tk), F32)],
        compiler_params=_cparams(("parallel", "arbitrary")),
        name="dsa",
    )(qi, proj, qb, ki2, kb, vb)


def _bmm(a, b):
    return jnp.einsum('bij,bjk->bik', a.astype(BF16), b.astype(BF16), preferred_element_type=F32)


def _bmm_nt(a, b):
    return jnp.einsum('bik,bjk->bij', a.astype(BF16), b.astype(BF16), preferred_element_type=F32)


def _gdn_intra_kernel(x_ref, halo_ref, sm_ref, cw_ref, cbuf_ref, alog_ref, dtb_ref,
                      u_o, w_o, qt_o, ktT_o, qkg_o, dec_o, buf_s, *, cc, nck):
    i = pl.program_id(1)
    R = nck * cc
    HW = C_HEADS * C_HD

    @pl.when(i == 0)
    def _():
        buf_s[0:SUBLANES, :] = jnp.zeros((SUBLANES, C_QKV), F32)
        buf_s[SUBLANES - (C_CONV - 1):SUBLANES, :] = cbuf_ref[0]

    @pl.when(i > 0)
    def _():
        buf_s[0:SUBLANES, :] = halo_ref[0]

    buf_s[SUBLANES:SUBLANES + R, :] = x_ref[0]
    y = None
    for jw in range(C_CONV):
        off = SUBLANES - (C_CONV - 1) + jw
        term = buf_s[off:off + R, :] * cw_ref[jw:jw + 1, :]
        y = term if y is None else y + term
    y = y * jax.nn.sigmoid(y)

    sm = sm_ref[0]
    xg = sm + dtb_ref[...]
    softplus = jnp.maximum(xg, 0.0) + jnp.log(1.0 + jnp.exp(-jnp.abs(xg)))
    g = -jnp.exp(alog_ref[...]) * softplus
    beta = jax.nn.sigmoid(sm)

    cshift = cc.bit_length() - 1
    rr = lax.broadcasted_iota(jnp.int32, (R, R), 0)
    rc = lax.broadcasted_iota(jnp.int32, (R, R), 1)
    cum = ((rr >> cshift) == (rc >> cshift)) & (rr >= rc)
    G = jnp.dot(jnp.where(cum, 1.0, 0.0), g, preferred_element_type=F32,
                precision=lax.Precision.HIGHEST)

    qs, ks, vs, gcs, bcs = [], [], [], [], []
    for ck in range(nck):
        rows = slice(ck * cc, (ck + 1) * cc)
        for h in range(C_HEADS):
            qs.append(y[rows, h * C_HD:(h + 1) * C_HD])
            ks.append(y[rows, HW + h * C_HD:HW + (h + 1) * C_HD])
            vs.append(y[rows, 2 * HW + h * C_HD:2 * HW + (h + 1) * C_HD])
            gcs.append(G[rows, CA_LANE + h:CA_LANE + h + 1])
            bcs.append(beta[rows, CB_LANE + h:CB_LANE + h + 1])
    q3 = jnp.stack(qs)
    k3 = jnp.stack(ks)
    v3 = jnp.stack(vs)
    Gc = jnp.stack(gcs)
    bc = jnp.stack(bcs)
    q3 = q3 * lax.rsqrt(jnp.sum(q3 * q3, axis=-1, keepdims=True) + EPS) * (C_HD ** -0.5)
    k3 = k3 * lax.rsqrt(jnp.sum(k3 * k3, axis=-1, keepdims=True) + EPS)

    ri = lax.broadcasted_iota(jnp.int32, (cc, cc), 0)
    ci = lax.broadcasted_iota(jnp.int32, (cc, cc), 1)
    incl = (ri >= ci)[None]
    strict = (ri > ci)[None]
    eye = (ri == ci)[None]
    bdiag = ((ri // SOLVE_BLOCK) == (ci // SOLVE_BLOCK))[None]

    Gr = jnp.sum(jnp.where(eye, Gc, 0.0), axis=1, keepdims=True)
    gam = jnp.where(incl, jnp.exp(jnp.where(incl, Gc - Gr, 0.0)), 0.0)
    k16 = k3.astype(BF16)
    A = jnp.where(strict, bc * _bmm_nt(k16, k16) * gam, 0.0)
    Nd = jnp.where(bdiag, -A, 0.0)
    E = jnp.where(bdiag, 0.0, A)
    N2 = _bmm(Nd, Nd)
    N4 = _bmm(N2, N2)
    N8 = _bmm(N4, N4)
    Q = Nd + N2 + _bmm(Nd, N2)
    Q = Q + N4 + _bmm(Q, N4)
    Q = Q + N8 + _bmm(Q, N8)
    Fm = E + _bmm(Q, E)
    F2 = _bmm(Fm, Fm)
    Rm = F2 - Fm - _bmm(Fm, F2)
    Wm = Rm + Q + _bmm(Rm, Q)
    eG = jnp.exp(Gc)
    rhs = jnp.concatenate([bc * v3, (bc * eG) * k3], axis=-1)
    sol = rhs + _bmm(Wm, rhs)
    qkg = _bmm_nt(q3, k16) * gam
    gl = Gc[:, cc - 1:cc, :]
    qt = q3 * eG
    kt = k3 * jnp.exp(gl - Gc)
    dec = jnp.exp(gl)

    for ck in range(nck):
        rows = slice(ck * cc, (ck + 1) * cc)
        for h in range(C_HEADS):
            b = ck * C_HEADS + h
            cols = slice(h * C_HD, (h + 1) * C_HD)
            u_o[0, rows, cols] = sol[b, :, 0:C_HD]
            w_o[0, rows, cols] = sol[b, :, C_HD:2 * C_HD].astype(BF16)
            qt_o[0, rows, cols] = qt[b].astype(BF16)
            ktT_o[0, ck, h] = kt[b].T.astype(BF16)
            qkg_o[0, ck, h] = qkg[b].astype(BF16)
            dec_o[0, ck, h] = jnp.broadcast_to(dec[b], (1, C_HD))


def _gdn_inter_kernel(u_ref, w_ref, qt_ref, ktT_ref, qkg_ref, dec_ref, z_ref, s0_ref, gn_ref,
                      o_ref, sfin_ref, st_s, *, cc, nck):
    c = pl.program_id(1)

    @pl.when(c == 0)
    def _():
        st_s[...] = s0_ref[0]

    S = st_s[...]
    for ck in range(nck):
        rows = slice(ck * cc, (ck + 1) * cc)

        def heads(ref):
            return jnp.stack([ref[0, rows, h * C_HD:(h + 1) * C_HD] for h in range(C_HEADS)])

        S16 = S.astype(BF16)
        v_new = heads(u_ref) - _bmm(heads(w_ref), S16)
        vn16 = v_new.astype(BF16)
        o = _bmm(heads(qt_ref), S16) + _bmm(qkg_ref[0, ck], vn16)
        S = dec_ref[0, ck] * S + _bmm(ktT_ref[0, ck], vn16)
        on = o * lax.rsqrt(jnp.mean(o * o, axis=-1, keepdims=True) + EPS) * gn_ref[...]
        for h in range(C_HEADS):
            cols = slice(h * C_HD, (h + 1) * C_HD)
            z = z_ref[0, rows, cols]
            o_ref[0, rows, cols] = (on[h] * (z * jax.nn.sigmoid(z))).astype(o_ref.dtype)
    st_s[...] = S

    @pl.when(c == pl.num_programs(1) - 1)
    def _():
        sfin_ref[0] = S


def _gdn(proj, conv_w, cbuf, S0, a_log, dt_bias, out_norm):
    B, T, _ = proj.shape
    cc = min(CHUNK, T)
    assert T % cc == 0 and cc % SOLVE_BLOCK == 0 and cc // SOLVE_BLOCK <= 4 and cc >= SUBLANES
    assert cc & (cc - 1) == 0
    HW = C_HEADS * C_HD
    NC = T // cc
    nck_a = 2 if NC % 2 == 0 else 1
    nck_b = 4 if NC % 4 == 0 else 1

    def at_ca(p):
        return jnp.zeros((1, LANES), F32).at[0, CA_LANE:CA_LANE + C_HEADS].set(p.astype(F32))

    Ra = nck_a * cc
    hb = Ra // SUBLANES
    u, w, qt, ktT, qkg, dec = pl.pallas_call(
        functools.partial(_gdn_intra_kernel, cc=cc, nck=nck_a),
        grid=(B, NC // nck_a),
        in_specs=[pl.BlockSpec((1, Ra, C_QKV), lambda b, i: (b, i, OFF_CQKV // C_QKV)),
                  pl.BlockSpec((1, SUBLANES, C_QKV),
                               lambda b, i: (b, jnp.maximum(i * hb - 1, 0), OFF_CQKV // C_QKV)),
                  pl.BlockSpec((1, Ra, LANES), lambda b, i: (b, i, OFF_SMALL // LANES)),
                  pl.BlockSpec((C_CONV, C_QKV), lambda b, i: (0, 0)),
                  pl.BlockSpec((1, C_CONV - 1, C_QKV), lambda b, i: (b, 0, 0)),
                  pl.BlockSpec((1, LANES), lambda b, i: (0, 0)),
                  pl.BlockSpec((1, LANES), lambda b, i: (0, 0))],
        out_specs=[pl.BlockSpec((1, Ra, HW), lambda b, i: (b, i, 0)),
                   pl.BlockSpec((1, Ra, HW), lambda b, i: (b, i, 0)),
                   pl.BlockSpec((1, Ra, HW), lambda b, i: (b, i, 0)),
                   pl.BlockSpec((1, nck_a, C_HEADS, C_HD, cc), lambda b, i: (b, i, 0, 0, 0)),
                   pl.BlockSpec((1, nck_a, C_HEADS, cc, cc), lambda b, i: (b, i, 0, 0, 0)),
                   pl.BlockSpec((1, nck_a, C_HEADS, 1, C_HD), lambda b, i: (b, i, 0, 0, 0))],
        out_shape=[jax.ShapeDtypeStruct((B, T, HW), F32),
                   jax.ShapeDtypeStruct((B, T, HW), BF16),
                   jax.ShapeDtypeStruct((B, T, HW), BF16),
                   jax.ShapeDtypeStruct((B, NC, C_HEADS, C_HD, cc), BF16),
                   jax.ShapeDtypeStruct((B, NC, C_HEADS, cc, cc), BF16),
                   jax.ShapeDtypeStruct((B, NC, C_HEADS, 1, C_HD), F32)],
        scratch_shapes=[pltpu.VMEM((Ra + SUBLANES, C_QKV), F32)],
        compiler_params=_cparams(("parallel", "parallel")),
        name="gdn_intra",
    )(proj, proj, proj, conv_w, cbuf, at_ca(a_log), at_ca(dt_bias))

    Rb = nck_b * cc
    return pl.pallas_call(
        functools.partial(_gdn_inter_kernel, cc=cc, nck=nck_b),
        grid=(B, NC // nck_b),
        in_specs=[pl.BlockSpec((1, Rb, HW), lambda b, c: (b, c, 0)),
                  pl.BlockSpec((1, Rb, HW), lambda b, c: (b, c, 0)),
                  pl.BlockSpec((1, Rb, HW), lambda b, c: (b, c, 0)),
                  pl.BlockSpec((1, nck_b, C_HEADS, C_HD, cc), lambda b, c: (b, c, 0, 0, 0)),
                  pl.BlockSpec((1, nck_b, C_HEADS, cc, cc), lambda b, c: (b, c, 0, 0, 0)),
                  pl.BlockSpec((1, nck_b, C_HEADS, 1, C_HD), lambda b, c: (b, c, 0, 0, 0)),
                  pl.BlockSpec((1, Rb, HW), lambda b, c: (b, c, OFF_CZ // HW)),
                  pl.BlockSpec((1, C_HEADS, C_HD, C_HD), lambda b, c: (b, 0, 0, 0)),
                  pl.BlockSpec((1, C_HD), lambda b, c: (0, 0))],
        out_specs=[pl.BlockSpec((1, Rb, HW), lambda b, c: (b, c, 0)),
                   pl.BlockSpec((1, C_HEADS, C_HD, C_HD), lambda b, c: (b, 0, 0, 0))],
        out_shape=[jax.ShapeDtypeStruct((B, T, HW), BF16),
                   jax.ShapeDtypeStruct((B, C_HEADS, C_HD, C_HD), F32)],
        scratch_shapes=[pltpu.VMEM((C_HEADS, C_HD, C_HD), F32)],
        compiler_params=_cparams(("parallel", "arbitrary")),
        name="gdn_inter",
    )(u, w, qt, ktT, qkg, dec, proj, S0, out_norm.reshape(1, C_HD))


def _ffn_act_kernel(ug_ref, hg_ref, uu_ref, hu_ref, fg_ref, fu_ref, wg_ref, wu_ref, o_ref, bg_s, bu_s):
    i = pl.program_id(1)
    tt = ug_ref.shape[1]

    def conv(u_ref, h_ref, f_ref, w_ref, buf):
        @pl.when(i == 0)
        def _():
            buf[0:SUBLANES, :] = jnp.zeros((SUBLANES, buf.shape[1]), F32)
            buf[SUBLANES - (FFN_CONV - 1):SUBLANES, :] = f_ref[0]

        @pl.when(i > 0)
        def _():
            buf[0:SUBLANES, :] = h_ref[0]

        buf[SUBLANES:SUBLANES + tt, :] = u_ref[0]
        y = None
        for jw in range(FFN_CONV):
            off = SUBLANES - (FFN_CONV - 1) + jw
            term = buf[off:off + tt, :] * w_ref[jw:jw + 1, :]
            y = term if y is None else y + term
        return y

    yg = conv(ug_ref, hg_ref, fg_ref, wg_ref, bg_s)
    yu = conv(uu_ref, hu_ref, fu_ref, wu_ref, bu_s)
    o_ref[0] = (yg * jax.nn.sigmoid(yg) * yu).astype(o_ref.dtype)


def _ffn_act(u, fbuf, conv_w):
    B, T, F2 = u.shape
    FF = F2 // 2
    tt = min(T, 512)
    tc = 512
    assert T % tt == 0 and FF % tc == 0 and tt % SUBLANES == 0
    nco = FF // tc
    hb = tt // SUBLANES

    def main(off):
        return pl.BlockSpec((1, tt, tc), lambda b, i, c: (b, i, off + c))

    def halo(off):
        return pl.BlockSpec((1, SUBLANES, tc), lambda b, i, c: (b, jnp.maximum(i * hb - 1, 0), off + c))

    def fb(off):
        return pl.BlockSpec((1, FFN_CONV - 1, tc), lambda b, i, c: (b, 0, off + c))

    def wspec(off):
        return pl.BlockSpec((FFN_CONV, tc), lambda b, i, c: (0, off + c))

    return pl.pallas_call(
        _ffn_act_kernel,
        grid=(B, T // tt, nco),
        in_specs=[main(0), halo(0), main(nco), halo(nco), fb(0), fb(nco), wspec(0), wspec(nco)],
        out_specs=pl.BlockSpec((1, tt, tc), lambda b, i, c: (b, i, c)),
        out_shape=jax.ShapeDtypeStruct((B, T, FF), BF16),
        scratch_shapes=[pltpu.VMEM((tt + SUBLANES, tc), F32), pltpu.VMEM((tt + SUBLANES, tc), F32)],
        compiler_params=_cparams(("parallel", "parallel", "parallel")),
        name="ffn_act",
    )(u, u, u, u, fbuf, fbuf, conv_w, conv_w)


def _prep_w_in(w_in):
    D = w_in.shape[0]
    sizes = (1024, 1024, 1024, 1024, 256, 256, 512, 64, 8, C_QKV, 8, 8, 1024, N_BRANCH * D)
    offs = np.concatenate([[0], np.cumsum(sizes)])
    (aq, ak, av, bq, bk, bv, bqi, bki, bwi, cqkv, ca, cb, cz, gt) = [
        w_in[:, int(offs[k]):int(offs[k + 1])] for k in range(len(sizes))]

    def z(n):
        return jnp.zeros((D, n), w_in.dtype)

    cols = [cqkv, cz, aq, ak, av, bq, bk, bv, bqi, bki, z(LANES - I_HD), bwi, ca, cb, z(LANES - 24),
            z(OFF_GT - OFF_SMALL - LANES), gt]
    w = jnp.concatenate(cols, axis=1).astype(BF16)
    assert w.shape[1] == N_PROJ
    return w


def _pad_keys(past, new, Lp):
    B, P, W = past.shape
    T = new.shape[1]
    parts = [past.astype(BF16), new]
    if Lp > P + T:
        parts.append(jnp.zeros((B, Lp - P - T, W), BF16))
    return jnp.concatenate(parts, axis=1)


def _layer(x, mod, past, lw, layer_idx):
    kA_p, vA_p, kB_p, vB_p, kI_p, cbuf, S0, fbuf = past
    B, T, D = x.shape
    P = 0 if kA_p is None else kA_p.shape[1]
    L = P + T
    sh1, sc1, g1, sh2, sc2, g2 = [m.reshape(B, 1, D) for m in jnp.split(mod, 6, axis=-1)]

    proj = _mm_norm(x, lw['norm_mix'], sh1, sc1, lw['w_in'])
    rope = _rope_table(P + jnp.arange(T, dtype=jnp.int32))
    (qa, ka, kab, va, vab, qb, kb, kbb, vb, vbb, qi, ki, ki2) = _prep(
        proj, rope, lw['a_q_norm'], lw['a_k_norm'], lw['b_q_norm'], lw['b_k_norm'])

    lam_init = 0.8 - 0.6 * math.exp(-0.3 * layer_idx)
    if P:
        kp = kA_p.reshape(B, P, -1)
        vp = vA_p.reshape(B, P, -1)
    else:
        kp = vp = None
    oa = _diff_attn(qa, kab, vab, kp, vp, lw['a_lambda'], lw['a_subln'], P, lam_init)

    if P:
        Lp = -(-L // LANES) * LANES
        tk = Lp
        ki2_all = _pad_keys(jnp.tile(kI_p, (1, 1, LANES // I_HD)), ki2, Lp)
        kb_all = _pad_keys(kB_p.reshape(B, P, -1), kbb, Lp)
        vb_all = _pad_keys(vB_p.reshape(B, P, -1), vbb, Lp)
    else:
        tk = min(T, 512)
        ki2_all, kb_all, vb_all = ki2, kbb, vbb
    ob = _dsa(qi, proj, qb, ki2_all, kb_all, vb_all, P, L, tk)

    oc, S_fin = _gdn(proj, lw['c_conv'], cbuf, S0, lw['c_a_log'], lw['c_dt_bias'], lw['c_out_norm'])
    assert T >= C_CONV - 1 and T >= FFN_CONV - 1
    cbuf_new = proj[:, T - (C_CONV - 1):, OFF_CQKV:OFF_CQKV + C_QKV]

    mixed = _merge(oa, ob, oc, lw['w_branch'], proj, D)
    x = _mm_res(mixed, lw['w_out'], x, g1)

    u = _mm_norm(x, lw['norm_ffn'], sh2, sc2, lw['w_up'])
    act = _ffn_act(u, fbuf, lw['ffn_conv'])
    fbuf_new = u[:, T - (FFN_CONV - 1):, :]
    x = _mm_res(act, lw['w_down'], x, g2, row_target=512)

    new_state = (ka.reshape(B, T, A_HEADS, 2, A_HD), va.reshape(B, T, A_HEADS, A_VD),
                 kb.reshape(B, T, B_KV, B_HD), vb.reshape(B, T, B_KV, B_HD), ki,
                 cbuf_new, S_fin, fbuf_new)
    return x, new_state


def kernel(x_prompt, x_sample, c_prompt, c_sample, cache_diff_k, cache_diff_v, cache_dsa_k, cache_dsa_v, cache_dsa_kidx, state_gdn_conv, state_gdn, state_ffn_conv, w_ada, b_ada, norm_mix, w_in, a_q_norm, a_k_norm, a_lambda, a_subln, b_q_norm, b_k_norm, c_conv, c_a_log, c_dt_bias, c_out_norm, w_branch, w_out, norm_ffn, w_up, ffn_conv, w_down):
    depth = w_in.shape[0]
    bp, bs = x_prompt.shape[0], x_sample.shape[0]
    d_ff2 = w_up.shape[2]
    dt_ = x_prompt.dtype
    prompt_past = (None, None, None, None, None, jnp.zeros((bp, C_CONV - 1, C_QKV), dt_),
                   jnp.zeros((bp, C_HEADS, C_HD, C_HD), dt_), jnp.zeros((bp, FFN_CONV - 1, d_ff2), dt_))
    nrow = -(-(bp + bs) // SUBLANES) * SUBLANES
    c_all = jnp.concatenate([c_prompt, c_sample, jnp.zeros((nrow - bp - bs, c_prompt.shape[1]), dt_)], axis=0)
    xp, xs = x_prompt, x_sample
    prompt_states, sample_states = [], []
    for l in range(depth):
        lw = dict(norm_mix=norm_mix[l], w_in=_prep_w_in(w_in[l]),
                  a_q_norm=a_q_norm[l], a_k_norm=a_k_norm[l], a_lambda=a_lambda[l], a_subln=a_subln[l],
                  b_q_norm=b_q_norm[l], b_k_norm=b_k_norm[l], c_conv=c_conv[l], c_a_log=c_a_log[l],
                  c_dt_bias=c_dt_bias[l], c_out_norm=c_out_norm[l], w_branch=w_branch[l].astype(BF16),
                  w_out=w_out[l].astype(BF16), norm_ffn=norm_ffn[l], w_up=w_up[l].astype(BF16),
                  ffn_conv=ffn_conv[l], w_down=w_down[l].astype(BF16))
        mod = _ada(c_all, w_ada[l], b_ada[l])
        xp, st_p = _layer(xp, mod[:bp], prompt_past, lw, l)
        sample_past = (cache_diff_k[l], cache_diff_v[l], cache_dsa_k[l], cache_dsa_v[l], cache_dsa_kidx[l],
                       state_gdn_conv[l], state_gdn[l], state_ffn_conv[l])
        xs, st_s = _layer(xs, mod[bp:bp + bs], sample_past, lw, l)
        prompt_states.append(st_p)
        sample_states.append(st_s)
    p_out = [jnp.stack(s, axis=0) for s in zip(*prompt_states)]
    s_out = [jnp.stack(s, axis=0) for s in zip(*sample_states)]
    return (xp, xs, *p_out, *s_out)
```

```python
import functools
import math

import jax
import jax.numpy as jnp
import numpy as np
from jax import lax
from jax.experimental import pallas as pl
from jax.experimental.pallas import tpu as pltpu

F32 = jnp.float32
BF16 = jnp.bfloat16

CHUNK = 64
CHUNK_SHIFT = 6
ROPE_THETA = 10000.0
EPS = 1e-6
A_HEADS, A_HD, A_VD = 4, 128, 256
B_HEADS, B_KV, B_HD = 8, 2, 128
I_HEADS, I_HD = 8, 64
I_HD_SHIFT = 6
TOPK_MAX = 256
C_HEADS, C_HD, C_CONV = 8, 128, 4
C_QKV = 3 * C_HEADS * C_HD
BRANCH_W = 1024
N_BRANCH = 3
FFN_CONV = 3

LANES = 128
SUBLANES = 8
VMEM_LIMIT = 56 * 1024 * 1024
ROW_TILE = 1024
COL_TILE = 512
SOLVE_BLOCK = 16
TILE_UNROLL = 4
NEG = -1e30
KEY_MIN = -(2 ** 31)

OFF_CQKV = 0
OFF_CZ = 3072
OFF_AQ = 4096
OFF_AK = 5120
OFF_AV = 6144
OFF_BQ = 7168
OFF_BK = 8192
OFF_BV = 8448
OFF_BQI = 8704
OFF_BKI = 9216
OFF_SMALL = 9344
CA_LANE = 8
CB_LANE = 16
OFF_GT = 9728
N_PROJ = 15872


def _cparams(sem):
    return pltpu.CompilerParams(dimension_semantics=sem, vmem_limit_bytes=VMEM_LIMIT)


def _row_tiles(B, T, target=ROW_TILE):
    tt = min(T, target)
    assert T % tt == 0 and tt % SUBLANES == 0
    bb = max(1, min(B, target // tt))
    while B % bb:
        bb -= 1
    return bb, tt


def _dot(a, b):
    return jnp.dot(a, b, preferred_element_type=F32)


def _dot_nt(a, b):
    return lax.dot_general(a, b, (((1,), (1,)), ((), ())), preferred_element_type=F32)


def _bdot(a, b):
    return _dot(a.astype(BF16), b.astype(BF16))


def _tile_loop(n, body, init, unroll):
    nb = n // unroll

    def trip(tb, c):
        for u in range(unroll):
            c = body(tb * unroll + u, c)
        return c

    c = lax.fori_loop(0, nb, trip, init)
    return lax.fori_loop(nb * unroll, n, body, c)


def _ada_kernel(c_ref, w_ref, b_ref, o_ref):
    c = c_ref[...]
    s = (c * jax.nn.sigmoid(c)).astype(BF16)
    o_ref[...] = _dot(s, w_ref[...].astype(BF16)) + b_ref[...]


def _ada(c, w, b):
    M, D = c.shape
    N = w.shape[1]
    tn = 1024
    return pl.pallas_call(
        _ada_kernel,
        grid=(N // tn,),
        in_specs=[pl.BlockSpec((M, D), lambda j: (0, 0)),
                  pl.BlockSpec((D, tn), lambda j: (0, j)),
                  pl.BlockSpec((1, tn), lambda j: (0, j))],
        out_specs=pl.BlockSpec((M, tn), lambda j: (0, j)),
        out_shape=jax.ShapeDtypeStruct((M, N), F32),
        compiler_params=_cparams(("parallel",)),
        name="ada",
    )(c, w, b.reshape(1, N))


def _mm_norm_kernel(x_ref, g_ref, sh_ref, sc_ref, w_ref, o_ref, h_ref):
    bb, tt, D = x_ref.shape
    tn = w_ref.shape[1]

    @pl.when(pl.program_id(2) == 0)
    def _():
        x = x_ref[...]
        y = x * lax.rsqrt(jnp.mean(x * x, axis=-1, keepdims=True) + EPS) * g_ref[...]
        h = y * (1.0 + sc_ref[...]) + sh_ref[...]
        h_ref[...] = h.reshape(bb * tt, D).astype(BF16)

    acc = _dot(h_ref[...], w_ref[...])
    o_ref[...] = acc.reshape(bb, tt, tn).astype(o_ref.dtype)


def _mm_norm(x, g, shift, scale, w, out_dtype=F32):
    B, T, D = x.shape
    N = w.shape[1]
    bb, tt = _row_tiles(B, T)
    tn = COL_TILE
    assert N % tn == 0
    return pl.pallas_call(
        _mm_norm_kernel,
        grid=(B // bb, T // tt, N // tn),
        in_specs=[pl.BlockSpec((bb, tt, D), lambda b, i, j: (b, i, 0)),
                  pl.BlockSpec((1, 1, D), lambda b, i, j: (0, 0, 0)),
                  pl.BlockSpec((bb, 1, D), lambda b, i, j: (b, 0, 0)),
                  pl.BlockSpec((bb, 1, D), lambda b, i, j: (b, 0, 0)),
                  pl.BlockSpec((D, tn), lambda b, i, j: (0, j))],
        out_specs=pl.BlockSpec((bb, tt, tn), lambda b, i, j: (b, i, j)),
        out_shape=jax.ShapeDtypeStruct((B, T, N), out_dtype),
        scratch_shapes=[pltpu.VMEM((bb * tt, D), BF16)],
        compiler_params=_cparams(("parallel", "parallel", "arbitrary")),
        name="mm_norm",
    )(x, g.reshape(1, 1, D), shift, scale, w)


def _mm_res_kernel(a_ref, w_ref, res_ref, g_ref, o_ref):
    bb, tt, K = a_ref.shape
    tn = w_ref.shape[1]
    acc = _dot(a_ref[...].reshape(bb * tt, K), w_ref[...])
    o_ref[...] = res_ref[...] + g_ref[...] * acc.reshape(bb, tt, tn)


def _mm_res(a, w, res, gate, row_target=ROW_TILE):
    B, T, K = a.shape
    N = w.shape[1]
    bb, tt = _row_tiles(B, T, row_target)
    tn = COL_TILE
    assert N % tn == 0
    return pl.pallas_call(
        _mm_res_kernel,
        grid=(B // bb, T // tt, N // tn),
        in_specs=[pl.BlockSpec((bb, tt, K), lambda b, i, j: (b, i, 0)),
                  pl.BlockSpec((K, tn), lambda b, i, j: (0, j)),
                  pl.BlockSpec((bb, tt, tn), lambda b, i, j: (b, i, j)),
                  pl.BlockSpec((bb, 1, tn), lambda b, i, j: (b, 0, j))],
        out_specs=pl.BlockSpec((bb, tt, tn), lambda b, i, j: (b, i, j)),
        out_shape=jax.ShapeDtypeStruct((B, T, N), F32),
        compiler_params=_cparams(("parallel", "parallel", "arbitrary")),
        name="mm_res",
    )(a, w, res, gate)


def _merge_kernel(b0_ref, b1_ref, b2_ref, w_ref, g0_ref, g1_ref, g2_ref, o_ref):
    bb, tt, W = b0_ref.shape
    tn = w_ref.shape[2]
    acc = None
    for n, (br, gr) in enumerate(((b0_ref, g0_ref), (b1_ref, g1_ref), (b2_ref, g2_ref))):
        y = _dot(br[...].reshape(bb * tt, W), w_ref[n])
        t = jax.nn.sigmoid(gr[...].reshape(bb * tt, tn)) * y
        acc = t if acc is None else acc + t
    o_ref[...] = acc.reshape(bb, tt, tn).astype(o_ref.dtype)


def _merge(oa, ob, oc, w_branch, proj, D):
    B, T, W = oa.shape
    bb, tt = _row_tiles(B, T)
    tn = COL_TILE
    gblk = OFF_GT // tn
    nper = D // tn
    br_spec = pl.BlockSpec((bb, tt, W), lambda b, i, j: (b, i, 0))

    def gate_spec(n):
        return pl.BlockSpec((bb, tt, tn), lambda b, i, j: (b, i, gblk + n * nper + j))

    return pl.pallas_call(
        _merge_kernel,
        grid=(B // bb, T // tt, D // tn),
        in_specs=[br_spec, br_spec, br_spec,
                  pl.BlockSpec((N_BRANCH, W, tn), lambda b, i, j: (0, 0, j)),
                  gate_spec(0), gate_spec(1), gate_spec(2)],
        out_specs=pl.BlockSpec((bb, tt, tn), lambda b, i, j: (b, i, j)),
        out_shape=jax.ShapeDtypeStruct((B, T, D), BF16),
        compiler_params=_cparams(("parallel", "parallel", "arbitrary")),
        name="merge",
    )(oa, ob, oc, w_branch, proj, proj, proj)


def _prep_kernel(aqk_ref, av_ref, bq_ref, bkv_ref, bqi_ref, bki_ref, rope_ref,
                 gaq_ref, gak_ref, gbq_ref, gbk_ref,
                 qa_o, ka_o, kab_o, va_o, vab_o, qb_o, kb_o, kbb_o, vb_o, vbb_o,
                 qi_o, ki_o, ki2_o):
    c1 = rope_ref[:, 0:LANES]
    s1 = rope_ref[:, LANES:2 * LANES]
    c2 = rope_ref[:, 2 * LANES:3 * LANES]
    s2 = rope_ref[:, 3 * LANES:4 * LANES]
    tt = c1.shape[0]
    lane = lax.broadcasted_iota(jnp.int32, (tt, LANES), 1)
    low_half = (lane & (I_HD - 1)) < (I_HD // 2)

    def norm_rope(x, g):
        y = x * lax.rsqrt(jnp.mean(x * x, axis=-1, keepdims=True) + EPS) * g
        return y * c1 + pltpu.roll(y, A_HD // 2, 1) * s1

    def rope64(x):
        r = jnp.where(low_half, pltpu.roll(x, LANES - I_HD // 2, 1), pltpu.roll(x, I_HD // 2, 1))
        return x * c2 + r * s2

    gaq, gak, gbq, gbk = gaq_ref[...], gak_ref[...], gbq_ref[...], gbk_ref[...]
    for hm in range(2 * A_HEADS):
        sl = slice(hm * LANES, (hm + 1) * LANES)
        q = norm_rope(aqk_ref[0, :, sl], gaq) * (A_HD ** -0.5)
        qa_o[0, :, sl] = q.astype(BF16)
        k = norm_rope(aqk_ref[0, :, 2 * A_HEADS * LANES + hm * LANES:2 * A_HEADS * LANES + (hm + 1) * LANES], gak)
        ka_o[0, :, sl] = k
        kab_o[0, :, sl] = k.astype(BF16)
    va = av_ref[0]
    va_o[0] = va
    vab_o[0] = va.astype(BF16)
    for h in range(B_HEADS):
        sl = slice(h * LANES, (h + 1) * LANES)
        q = norm_rope(bq_ref[0, :, sl], gbq) * (B_HD ** -0.5)
        qb_o[0, :, sl] = q.astype(BF16)
    for n in range(B_KV):
        sl = slice(n * LANES, (n + 1) * LANES)
        k = norm_rope(bkv_ref[0, :, sl], gbk)
        kb_o[0, :, sl] = k
        kbb_o[0, :, sl] = k.astype(BF16)
    vb = bkv_ref[0, :, B_KV * LANES:2 * B_KV * LANES]
    vb_o[0] = vb
    vbb_o[0] = vb.astype(BF16)
    for p in range(I_HEADS * I_HD // LANES):
        sl = slice(p * LANES, (p + 1) * LANES)
        qi_o[0, :, sl] = (rope64(bqi_ref[0, :, sl]) * (I_HD ** -0.5)).astype(BF16)
    ki = rope64(bki_ref[0, :, 0:LANES])
    ki_o[0] = ki[:, 0:I_HD]
    ki2_o[0] = (ki + pltpu.roll(ki, I_HD, 1)).astype(BF16)


def _rope_table(pos):
    def tab(half):
        inv = jnp.power(ROPE_THETA, -jnp.arange(half, dtype=F32) / half)
        ang = pos.astype(F32)[:, None] * inv[None, :]
        return jnp.cos(ang), jnp.sin(ang)

    c, s = tab(A_HD // 2)
    ci, si = tab(I_HD // 2)
    return jnp.concatenate([c, c, -s, s, ci, ci, ci, ci, -si, si, -si, si], axis=1)


def _prep(proj, rope, gaq, gak, gbq, gbk):
    B, T, _ = proj.shape
    tt = min(T, 256)
    assert T % tt == 0

    def pspec(width, off):
        assert off % width == 0
        return pl.BlockSpec((1, tt, width), lambda b, i: (b, i, off // width))

    def ospec(width):
        return pl.BlockSpec((1, tt, width), lambda b, i: (b, i, 0))

    gspec = pl.BlockSpec((1, LANES), lambda b, i: (0, 0))
    outs = [(1024, BF16), (1024, F32), (1024, BF16), (1024, F32), (1024, BF16),
            (1024, BF16), (256, F32), (256, BF16), (256, F32), (256, BF16),
            (512, BF16), (I_HD, F32), (LANES, BF16)]
    return pl.pallas_call(
        _prep_kernel,
        grid=(B, T // tt),
        in_specs=[pspec(2048, OFF_AQ), pspec(1024, OFF_AV), pspec(1024, OFF_BQ), pspec(512, OFF_BK),
                  pspec(512, OFF_BQI), pspec(256, OFF_BKI),
                  pl.BlockSpec((tt, 4 * LANES), lambda b, i: (i, 0)),
                  gspec, gspec, gspec, gspec],
        out_specs=[ospec(w) for w, _ in outs],
        out_shape=[jax.ShapeDtypeStruct((B, T, w), d) for w, d in outs],
        compiler_params=_cparams(("parallel", "parallel")),
        name="prep",
    )(proj, proj, proj, proj, proj, proj, rope,
      gaq.reshape(1, LANES), gak.reshape(1, LANES), gbq.reshape(1, LANES), gbk.reshape(1, LANES))


def _diff_kernel(q_ref, kn_ref, vn_ref, lam_ref, sub_ref, o_ref, acc0_s, acc1_s, sn_s, *, tq, lam_init):
    acc = (acc0_s, acc1_s)
    i = pl.program_id(2)
    q_pos0 = i * tq
    q = [q_ref[0, :, m * A_HD:(m + 1) * A_HD] for m in range(2)]
    msl = [slice(m * A_HD, (m + 1) * A_HD) for m in range(2)]
    wide = tq % LANES == 0

    def fold(x, op):
        f = x[:, 0:LANES]
        for c in range(1, x.shape[1] // LANES):
            f = op(f, x[:, c * LANES:(c + 1) * LANES])
        return f

    neg = tuple(jnp.full((tq, LANES), NEG, F32) for _ in range(2))
    zero = tuple(jnp.zeros((tq, LANES), F32) for _ in range(2))

    def scores(t, mr):
        r0 = pl.multiple_of(t * tq, tq)
        k = kn_ref[0, pl.ds(r0, tq), :]
        out = []
        for m in range(2):
            s = _dot_nt(q[m], k[:, msl[m]])
            sn_s[m, t] = s
            out.append(jnp.maximum(mr[m], fold(s, jnp.maximum)) if wide else mr[m])
        return tuple(out)
    mn = _tile_loop(i, scores, neg, TILE_UNROLL)

    r0 = pl.multiple_of(i * tq, tq)
    qpos = q_pos0 + lax.broadcasted_iota(jnp.int32, (tq, 1), 0)
    kpos = q_pos0 + lax.broadcasted_iota(jnp.int32, (1, tq), 1)
    vis = (kpos >> CHUNK_SHIFT) <= (qpos >> CHUNK_SHIFT)
    kd = kn_ref[0, pl.ds(r0, tq), :]
    mrow = []
    for m in range(2):
        s = jnp.where(vis, _dot_nt(q[m], kd[:, msl[m]]), NEG)
        sn_s[m, i] = s
        mx = jnp.max(s, axis=-1, keepdims=True)
        if wide:
            mx = jnp.maximum(mx, jnp.max(mn[m], axis=-1, keepdims=True))
        mrow.append(mx)

    for m in range(2):
        acc[m][...] = jnp.zeros(acc[m].shape, F32)

    def weigh(t, lr):
        r0 = pl.multiple_of(t * tq, tq)
        v = vn_ref[0, pl.ds(r0, tq), :]
        out = []
        for m in range(2):
            p = jnp.exp(sn_s[m, t] - mrow[m])
            if wide:
                out.append(lr[m] + fold(p, jnp.add))
            else:
                out.append(lr[m] + jnp.sum(p, axis=-1, keepdims=True))
            acc[m][...] += _dot(p.astype(BF16), v)
        return tuple(out)
    ln = _tile_loop(i + 1, weigh, zero if wide else tuple(jnp.zeros((tq, 1), F32) for _ in range(2)),
                    TILE_UNROLL)
    lrow = [jnp.sum(ln[m], axis=-1, keepdims=True) for m in range(2)]

    lp = lam_ref[...]
    lam = (jnp.exp(jnp.sum(lp[0:1] * lp[1:2], axis=-1, keepdims=True))
           - jnp.exp(jnp.sum(lp[2:3] * lp[3:4], axis=-1, keepdims=True)) + lam_init)
    o = acc[0][...] / lrow[0] - lam * (acc[1][...] / lrow[1])
    o = o * lax.rsqrt(jnp.mean(o * o, axis=-1, keepdims=True) + EPS) * sub_ref[...]
    o_ref[0] = (o * (1.0 - lam_init)).astype(o_ref.dtype)


def _diff_attn(qa, kn, vn, lam_p, subln, lam_init):
    B, T, _ = qa.shape
    tq = min(T, 256)
    assert T % tq == 0 and (tq % CHUNK == 0 or tq == T)
    W = 2 * A_HD
    return pl.pallas_call(
        functools.partial(_diff_kernel, tq=tq, lam_init=lam_init),
        grid=(B, A_HEADS, T // tq),
        in_specs=[pl.BlockSpec((1, tq, W), lambda b, h, i: (b, i, h)),
                  pl.BlockSpec((1, T, W), lambda b, h, i: (b, 0, h)),
                  pl.BlockSpec((1, T, W), lambda b, h, i: (b, 0, h)),
                  pl.BlockSpec((4, A_HD), lambda b, h, i: (0, 0)),
                  pl.BlockSpec((1, A_VD), lambda b, h, i: (0, 0))],
        out_specs=pl.BlockSpec((1, tq, W), lambda b, h, i: (b, i, h)),
        out_shape=jax.ShapeDtypeStruct((B, T, A_HEADS * A_VD), BF16),
        scratch_shapes=[pltpu.VMEM((tq, A_VD), F32), pltpu.VMEM((tq, A_VD), F32),
                        pltpu.VMEM((2, T // tq, tq, tq), F32)],
        compiler_params=_cparams(("parallel", "parallel", "arbitrary")),
        name="diff_attn",
    )(qa, kn, vn, lam_p, subln.reshape(1, A_VD))


def _diff_decode_kernel(q_ref, kn_ref, vn_ref, kp_ref, vp_ref, lam_ref, sub_ref, o_ref, *, P, lam_init):
    T = q_ref.shape[1]
    per_pos = A_HEADS * 2
    qpos = P + lax.broadcasted_iota(jnp.int32, (T, 1), 0)
    kpos = P + lax.broadcasted_iota(jnp.int32, (1, T), 1)
    vis = (kpos >> CHUNK_SHIFT) <= (qpos >> CHUNK_SHIFT)
    lp = lam_ref[...]
    lam = (jnp.exp(jnp.sum(lp[0:1] * lp[1:2], axis=-1, keepdims=True))
           - jnp.exp(jnp.sum(lp[2:3] * lp[3:4], axis=-1, keepdims=True)) + lam_init)
    for h in range(A_HEADS):
        v_past = jnp.concatenate(
            [vp_ref[pl.ds(c * A_HEADS + h, P, stride=per_pos), :] for c in range(A_VD // LANES)],
            axis=-1).astype(BF16)
        v_new = vn_ref[0, :, h * A_VD:(h + 1) * A_VD]
        outs = []
        for m in range(2):
            hm = h * 2 + m
            q = q_ref[0, :, hm * A_HD:(hm + 1) * A_HD]
            k_past = kp_ref[pl.ds(hm, P, stride=per_pos), :].astype(BF16)
            s_p = _dot_nt(q, k_past)
            s_n = jnp.where(vis, _dot_nt(q, kn_ref[0, :, hm * A_HD:(hm + 1) * A_HD]), NEG)
            mx = jnp.maximum(jnp.max(s_p, axis=-1, keepdims=True), jnp.max(s_n, axis=-1, keepdims=True))
            p_p = jnp.exp(s_p - mx)
            p_n = jnp.exp(s_n - mx)
            l = jnp.sum(p_p, axis=-1, keepdims=True) + jnp.sum(p_n, axis=-1, keepdims=True)
            outs.append((_dot(p_p.astype(BF16), v_past) + _dot(p_n.astype(BF16), v_new)) / l)
        o = outs[0] - lam * outs[1]
        o = o * lax.rsqrt(jnp.mean(o * o, axis=-1, keepdims=True) + EPS) * sub_ref[...]
        o_ref[0, :, h * A_VD:(h + 1) * A_VD] = (o * (1.0 - lam_init)).astype(o_ref.dtype)


def _diff_decode(qa, kn, vn, kp_rows, vp_rows, layer, lam_p, subln, P, lam_init):
    B, T, W = qa.shape
    assert P % CHUNK == 0 and kp_rows.shape[2] == P * A_HEADS * 2 and vp_rows.shape[2] == P * A_HEADS * 2
    new_spec = pl.BlockSpec((1, T, W), lambda b: (b, 0, 0))
    cache_spec = pl.BlockSpec((None, None, P * A_HEADS * 2, LANES), lambda b: (layer, b, 0, 0))
    return pl.pallas_call(
        functools.partial(_diff_decode_kernel, P=P, lam_init=lam_init),
        grid=(B,),
        in_specs=[new_spec, new_spec, new_spec, cache_spec, cache_spec,
                  pl.BlockSpec((4, A_HD), lambda b: (0, 0)),
                  pl.BlockSpec((1, A_VD), lambda b: (0, 0))],
        out_specs=new_spec,
        out_shape=jax.ShapeDtypeStruct((B, T, W), BF16),
        compiler_params=_cparams(("parallel",)),
        name="diff_decode",
    )(qa, kn, vn, kp_rows, vp_rows, lam_p, subln.reshape(1, A_VD))


def _dsa_kernel(qi_ref, wi_ref, qb_ref, ki_ref, kb_ref, vb_ref, o_ref,
                key_s, bias_s, s_s, *, P, L, tq, tk, topk):
    i = pl.program_id(1)
    q_pos0 = P + i * tq
    Lp = ki_ref.shape[1]
    ncol = jnp.minimum(L, ((q_pos0 + tq - 1) // CHUNK + 1) * CHUNK)
    nt = (ncol + tk - 1) // tk
    qpos = q_pos0 + lax.broadcasted_iota(jnp.int32, (tq, 1), 0)
    grp = B_HEADS // B_KV
    lane = lax.broadcasted_iota(jnp.int32, (tq, LANES), 1)
    wi = wi_ref[0] * (I_HEADS ** -0.5)

    def vis_of(t):
        kpos = t * tk + lax.broadcasted_iota(jnp.int32, (1, tk), 1)
        return ((kpos >> CHUNK_SHIFT) <= (qpos >> CHUNK_SHIFT)) & (kpos < L)

    def score_tile(t, carry):
        c0 = pl.multiple_of(t * tk, tk)
        kit = ki_ref[0, pl.ds(c0, tk), :]
        sc = jnp.zeros((tq, tk), F32)
        for pr in range(I_HEADS * I_HD // LANES):
            qpair = qi_ref[0, :, pr * LANES:(pr + 1) * LANES]
            for half in range(LANES // I_HD):
                h = pr * (LANES // I_HD) + half
                qh = jnp.where((lane >> I_HD_SHIFT) == half, qpair, jnp.zeros_like(qpair))
                isc = _dot_nt(qh, kit)
                sc = sc + jnp.maximum(isc, 0.0) * wi[:, h:h + 1]
        sc = jnp.where(vis_of(t), sc, -jnp.inf)
        bits = pltpu.bitcast(sc, jnp.int32)
        bits = jnp.where(bits == KEY_MIN, 0, bits)
        key_s[t] = bits ^ ((bits >> 31) & 0x7FFFFFFF)
        return carry

    lax.fori_loop(0, nt, score_tile, 0)

    def count(pred_fn):
        def body(t, acc):
            hit = jnp.where(pred_fn(key_s[t]), 1, 0)
            part = hit[:, 0:LANES]
            for c in range(1, tk // LANES):
                part = part + hit[:, c * LANES:(c + 1) * LANES]
            return acc + part
        acc = lax.fori_loop(0, nt, body, jnp.zeros((tq, LANES), jnp.int32))
        return jnp.sum(acc, axis=-1, keepdims=True)

    def select(_):
        def bit_body(bi, thr):
            cand = thr + jnp.left_shift(jnp.int32(1), 31 - bi)
            cnt = count(lambda kk: kk >= cand)
            return jnp.where(cnt >= topk, cand, thr)
        return lax.fori_loop(0, 32, bit_body, jnp.full((tq, 1), KEY_MIN, jnp.int32))

    thr = lax.cond(ncol > topk, select, lambda _: jnp.full((tq, 1), KEY_MIN, jnp.int32), 0)
    n_gt = count(lambda kk: kk > thr)
    n_eq = count(lambda kk: kk == thr)
    need = topk - n_gt
    neg_inf_key = jnp.int32(KEY_MIN + 0x7FFFFF)
    tie_break = jnp.max(jnp.where((n_eq > need) & (thr != neg_inf_key), 1.0, 0.0)) > 0.5

    def bias_fast(_):
        def body(t, c):
            sel = (key_s[t] >= thr) & vis_of(t)
            bias_s[t] = jnp.where(sel, 0.0, NEG)
            return c
        lax.fori_loop(0, nt, body, 0)
        return 0

    def bias_ties(_):
        r = lax.broadcasted_iota(jnp.int32, (LANES, LANES), 0)
        c = lax.broadcasted_iota(jnp.int32, (LANES, LANES), 1)
        before = jnp.where(r < c, 1.0, 0.0).astype(BF16)

        def body(t, seen):
            kk = key_s[t]
            vis = vis_of(t)
            for cb in range(tk // LANES):
                sl = slice(cb * LANES, (cb + 1) * LANES)
                eq = kk[:, sl] == thr
                eqf = jnp.where(eq, 1.0, 0.0)
                rank = seen + _dot(eqf.astype(BF16), before)
                sel = ((kk[:, sl] > thr) | (eq & (rank < need.astype(F32)))) & vis[:, sl]
                bias_s[t, :, sl] = jnp.where(sel, 0.0, NEG)
                seen = seen + jnp.sum(eqf, axis=-1, keepdims=True)
            return seen
        lax.fori_loop(0, nt, body, jnp.zeros((tq, 1), F32))
        return 0

    lax.cond(tie_break, bias_ties, bias_fast, 0)

    def fold(x, op):
        f = x[:, 0:LANES]
        for c in range(1, tk // LANES):
            f = op(f, x[:, c * LANES:(c + 1) * LANES])
        return f

    for n in range(B_KV):
        kv = slice(n * B_HD, (n + 1) * B_HD)

        def scores(t, mrun, kv=kv, n=n):
            c0 = pl.multiple_of(t * tk, tk)
            bias = bias_s[t]
            k = kb_ref[0, pl.ds(c0, tk), kv]
            out = []
            for g in range(grp):
                h = n * grp + g
                s = _dot_nt(qb_ref[0, :, h * B_HD:(h + 1) * B_HD], k) + bias
                s_s[g, t] = s
                out.append(jnp.maximum(mrun[g], fold(s, jnp.maximum)))
            return tuple(out)

        mrun = _tile_loop(nt, scores, tuple(jnp.full((tq, LANES), NEG, F32) for _ in range(grp)), 2)
        mrow = [jnp.max(m, axis=-1, keepdims=True) for m in mrun]

        def weigh(t, carry, kv=kv, mrow=mrow):
            lrun, acc = carry
            c0 = pl.multiple_of(t * tk, tk)
            v = vb_ref[0, pl.ds(c0, tk), kv]
            lnew, anew = [], []
            for g in range(grp):
                p = jnp.exp(s_s[g, t] - mrow[g])
                lnew.append(lrun[g] + fold(p, jnp.add))
                anew.append(acc[g] + _dot(p.astype(BF16), v))
            return tuple(lnew), tuple(anew)

        lrun, acc = _tile_loop(
            nt, weigh,
            (tuple(jnp.zeros((tq, LANES), F32) for _ in range(grp)),
             tuple(jnp.zeros((tq, B_HD), F32) for _ in range(grp))), 2)
        for g in range(grp):
            h = n * grp + g
            l = jnp.sum(lrun[g], axis=-1, keepdims=True)
            o_ref[0, :, h * B_HD:(h + 1) * B_HD] = (acc[g] / l).astype(o_ref.dtype)


def _dsa(qi, proj, qb, ki2, kb, vb, P, L, tk):
    B, T, _ = qb.shape
    Lp = ki2.shape[1]
    tq = min(T, 128)
    assert T % tq == 0 and Lp % tk == 0 and tk % LANES == 0
    topk = min(TOPK_MAX, L // 4)
    kern = functools.partial(_dsa_kernel, P=P, L=L, tq=tq, tk=tk, topk=topk)
    return pl.pallas_call(
        kern,
        grid=(B, T // tq),
        in_specs=[pl.BlockSpec((1, tq, I_HEADS * I_HD), lambda b, i: (b, i, 0)),
                  pl.BlockSpec((1, tq, LANES), lambda b, i: (b, i, OFF_SMALL // LANES)),
                  pl.BlockSpec((1, tq, B_HEADS * B_HD), lambda b, i: (b, i, 0)),
                  pl.BlockSpec((1, Lp, LANES), lambda b, i: (b, 0, 0)),
                  pl.BlockSpec((1, Lp, B_KV * B_HD), lambda b, i: (b, 0, 0)),
                  pl.BlockSpec((1, Lp, B_KV * B_HD), lambda b, i: (b, 0, 0))],
        out_specs=pl.BlockSpec((1, tq, B_HEADS * B_HD), lambda b, i: (b, i, 0)),
        out_shape=jax.ShapeDtypeStruct((B, T, B_HEADS * B_HD), BF16),
        scratch_shapes=[pltpu.VMEM((Lp // tk, tq, tk), jnp.int32), pltpu.VMEM((Lp // tk, tq, tk), F32),
                        pltpu.VMEM((B_HEADS // B_KV, Lp // tk, tq, tk), F32)],
        compiler_params=_cparams(("parallel", "arbitrary")),
        name="dsa",
    )(qi, proj, qb, ki2, kb, vb)


def _float_order_key(x):
    bits = pltpu.bitcast(x, jnp.int32)
    bits = jnp.where(bits == KEY_MIN, 0, bits)
    return bits ^ ((bits >> 31) & 0x7FFFFFFF)


def _dsa_decode_kernel(qi_ref, wi_ref, qb_ref, kin_ref, kbn_ref, vbn_ref, kip_ref, kp_ref, vp_ref, o_ref,
                       *, P, topk):
    T = qi_ref.shape[1]
    grp = B_HEADS // B_KV
    qpos = P + lax.broadcasted_iota(jnp.int32, (T, 1), 0)
    kpos = P + lax.broadcasted_iota(jnp.int32, (1, T), 1)
    vis_n = (kpos >> CHUNK_SHIFT) <= (qpos >> CHUNK_SHIFT)
    wi = wi_ref[0] * (I_HEADS ** -0.5)
    ki_pT = kip_ref[...].astype(BF16)
    ki_n = kin_ref[0, :, 0:I_HD]

    sc_p = jnp.zeros((T, P), F32)
    sc_n = jnp.zeros((T, T), F32)
    for h in range(I_HEADS):
        qh = qi_ref[0, :, h * I_HD:(h + 1) * I_HD]
        w = wi[:, h:h + 1]
        sc_p = sc_p + jnp.maximum(_dot(qh, ki_pT), 0.0) * w
        sc_n = sc_n + jnp.maximum(_dot_nt(qh, ki_n), 0.0) * w
    key_p = _float_order_key(sc_p)
    key_n = _float_order_key(jnp.where(vis_n, sc_n, -jnp.inf))

    def count(pred):
        return (jnp.sum(jnp.where(pred(key_p), 1, 0), axis=-1, keepdims=True)
                + jnp.sum(jnp.where(pred(key_n), 1, 0), axis=-1, keepdims=True))

    def bit_body(bi, thr):
        cand = thr + jnp.left_shift(jnp.int32(1), 31 - bi)
        return jnp.where(count(lambda kk: kk >= cand) >= topk, cand, thr)

    thr = lax.fori_loop(0, 32, bit_body, jnp.full((T, 1), KEY_MIN, jnp.int32))
    need = (topk - count(lambda kk: kk > thr)).astype(F32)

    def before(n):
        r = lax.broadcasted_iota(jnp.int32, (n, n), 0)
        c = lax.broadcasted_iota(jnp.int32, (n, n), 1)
        return jnp.where(r < c, 1.0, 0.0).astype(BF16)

    def select(kk, seen, pre):
        eq = kk == thr
        eqf = jnp.where(eq, 1.0, 0.0)
        rank = seen + _dot(eqf.astype(BF16), pre)
        sel = (kk > thr) | (eq & (rank < need))
        return sel, seen + jnp.sum(eqf, axis=-1, keepdims=True)

    pre_l = before(LANES)
    seen = jnp.zeros((T, 1), F32)
    bias_blocks = []
    for cb in range(P // LANES):
        sel, seen = select(key_p[:, cb * LANES:(cb + 1) * LANES], seen, pre_l)
        bias_blocks.append(jnp.where(sel, 0.0, NEG))
    bias_p = jnp.concatenate(bias_blocks, axis=-1)
    sel, _ = select(key_n, seen, before(T))
    bias_n = jnp.where(sel & vis_n, 0.0, NEG)

    for n in range(B_KV):
        kv = slice(n * B_HD, (n + 1) * B_HD)
        k_p = kp_ref[pl.ds(n, P, stride=B_KV), :].astype(BF16)
        v_p = vp_ref[pl.ds(n, P, stride=B_KV), :].astype(BF16)
        k_n = kbn_ref[0, :, kv]
        v_n = vbn_ref[0, :, kv]
        for g in range(grp):
            h = n * grp + g
            q = qb_ref[0, :, h * B_HD:(h + 1) * B_HD]
            s_p = _dot_nt(q, k_p) + bias_p
            s_n = _dot_nt(q, k_n) + bias_n
            mx = jnp.maximum(jnp.max(s_p, axis=-1, keepdims=True), jnp.max(s_n, axis=-1, keepdims=True))
            p_p = jnp.exp(s_p - mx)
            p_n = jnp.exp(s_n - mx)
            l = jnp.sum(p_p, axis=-1, keepdims=True) + jnp.sum(p_n, axis=-1, keepdims=True)
            o = _dot(p_p.astype(BF16), v_p) + _dot(p_n.astype(BF16), v_n)
            o_ref[0, :, h * B_HD:(h + 1) * B_HD] = (o / l).astype(o_ref.dtype)


def _dsa_decode(qi, proj, qb, ki2, kbb, vbb, ki_past_t, kp_rows, vp_rows, layer):
    B, T, _ = qb.shape
    P = ki_past_t.shape[3]
    assert P % CHUNK == 0 and P % LANES == 0 and kp_rows.shape[2] == P * B_KV
    topk = min(TOPK_MAX, (P + T) // 4)

    def new(width, blk=0):
        return pl.BlockSpec((1, T, width), lambda b: (b, 0, blk))

    rows = pl.BlockSpec((None, None, P * B_KV, LANES), lambda b: (layer, b, 0, 0))
    return pl.pallas_call(
        functools.partial(_dsa_decode_kernel, P=P, topk=topk),
        grid=(B,),
        in_specs=[new(I_HEADS * I_HD), new(LANES, OFF_SMALL // LANES), new(B_HEADS * B_HD),
                  new(LANES), new(B_KV * B_HD), new(B_KV * B_HD),
                  pl.BlockSpec((None, None, I_HD, P), lambda b: (layer, b, 0, 0)), rows, rows],
        out_specs=new(B_HEADS * B_HD),
        out_shape=jax.ShapeDtypeStruct((B, T, B_HEADS * B_HD), BF16),
        compiler_params=_cparams(("parallel",)),
        name="dsa_decode",
    )(qi, proj, qb, ki2, kbb, vbb, ki_past_t, kp_rows, vp_rows)


def _bmm(a, b):
    return jnp.einsum('bij,bjk->bik', a.astype(BF16), b.astype(BF16), preferred_element_type=F32)


def _bmm_nt(a, b):
    return jnp.einsum('bik,bjk->bij', a.astype(BF16), b.astype(BF16), preferred_element_type=F32)


def _gdn_intra_kernel(x_ref, halo_ref, sm_ref, cw_ref, cbuf_ref, alog_ref, dtb_ref,
                      u_o, w_o, qt_o, ktT_o, qkg_o, dec_o, buf_s, *, cc, nck):
    i = pl.program_id(1)
    R = nck * cc
    HW = C_HEADS * C_HD

    @pl.when(i == 0)
    def _():
        buf_s[0:SUBLANES, :] = jnp.zeros((SUBLANES, C_QKV), F32)
        buf_s[SUBLANES - (C_CONV - 1):SUBLANES, :] = cbuf_ref[0]

    @pl.when(i > 0)
    def _():
        buf_s[0:SUBLANES, :] = halo_ref[0]

    buf_s[SUBLANES:SUBLANES + R, :] = x_ref[0]
    y = None
    for jw in range(C_CONV):
        off = SUBLANES - (C_CONV - 1) + jw
        term = buf_s[off:off + R, :] * cw_ref[jw:jw + 1, :]
        y = term if y is None else y + term
    y = y * jax.nn.sigmoid(y)

    sm = sm_ref[0]
    xg = sm + dtb_ref[...]
    softplus = jnp.maximum(xg, 0.0) + jnp.log(1.0 + jnp.exp(-jnp.abs(xg)))
    g = -jnp.exp(alog_ref[...]) * softplus
    beta = jax.nn.sigmoid(sm)

    cshift = cc.bit_length() - 1
    rr = lax.broadcasted_iota(jnp.int32, (R, R), 0)
    rc = lax.broadcasted_iota(jnp.int32, (R, R), 1)
    cum = ((rr >> cshift) == (rc >> cshift)) & (rr >= rc)
    G = jnp.dot(jnp.where(cum, 1.0, 0.0), g, preferred_element_type=F32,
                precision=lax.Precision.HIGHEST)

    qs, ks, vs, gcs, bcs = [], [], [], [], []
    for ck in range(nck):
        rows = slice(ck * cc, (ck + 1) * cc)
        for h in range(C_HEADS):
            qs.append(y[rows, h * C_HD:(h + 1) * C_HD])
            ks.append(y[rows, HW + h * C_HD:HW + (h + 1) * C_HD])
            vs.append(y[rows, 2 * HW + h * C_HD:2 * HW + (h + 1) * C_HD])
            gcs.append(G[rows, CA_LANE + h:CA_LANE + h + 1])
            bcs.append(beta[rows, CB_LANE + h:CB_LANE + h + 1])
    q3 = jnp.stack(qs)
    k3 = jnp.stack(ks)
    v3 = jnp.stack(vs)
    Gc = jnp.stack(gcs)
    bc = jnp.stack(bcs)
    q3 = q3 * lax.rsqrt(jnp.sum(q3 * q3, axis=-1, keepdims=True) + EPS) * (C_HD ** -0.5)
    k3 = k3 * lax.rsqrt(jnp.sum(k3 * k3, axis=-1, keepdims=True) + EPS)

    ri = lax.broadcasted_iota(jnp.int32, (cc, cc), 0)
    ci = lax.broadcasted_iota(jnp.int32, (cc, cc), 1)
    incl = (ri >= ci)[None]
    strict = (ri > ci)[None]
    eye = (ri == ci)[None]
    bdiag = ((ri // SOLVE_BLOCK) == (ci // SOLVE_BLOCK))[None]

    Gr = jnp.sum(jnp.where(eye, Gc, 0.0), axis=1, keepdims=True)
    gam = jnp.where(incl, jnp.exp(jnp.where(incl, Gc - Gr, 0.0)), 0.0)
    k16 = k3.astype(BF16)
    A = jnp.where(strict, bc * _bmm_nt(k16, k16) * gam, 0.0)
    Nd = jnp.where(bdiag, -A, 0.0)
    E = jnp.where(bdiag, 0.0, A)
    N2 = _bmm(Nd, Nd)
    N4 = _bmm(N2, N2)
    N8 = _bmm(N4, N4)
    Q = Nd + N2 + _bmm(Nd, N2)
    Q = Q + N4 + _bmm(Q, N4)
    Q = Q + N8 + _bmm(Q, N8)
    Fm = E + _bmm(Q, E)
    F2 = _bmm(Fm, Fm)
    Rm = F2 - Fm - _bmm(Fm, F2)
    Wm = Rm + Q + _bmm(Rm, Q)
    eG = jnp.exp(Gc)
    rhs = jnp.concatenate([bc * v3, (bc * eG) * k3], axis=-1)
    sol = rhs + _bmm(Wm, rhs)
    qkg = _bmm_nt(q3, k16) * gam
    gl = Gc[:, cc - 1:cc, :]
    qt = q3 * eG
    kt = k3 * jnp.exp(gl - Gc)
    dec = jnp.exp(gl)

    for ck in range(nck):
        rows = slice(ck * cc, (ck + 1) * cc)
        for h in range(C_HEADS):
            b = ck * C_HEADS + h
            cols = slice(h * C_HD, (h + 1) * C_HD)
            u_o[0, rows, cols] = sol[b, :, 0:C_HD]
            w_o[0, rows, cols] = sol[b, :, C_HD:2 * C_HD].astype(BF16)
            qt_o[0, rows, cols] = qt[b].astype(BF16)
            ktT_o[0, ck, h] = kt[b].T.astype(BF16)
            qkg_o[0, ck, h] = qkg[b].astype(BF16)
            dec_o[0, ck, h] = jnp.broadcast_to(dec[b], (1, C_HD))


def _gdn_inter_kernel(u_ref, w_ref, qt_ref, ktT_ref, qkg_ref, dec_ref, z_ref, s0_ref, gn_ref,
                      o_ref, sfin_ref, st_s, *, cc, nck):
    c = pl.program_id(1)

    @pl.when(c == 0)
    def _():
        st_s[...] = s0_ref[0]

    S = st_s[...]
    for ck in range(nck):
        rows = slice(ck * cc, (ck + 1) * cc)

        def heads(ref):
            return jnp.stack([ref[0, rows, h * C_HD:(h + 1) * C_HD] for h in range(C_HEADS)])

        S16 = S.astype(BF16)
        v_new = heads(u_ref) - _bmm(heads(w_ref), S16)
        vn16 = v_new.astype(BF16)
        o = _bmm(heads(qt_ref), S16) + _bmm(qkg_ref[0, ck], vn16)
        S = dec_ref[0, ck] * S + _bmm(ktT_ref[0, ck], vn16)
        on = o * lax.rsqrt(jnp.mean(o * o, axis=-1, keepdims=True) + EPS) * gn_ref[...]
        for h in range(C_HEADS):
            cols = slice(h * C_HD, (h + 1) * C_HD)
            z = z_ref[0, rows, cols]
            o_ref[0, rows, cols] = (on[h] * (z * jax.nn.sigmoid(z))).astype(o_ref.dtype)
    st_s[...] = S

    @pl.when(c == pl.num_programs(1) - 1)
    def _():
        sfin_ref[0] = S


def _gdn(proj, conv_w, cbuf, S0, a_log, dt_bias, out_norm):
    B, T, _ = proj.shape
    cc = min(CHUNK, T)
    assert T % cc == 0 and cc % SOLVE_BLOCK == 0 and cc // SOLVE_BLOCK <= 4 and cc >= SUBLANES
    assert cc & (cc - 1) == 0
    HW = C_HEADS * C_HD
    NC = T // cc
    nck_a = 2 if NC % 2 == 0 else 1
    nck_b = 4 if NC % 4 == 0 else 1

    def at_ca(p):
        return jnp.zeros((1, LANES), F32).at[0, CA_LANE:CA_LANE + C_HEADS].set(p.astype(F32))

    Ra = nck_a * cc
    hb = Ra // SUBLANES
    u, w, qt, ktT, qkg, dec = pl.pallas_call(
        functools.partial(_gdn_intra_kernel, cc=cc, nck=nck_a),
        grid=(B, NC // nck_a),
        in_specs=[pl.BlockSpec((1, Ra, C_QKV), lambda b, i: (b, i, OFF_CQKV // C_QKV)),
                  pl.BlockSpec((1, SUBLANES, C_QKV),
                               lambda b, i: (b, jnp.maximum(i * hb - 1, 0), OFF_CQKV // C_QKV)),
                  pl.BlockSpec((1, Ra, LANES), lambda b, i: (b, i, OFF_SMALL // LANES)),
                  pl.BlockSpec((C_CONV, C_QKV), lambda b, i: (0, 0)),
                  pl.BlockSpec((1, C_CONV - 1, C_QKV), lambda b, i: (b, 0, 0)),
                  pl.BlockSpec((1, LANES), lambda b, i: (0, 0)),
                  pl.BlockSpec((1, LANES), lambda b, i: (0, 0))],
        out_specs=[pl.BlockSpec((1, Ra, HW), lambda b, i: (b, i, 0)),
                   pl.BlockSpec((1, Ra, HW), lambda b, i: (b, i, 0)),
                   pl.BlockSpec((1, Ra, HW), lambda b, i: (b, i, 0)),
                   pl.BlockSpec((1, nck_a, C_HEADS, C_HD, cc), lambda b, i: (b, i, 0, 0, 0)),
                   pl.BlockSpec((1, nck_a, C_HEADS, cc, cc), lambda b, i: (b, i, 0, 0, 0)),
                   pl.BlockSpec((1, nck_a, C_HEADS, 1, C_HD), lambda b, i: (b, i, 0, 0, 0))],
        out_shape=[jax.ShapeDtypeStruct((B, T, HW), F32),
                   jax.ShapeDtypeStruct((B, T, HW), BF16),
                   jax.ShapeDtypeStruct((B, T, HW), BF16),
                   jax.ShapeDtypeStruct((B, NC, C_HEADS, C_HD, cc), BF16),
                   jax.ShapeDtypeStruct((B, NC, C_HEADS, cc, cc), BF16),
                   jax.ShapeDtypeStruct((B, NC, C_HEADS, 1, C_HD), F32)],
        scratch_shapes=[pltpu.VMEM((Ra + SUBLANES, C_QKV), F32)],
        compiler_params=_cparams(("parallel", "parallel")),
        name="gdn_intra",
    )(proj, proj, proj, conv_w, cbuf, at_ca(a_log), at_ca(dt_bias))

    Rb = nck_b * cc
    return pl.pallas_call(
        functools.partial(_gdn_inter_kernel, cc=cc, nck=nck_b),
        grid=(B, NC // nck_b),
        in_specs=[pl.BlockSpec((1, Rb, HW), lambda b, c: (b, c, 0)),
                  pl.BlockSpec((1, Rb, HW), lambda b, c: (b, c, 0)),
                  pl.BlockSpec((1, Rb, HW), lambda b, c: (b, c, 0)),
                  pl.BlockSpec((1, nck_b, C_HEADS, C_HD, cc), lambda b, c: (b, c, 0, 0, 0)),
                  pl.BlockSpec((1, nck_b, C_HEADS, cc, cc), lambda b, c: (b, c, 0, 0, 0)),
                  pl.BlockSpec((1, nck_b, C_HEADS, 1, C_HD), lambda b, c: (b, c, 0, 0, 0)),
                  pl.BlockSpec((1, Rb, HW), lambda b, c: (b, c, OFF_CZ // HW)),
                  pl.BlockSpec((1, C_HEADS, C_HD, C_HD), lambda b, c: (b, 0, 0, 0)),
                  pl.BlockSpec((1, C_HD), lambda b, c: (0, 0))],
        out_specs=[pl.BlockSpec((1, Rb, HW), lambda b, c: (b, c, 0)),
                   pl.BlockSpec((1, C_HEADS, C_HD, C_HD), lambda b, c: (b, 0, 0, 0))],
        out_shape=[jax.ShapeDtypeStruct((B, T, HW), BF16),
                   jax.ShapeDtypeStruct((B, C_HEADS, C_HD, C_HD), F32)],
        scratch_shapes=[pltpu.VMEM((C_HEADS, C_HD, C_HD), F32)],
        compiler_params=_cparams(("parallel", "arbitrary")),
        name="gdn_inter",
    )(u, w, qt, ktT, qkg, dec, proj, S0, out_norm.reshape(1, C_HD))


def _ffn_up_kernel(x_ref, g_ref, sh_ref, sc_ref, wg_ref, wu_ref, cwg_ref, cwu_ref, fg_ref, fu_ref,
                   act_o, fng_o, fnu_o, h_ref, bg_s, bu_s, carry_s, *, split_rows):
    bb, tt, D = x_ref.shape
    tn = wg_ref.shape[1]
    i = pl.program_id(1)
    j = pl.program_id(2)
    lo = SUBLANES - (FFN_CONV - 1)

    @pl.when(j == 0)
    def _():
        x = x_ref[...]
        y = x * lax.rsqrt(jnp.mean(x * x, axis=-1, keepdims=True) + EPS) * g_ref[...]
        h = y * (1.0 + sc_ref[...]) + sh_ref[...]
        h_ref[...] = h.reshape(bb * tt, D).astype(BF16)

    def half(w_ref, cw_ref, f_ref, fn_o, buf, idx):
        u = _dot(h_ref[...], w_ref[...]).reshape(bb, tt, tn)
        if split_rows:
            @pl.when(i == 0)
            def _():
                buf[:, lo:SUBLANES, :] = f_ref[...]

            @pl.when(i > 0)
            def _():
                buf[:, 0:SUBLANES, :] = carry_s[idx, j]

            carry_s[idx, j] = u[:, tt - SUBLANES:tt, :]
        else:
            buf[:, lo:SUBLANES, :] = f_ref[...]
        buf[:, SUBLANES:SUBLANES + tt, :] = u
        fn_o[...] = u[:, tt - (FFN_CONV - 1):tt, :]
        y = None
        for jw in range(FFN_CONV):
            term = buf[:, lo + jw:lo + jw + tt, :] * cw_ref[jw:jw + 1, :]
            y = term if y is None else y + term
        return y

    yg = half(wg_ref, cwg_ref, fg_ref, fng_o, bg_s, 0)
    yu = half(wu_ref, cwu_ref, fu_ref, fnu_o, bu_s, 1)
    act_o[...] = (yg * jax.nn.sigmoid(yg) * yu).astype(act_o.dtype)


def _ffn_up(x, g, shift, scale, w_up, conv_w, fbuf):
    B, T, D = x.shape
    FF = w_up.shape[1] // 2
    bb, tt = _row_tiles(B, T)
    tn = COL_TILE
    assert FF % tn == 0 and tt >= SUBLANES
    nj = FF // tn
    split_rows = T > tt
    assert not split_rows or bb == 1
    nf = FFN_CONV - 1

    def cols(off, rows):
        return pl.BlockSpec(rows + (tn,), lambda b, i, j: (0,) * len(rows) + (off + j,))

    def fb(off):
        return pl.BlockSpec((bb, nf, tn), lambda b, i, j: (b, 0, off + j))

    act, fng, fnu = pl.pallas_call(
        functools.partial(_ffn_up_kernel, split_rows=split_rows),
        grid=(B // bb, T // tt, nj),
        in_specs=[pl.BlockSpec((bb, tt, D), lambda b, i, j: (b, i, 0)),
                  pl.BlockSpec((1, 1, D), lambda b, i, j: (0, 0, 0)),
                  pl.BlockSpec((bb, 1, D), lambda b, i, j: (b, 0, 0)),
                  pl.BlockSpec((bb, 1, D), lambda b, i, j: (b, 0, 0)),
                  cols(0, (D,)), cols(nj, (D,)), cols(0, (FFN_CONV,)), cols(nj, (FFN_CONV,)),
                  fb(0), fb(nj)],
        out_specs=[pl.BlockSpec((bb, tt, tn), lambda b, i, j: (b, i, j)),
                   pl.BlockSpec((bb, None, nf, tn), lambda b, i, j: (b, i, 0, j)),
                   pl.BlockSpec((bb, None, nf, tn), lambda b, i, j: (b, i, 0, j))],
        out_shape=[jax.ShapeDtypeStruct((B, T, FF), BF16),
                   jax.ShapeDtypeStruct((B, T // tt, nf, FF), F32),
                   jax.ShapeDtypeStruct((B, T // tt, nf, FF), F32)],
        scratch_shapes=[pltpu.VMEM((bb * tt, D), BF16),
                        pltpu.VMEM((bb, tt + SUBLANES, tn), F32),
                        pltpu.VMEM((bb, tt + SUBLANES, tn), F32),
                        pltpu.VMEM((2, nj, 1, SUBLANES, tn), F32)],
        compiler_params=_cparams(("parallel", "arbitrary", "arbitrary")),
        name="ffn_up",
    )(x, g.reshape(1, 1, D), shift, scale, w_up, w_up, conv_w, conv_w, fbuf, fbuf)
    return act, jnp.concatenate([fng[:, -1], fnu[:, -1]], axis=-1)


def _prep_w_in(w_in):
    D = w_in.shape[0]
    sizes = (1024, 1024, 1024, 1024, 256, 256, 512, 64, 8, C_QKV, 8, 8, 1024, N_BRANCH * D)
    offs = np.concatenate([[0], np.cumsum(sizes)])
    (aq, ak, av, bq, bk, bv, bqi, bki, bwi, cqkv, ca, cb, cz, gt) = [
        w_in[:, int(offs[k]):int(offs[k + 1])] for k in range(len(sizes))]

    def z(n):
        return jnp.zeros((D, n), w_in.dtype)

    cols = [cqkv, cz, aq, ak, av, bq, bk, bv, bqi, bki, z(LANES - I_HD), bwi, ca, cb, z(LANES - 24),
            z(OFF_GT - OFF_SMALL - LANES), gt]
    w = jnp.concatenate(cols, axis=1).astype(BF16)
    assert w.shape[1] == N_PROJ
    return w


def _layer(x, mod, past, lw, layer_idx):
    kA_p, vA_p, kB_p, vB_p, kI_p, cbuf, S0, fbuf = past
    B, T, D = x.shape
    P = 0 if kI_p is None else kI_p.shape[3]
    sh1, sc1, g1, sh2, sc2, g2 = [m.reshape(B, 1, D) for m in jnp.split(mod, 6, axis=-1)]

    proj = _mm_norm(x, lw['norm_mix'], sh1, sc1, lw['w_in'])
    rope = _rope_table(P + jnp.arange(T, dtype=jnp.int32))
    (qa, ka, kab, va, vab, qb, kb, kbb, vb, vbb, qi, ki, ki2) = _prep(
        proj, rope, lw['a_q_norm'], lw['a_k_norm'], lw['b_q_norm'], lw['b_k_norm'])

    lam_init = 0.8 - 0.6 * math.exp(-0.3 * layer_idx)
    if P:
        oa = _diff_decode(qa, kab, vab, kA_p, vA_p, layer_idx, lw['a_lambda'], lw['a_subln'], P, lam_init)
        ob = _dsa_decode(qi, proj, qb, ki2, kbb, vbb, kI_p, kB_p, vB_p, layer_idx)
    else:
        oa = _diff_attn(qa, kab, vab, lw['a_lambda'], lw['a_subln'], lam_init)
        ob = _dsa(qi, proj, qb, ki2, kbb, vbb, 0, T, min(T, 512))

    oc, S_fin = _gdn(proj, lw['c_conv'], cbuf, S0, lw['c_a_log'], lw['c_dt_bias'], lw['c_out_norm'])
    assert T >= C_CONV - 1 and T >= FFN_CONV - 1
    cbuf_new = proj[:, T - (C_CONV - 1):, OFF_CQKV:OFF_CQKV + C_QKV]

    mixed = _merge(oa, ob, oc, lw['w_branch'], proj, D)
    x = _mm_res(mixed, lw['w_out'], x, g1)

    act, fbuf_new = _ffn_up(x, lw['norm_ffn'], sh2, sc2, lw['w_up'], lw['ffn_conv'], fbuf)
    x = _mm_res(act, lw['w_down'], x, g2, row_target=512)

    new_state = (ka.reshape(B, T, A_HEADS, 2, A_HD), va.reshape(B, T, A_HEADS, A_VD),
                 kb.reshape(B, T, B_KV, B_HD), vb.reshape(B, T, B_KV, B_HD), ki,
                 cbuf_new, S_fin, fbuf_new)
    return x, new_state


def kernel(x_prompt, x_sample, c_prompt, c_sample, cache_diff_k, cache_diff_v, cache_dsa_k, cache_dsa_v, cache_dsa_kidx, state_gdn_conv, state_gdn, state_ffn_conv, w_ada, b_ada, norm_mix, w_in, a_q_norm, a_k_norm, a_lambda, a_subln, b_q_norm, b_k_norm, c_conv, c_a_log, c_dt_bias, c_out_norm, w_branch, w_out, norm_ffn, w_up, ffn_conv, w_down):
    depth = w_in.shape[0]
    bp, bs = x_prompt.shape[0], x_sample.shape[0]
    d_ff2 = w_up.shape[2]
    dt_ = x_prompt.dtype
    prompt_past = (None, None, None, None, None, jnp.zeros((bp, C_CONV - 1, C_QKV), dt_),
                   jnp.zeros((bp, C_HEADS, C_HD, C_HD), dt_), jnp.zeros((bp, FFN_CONV - 1, d_ff2), dt_))
    nrow = -(-(bp + bs) // SUBLANES) * SUBLANES
    c_all = jnp.concatenate([c_prompt, c_sample, jnp.zeros((nrow - bp - bs, c_prompt.shape[1]), dt_)], axis=0)
    past_len = cache_diff_k.shape[2]
    diff_k_rows = cache_diff_k.reshape(depth, bs, past_len * A_HEADS * 2, A_HD)
    diff_v_rows = cache_diff_v.reshape(depth, bs, past_len, A_HEADS, A_VD // LANES, LANES)
    diff_v_rows = diff_v_rows.transpose(0, 1, 2, 4, 3, 5).reshape(depth, bs, past_len * A_HEADS * 2, LANES)
    dsa_k_rows = cache_dsa_k.reshape(depth, bs, past_len * B_KV, B_HD)
    dsa_v_rows = cache_dsa_v.reshape(depth, bs, past_len * B_KV, B_HD)
    dsa_kidx_t = jnp.swapaxes(cache_dsa_kidx, 2, 3)
    xp, xs = x_prompt, x_sample
    prompt_states, sample_states = [], []
    for l in range(depth):
        lw = dict(norm_mix=norm_mix[l], w_in=_prep_w_in(w_in[l]),
                  a_q_norm=a_q_norm[l], a_k_norm=a_k_norm[l], a_lambda=a_lambda[l], a_subln=a_subln[l],
                  b_q_norm=b_q_norm[l], b_k_norm=b_k_norm[l], c_conv=c_conv[l], c_a_log=c_a_log[l],
                  c_dt_bias=c_dt_bias[l], c_out_norm=c_out_norm[l], w_branch=w_branch[l].astype(BF16),
                  w_out=w_out[l].astype(BF16), norm_ffn=norm_ffn[l], w_up=w_up[l].astype(BF16),
                  ffn_conv=ffn_conv[l], w_down=w_down[l].astype(BF16))
        mod = _ada(c_all, w_ada[l], b_ada[l])
        xp, st_p = _layer(xp, mod[:bp], prompt_past, lw, l)
        sample_past = (diff_k_rows, diff_v_rows, dsa_k_rows, dsa_v_rows, dsa_kidx_t,
                       state_gdn_conv[l], state_gdn[l], state_ffn_conv[l])
        xs, st_s = _layer(xs, mod[bp:bp + bs], sample_past, lw, l)
        prompt_states.append(st_p)
        sample_states.append(st_s)
    p_out = [jnp.stack(s, axis=0) for s in zip(*prompt_states)]
    s_out = [jnp.stack(s, axis=0) for s in zip(*sample_states)]
    return (xp, xs, *p_out, *s_out)
```

```python
import functools
import math

import jax
import jax.numpy as jnp
import numpy as np
from jax import lax
from jax.experimental import pallas as pl
from jax.experimental.pallas import tpu as pltpu

F32 = jnp.float32
BF16 = jnp.bfloat16

CHUNK = 64
CHUNK_SHIFT = 6
ROPE_THETA = 10000.0
EPS = 1e-6
A_HEADS, A_HD, A_VD = 4, 128, 256
B_HEADS, B_KV, B_HD = 8, 2, 128
I_HEADS, I_HD = 8, 64
I_HD_SHIFT = 6
TOPK_MAX = 256
C_HEADS, C_HD, C_CONV = 8, 128, 4
C_QKV = 3 * C_HEADS * C_HD
BRANCH_W = 1024
N_BRANCH = 3
FFN_CONV = 3

LANES = 128
SUBLANES = 8
VMEM_LIMIT = 56 * 1024 * 1024
ROW_TILE = 1024
COL_TILE = 512
SOLVE_BLOCK = 16
TILE_UNROLL = 4
FFN_ROW_CHUNK = 256
NEG = -1e30
KEY_MIN = -(2 ** 31)

OFF_CQKV = 0
OFF_CZ = 3072
OFF_AQ = 4096
OFF_AK = 5120
OFF_AV = 6144
OFF_BQ = 7168
OFF_BK = 8192
OFF_BV = 8448
OFF_BQI = 8704
OFF_BKI = 9216
OFF_SMALL = 9344
CA_LANE = 8
CB_LANE = 16
OFF_GT = 9728
N_PROJ = 15872


def _cparams(sem):
    return pltpu.CompilerParams(dimension_semantics=sem, vmem_limit_bytes=VMEM_LIMIT)


def _row_tiles(B, T, target=ROW_TILE):
    tt = min(T, target)
    assert T % tt == 0 and tt % SUBLANES == 0
    bb = max(1, min(B, target // tt))
    while B % bb:
        bb -= 1
    return bb, tt


def _dot(a, b):
    return jnp.dot(a, b, preferred_element_type=F32)


def _dot_nt(a, b):
    return lax.dot_general(a, b, (((1,), (1,)), ((), ())), preferred_element_type=F32)


def _bdot(a, b):
    return _dot(a.astype(BF16), b.astype(BF16))


def _tile_loop(n, body, init, unroll):
    nb = n // unroll

    def trip(tb, c):
        for u in range(unroll):
            c = body(tb * unroll + u, c)
        return c

    c = lax.fori_loop(0, nb, trip, init)
    return lax.fori_loop(nb * unroll, n, body, c)


def _ada_kernel(c_ref, w_ref, b_ref, o_ref):
    c = c_ref[...]
    s = (c * jax.nn.sigmoid(c)).astype(BF16)
    o_ref[...] = _dot(s, w_ref[...].astype(BF16)) + b_ref[...]


def _ada(c, w, b):
    M, D = c.shape
    N = w.shape[1]
    tn = 1024
    return pl.pallas_call(
        _ada_kernel,
        grid=(N // tn,),
        in_specs=[pl.BlockSpec((M, D), lambda j: (0, 0)),
                  pl.BlockSpec((D, tn), lambda j: (0, j)),
                  pl.BlockSpec((1, tn), lambda j: (0, j))],
        out_specs=pl.BlockSpec((M, tn), lambda j: (0, j)),
        out_shape=jax.ShapeDtypeStruct((M, N), F32),
        compiler_params=_cparams(("parallel",)),
        name="ada",
    )(c, w, b.reshape(1, N))


def _mm_norm_kernel(x_ref, g_ref, sh_ref, sc_ref, w_ref, o_ref, h_ref):
    bb, tt, D = x_ref.shape
    tn = w_ref.shape[1]

    @pl.when(pl.program_id(2) == 0)
    def _():
        x = x_ref[...]
        y = x * lax.rsqrt(jnp.mean(x * x, axis=-1, keepdims=True) + EPS) * g_ref[...]
        h = y * (1.0 + sc_ref[...]) + sh_ref[...]
        h_ref[...] = h.reshape(bb * tt, D).astype(BF16)

    acc = _dot(h_ref[...], w_ref[...])
    o_ref[...] = acc.reshape(bb, tt, tn).astype(o_ref.dtype)


def _mm_norm(x, g, shift, scale, w, out_dtype=F32):
    B, T, D = x.shape
    N = w.shape[1]
    bb, tt = _row_tiles(B, T)
    tn = COL_TILE
    assert N % tn == 0
    return pl.pallas_call(
        _mm_norm_kernel,
        grid=(B // bb, T // tt, N // tn),
        in_specs=[pl.BlockSpec((bb, tt, D), lambda b, i, j: (b, i, 0)),
                  pl.BlockSpec((1, 1, D), lambda b, i, j: (0, 0, 0)),
                  pl.BlockSpec((bb, 1, D), lambda b, i, j: (b, 0, 0)),
                  pl.BlockSpec((bb, 1, D), lambda b, i, j: (b, 0, 0)),
                  pl.BlockSpec((D, tn), lambda b, i, j: (0, j))],
        out_specs=pl.BlockSpec((bb, tt, tn), lambda b, i, j: (b, i, j)),
        out_shape=jax.ShapeDtypeStruct((B, T, N), out_dtype),
        scratch_shapes=[pltpu.VMEM((bb * tt, D), BF16)],
        compiler_params=_cparams(("parallel", "parallel", "arbitrary")),
        name="mm_norm",
    )(x, g.reshape(1, 1, D), shift, scale, w)


def _mm_res_kernel(a_ref, w_ref, res_ref, g_ref, o_ref):
    bb, tt, K = a_ref.shape
    tn = w_ref.shape[1]
    acc = _dot(a_ref[...].reshape(bb * tt, K), w_ref[...])
    o_ref[...] = res_ref[...] + g_ref[...] * acc.reshape(bb, tt, tn)


def _mm_res(a, w, res, gate, row_target=ROW_TILE):
    B, T, K = a.shape
    N = w.shape[1]
    bb, tt = _row_tiles(B, T, row_target)
    tn = COL_TILE
    assert N % tn == 0
    return pl.pallas_call(
        _mm_res_kernel,
        grid=(B // bb, T // tt, N // tn),
        in_specs=[pl.BlockSpec((bb, tt, K), lambda b, i, j: (b, i, 0)),
                  pl.BlockSpec((K, tn), lambda b, i, j: (0, j)),
                  pl.BlockSpec((bb, tt, tn), lambda b, i, j: (b, i, j)),
                  pl.BlockSpec((bb, 1, tn), lambda b, i, j: (b, 0, j))],
        out_specs=pl.BlockSpec((bb, tt, tn), lambda b, i, j: (b, i, j)),
        out_shape=jax.ShapeDtypeStruct((B, T, N), F32),
        compiler_params=_cparams(("parallel", "parallel", "arbitrary")),
        name="mm_res",
    )(a, w, res, gate)


def _merge_kernel(b0_ref, b1_ref, b2_ref, w_ref, g0_ref, g1_ref, g2_ref, o_ref):
    bb, tt, W = b0_ref.shape
    tn = w_ref.shape[2]
    acc = None
    for n, (br, gr) in enumerate(((b0_ref, g0_ref), (b1_ref, g1_ref), (b2_ref, g2_ref))):
        y = _dot(br[...].reshape(bb * tt, W), w_ref[n])
        t = jax.nn.sigmoid(gr[...].reshape(bb * tt, tn)) * y
        acc = t if acc is None else acc + t
    o_ref[...] = acc.reshape(bb, tt, tn).astype(o_ref.dtype)


def _merge(oa, ob, oc, w_branch, proj, D):
    B, T, W = oa.shape
    bb, tt = _row_tiles(B, T)
    tn = COL_TILE
    gblk = OFF_GT // tn
    nper = D // tn
    br_spec = pl.BlockSpec((bb, tt, W), lambda b, i, j: (b, i, 0))

    def gate_spec(n):
        return pl.BlockSpec((bb, tt, tn), lambda b, i, j: (b, i, gblk + n * nper + j))

    return pl.pallas_call(
        _merge_kernel,
        grid=(B // bb, T // tt, D // tn),
        in_specs=[br_spec, br_spec, br_spec,
                  pl.BlockSpec((N_BRANCH, W, tn), lambda b, i, j: (0, 0, j)),
                  gate_spec(0), gate_spec(1), gate_spec(2)],
        out_specs=pl.BlockSpec((bb, tt, tn), lambda b, i, j: (b, i, j)),
        out_shape=jax.ShapeDtypeStruct((B, T, D), BF16),
        compiler_params=_cparams(("parallel", "parallel", "arbitrary")),
        name="merge",
    )(oa, ob, oc, w_branch, proj, proj, proj)


def _prep_kernel(aqk_ref, av_ref, bq_ref, bkv_ref, bqi_ref, bki_ref, rope_ref,
                 gaq_ref, gak_ref, gbq_ref, gbk_ref,
                 qa_o, ka_o, kab_o, va_o, vab_o, qb_o, kb_o, kbb_o, vb_o, vbb_o,
                 qi_o, ki_o, ki2_o):
    c1 = rope_ref[:, 0:LANES]
    s1 = rope_ref[:, LANES:2 * LANES]
    c2 = rope_ref[:, 2 * LANES:3 * LANES]
    s2 = rope_ref[:, 3 * LANES:4 * LANES]
    tt = c1.shape[0]
    lane = lax.broadcasted_iota(jnp.int32, (tt, LANES), 1)
    low_half = (lane & (I_HD - 1)) < (I_HD // 2)

    def norm_rope(x, g):
        y = x * lax.rsqrt(jnp.mean(x * x, axis=-1, keepdims=True) + EPS) * g
        return y * c1 + pltpu.roll(y, A_HD // 2, 1) * s1

    def rope64(x):
        r = jnp.where(low_half, pltpu.roll(x, LANES - I_HD // 2, 1), pltpu.roll(x, I_HD // 2, 1))
        return x * c2 + r * s2

    gaq, gak, gbq, gbk = gaq_ref[...], gak_ref[...], gbq_ref[...], gbk_ref[...]
    for hm in range(2 * A_HEADS):
        sl = slice(hm * LANES, (hm + 1) * LANES)
        q = norm_rope(aqk_ref[0, :, sl], gaq) * (A_HD ** -0.5)
        qa_o[0, :, sl] = q.astype(BF16)
        k = norm_rope(aqk_ref[0, :, 2 * A_HEADS * LANES + hm * LANES:2 * A_HEADS * LANES + (hm + 1) * LANES], gak)
        ka_o[0, :, sl] = k
        kab_o[0, :, sl] = k.astype(BF16)
    va = av_ref[0]
    va_o[0] = va
    vab_o[0] = va.astype(BF16)
    for h in range(B_HEADS):
        sl = slice(h * LANES, (h + 1) * LANES)
        q = norm_rope(bq_ref[0, :, sl], gbq) * (B_HD ** -0.5)
        qb_o[0, :, sl] = q.astype(BF16)
    for n in range(B_KV):
        sl = slice(n * LANES, (n + 1) * LANES)
        k = norm_rope(bkv_ref[0, :, sl], gbk)
        kb_o[0, :, sl] = k
        kbb_o[0, :, sl] = k.astype(BF16)
    vb = bkv_ref[0, :, B_KV * LANES:2 * B_KV * LANES]
    vb_o[0] = vb
    vbb_o[0] = vb.astype(BF16)
    for p in range(I_HEADS * I_HD // LANES):
        sl = slice(p * LANES, (p + 1) * LANES)
        qi_o[0, :, sl] = (rope64(bqi_ref[0, :, sl]) * (I_HD ** -0.5)).astype(BF16)
    ki = rope64(bki_ref[0, :, 0:LANES])
    ki_o[0] = ki[:, 0:I_HD]
    ki2_o[0] = (ki + pltpu.roll(ki, I_HD, 1)).astype(BF16)


def _rope_table(pos):
    def tab(half):
        inv = jnp.power(ROPE_THETA, -jnp.arange(half, dtype=F32) / half)
        ang = pos.astype(F32)[:, None] * inv[None, :]
        return jnp.cos(ang), jnp.sin(ang)

    c, s = tab(A_HD // 2)
    ci, si = tab(I_HD // 2)
    return jnp.concatenate([c, c, -s, s, ci, ci, ci, ci, -si, si, -si, si], axis=1)


def _prep(proj, rope, gaq, gak, gbq, gbk):
    B, T, _ = proj.shape
    tt = min(T, 256)
    assert T % tt == 0

    def pspec(width, off):
        assert off % width == 0
        return pl.BlockSpec((1, tt, width), lambda b, i: (b, i, off // width))

    def ospec(width):
        return pl.BlockSpec((1, tt, width), lambda b, i: (b, i, 0))

    gspec = pl.BlockSpec((1, LANES), lambda b, i: (0, 0))
    outs = [(1024, BF16), (1024, F32), (1024, BF16), (1024, F32), (1024, BF16),
            (1024, BF16), (256, F32), (256, BF16), (256, F32), (256, BF16),
            (512, BF16), (I_HD, F32), (LANES, BF16)]
    return pl.pallas_call(
        _prep_kernel,
        grid=(B, T // tt),
        in_specs=[pspec(2048, OFF_AQ), pspec(1024, OFF_AV), pspec(1024, OFF_BQ), pspec(512, OFF_BK),
                  pspec(512, OFF_BQI), pspec(256, OFF_BKI),
                  pl.BlockSpec((tt, 4 * LANES), lambda b, i: (i, 0)),
                  gspec, gspec, gspec, gspec],
        out_specs=[ospec(w) for w, _ in outs],
        out_shape=[jax.ShapeDtypeStruct((B, T, w), d) for w, d in outs],
        compiler_params=_cparams(("parallel", "parallel")),
        name="prep",
    )(proj, proj, proj, proj, proj, proj, rope,
      gaq.reshape(1, LANES), gak.reshape(1, LANES), gbq.reshape(1, LANES), gbk.reshape(1, LANES))


def _diff_kernel(q_ref, kn_ref, vn_ref, lam_ref, sub_ref, o_ref, acc0_s, acc1_s, sn_s, *, tq, lam_init):
    acc = (acc0_s, acc1_s)
    i = pl.program_id(2)
    q_pos0 = i * tq
    q = [q_ref[0, :, m * A_HD:(m + 1) * A_HD] for m in range(2)]
    msl = [slice(m * A_HD, (m + 1) * A_HD) for m in range(2)]
    wide = tq % LANES == 0

    def fold(x, op):
        f = x[:, 0:LANES]
        for c in range(1, x.shape[1] // LANES):
            f = op(f, x[:, c * LANES:(c + 1) * LANES])
        return f

    neg = tuple(jnp.full((tq, LANES), NEG, F32) for _ in range(2))
    zero = tuple(jnp.zeros((tq, LANES), F32) for _ in range(2))

    def scores(t, mr):
        r0 = pl.multiple_of(t * tq, tq)
        k = kn_ref[0, pl.ds(r0, tq), :]
        out = []
        for m in range(2):
            s = _dot_nt(q[m], k[:, msl[m]])
            sn_s[m, t] = s
            out.append(jnp.maximum(mr[m], fold(s, jnp.maximum)) if wide else mr[m])
        return tuple(out)
    mn = _tile_loop(i, scores, neg, TILE_UNROLL)

    r0 = pl.multiple_of(i * tq, tq)
    qpos = q_pos0 + lax.broadcasted_iota(jnp.int32, (tq, 1), 0)
    kpos = q_pos0 + lax.broadcasted_iota(jnp.int32, (1, tq), 1)
    vis = (kpos >> CHUNK_SHIFT) <= (qpos >> CHUNK_SHIFT)
    kd = kn_ref[0, pl.ds(r0, tq), :]
    mrow = []
    for m in range(2):
        s = jnp.where(vis, _dot_nt(q[m], kd[:, msl[m]]), NEG)
        sn_s[m, i] = s
        mx = jnp.max(s, axis=-1, keepdims=True)
        if wide:
            mx = jnp.maximum(mx, jnp.max(mn[m], axis=-1, keepdims=True))
        mrow.append(mx)

    for m in range(2):
        acc[m][...] = jnp.zeros(acc[m].shape, F32)

    def weigh(t, lr):
        r0 = pl.multiple_of(t * tq, tq)
        v = vn_ref[0, pl.ds(r0, tq), :]
        out = []
        for m in range(2):
            p = jnp.exp(sn_s[m, t] - mrow[m])
            if wide:
                out.append(lr[m] + fold(p, jnp.add))
            else:
                out.append(lr[m] + jnp.sum(p, axis=-1, keepdims=True))
            acc[m][...] += _dot(p.astype(BF16), v)
        return tuple(out)
    ln = _tile_loop(i + 1, weigh, zero if wide else tuple(jnp.zeros((tq, 1), F32) for _ in range(2)),
                    TILE_UNROLL)
    lrow = [jnp.sum(ln[m], axis=-1, keepdims=True) for m in range(2)]

    lp = lam_ref[...]
    lam = (jnp.exp(jnp.sum(lp[0:1] * lp[1:2], axis=-1, keepdims=True))
           - jnp.exp(jnp.sum(lp[2:3] * lp[3:4], axis=-1, keepdims=True)) + lam_init)
    o = acc[0][...] / lrow[0] - lam * (acc[1][...] / lrow[1])
    o = o * lax.rsqrt(jnp.mean(o * o, axis=-1, keepdims=True) + EPS) * sub_ref[...]
    o_ref[0] = (o * (1.0 - lam_init)).astype(o_ref.dtype)


def _diff_attn(qa, kn, vn, lam_p, subln, lam_init):
    B, T, _ = qa.shape
    tq = min(T, 256)
    assert T % tq == 0 and (tq % CHUNK == 0 or tq == T)
    W = 2 * A_HD
    return pl.pallas_call(
        functools.partial(_diff_kernel, tq=tq, lam_init=lam_init),
        grid=(B, A_HEADS, T // tq),
        in_specs=[pl.BlockSpec((1, tq, W), lambda b, h, i: (b, i, h)),
                  pl.BlockSpec((1, T, W), lambda b, h, i: (b, 0, h)),
                  pl.BlockSpec((1, T, W), lambda b, h, i: (b, 0, h)),
                  pl.BlockSpec((4, A_HD), lambda b, h, i: (0, 0)),
                  pl.BlockSpec((1, A_VD), lambda b, h, i: (0, 0))],
        out_specs=pl.BlockSpec((1, tq, W), lambda b, h, i: (b, i, h)),
        out_shape=jax.ShapeDtypeStruct((B, T, A_HEADS * A_VD), BF16),
        scratch_shapes=[pltpu.VMEM((tq, A_VD), F32), pltpu.VMEM((tq, A_VD), F32),
                        pltpu.VMEM((2, T // tq, tq, tq), F32)],
        compiler_params=_cparams(("parallel", "parallel", "arbitrary")),
        name="diff_attn",
    )(qa, kn, vn, lam_p, subln.reshape(1, A_VD))


def _diff_decode_kernel(q_ref, kn_ref, vn_ref, kp_ref, vp_ref, lam_ref, sub_ref, o_ref, *, P, lam_init):
    T = q_ref.shape[1]
    per_pos = A_HEADS * 2
    qpos = P + lax.broadcasted_iota(jnp.int32, (T, 1), 0)
    kpos = P + lax.broadcasted_iota(jnp.int32, (1, T), 1)
    vis = (kpos >> CHUNK_SHIFT) <= (qpos >> CHUNK_SHIFT)
    lp = lam_ref[...]
    lam = (jnp.exp(jnp.sum(lp[0:1] * lp[1:2], axis=-1, keepdims=True))
           - jnp.exp(jnp.sum(lp[2:3] * lp[3:4], axis=-1, keepdims=True)) + lam_init)
    for h in range(A_HEADS):
        v_past = jnp.concatenate(
            [vp_ref[pl.ds(c * A_HEADS + h, P, stride=per_pos), :] for c in range(A_VD // LANES)],
            axis=-1).astype(BF16)
        v_new = vn_ref[0, :, h * A_VD:(h + 1) * A_VD]
        outs = []
        for m in range(2):
            hm = h * 2 + m
            q = q_ref[0, :, hm * A_HD:(hm + 1) * A_HD]
            k_past = kp_ref[pl.ds(hm, P, stride=per_pos), :].astype(BF16)
            s_p = _dot_nt(q, k_past)
            s_n = jnp.where(vis, _dot_nt(q, kn_ref[0, :, hm * A_HD:(hm + 1) * A_HD]), NEG)
            mx = jnp.maximum(jnp.max(s_p, axis=-1, keepdims=True), jnp.max(s_n, axis=-1, keepdims=True))
            p_p = jnp.exp(s_p - mx)
            p_n = jnp.exp(s_n - mx)
            l = jnp.sum(p_p, axis=-1, keepdims=True) + jnp.sum(p_n, axis=-1, keepdims=True)
            outs.append((_dot(p_p.astype(BF16), v_past) + _dot(p_n.astype(BF16), v_new)) / l)
        o = outs[0] - lam * outs[1]
        o = o * lax.rsqrt(jnp.mean(o * o, axis=-1, keepdims=True) + EPS) * sub_ref[...]
        o_ref[0, :, h * A_VD:(h + 1) * A_VD] = (o * (1.0 - lam_init)).astype(o_ref.dtype)


def _diff_decode(qa, kn, vn, kp_rows, vp_rows, layer, lam_p, subln, P, lam_init):
    B, T, W = qa.shape
    assert P % CHUNK == 0 and kp_rows.shape[2] == P * A_HEADS * 2 and vp_rows.shape[2] == P * A_HEADS * 2
    new_spec = pl.BlockSpec((1, T, W), lambda b: (b, 0, 0))
    cache_spec = pl.BlockSpec((None, None, P * A_HEADS * 2, LANES), lambda b: (layer, b, 0, 0))
    return pl.pallas_call(
        functools.partial(_diff_decode_kernel, P=P, lam_init=lam_init),
        grid=(B,),
        in_specs=[new_spec, new_spec, new_spec, cache_spec, cache_spec,
                  pl.BlockSpec((4, A_HD), lambda b: (0, 0)),
                  pl.BlockSpec((1, A_VD), lambda b: (0, 0))],
        out_specs=new_spec,
        out_shape=jax.ShapeDtypeStruct((B, T, W), BF16),
        compiler_params=_cparams(("parallel",)),
        name="diff_decode",
    )(qa, kn, vn, kp_rows, vp_rows, lam_p, subln.reshape(1, A_VD))


def _dsa_kernel(qi_ref, wi_ref, qb_ref, ki_ref, kb_ref, vb_ref, o_ref,
                key_s, bias_s, s_s, hi_s, lo_s, *, P, L, tq, tk, topk):
    i = pl.program_id(1)
    q_pos0 = P + i * tq
    Lp = ki_ref.shape[1]
    ncol = jnp.minimum(L, ((q_pos0 + tq - 1) // CHUNK + 1) * CHUNK)
    nt = (ncol + tk - 1) // tk
    qpos = q_pos0 + lax.broadcasted_iota(jnp.int32, (tq, 1), 0)
    grp = B_HEADS // B_KV
    lane = lax.broadcasted_iota(jnp.int32, (tq, LANES), 1)
    wi = wi_ref[0] * (I_HEADS ** -0.5)

    def vis_of(t):
        kpos = t * tk + lax.broadcasted_iota(jnp.int32, (1, tk), 1)
        return ((kpos >> CHUNK_SHIFT) <= (qpos >> CHUNK_SHIFT)) & (kpos < L)

    def score_tile(t, carry):
        c0 = pl.multiple_of(t * tk, tk)
        kit = ki_ref[0, pl.ds(c0, tk), :]
        sc = jnp.zeros((tq, tk), F32)
        for pr in range(I_HEADS * I_HD // LANES):
            qpair = qi_ref[0, :, pr * LANES:(pr + 1) * LANES]
            for half in range(LANES // I_HD):
                h = pr * (LANES // I_HD) + half
                qh = jnp.where((lane >> I_HD_SHIFT) == half, qpair, jnp.zeros_like(qpair))
                isc = _dot_nt(qh, kit)
                sc = sc + jnp.maximum(isc, 0.0) * wi[:, h:h + 1]
        sc = jnp.where(vis_of(t), sc, -jnp.inf)
        bits = pltpu.bitcast(sc, jnp.int32)
        bits = jnp.where(bits == KEY_MIN, 0, bits)
        key_s[t] = bits ^ ((bits >> 31) & 0x7FFFFFFF)
        return carry

    lax.fori_loop(0, nt, score_tile, 0)

    def count(pred_fn):
        def body(t, acc):
            hit = jnp.where(pred_fn(key_s[t]), 1, 0)
            part = hit[:, 0:LANES]
            for c in range(1, tk // LANES):
                part = part + hit[:, c * LANES:(c + 1) * LANES]
            return acc + part
        acc = lax.fori_loop(0, nt, body, jnp.zeros((tq, LANES), jnp.int32))
        return jnp.sum(acc, axis=-1, keepdims=True)

    I16 = jnp.int16
    HALF_MIN = -(2 ** 15)

    def count16(ref, cand):
        c16 = jnp.broadcast_to(cand, (tq, LANES)).astype(I16)

        def body(t, acc):
            blk = ref[t]
            for c in range(tk // LANES):
                acc = acc + jnp.where(blk[:, c * LANES:(c + 1) * LANES] >= c16, I16(1), I16(0))
            return acc
        acc = lax.fori_loop(0, nt, body, jnp.zeros((tq, LANES), I16))
        return jnp.sum(acc.astype(jnp.int32), axis=-1, keepdims=True)

    def bisect16(ref, want):
        def bit_body(bi, thr):
            cand = thr + jnp.left_shift(jnp.int32(1), 15 - bi)
            return jnp.where(count16(ref, cand) >= want, cand, thr)
        return lax.fori_loop(0, 16, bit_body, jnp.full((tq, 1), HALF_MIN, jnp.int32))

    def select(_):
        def split(t, c):
            kk = key_s[t]
            hi_s[t] = (kk >> 16).astype(I16)
            return c
        lax.fori_loop(0, nt, split, 0)
        t_hi = bisect16(hi_s, topk)
        above = count16(hi_s, t_hi + 1)
        t_hi16 = jnp.broadcast_to(t_hi, (tq, LANES)).astype(I16)

        def low(t, c):
            kk = key_s[t]
            lo = ((kk & 0xFFFF) + HALF_MIN).astype(I16)
            hi = hi_s[t]
            for cb in range(tk // LANES):
                sl = slice(cb * LANES, (cb + 1) * LANES)
                lo_s[t, :, sl] = jnp.where(hi[:, sl] == t_hi16, lo[:, sl], I16(HALF_MIN))
            return c
        lax.fori_loop(0, nt, low, 0)
        t_lo = bisect16(lo_s, topk - above)
        return (t_hi << 16) + (t_lo - HALF_MIN)

    thr = lax.cond(ncol > topk, select, lambda _: jnp.full((tq, 1), KEY_MIN, jnp.int32), 0)
    n_gt = count(lambda kk: kk > thr)
    n_eq = count(lambda kk: kk == thr)
    need = topk - n_gt
    neg_inf_key = jnp.int32(KEY_MIN + 0x7FFFFF)
    tie_break = jnp.max(jnp.where((n_eq > need) & (thr != neg_inf_key), 1.0, 0.0)) > 0.5

    def bias_fast(_):
        def body(t, c):
            sel = (key_s[t] >= thr) & vis_of(t)
            bias_s[t] = jnp.where(sel, 0.0, NEG)
            return c
        lax.fori_loop(0, nt, body, 0)
        return 0

    def bias_ties(_):
        r = lax.broadcasted_iota(jnp.int32, (LANES, LANES), 0)
        c = lax.broadcasted_iota(jnp.int32, (LANES, LANES), 1)
        before = jnp.where(r < c, 1.0, 0.0).astype(BF16)

        def body(t, seen):
            kk = key_s[t]
            vis = vis_of(t)
            for cb in range(tk // LANES):
                sl = slice(cb * LANES, (cb + 1) * LANES)
                eq = kk[:, sl] == thr
                eqf = jnp.where(eq, 1.0, 0.0)
                rank = seen + _dot(eqf.astype(BF16), before)
                sel = ((kk[:, sl] > thr) | (eq & (rank < need.astype(F32)))) & vis[:, sl]
                bias_s[t, :, sl] = jnp.where(sel, 0.0, NEG)
                seen = seen + jnp.sum(eqf, axis=-1, keepdims=True)
            return seen
        lax.fori_loop(0, nt, body, jnp.zeros((tq, 1), F32))
        return 0

    lax.cond(tie_break, bias_ties, bias_fast, 0)

    def fold(x, op):
        f = x[:, 0:LANES]
        for c in range(1, tk // LANES):
            f = op(f, x[:, c * LANES:(c + 1) * LANES])
        return f

    for n in range(B_KV):
        kv = slice(n * B_HD, (n + 1) * B_HD)

        def scores(t, mrun, kv=kv, n=n):
            c0 = pl.multiple_of(t * tk, tk)
            bias = bias_s[t]
            k = kb_ref[0, pl.ds(c0, tk), kv]
            out = []
            for g in range(grp):
                h = n * grp + g
                s = _dot_nt(qb_ref[0, :, h * B_HD:(h + 1) * B_HD], k) + bias
                s_s[g, t] = s
                out.append(jnp.maximum(mrun[g], fold(s, jnp.maximum)))
            return tuple(out)

        mrun = _tile_loop(nt, scores, tuple(jnp.full((tq, LANES), NEG, F32) for _ in range(grp)), 2)
        mrow = [jnp.max(m, axis=-1, keepdims=True) for m in mrun]

        def weigh(t, carry, kv=kv, mrow=mrow):
            lrun, acc = carry
            c0 = pl.multiple_of(t * tk, tk)
            v = vb_ref[0, pl.ds(c0, tk), kv]
            lnew, anew = [], []
            for g in range(grp):
                p = jnp.exp(s_s[g, t] - mrow[g])
                lnew.append(lrun[g] + fold(p, jnp.add))
                anew.append(acc[g] + _dot(p.astype(BF16), v))
            return tuple(lnew), tuple(anew)

        lrun, acc = _tile_loop(
            nt, weigh,
            (tuple(jnp.zeros((tq, LANES), F32) for _ in range(grp)),
             tuple(jnp.zeros((tq, B_HD), F32) for _ in range(grp))), 2)
        for g in range(grp):
            h = n * grp + g
            l = jnp.sum(lrun[g], axis=-1, keepdims=True)
            o_ref[0, :, h * B_HD:(h + 1) * B_HD] = (acc[g] / l).astype(o_ref.dtype)


def _dsa(qi, proj, qb, ki2, kb, vb, P, L, tk):
    B, T, _ = qb.shape
    Lp = ki2.shape[1]
    tq = min(T, 128)
    assert T % tq == 0 and Lp % tk == 0 and tk % LANES == 0
    topk = min(TOPK_MAX, L // 4)
    kern = functools.partial(_dsa_kernel, P=P, L=L, tq=tq, tk=tk, topk=topk)
    return pl.pallas_call(
        kern,
        grid=(B, T // tq),
        in_specs=[pl.BlockSpec((1, tq, I_HEADS * I_HD), lambda b, i: (b, i, 0)),
                  pl.BlockSpec((1, tq, LANES), lambda b, i: (b, i, OFF_SMALL // LANES)),
                  pl.BlockSpec((1, tq, B_HEADS * B_HD), lambda b, i: (b, i, 0)),
                  pl.BlockSpec((1, Lp, LANES), lambda b, i: (b, 0, 0)),
                  pl.BlockSpec((1, Lp, B_KV * B_HD), lambda b, i: (b, 0, 0)),
                  pl.BlockSpec((1, Lp, B_KV * B_HD), lambda b, i: (b, 0, 0))],
        out_specs=pl.BlockSpec((1, tq, B_HEADS * B_HD), lambda b, i: (b, i, 0)),
        out_shape=jax.ShapeDtypeStruct((B, T, B_HEADS * B_HD), BF16),
        scratch_shapes=[pltpu.VMEM((Lp // tk, tq, tk), jnp.int32), pltpu.VMEM((Lp // tk, tq, tk), F32),
                        pltpu.VMEM((B_HEADS // B_KV, Lp // tk, tq, tk), F32),
                        pltpu.VMEM((Lp // tk, tq, tk), jnp.int16), pltpu.VMEM((Lp // tk, tq, tk), jnp.int16)],
        compiler_params=_cparams(("parallel", "arbitrary")),
        name="dsa",
    )(qi, proj, qb, ki2, kb, vb)


def _float_order_key(x):
    bits = pltpu.bitcast(x, jnp.int32)
    bits = jnp.where(bits == KEY_MIN, 0, bits)
    return bits ^ ((bits >> 31) & 0x7FFFFFFF)


def _dsa_decode_kernel(qi_ref, wi_ref, qb_ref, kin_ref, kbn_ref, vbn_ref, kip_ref, kp_ref, vp_ref, o_ref,
                       *, P, topk):
    T = qi_ref.shape[1]
    grp = B_HEADS // B_KV
    qpos = P + lax.broadcasted_iota(jnp.int32, (T, 1), 0)
    kpos = P + lax.broadcasted_iota(jnp.int32, (1, T), 1)
    vis_n = (kpos >> CHUNK_SHIFT) <= (qpos >> CHUNK_SHIFT)
    wi = wi_ref[0] * (I_HEADS ** -0.5)
    ki_pT = kip_ref[...].astype(BF16)
    ki_n = kin_ref[0, :, 0:I_HD]

    sc_p = jnp.zeros((T, P), F32)
    sc_n = jnp.zeros((T, T), F32)
    for h in range(I_HEADS):
        qh = qi_ref[0, :, h * I_HD:(h + 1) * I_HD]
        w = wi[:, h:h + 1]
        sc_p = sc_p + jnp.maximum(_dot(qh, ki_pT), 0.0) * w
        sc_n = sc_n + jnp.maximum(_dot_nt(qh, ki_n), 0.0) * w
    key_p = _float_order_key(sc_p)
    key_n = _float_order_key(jnp.where(vis_n, sc_n, -jnp.inf))

    def count(pred):
        return (jnp.sum(jnp.where(pred(key_p), 1, 0), axis=-1, keepdims=True)
                + jnp.sum(jnp.where(pred(key_n), 1, 0), axis=-1, keepdims=True))

    def bit_body(bi, thr):
        cand = thr + jnp.left_shift(jnp.int32(1), 31 - bi)
        return jnp.where(count(lambda kk: kk >= cand) >= topk, cand, thr)

    thr = lax.fori_loop(0, 32, bit_body, jnp.full((T, 1), KEY_MIN, jnp.int32))
    need = (topk - count(lambda kk: kk > thr)).astype(F32)

    def before(n):
        r = lax.broadcasted_iota(jnp.int32, (n, n), 0)
        c = lax.broadcasted_iota(jnp.int32, (n, n), 1)
        return jnp.where(r < c, 1.0, 0.0).astype(BF16)

    def select(kk, seen, pre):
        eq = kk == thr
        eqf = jnp.where(eq, 1.0, 0.0)
        rank = seen + _dot(eqf.astype(BF16), pre)
        sel = (kk > thr) | (eq & (rank < need))
        return sel, seen + jnp.sum(eqf, axis=-1, keepdims=True)

    pre_l = before(LANES)
    seen = jnp.zeros((T, 1), F32)
    bias_blocks = []
    for cb in range(P // LANES):
        sel, seen = select(key_p[:, cb * LANES:(cb + 1) * LANES], seen, pre_l)
        bias_blocks.append(jnp.where(sel, 0.0, NEG))
    bias_p = jnp.concatenate(bias_blocks, axis=-1)
    sel, _ = select(key_n, seen, before(T))
    bias_n = jnp.where(sel & vis_n, 0.0, NEG)

    for n in range(B_KV):
        kv = slice(n * B_HD, (n + 1) * B_HD)
        k_p = kp_ref[pl.ds(n, P, stride=B_KV), :].astype(BF16)
        v_p = vp_ref[pl.ds(n, P, stride=B_KV), :].astype(BF16)
        k_n = kbn_ref[0, :, kv]
        v_n = vbn_ref[0, :, kv]
        for g in range(grp):
            h = n * grp + g
            q = qb_ref[0, :, h * B_HD:(h + 1) * B_HD]
            s_p = _dot_nt(q, k_p) + bias_p
            s_n = _dot_nt(q, k_n) + bias_n
            mx = jnp.maximum(jnp.max(s_p, axis=-1, keepdims=True), jnp.max(s_n, axis=-1, keepdims=True))
            p_p = jnp.exp(s_p - mx)
            p_n = jnp.exp(s_n - mx)
            l = jnp.sum(p_p, axis=-1, keepdims=True) + jnp.sum(p_n, axis=-1, keepdims=True)
            o = _dot(p_p.astype(BF16), v_p) + _dot(p_n.astype(BF16), v_n)
            o_ref[0, :, h * B_HD:(h + 1) * B_HD] = (o / l).astype(o_ref.dtype)


def _dsa_decode(qi, proj, qb, ki2, kbb, vbb, ki_past_t, kp_rows, vp_rows, layer):
    B, T, _ = qb.shape
    P = ki_past_t.shape[3]
    assert P % CHUNK == 0 and P % LANES == 0 and kp_rows.shape[2] == P * B_KV
    topk = min(TOPK_MAX, (P + T) // 4)

    def new(width, blk=0):
        return pl.BlockSpec((1, T, width), lambda b: (b, 0, blk))

    rows = pl.BlockSpec((None, None, P * B_KV, LANES), lambda b: (layer, b, 0, 0))
    return pl.pallas_call(
        functools.partial(_dsa_decode_kernel, P=P, topk=topk),
        grid=(B,),
        in_specs=[new(I_HEADS * I_HD), new(LANES, OFF_SMALL // LANES), new(B_HEADS * B_HD),
                  new(LANES), new(B_KV * B_HD), new(B_KV * B_HD),
                  pl.BlockSpec((None, None, I_HD, P), lambda b: (layer, b, 0, 0)), rows, rows],
        out_specs=new(B_HEADS * B_HD),
        out_shape=jax.ShapeDtypeStruct((B, T, B_HEADS * B_HD), BF16),
        compiler_params=_cparams(("parallel",)),
        name="dsa_decode",
    )(qi, proj, qb, ki2, kbb, vbb, ki_past_t, kp_rows, vp_rows)


def _bmm(a, b):
    return jnp.einsum('bij,bjk->bik', a.astype(BF16), b.astype(BF16), preferred_element_type=F32)


def _bmm_nt(a, b):
    return jnp.einsum('bik,bjk->bij', a.astype(BF16), b.astype(BF16), preferred_element_type=F32)


def _gdn_intra_kernel(x_ref, halo_ref, sm_ref, cw_ref, cbuf_ref, alog_ref, dtb_ref,
                      u_o, w_o, qt_o, ktT_o, qkg_o, dec_o, buf_s, *, cc, nck):
    i = pl.program_id(1)
    R = nck * cc
    HW = C_HEADS * C_HD

    @pl.when(i == 0)
    def _():
        buf_s[0:SUBLANES, :] = jnp.zeros((SUBLANES, C_QKV), F32)
        buf_s[SUBLANES - (C_CONV - 1):SUBLANES, :] = cbuf_ref[0]

    @pl.when(i > 0)
    def _():
        buf_s[0:SUBLANES, :] = halo_ref[0]

    buf_s[SUBLANES:SUBLANES + R, :] = x_ref[0]
    y = None
    for jw in range(C_CONV):
        off = SUBLANES - (C_CONV - 1) + jw
        term = buf_s[off:off + R, :] * cw_ref[jw:jw + 1, :]
        y = term if y is None else y + term
    y = y * jax.nn.sigmoid(y)

    sm = sm_ref[0]
    xg = sm + dtb_ref[...]
    softplus = jnp.maximum(xg, 0.0) + jnp.log(1.0 + jnp.exp(-jnp.abs(xg)))
    g = -jnp.exp(alog_ref[...]) * softplus
    beta = jax.nn.sigmoid(sm)

    cshift = cc.bit_length() - 1
    rr = lax.broadcasted_iota(jnp.int32, (R, R), 0)
    rc = lax.broadcasted_iota(jnp.int32, (R, R), 1)
    cum = ((rr >> cshift) == (rc >> cshift)) & (rr >= rc)
    G = jnp.dot(jnp.where(cum, 1.0, 0.0), g, preferred_element_type=F32,
                precision=lax.Precision.HIGHEST)

    qs, ks, vs, gcs, bcs = [], [], [], [], []
    for ck in range(nck):
        rows = slice(ck * cc, (ck + 1) * cc)
        for h in range(C_HEADS):
            qs.append(y[rows, h * C_HD:(h + 1) * C_HD])
            ks.append(y[rows, HW + h * C_HD:HW + (h + 1) * C_HD])
            vs.append(y[rows, 2 * HW + h * C_HD:2 * HW + (h + 1) * C_HD])
            gcs.append(G[rows, CA_LANE + h:CA_LANE + h + 1])
            bcs.append(beta[rows, CB_LANE + h:CB_LANE + h + 1])
    q3 = jnp.stack(qs)
    k3 = jnp.stack(ks)
    v3 = jnp.stack(vs)
    Gc = jnp.stack(gcs)
    bc = jnp.stack(bcs)
    q3 = q3 * lax.rsqrt(jnp.sum(q3 * q3, axis=-1, keepdims=True) + EPS) * (C_HD ** -0.5)
    k3 = k3 * lax.rsqrt(jnp.sum(k3 * k3, axis=-1, keepdims=True) + EPS)

    ri = lax.broadcasted_iota(jnp.int32, (cc, cc), 0)
    ci = lax.broadcasted_iota(jnp.int32, (cc, cc), 1)
    incl = (ri >= ci)[None]
    strict = (ri > ci)[None]
    eye = (ri == ci)[None]
    bdiag = ((ri // SOLVE_BLOCK) == (ci // SOLVE_BLOCK))[None]

    Gr = jnp.sum(jnp.where(eye, Gc, 0.0), axis=1, keepdims=True)
    gam = jnp.where(incl, jnp.exp(jnp.where(incl, Gc - Gr, 0.0)), 0.0)
    k16 = k3.astype(BF16)
    A = jnp.where(strict, bc * _bmm_nt(k16, k16) * gam, 0.0)
    Nd = jnp.where(bdiag, -A, 0.0)
    E = jnp.where(bdiag, 0.0, A)
    N2 = _bmm(Nd, Nd)
    N4 = _bmm(N2, N2)
    N8 = _bmm(N4, N4)
    Q = Nd + N2 + _bmm(Nd, N2)
    Q = Q + N4 + _bmm(Q, N4)
    Q = Q + N8 + _bmm(Q, N8)
    Fm = E + _bmm(Q, E)
    F2 = _bmm(Fm, Fm)
    Rm = F2 - Fm - _bmm(Fm, F2)
    Wm = Rm + Q + _bmm(Rm, Q)
    eG = jnp.exp(Gc)
    rhs = jnp.concatenate([bc * v3, (bc * eG) * k3], axis=-1)
    sol = rhs + _bmm(Wm, rhs)
    qkg = _bmm_nt(q3, k16) * gam
    gl = Gc[:, cc - 1:cc, :]
    qt = q3 * eG
    kt = k3 * jnp.exp(gl - Gc)
    dec = jnp.exp(gl)

    for ck in range(nck):
        rows = slice(ck * cc, (ck + 1) * cc)
        for h in range(C_HEADS):
            b = ck * C_HEADS + h
            cols = slice(h * C_HD, (h + 1) * C_HD)
            u_o[0, rows, cols] = sol[b, :, 0:C_HD]
            w_o[0, rows, cols] = sol[b, :, C_HD:2 * C_HD].astype(BF16)
            qt_o[0, rows, cols] = qt[b].astype(BF16)
            ktT_o[0, ck, h] = kt[b].T.astype(BF16)
            qkg_o[0, ck, h] = qkg[b].astype(BF16)
            dec_o[0, ck, h] = jnp.broadcast_to(dec[b], (1, C_HD))


def _gdn_inter_kernel(u_ref, w_ref, qt_ref, ktT_ref, qkg_ref, dec_ref, z_ref, s0_ref, gn_ref,
                      o_ref, sfin_ref, st_s, *, cc, nck):
    c = pl.program_id(1)

    @pl.when(c == 0)
    def _():
        st_s[...] = s0_ref[0]

    S = st_s[...]
    for ck in range(nck):
        rows = slice(ck * cc, (ck + 1) * cc)

        def heads(ref):
            return jnp.stack([ref[0, rows, h * C_HD:(h + 1) * C_HD] for h in range(C_HEADS)])

        S16 = S.astype(BF16)
        v_new = heads(u_ref) - _bmm(heads(w_ref), S16)
        vn16 = v_new.astype(BF16)
        o = _bmm(heads(qt_ref), S16) + _bmm(qkg_ref[0, ck], vn16)
        S = dec_ref[0, ck] * S + _bmm(ktT_ref[0, ck], vn16)
        on = o * lax.rsqrt(jnp.mean(o * o, axis=-1, keepdims=True) + EPS) * gn_ref[...]
        for h in range(C_HEADS):
            cols = slice(h * C_HD, (h + 1) * C_HD)
            z = z_ref[0, rows, cols]
            o_ref[0, rows, cols] = (on[h] * (z * jax.nn.sigmoid(z))).astype(o_ref.dtype)
    st_s[...] = S

    @pl.when(c == pl.num_programs(1) - 1)
    def _():
        sfin_ref[0] = S


def _gdn(proj, conv_w, cbuf, S0, a_log, dt_bias, out_norm):
    B, T, _ = proj.shape
    cc = min(CHUNK, T)
    assert T % cc == 0 and cc % SOLVE_BLOCK == 0 and cc // SOLVE_BLOCK <= 4 and cc >= SUBLANES
    assert cc & (cc - 1) == 0
    HW = C_HEADS * C_HD
    NC = T // cc
    nck_a = 2 if NC % 2 == 0 else 1
    nck_b = 4 if NC % 4 == 0 else 1

    def at_ca(p):
        return jnp.zeros((1, LANES), F32).at[0, CA_LANE:CA_LANE + C_HEADS].set(p.astype(F32))

    Ra = nck_a * cc
    hb = Ra // SUBLANES
    u, w, qt, ktT, qkg, dec = pl.pallas_call(
        functools.partial(_gdn_intra_kernel, cc=cc, nck=nck_a),
        grid=(B, NC // nck_a),
        in_specs=[pl.BlockSpec((1, Ra, C_QKV), lambda b, i: (b, i, OFF_CQKV // C_QKV)),
                  pl.BlockSpec((1, SUBLANES, C_QKV),
                               lambda b, i: (b, jnp.maximum(i * hb - 1, 0), OFF_CQKV // C_QKV)),
                  pl.BlockSpec((1, Ra, LANES), lambda b, i: (b, i, OFF_SMALL // LANES)),
                  pl.BlockSpec((C_CONV, C_QKV), lambda b, i: (0, 0)),
                  pl.BlockSpec((1, C_CONV - 1, C_QKV), lambda b, i: (b, 0, 0)),
                  pl.BlockSpec((1, LANES), lambda b, i: (0, 0)),
                  pl.BlockSpec((1, LANES), lambda b, i: (0, 0))],
        out_specs=[pl.BlockSpec((1, Ra, HW), lambda b, i: (b, i, 0)),
                   pl.BlockSpec((1, Ra, HW), lambda b, i: (b, i, 0)),
                   pl.BlockSpec((1, Ra, HW), lambda b, i: (b, i, 0)),
                   pl.BlockSpec((1, nck_a, C_HEADS, C_HD, cc), lambda b, i: (b, i, 0, 0, 0)),
                   pl.BlockSpec((1, nck_a, C_HEADS, cc, cc), lambda b, i: (b, i, 0, 0, 0)),
                   pl.BlockSpec((1, nck_a, C_HEADS, 1, C_HD), lambda b, i: (b, i, 0, 0, 0))],
        out_shape=[jax.ShapeDtypeStruct((B, T, HW), F32),
                   jax.ShapeDtypeStruct((B, T, HW), BF16),
                   jax.ShapeDtypeStruct((B, T, HW), BF16),
                   jax.ShapeDtypeStruct((B, NC, C_HEADS, C_HD, cc), BF16),
                   jax.ShapeDtypeStruct((B, NC, C_HEADS, cc, cc), BF16),
                   jax.ShapeDtypeStruct((B, NC, C_HEADS, 1, C_HD), F32)],
        scratch_shapes=[pltpu.VMEM((Ra + SUBLANES, C_QKV), F32)],
        compiler_params=_cparams(("parallel", "parallel")),
        name="gdn_intra",
    )(proj, proj, proj, conv_w, cbuf, at_ca(a_log), at_ca(dt_bias))

    Rb = nck_b * cc
    return pl.pallas_call(
        functools.partial(_gdn_inter_kernel, cc=cc, nck=nck_b),
        grid=(B, NC // nck_b),
        in_specs=[pl.BlockSpec((1, Rb, HW), lambda b, c: (b, c, 0)),
                  pl.BlockSpec((1, Rb, HW), lambda b, c: (b, c, 0)),
                  pl.BlockSpec((1, Rb, HW), lambda b, c: (b, c, 0)),
                  pl.BlockSpec((1, nck_b, C_HEADS, C_HD, cc), lambda b, c: (b, c, 0, 0, 0)),
                  pl.BlockSpec((1, nck_b, C_HEADS, cc, cc), lambda b, c: (b, c, 0, 0, 0)),
                  pl.BlockSpec((1, nck_b, C_HEADS, 1, C_HD), lambda b, c: (b, c, 0, 0, 0)),
                  pl.BlockSpec((1, Rb, HW), lambda b, c: (b, c, OFF_CZ // HW)),
                  pl.BlockSpec((1, C_HEADS, C_HD, C_HD), lambda b, c: (b, 0, 0, 0)),
                  pl.BlockSpec((1, C_HD), lambda b, c: (0, 0))],
        out_specs=[pl.BlockSpec((1, Rb, HW), lambda b, c: (b, c, 0)),
                   pl.BlockSpec((1, C_HEADS, C_HD, C_HD), lambda b, c: (b, 0, 0, 0))],
        out_shape=[jax.ShapeDtypeStruct((B, T, HW), BF16),
                   jax.ShapeDtypeStruct((B, C_HEADS, C_HD, C_HD), F32)],
        scratch_shapes=[pltpu.VMEM((C_HEADS, C_HD, C_HD), F32)],
        compiler_params=_cparams(("parallel", "arbitrary")),
        name="gdn_inter",
    )(u, w, qt, ktT, qkg, dec, proj, S0, out_norm.reshape(1, C_HD))


def _ffn_up_kernel(x_ref, g_ref, sh_ref, sc_ref, wg_ref, wu_ref, cwg_ref, cwu_ref, fg_ref, fu_ref,
                   act_o, fng_o, fnu_o, h_ref, *scr, split_rows):
    bb, tt, D = x_ref.shape
    tn = wg_ref.shape[1]
    i = pl.program_id(1)
    j = pl.program_id(2)
    lo = SUBLANES - (FFN_CONV - 1)

    @pl.when(j == 0)
    def _():
        x = x_ref[...]
        y = x * lax.rsqrt(jnp.mean(x * x, axis=-1, keepdims=True) + EPS) * g_ref[...]
        h = y * (1.0 + sc_ref[...]) + sh_ref[...]
        h_ref[...] = h.reshape(bb * tt, D).astype(BF16)

    if split_rows:
        (carry_s,) = scr
        halves = ((wg_ref, cwg_ref, fg_ref, fng_o, 0), (wu_ref, cwu_ref, fu_ref, fnu_o, 1))
        for _, _, f_ref, _, idx in halves:
            @pl.when(i == 0)
            def _(f_ref=f_ref, idx=idx):
                carry_s[idx, j, 0, lo:SUBLANES, :] = f_ref[0]
        prev = [carry_s[idx, j, 0] for idx in range(2)]
        row8 = lax.broadcasted_iota(jnp.int32, (SUBLANES, tn), 0)
        rc = min(tt, FFN_ROW_CHUNK)
        for r in range(tt // rc):
            rows = slice(r * rc, (r + 1) * rc)
            ys = []
            for w_ref, cw_ref, _, _, idx in halves:
                u = _dot(h_ref[rows, :], w_ref[...])
                y = u * cw_ref[FFN_CONV - 1:FFN_CONV, :]
                for k in range(1, FFN_CONV):
                    s = pltpu.roll(u, k, 0)
                    head = jnp.where(row8 < k, pltpu.roll(prev[idx], k, 0), s[0:SUBLANES])
                    s = jnp.concatenate([head, s[SUBLANES:]], axis=0)
                    y = y + s * cw_ref[FFN_CONV - 1 - k:FFN_CONV - k, :]
                prev[idx] = u[rc - SUBLANES:rc]
                ys.append(y)
            act_o[0, rows, :] = (ys[0] * jax.nn.sigmoid(ys[0]) * ys[1]).astype(act_o.dtype)
        for _, _, _, fn_o, idx in halves:
            carry_s[idx, j, 0] = prev[idx]
            fn_o[...] = prev[idx][lo:SUBLANES][None]
    else:
        bg_s, bu_s = scr

        def half(w_ref, cw_ref, f_ref, fn_o, buf):
            u = _dot(h_ref[...], w_ref[...]).reshape(bb, tt, tn)
            buf[:, lo:SUBLANES, :] = f_ref[...]
            buf[:, SUBLANES:SUBLANES + tt, :] = u
            fn_o[...] = u[:, tt - (FFN_CONV - 1):tt, :]
            y = None
            for jw in range(FFN_CONV):
                term = buf[:, lo + jw:lo + jw + tt, :] * cw_ref[jw:jw + 1, :]
                y = term if y is None else y + term
            return y

        yg = half(wg_ref, cwg_ref, fg_ref, fng_o, bg_s)
        yu = half(wu_ref, cwu_ref, fu_ref, fnu_o, bu_s)
        act_o[...] = (yg * jax.nn.sigmoid(yg) * yu).astype(act_o.dtype)


def _ffn_up(x, g, shift, scale, w_up, conv_w, fbuf):
    B, T, D = x.shape
    FF = w_up.shape[1] // 2
    bb, tt = _row_tiles(B, T)
    tn = COL_TILE
    assert FF % tn == 0 and tt >= SUBLANES
    nj = FF // tn
    split_rows = T > tt
    assert not split_rows or bb == 1
    nf = FFN_CONV - 1

    def cols(off, rows):
        return pl.BlockSpec(rows + (tn,), lambda b, i, j: (0,) * len(rows) + (off + j,))

    def fb(off):
        return pl.BlockSpec((bb, nf, tn), lambda b, i, j: (b, 0, off + j))

    act, fng, fnu = pl.pallas_call(
        functools.partial(_ffn_up_kernel, split_rows=split_rows),
        grid=(B // bb, T // tt, nj),
        in_specs=[pl.BlockSpec((bb, tt, D), lambda b, i, j: (b, i, 0)),
                  pl.BlockSpec((1, 1, D), lambda b, i, j: (0, 0, 0)),
                  pl.BlockSpec((bb, 1, D), lambda b, i, j: (b, 0, 0)),
                  pl.BlockSpec((bb, 1, D), lambda b, i, j: (b, 0, 0)),
                  cols(0, (D,)), cols(nj, (D,)), cols(0, (FFN_CONV,)), cols(nj, (FFN_CONV,)),
                  fb(0), fb(nj)],
        out_specs=[pl.BlockSpec((bb, tt, tn), lambda b, i, j: (b, i, j)),
                   pl.BlockSpec((bb, None, nf, tn), lambda b, i, j: (b, i, 0, j)),
                   pl.BlockSpec((bb, None, nf, tn), lambda b, i, j: (b, i, 0, j))],
        out_shape=[jax.ShapeDtypeStruct((B, T, FF), BF16),
                   jax.ShapeDtypeStruct((B, T // tt, nf, FF), F32),
                   jax.ShapeDtypeStruct((B, T // tt, nf, FF), F32)],
        scratch_shapes=[pltpu.VMEM((bb * tt, D), BF16)] + (
            [pltpu.VMEM((2, nj, 1, SUBLANES, tn), F32)] if split_rows else
            [pltpu.VMEM((bb, tt + SUBLANES, tn), F32), pltpu.VMEM((bb, tt + SUBLANES, tn), F32)]),
        compiler_params=_cparams(("parallel", "arbitrary", "arbitrary")),
        name="ffn_up",
    )(x, g.reshape(1, 1, D), shift, scale, w_up, w_up, conv_w, conv_w, fbuf, fbuf)
    return act, jnp.concatenate([fng[:, -1], fnu[:, -1]], axis=-1)


def _prep_w_in(w_in):
    D = w_in.shape[0]
    sizes = (1024, 1024, 1024, 1024, 256, 256, 512, 64, 8, C_QKV, 8, 8, 1024, N_BRANCH * D)
    offs = np.concatenate([[0], np.cumsum(sizes)])
    (aq, ak, av, bq, bk, bv, bqi, bki, bwi, cqkv, ca, cb, cz, gt) = [
        w_in[:, int(offs[k]):int(offs[k + 1])] for k in range(len(sizes))]

    def z(n):
        return jnp.zeros((D, n), w_in.dtype)

    cols = [cqkv, cz, aq, ak, av, bq, bk, bv, bqi, bki, z(LANES - I_HD), bwi, ca, cb, z(LANES - 24),
            z(OFF_GT - OFF_SMALL - LANES), gt]
    w = jnp.concatenate(cols, axis=1).astype(BF16)
    assert w.shape[1] == N_PROJ
    return w


def _layer(x, mod, past, lw, layer_idx):
    kA_p, vA_p, kB_p, vB_p, kI_p, cbuf, S0, fbuf = past
    B, T, D = x.shape
    P = 0 if kI_p is None else kI_p.shape[3]
    sh1, sc1, g1, sh2, sc2, g2 = [m.reshape(B, 1, D) for m in jnp.split(mod, 6, axis=-1)]

    proj = _mm_norm(x, lw['norm_mix'], sh1, sc1, lw['w_in'])
    rope = _rope_table(P + jnp.arange(T, dtype=jnp.int32))
    (qa, ka, kab, va, vab, qb, kb, kbb, vb, vbb, qi, ki, ki2) = _prep(
        proj, rope, lw['a_q_norm'], lw['a_k_norm'], lw['b_q_norm'], lw['b_k_norm'])

    lam_init = 0.8 - 0.6 * math.exp(-0.3 * layer_idx)
    if P:
        oa = _diff_decode(qa, kab, vab, kA_p, vA_p, layer_idx, lw['a_lambda'], lw['a_subln'], P, lam_init)
        ob = _dsa_decode(qi, proj, qb, ki2, kbb, vbb, kI_p, kB_p, vB_p, layer_idx)
    else:
        oa = _diff_attn(qa, kab, vab, lw['a_lambda'], lw['a_subln'], lam_init)
        ob = _dsa(qi, proj, qb, ki2, kbb, vbb, 0, T, min(T, 512))

    oc, S_fin = _gdn(proj, lw['c_conv'], cbuf, S0, lw['c_a_log'], lw['c_dt_bias'], lw['c_out_norm'])
    assert T >= C_CONV - 1 and T >= FFN_CONV - 1
    cbuf_new = proj[:, T - (C_CONV - 1):, OFF_CQKV:OFF_CQKV + C_QKV]

    mixed = _merge(oa, ob, oc, lw['w_branch'], proj, D)
    x = _mm_res(mixed, lw['w_out'], x, g1)

    act, fbuf_new = _ffn_up(x, lw['norm_ffn'], sh2, sc2, lw['w_up'], lw['ffn_conv'], fbuf)
    x = _mm_res(act, lw['w_down'], x, g2, row_target=512)

    new_state = (ka.reshape(B, T, A_HEADS, 2, A_HD), va.reshape(B, T, A_HEADS, A_VD),
                 kb.reshape(B, T, B_KV, B_HD), vb.reshape(B, T, B_KV, B_HD), ki,
                 cbuf_new, S_fin, fbuf_new)
    return x, new_state


def kernel(x_prompt, x_sample, c_prompt, c_sample, cache_diff_k, cache_diff_v, cache_dsa_k, cache_dsa_v, cache_dsa_kidx, state_gdn_conv, state_gdn, state_ffn_conv, w_ada, b_ada, norm_mix, w_in, a_q_norm, a_k_norm, a_lambda, a_subln, b_q_norm, b_k_norm, c_conv, c_a_log, c_dt_bias, c_out_norm, w_branch, w_out, norm_ffn, w_up, ffn_conv, w_down):
    depth = w_in.shape[0]
    bp, bs = x_prompt.shape[0], x_sample.shape[0]
    d_ff2 = w_up.shape[2]
    dt_ = x_prompt.dtype
    prompt_past = (None, None, None, None, None, jnp.zeros((bp, C_CONV - 1, C_QKV), dt_),
                   jnp.zeros((bp, C_HEADS, C_HD, C_HD), dt_), jnp.zeros((bp, FFN_CONV - 1, d_ff2), dt_))
    nrow = -(-(bp + bs) // SUBLANES) * SUBLANES
    c_all = jnp.concatenate([c_prompt, c_sample, jnp.zeros((nrow - bp - bs, c_prompt.shape[1]), dt_)], axis=0)
    past_len = cache_diff_k.shape[2]
    diff_k_rows = cache_diff_k.reshape(depth, bs, past_len * A_HEADS * 2, A_HD)
    diff_v_rows = cache_diff_v.reshape(depth, bs, past_len, A_HEADS, A_VD // LANES, LANES)
    diff_v_rows = diff_v_rows.transpose(0, 1, 2, 4, 3, 5).reshape(depth, bs, past_len * A_HEADS * 2, LANES)
    dsa_k_rows = cache_dsa_k.reshape(depth, bs, past_len * B_KV, B_HD)
    dsa_v_rows = cache_dsa_v.reshape(depth, bs, past_len * B_KV, B_HD)
    dsa_kidx_t = jnp.swapaxes(cache_dsa_kidx, 2, 3)
    xp, xs = x_prompt, x_sample
    prompt_states, sample_states = [], []
    for l in range(depth):
        lw = dict(norm_mix=norm_mix[l], w_in=_prep_w_in(w_in[l]),
                  a_q_norm=a_q_norm[l], a_k_norm=a_k_norm[l], a_lambda=a_lambda[l], a_subln=a_subln[l],
                  b_q_norm=b_q_norm[l], b_k_norm=b_k_norm[l], c_conv=c_conv[l], c_a_log=c_a_log[l],
                  c_dt_bias=c_dt_bias[l], c_out_norm=c_out_norm[l], w_branch=w_branch[l].astype(BF16),
                  w_out=w_out[l].astype(BF16), norm_ffn=norm_ffn[l], w_up=w_up[l].astype(BF16),
                  ffn_conv=ffn_conv[l], w_down=w_down[l].astype(BF16))
        mod = _ada(c_all, w_ada[l], b_ada[l])
        xp, st_p = _layer(xp, mod[:bp], prompt_past, lw, l)
        sample_past = (diff_k_rows, diff_v_rows, dsa_k_rows, dsa_v_rows, dsa_kidx_t,
                       state_gdn_conv[l], state_gdn[l], state_ffn_conv[l])
        xs, st_s = _layer(xs, mod[bp:bp + bs], sample_past, lw, l)
        prompt_states.append(st_p)
        sample_states.append(st_s)
    p_out = [jnp.stack(s, axis=0) for s in zip(*prompt_states)]
    s_out = [jnp.stack(s, axis=0) for s in zip(*sample_states)]
    return (xp, xs, *p_out, *s_out)
```

```python
import functools
import math

import jax
import jax.numpy as jnp
import numpy as np
from jax import lax
from jax.experimental import pallas as pl
from jax.experimental.pallas import tpu as pltpu

F32 = jnp.float32
BF16 = jnp.bfloat16

CHUNK = 64
CHUNK_SHIFT = 6
ROPE_THETA = 10000.0
EPS = 1e-6
A_HEADS, A_HD, A_VD = 4, 128, 256
B_HEADS, B_KV, B_HD = 8, 2, 128
I_HEADS, I_HD = 8, 64
I_HD_SHIFT = 6
TOPK_MAX = 256
C_HEADS, C_HD, C_CONV = 8, 128, 4
C_QKV = 3 * C_HEADS * C_HD
BRANCH_W = 1024
N_BRANCH = 3
FFN_CONV = 3

LANES = 128
SUBLANES = 8
VMEM_LIMIT = 56 * 1024 * 1024
ROW_TILE = 1024
COL_TILE = 512
SOLVE_BLOCK = 16
TILE_UNROLL = 4
FFN_ROW_CHUNK = 256
NEG = -1e30
KEY_MIN = -(2 ** 31)

OFF_CQKV = 0
OFF_CZ = 3072
OFF_AQ = 4096
OFF_AK = 5120
OFF_AV = 6144
OFF_BQ = 7168
OFF_BK = 8192
OFF_BV = 8448
OFF_BQI = 8704
OFF_BKI = 9216
OFF_SMALL = 9344
CA_LANE = 8
CB_LANE = 16
OFF_GT = 9728
N_PROJ = 15872


def _cparams(sem):
    return pltpu.CompilerParams(dimension_semantics=sem, vmem_limit_bytes=VMEM_LIMIT)


def _row_tiles(B, T, target=ROW_TILE):
    tt = min(T, target)
    assert T % tt == 0 and tt % SUBLANES == 0
    bb = max(1, min(B, target // tt))
    while B % bb:
        bb -= 1
    return bb, tt


def _dot(a, b):
    return jnp.dot(a, b, preferred_element_type=F32)


def _dot_nt(a, b):
    return lax.dot_general(a, b, (((1,), (1,)), ((), ())), preferred_element_type=F32)


def _bdot(a, b):
    return _dot(a.astype(BF16), b.astype(BF16))


def _tile_loop(n, body, init, unroll):
    nb = n // unroll

    def trip(tb, c):
        for u in range(unroll):
            c = body(tb * unroll + u, c)
        return c

    c = lax.fori_loop(0, nb, trip, init)
    return lax.fori_loop(nb * unroll, n, body, c)


def _ada_kernel(c_ref, w_ref, b_ref, o_ref):
    c = c_ref[...]
    s = (c * jax.nn.sigmoid(c)).astype(BF16)
    o_ref[...] = _dot(s, w_ref[...].astype(BF16)) + b_ref[...]


def _ada(c, w, b, layer):
    M, D = c.shape
    N = w.shape[2]
    tn = 1024
    return pl.pallas_call(
        _ada_kernel,
        grid=(N // tn,),
        in_specs=[pl.BlockSpec((M, D), lambda j: (0, 0)),
                  pl.BlockSpec((None, D, tn), lambda j: (layer, 0, j)),
                  pl.BlockSpec((None, 1, tn), lambda j: (layer, 0, j))],
        out_specs=pl.BlockSpec((M, tn), lambda j: (0, j)),
        out_shape=jax.ShapeDtypeStruct((M, N), F32),
        compiler_params=_cparams(("parallel",)),
        name="ada",
    )(c, w, b.reshape(-1, 1, N))


def _mm_norm_kernel(x_ref, g_ref, sh_ref, sc_ref, w_ref, o_ref, h_ref):
    bb, tt, D = x_ref.shape
    tn = w_ref.shape[1]

    @pl.when(pl.program_id(2) == 0)
    def _():
        x = x_ref[...]
        y = x * lax.rsqrt(jnp.mean(x * x, axis=-1, keepdims=True) + EPS) * g_ref[...]
        h = y * (1.0 + sc_ref[...]) + sh_ref[...]
        h_ref[...] = h.reshape(bb * tt, D).astype(BF16)

    acc = _dot(h_ref[...], w_ref[...])
    o_ref[...] = acc.reshape(bb, tt, tn).astype(o_ref.dtype)


def _mm_norm(x, g, shift, scale, w, layer, out_dtype=F32):
    B, T, D = x.shape
    N = w.shape[2]
    bb, tt = _row_tiles(B, T)
    tn = COL_TILE
    assert N % tn == 0
    return pl.pallas_call(
        _mm_norm_kernel,
        grid=(B // bb, T // tt, N // tn),
        in_specs=[pl.BlockSpec((bb, tt, D), lambda b, i, j: (b, i, 0)),
                  pl.BlockSpec((1, 1, D), lambda b, i, j: (0, 0, 0)),
                  pl.BlockSpec((bb, 1, D), lambda b, i, j: (b, 0, 0)),
                  pl.BlockSpec((bb, 1, D), lambda b, i, j: (b, 0, 0)),
                  pl.BlockSpec((None, D, tn), lambda b, i, j: (layer, 0, j))],
        out_specs=pl.BlockSpec((bb, tt, tn), lambda b, i, j: (b, i, j)),
        out_shape=jax.ShapeDtypeStruct((B, T, N), out_dtype),
        scratch_shapes=[pltpu.VMEM((bb * tt, D), BF16)],
        compiler_params=_cparams(("parallel", "parallel", "arbitrary")),
        name="mm_norm",
    )(x, g.reshape(1, 1, D), shift, scale, w)


def _mm_res_kernel(a_ref, w_ref, res_ref, g_ref, o_ref):
    bb, tt, K = a_ref.shape
    tn = w_ref.shape[1]
    acc = _dot(a_ref[...].reshape(bb * tt, K), w_ref[...])
    o_ref[...] = res_ref[...] + g_ref[...] * acc.reshape(bb, tt, tn)


def _mm_res(a, w, layer, res, gate, row_target=ROW_TILE):
    B, T, K = a.shape
    N = w.shape[2]
    bb, tt = _row_tiles(B, T, row_target)
    tn = COL_TILE
    assert N % tn == 0
    return pl.pallas_call(
        _mm_res_kernel,
        grid=(B // bb, T // tt, N // tn),
        in_specs=[pl.BlockSpec((bb, tt, K), lambda b, i, j: (b, i, 0)),
                  pl.BlockSpec((None, K, tn), lambda b, i, j: (layer, 0, j)),
                  pl.BlockSpec((bb, tt, tn), lambda b, i, j: (b, i, j)),
                  pl.BlockSpec((bb, 1, tn), lambda b, i, j: (b, 0, j))],
        out_specs=pl.BlockSpec((bb, tt, tn), lambda b, i, j: (b, i, j)),
        out_shape=jax.ShapeDtypeStruct((B, T, N), F32),
        compiler_params=_cparams(("parallel", "parallel", "arbitrary")),
        name="mm_res",
    )(a, w, res, gate)


def _merge_kernel(b0_ref, b1_ref, b2_ref, w_ref, g0_ref, g1_ref, g2_ref, o_ref):
    bb, tt, W = b0_ref.shape
    tn = w_ref.shape[2]
    acc = None
    for n, (br, gr) in enumerate(((b0_ref, g0_ref), (b1_ref, g1_ref), (b2_ref, g2_ref))):
        y = _dot(br[...].reshape(bb * tt, W), w_ref[n])
        t = jax.nn.sigmoid(gr[...].reshape(bb * tt, tn)) * y
        acc = t if acc is None else acc + t
    o_ref[...] = acc.reshape(bb, tt, tn).astype(o_ref.dtype)


def _merge(oa, ob, oc, w_branch, layer, proj, D):
    B, T, W = oa.shape
    bb, tt = _row_tiles(B, T)
    tn = COL_TILE
    gblk = OFF_GT // tn
    nper = D // tn
    br_spec = pl.BlockSpec((bb, tt, W), lambda b, i, j: (b, i, 0))

    def gate_spec(n):
        return pl.BlockSpec((bb, tt, tn), lambda b, i, j: (b, i, gblk + n * nper + j))

    return pl.pallas_call(
        _merge_kernel,
        grid=(B // bb, T // tt, D // tn),
        in_specs=[br_spec, br_spec, br_spec,
                  pl.BlockSpec((None, N_BRANCH, W, tn), lambda b, i, j: (layer, 0, 0, j)),
                  gate_spec(0), gate_spec(1), gate_spec(2)],
        out_specs=pl.BlockSpec((bb, tt, tn), lambda b, i, j: (b, i, j)),
        out_shape=jax.ShapeDtypeStruct((B, T, D), BF16),
        compiler_params=_cparams(("parallel", "parallel", "arbitrary")),
        name="merge",
    )(oa, ob, oc, w_branch, proj, proj, proj)


def _prep_kernel(aqk_ref, av_ref, bq_ref, bkv_ref, bqi_ref, bki_ref, rope_ref,
                 gaq_ref, gak_ref, gbq_ref, gbk_ref,
                 qa_o, ka_o, kab_o, va_o, vab_o, qb_o, kb_o, kbb_o, vb_o, vbb_o,
                 qi_o, ki_o, ki2_o):
    c1 = rope_ref[:, 0:LANES]
    s1 = rope_ref[:, LANES:2 * LANES]
    c2 = rope_ref[:, 2 * LANES:3 * LANES]
    s2 = rope_ref[:, 3 * LANES:4 * LANES]
    tt = c1.shape[0]
    lane = lax.broadcasted_iota(jnp.int32, (tt, LANES), 1)
    low_half = (lane & (I_HD - 1)) < (I_HD // 2)

    def norm_rope(x, g):
        y = x * lax.rsqrt(jnp.mean(x * x, axis=-1, keepdims=True) + EPS) * g
        return y * c1 + pltpu.roll(y, A_HD // 2, 1) * s1

    def rope64(x):
        r = jnp.where(low_half, pltpu.roll(x, LANES - I_HD // 2, 1), pltpu.roll(x, I_HD // 2, 1))
        return x * c2 + r * s2

    gaq, gak, gbq, gbk = gaq_ref[...], gak_ref[...], gbq_ref[...], gbk_ref[...]
    for hm in range(2 * A_HEADS):
        sl = slice(hm * LANES, (hm + 1) * LANES)
        q = norm_rope(aqk_ref[0, :, sl], gaq) * (A_HD ** -0.5)
        qa_o[0, :, sl] = q.astype(BF16)
        k = norm_rope(aqk_ref[0, :, 2 * A_HEADS * LANES + hm * LANES:2 * A_HEADS * LANES + (hm + 1) * LANES], gak)
        ka_o[0, pl.ds(hm, tt, stride=2 * A_HEADS), :] = k
        kab_o[0, :, sl] = k.astype(BF16)
    va = av_ref[0]
    for h in range(A_HEADS):
        for c in range(A_VD // LANES):
            va_o[0, pl.ds(c * A_HEADS + h, tt, stride=2 * A_HEADS), :] = (
                va[:, h * A_VD + c * LANES:h * A_VD + (c + 1) * LANES])
    vab_o[0] = va.astype(BF16)
    for h in range(B_HEADS):
        sl = slice(h * LANES, (h + 1) * LANES)
        q = norm_rope(bq_ref[0, :, sl], gbq) * (B_HD ** -0.5)
        qb_o[0, :, sl] = q.astype(BF16)
    for n in range(B_KV):
        sl = slice(n * LANES, (n + 1) * LANES)
        k = norm_rope(bkv_ref[0, :, sl], gbk)
        kb_o[0, pl.ds(n, tt, stride=B_KV), :] = k
        kbb_o[0, :, sl] = k.astype(BF16)
    vb = bkv_ref[0, :, B_KV * LANES:2 * B_KV * LANES]
    for n in range(B_KV):
        vb_o[0, pl.ds(n, tt, stride=B_KV), :] = vb[:, n * LANES:(n + 1) * LANES]
    vbb_o[0] = vb.astype(BF16)
    for p in range(I_HEADS * I_HD // LANES):
        sl = slice(p * LANES, (p + 1) * LANES)
        qi_o[0, :, sl] = (rope64(bqi_ref[0, :, sl]) * (I_HD ** -0.5)).astype(BF16)
    ki = rope64(bki_ref[0, :, 0:LANES])
    ki_o[0] = ki[:, 0:I_HD]
    ki2_o[0] = (ki + pltpu.roll(ki, I_HD, 1)).astype(BF16)


def _rope_table(pos):
    def tab(half):
        inv = jnp.power(ROPE_THETA, -jnp.arange(half, dtype=F32) / half)
        ang = pos.astype(F32)[:, None] * inv[None, :]
        return jnp.cos(ang), jnp.sin(ang)

    c, s = tab(A_HD // 2)
    ci, si = tab(I_HD // 2)
    return jnp.concatenate([c, c, -s, s, ci, ci, ci, ci, -si, si, -si, si], axis=1)


def _prep(proj, rope, gaq, gak, gbq, gbk):
    B, T, _ = proj.shape
    tt = min(T, 256)
    assert T % tt == 0

    def pspec(width, off):
        assert off % width == 0
        return pl.BlockSpec((1, tt, width), lambda b, i: (b, i, off // width))

    gspec = pl.BlockSpec((1, LANES), lambda b, i: (0, 0))
    ra, rb = 2 * A_HEADS, B_KV
    outs = [(1, 1024, BF16), (ra, LANES, F32), (1, 1024, BF16), (ra, LANES, F32), (1, 1024, BF16),
            (1, 1024, BF16), (rb, LANES, F32), (1, 256, BF16), (rb, LANES, F32), (1, 256, BF16),
            (1, 512, BF16), (1, I_HD, F32), (1, LANES, BF16)]
    return pl.pallas_call(
        _prep_kernel,
        grid=(B, T // tt),
        in_specs=[pspec(2048, OFF_AQ), pspec(1024, OFF_AV), pspec(1024, OFF_BQ), pspec(512, OFF_BK),
                  pspec(512, OFF_BQI), pspec(256, OFF_BKI),
                  pl.BlockSpec((tt, 4 * LANES), lambda b, i: (i, 0)),
                  gspec, gspec, gspec, gspec],
        out_specs=[pl.BlockSpec((1, tt * r, w), lambda b, i: (b, i, 0)) for r, w, _ in outs],
        out_shape=[jax.ShapeDtypeStruct((B, T * r, w), d) for r, w, d in outs],
        compiler_params=_cparams(("parallel", "parallel")),
        name="prep",
    )(proj, proj, proj, proj, proj, proj, rope,
      gaq.reshape(1, LANES), gak.reshape(1, LANES), gbq.reshape(1, LANES), gbk.reshape(1, LANES))


def _diff_kernel(q_ref, kn_ref, vn_ref, lam_ref, sub_ref, o_ref, acc0_s, acc1_s, sn_s, *, tq, lam_init):
    acc = (acc0_s, acc1_s)
    i = pl.program_id(2)
    q_pos0 = i * tq
    q = [q_ref[0, :, m * A_HD:(m + 1) * A_HD] for m in range(2)]
    msl = [slice(m * A_HD, (m + 1) * A_HD) for m in range(2)]
    wide = tq % LANES == 0

    def fold(x, op):
        f = x[:, 0:LANES]
        for c in range(1, x.shape[1] // LANES):
            f = op(f, x[:, c * LANES:(c + 1) * LANES])
        return f

    neg = tuple(jnp.full((tq, LANES), NEG, F32) for _ in range(2))
    zero = tuple(jnp.zeros((tq, LANES), F32) for _ in range(2))

    def scores(t, mr):
        r0 = pl.multiple_of(t * tq, tq)
        k = kn_ref[0, pl.ds(r0, tq), :]
        out = []
        for m in range(2):
            s = _dot_nt(q[m], k[:, msl[m]])
            sn_s[m, t] = s
            out.append(jnp.maximum(mr[m], fold(s, jnp.maximum)) if wide else mr[m])
        return tuple(out)
    mn = _tile_loop(i, scores, neg, TILE_UNROLL)

    r0 = pl.multiple_of(i * tq, tq)
    qpos = q_pos0 + lax.broadcasted_iota(jnp.int32, (tq, 1), 0)
    kpos = q_pos0 + lax.broadcasted_iota(jnp.int32, (1, tq), 1)
    vis = (kpos >> CHUNK_SHIFT) <= (qpos >> CHUNK_SHIFT)
    kd = kn_ref[0, pl.ds(r0, tq), :]
    mrow = []
    for m in range(2):
        s = jnp.where(vis, _dot_nt(q[m], kd[:, msl[m]]), NEG)
        sn_s[m, i] = s
        mx = jnp.max(s, axis=-1, keepdims=True)
        if wide:
            mx = jnp.maximum(mx, jnp.max(mn[m], axis=-1, keepdims=True))
        mrow.append(mx)

    for m in range(2):
        acc[m][...] = jnp.zeros(acc[m].shape, F32)

    def weigh(t, lr):
        r0 = pl.multiple_of(t * tq, tq)
        v = vn_ref[0, pl.ds(r0, tq), :]
        out = []
        for m in range(2):
            p = jnp.exp(sn_s[m, t] - mrow[m])
            if wide:
                out.append(lr[m] + fold(p, jnp.add))
            else:
                out.append(lr[m] + jnp.sum(p, axis=-1, keepdims=True))
            acc[m][...] += _dot(p.astype(BF16), v)
        return tuple(out)
    ln = _tile_loop(i + 1, weigh, zero if wide else tuple(jnp.zeros((tq, 1), F32) for _ in range(2)),
                    TILE_UNROLL)
    lrow = [jnp.sum(ln[m], axis=-1, keepdims=True) for m in range(2)]

    lp = lam_ref[...]
    lam = (jnp.exp(jnp.sum(lp[0:1] * lp[1:2], axis=-1, keepdims=True))
           - jnp.exp(jnp.sum(lp[2:3] * lp[3:4], axis=-1, keepdims=True)) + lam_init)
    o = acc[0][...] / lrow[0] - lam * (acc[1][...] / lrow[1])
    o = o * lax.rsqrt(jnp.mean(o * o, axis=-1, keepdims=True) + EPS) * sub_ref[...]
    o_ref[0] = (o * (1.0 - lam_init)).astype(o_ref.dtype)


def _diff_attn(qa, kn, vn, lam_p, subln, lam_init):
    B, T, _ = qa.shape
    tq = min(T, 256)
    assert T % tq == 0 and (tq % CHUNK == 0 or tq == T)
    W = 2 * A_HD
    return pl.pallas_call(
        functools.partial(_diff_kernel, tq=tq, lam_init=lam_init),
        grid=(B, A_HEADS, T // tq),
        in_specs=[pl.BlockSpec((1, tq, W), lambda b, h, i: (b, i, h)),
                  pl.BlockSpec((1, T, W), lambda b, h, i: (b, 0, h)),
                  pl.BlockSpec((1, T, W), lambda b, h, i: (b, 0, h)),
                  pl.BlockSpec((4, A_HD), lambda b, h, i: (0, 0)),
                  pl.BlockSpec((1, A_VD), lambda b, h, i: (0, 0))],
        out_specs=pl.BlockSpec((1, tq, W), lambda b, h, i: (b, i, h)),
        out_shape=jax.ShapeDtypeStruct((B, T, A_HEADS * A_VD), BF16),
        scratch_shapes=[pltpu.VMEM((tq, A_VD), F32), pltpu.VMEM((tq, A_VD), F32),
                        pltpu.VMEM((2, T // tq, tq, tq), F32)],
        compiler_params=_cparams(("parallel", "parallel", "arbitrary")),
        name="diff_attn",
    )(qa, kn, vn, lam_p, subln.reshape(1, A_VD))


def _diff_decode_kernel(q_ref, kn_ref, vn_ref, kp_ref, vp_ref, lam_ref, sub_ref, o_ref, *, P, lam_init):
    T = q_ref.shape[1]
    per_pos = A_HEADS * 2
    qpos = P + lax.broadcasted_iota(jnp.int32, (T, 1), 0)
    kpos = P + lax.broadcasted_iota(jnp.int32, (1, T), 1)
    vis = (kpos >> CHUNK_SHIFT) <= (qpos >> CHUNK_SHIFT)
    lp = lam_ref[...]
    lam = (jnp.exp(jnp.sum(lp[0:1] * lp[1:2], axis=-1, keepdims=True))
           - jnp.exp(jnp.sum(lp[2:3] * lp[3:4], axis=-1, keepdims=True)) + lam_init)
    for h in range(A_HEADS):
        v_past = jnp.concatenate(
            [vp_ref[pl.ds(c * A_HEADS + h, P, stride=per_pos), :] for c in range(A_VD // LANES)],
            axis=-1).astype(BF16)
        v_new = vn_ref[0, :, h * A_VD:(h + 1) * A_VD]
        outs = []
        for m in range(2):
            hm = h * 2 + m
            q = q_ref[0, :, hm * A_HD:(hm + 1) * A_HD]
            k_past = kp_ref[pl.ds(hm, P, stride=per_pos), :].astype(BF16)
            s_p = _dot_nt(q, k_past)
            s_n = jnp.where(vis, _dot_nt(q, kn_ref[0, :, hm * A_HD:(hm + 1) * A_HD]), NEG)
            mx = jnp.maximum(jnp.max(s_p, axis=-1, keepdims=True), jnp.max(s_n, axis=-1, keepdims=True))
            p_p = jnp.exp(s_p - mx)
            p_n = jnp.exp(s_n - mx)
            l = jnp.sum(p_p, axis=-1, keepdims=True) + jnp.sum(p_n, axis=-1, keepdims=True)
            outs.append((_dot(p_p.astype(BF16), v_past) + _dot(p_n.astype(BF16), v_new)) / l)
        o = outs[0] - lam * outs[1]
        o = o * lax.rsqrt(jnp.mean(o * o, axis=-1, keepdims=True) + EPS) * sub_ref[...]
        o_ref[0, :, h * A_VD:(h + 1) * A_VD] = (o * (1.0 - lam_init)).astype(o_ref.dtype)


def _diff_decode(qa, kn, vn, kp_rows, vp_rows, layer, lam_p, subln, P, lam_init):
    B, T, W = qa.shape
    assert P % CHUNK == 0 and kp_rows.shape[2] == P * A_HEADS * 2 and vp_rows.shape[2] == P * A_HEADS * 2
    new_spec = pl.BlockSpec((1, T, W), lambda b: (b, 0, 0))
    cache_spec = pl.BlockSpec((None, None, P * A_HEADS * 2, LANES), lambda b: (layer, b, 0, 0))
    return pl.pallas_call(
        functools.partial(_diff_decode_kernel, P=P, lam_init=lam_init),
        grid=(B,),
        in_specs=[new_spec, new_spec, new_spec, cache_spec, cache_spec,
                  pl.BlockSpec((4, A_HD), lambda b: (0, 0)),
                  pl.BlockSpec((1, A_VD), lambda b: (0, 0))],
        out_specs=new_spec,
        out_shape=jax.ShapeDtypeStruct((B, T, W), BF16),
        compiler_params=_cparams(("parallel",)),
        name="diff_decode",
    )(qa, kn, vn, kp_rows, vp_rows, lam_p, subln.reshape(1, A_VD))


def _dsa_kernel(qi_ref, wi_ref, qb_ref, ki_ref, kb_ref, vb_ref, o_ref,
                key_s, bias_s, s_s, *, P, L, tq, tk, topk):
    i = pl.program_id(1)
    q_pos0 = P + i * tq
    Lp = ki_ref.shape[1]
    ncol = jnp.minimum(L, ((q_pos0 + tq - 1) // CHUNK + 1) * CHUNK)
    nt = (ncol + tk - 1) // tk
    qpos = q_pos0 + lax.broadcasted_iota(jnp.int32, (tq, 1), 0)
    grp = B_HEADS // B_KV
    lane = lax.broadcasted_iota(jnp.int32, (tq, LANES), 1)
    wi = wi_ref[0] * (I_HEADS ** -0.5)

    def vis_of(t):
        kpos = t * tk + lax.broadcasted_iota(jnp.int32, (1, tk), 1)
        return ((kpos >> CHUNK_SHIFT) <= (qpos >> CHUNK_SHIFT)) & (kpos < L)

    def score_tile(t, carry):
        c0 = pl.multiple_of(t * tk, tk)
        kit = ki_ref[0, pl.ds(c0, tk), :]
        sc = jnp.zeros((tq, tk), F32)
        for pr in range(I_HEADS * I_HD // LANES):
            qpair = qi_ref[0, :, pr * LANES:(pr + 1) * LANES]
            for half in range(LANES // I_HD):
                h = pr * (LANES // I_HD) + half
                qh = jnp.where((lane >> I_HD_SHIFT) == half, qpair, jnp.zeros_like(qpair))
                isc = _dot_nt(qh, kit)
                sc = sc + jnp.maximum(isc, 0.0) * wi[:, h:h + 1]
        sc = jnp.where(vis_of(t), sc, -jnp.inf)
        key_s[t] = _float_order_key(sc)
        return carry

    lax.fori_loop(0, nt, score_tile, 0)

    def counts(pred_fns):
        def body(t, accs):
            kk = key_s[t]
            out = []
            for pred_fn, acc in zip(pred_fns, accs):
                hit = jnp.where(pred_fn(kk), 1, 0)
                part = hit[:, 0:LANES]
                for c in range(1, tk // LANES):
                    part = part + hit[:, c * LANES:(c + 1) * LANES]
                out.append(acc + part)
            return tuple(out)
        accs = lax.fori_loop(0, nt, body, tuple(jnp.zeros((tq, LANES), jnp.int32) for _ in pred_fns))
        return [jnp.sum(a, axis=-1, keepdims=True) for a in accs]

    def select(_):
        def bit_body(bi, thr):
            cand = thr + jnp.left_shift(jnp.int32(1), 31 - bi)
            (cnt,) = counts([lambda kk: kk >= cand])
            return jnp.where(cnt >= topk, cand, thr)
        return lax.fori_loop(0, 32, bit_body, jnp.full((tq, 1), KEY_MIN, jnp.int32))

    thr = lax.cond(ncol > topk, select, lambda _: jnp.full((tq, 1), KEY_MIN, jnp.int32), 0)
    n_gt, n_eq = counts([lambda kk: kk > thr, lambda kk: kk == thr])
    need = topk - n_gt
    neg_inf_key = jnp.int32(KEY_MIN + 0x7FFFFF)
    tie_break = jnp.max(jnp.where((n_eq > need) & (thr != neg_inf_key), 1.0, 0.0)) > 0.5

    def bias_fast(_):
        def body(t, c):
            sel = (key_s[t] >= thr) & vis_of(t)
            bias_s[t] = jnp.where(sel, 0.0, NEG)
            return c
        lax.fori_loop(0, nt, body, 0)
        return 0

    def bias_ties(_):
        r = lax.broadcasted_iota(jnp.int32, (LANES, LANES), 0)
        c = lax.broadcasted_iota(jnp.int32, (LANES, LANES), 1)
        before = jnp.where(r < c, 1.0, 0.0).astype(BF16)

        def body(t, seen):
            kk = key_s[t]
            vis = vis_of(t)
            for cb in range(tk // LANES):
                sl = slice(cb * LANES, (cb + 1) * LANES)
                eq = kk[:, sl] == thr
                eqf = jnp.where(eq, 1.0, 0.0)
                rank = seen + _dot(eqf.astype(BF16), before)
                sel = ((kk[:, sl] > thr) | (eq & (rank < need.astype(F32)))) & vis[:, sl]
                bias_s[t, :, sl] = jnp.where(sel, 0.0, NEG)
                seen = seen + jnp.sum(eqf, axis=-1, keepdims=True)
            return seen
        lax.fori_loop(0, nt, body, jnp.zeros((tq, 1), F32))
        return 0

    lax.cond(tie_break, bias_ties, bias_fast, 0)

    def fold(x, op):
        f = x[:, 0:LANES]
        for c in range(1, tk // LANES):
            f = op(f, x[:, c * LANES:(c + 1) * LANES])
        return f

    for n in range(B_KV):
        kv = slice(n * B_HD, (n + 1) * B_HD)

        def scores(t, mrun, kv=kv, n=n):
            c0 = pl.multiple_of(t * tk, tk)
            bias = bias_s[t]
            k = kb_ref[0, pl.ds(c0, tk), kv]
            out = []
            for g in range(grp):
                h = n * grp + g
                s = _dot_nt(qb_ref[0, :, h * B_HD:(h + 1) * B_HD], k) + bias
                s_s[g, t] = s
                out.append(jnp.maximum(mrun[g], fold(s, jnp.maximum)))
            return tuple(out)

        mrun = _tile_loop(nt, scores, tuple(jnp.full((tq, LANES), NEG, F32) for _ in range(grp)), 2)
        mrow = [jnp.max(m, axis=-1, keepdims=True) for m in mrun]

        def weigh(t, carry, kv=kv, mrow=mrow):
            lrun, acc = carry
            c0 = pl.multiple_of(t * tk, tk)
            v = vb_ref[0, pl.ds(c0, tk), kv]
            lnew, anew = [], []
            for g in range(grp):
                p = jnp.exp(s_s[g, t] - mrow[g])
                lnew.append(lrun[g] + fold(p, jnp.add))
                anew.append(acc[g] + _dot(p.astype(BF16), v))
            return tuple(lnew), tuple(anew)

        lrun, acc = _tile_loop(
            nt, weigh,
            (tuple(jnp.zeros((tq, LANES), F32) for _ in range(grp)),
             tuple(jnp.zeros((tq, B_HD), F32) for _ in range(grp))), 2)
        for g in range(grp):
            h = n * grp + g
            l = jnp.sum(lrun[g], axis=-1, keepdims=True)
            o_ref[0, :, h * B_HD:(h + 1) * B_HD] = (acc[g] / l).astype(o_ref.dtype)


def _dsa(qi, proj, qb, ki2, kb, vb, P, L, tk):
    B, T, _ = qb.shape
    Lp = ki2.shape[1]
    tq = min(T, 128)
    assert T % tq == 0 and Lp % tk == 0 and tk % LANES == 0
    topk = min(TOPK_MAX, L // 4)
    kern = functools.partial(_dsa_kernel, P=P, L=L, tq=tq, tk=tk, topk=topk)
    return pl.pallas_call(
        kern,
        grid=(B, T // tq),
        in_specs=[pl.BlockSpec((1, tq, I_HEADS * I_HD), lambda b, i: (b, i, 0)),
                  pl.BlockSpec((1, tq, LANES), lambda b, i: (b, i, OFF_SMALL // LANES)),
                  pl.BlockSpec((1, tq, B_HEADS * B_HD), lambda b, i: (b, i, 0)),
                  pl.BlockSpec((1, Lp, LANES), lambda b, i: (b, 0, 0)),
                  pl.BlockSpec((1, Lp, B_KV * B_HD), lambda b, i: (b, 0, 0)),
                  pl.BlockSpec((1, Lp, B_KV * B_HD), lambda b, i: (b, 0, 0))],
        out_specs=pl.BlockSpec((1, tq, B_HEADS * B_HD), lambda b, i: (b, i, 0)),
        out_shape=jax.ShapeDtypeStruct((B, T, B_HEADS * B_HD), BF16),
        scratch_shapes=[pltpu.VMEM((Lp // tk, tq, tk), jnp.int32), pltpu.VMEM((Lp // tk, tq, tk), F32),
                        pltpu.VMEM((B_HEADS // B_KV, Lp // tk, tq, tk), F32)],
        compiler_params=_cparams(("parallel", "arbitrary")),
        name="dsa",
    )(qi, proj, qb, ki2, kb, vb)


def _float_order_key(x):
    bits = pltpu.bitcast(x, jnp.int32)
    bits = jnp.where(bits == KEY_MIN, 0, bits)
    return bits ^ ((bits >> 31) & 0x7FFFFFFF)


def _dsa_decode_kernel(qi_ref, wi_ref, qb_ref, kin_ref, kbn_ref, vbn_ref, kip_ref, kp_ref, vp_ref, o_ref,
                       *, P, topk):
    T = qi_ref.shape[1]
    grp = B_HEADS // B_KV
    qpos = P + lax.broadcasted_iota(jnp.int32, (T, 1), 0)
    kpos = P + lax.broadcasted_iota(jnp.int32, (1, T), 1)
    vis_n = (kpos >> CHUNK_SHIFT) <= (qpos >> CHUNK_SHIFT)
    wi = wi_ref[0] * (I_HEADS ** -0.5)
    ki_pT = kip_ref[...].astype(BF16)
    ki_n = kin_ref[0, :, 0:I_HD]

    sc_p = jnp.zeros((T, P), F32)
    sc_n = jnp.zeros((T, T), F32)
    for h in range(I_HEADS):
        qh = qi_ref[0, :, h * I_HD:(h + 1) * I_HD]
        w = wi[:, h:h + 1]
        sc_p = sc_p + jnp.maximum(_dot(qh, ki_pT), 0.0) * w
        sc_n = sc_n + jnp.maximum(_dot_nt(qh, ki_n), 0.0) * w
    key_p = _float_order_key(sc_p)
    key_n = _float_order_key(jnp.where(vis_n, sc_n, -jnp.inf))

    def count(pred):
        return (jnp.sum(jnp.where(pred(key_p), 1, 0), axis=-1, keepdims=True)
                + jnp.sum(jnp.where(pred(key_n), 1, 0), axis=-1, keepdims=True))

    def bit_body(bi, thr):
        cand = thr + jnp.left_shift(jnp.int32(1), 31 - bi)
        return jnp.where(count(lambda kk: kk >= cand) >= topk, cand, thr)

    thr = lax.fori_loop(0, 32, bit_body, jnp.full((T, 1), KEY_MIN, jnp.int32))
    need = (topk - count(lambda kk: kk > thr)).astype(F32)

    def before(n):
        r = lax.broadcasted_iota(jnp.int32, (n, n), 0)
        c = lax.broadcasted_iota(jnp.int32, (n, n), 1)
        return jnp.where(r < c, 1.0, 0.0).astype(BF16)

    def select(kk, seen, pre):
        eq = kk == thr
        eqf = jnp.where(eq, 1.0, 0.0)
        rank = seen + _dot(eqf.astype(BF16), pre)
        sel = (kk > thr) | (eq & (rank < need))
        return sel, seen + jnp.sum(eqf, axis=-1, keepdims=True)

    pre_l = before(LANES)
    seen = jnp.zeros((T, 1), F32)
    bias_blocks = []
    for cb in range(P // LANES):
        sel, seen = select(key_p[:, cb * LANES:(cb + 1) * LANES], seen, pre_l)
        bias_blocks.append(jnp.where(sel, 0.0, NEG))
    bias_p = jnp.concatenate(bias_blocks, axis=-1)
    sel, _ = select(key_n, seen, before(T))
    bias_n = jnp.where(sel & vis_n, 0.0, NEG)

    for n in range(B_KV):
        kv = slice(n * B_HD, (n + 1) * B_HD)
        k_p = kp_ref[pl.ds(n, P, stride=B_KV), :].astype(BF16)
        v_p = vp_ref[pl.ds(n, P, stride=B_KV), :].astype(BF16)
        k_n = kbn_ref[0, :, kv]
        v_n = vbn_ref[0, :, kv]
        for g in range(grp):
            h = n * grp + g
            q = qb_ref[0, :, h * B_HD:(h + 1) * B_HD]
            s_p = _dot_nt(q, k_p) + bias_p
            s_n = _dot_nt(q, k_n) + bias_n
            mx = jnp.maximum(jnp.max(s_p, axis=-1, keepdims=True), jnp.max(s_n, axis=-1, keepdims=True))
            p_p = jnp.exp(s_p - mx)
            p_n = jnp.exp(s_n - mx)
            l = jnp.sum(p_p, axis=-1, keepdims=True) + jnp.sum(p_n, axis=-1, keepdims=True)
            o = _dot(p_p.astype(BF16), v_p) + _dot(p_n.astype(BF16), v_n)
            o_ref[0, :, h * B_HD:(h + 1) * B_HD] = (o / l).astype(o_ref.dtype)


def _dsa_decode(qi, proj, qb, ki2, kbb, vbb, ki_past_t, kp_rows, vp_rows, layer):
    B, T, _ = qb.shape
    P = ki_past_t.shape[3]
    assert P % CHUNK == 0 and P % LANES == 0 and kp_rows.shape[2] == P * B_KV
    topk = min(TOPK_MAX, (P + T) // 4)

    def new(width, blk=0):
        return pl.BlockSpec((1, T, width), lambda b: (b, 0, blk))

    rows = pl.BlockSpec((None, None, P * B_KV, LANES), lambda b: (layer, b, 0, 0))
    return pl.pallas_call(
        functools.partial(_dsa_decode_kernel, P=P, topk=topk),
        grid=(B,),
        in_specs=[new(I_HEADS * I_HD), new(LANES, OFF_SMALL // LANES), new(B_HEADS * B_HD),
                  new(LANES), new(B_KV * B_HD), new(B_KV * B_HD),
                  pl.BlockSpec((None, None, I_HD, P), lambda b: (layer, b, 0, 0)), rows, rows],
        out_specs=new(B_HEADS * B_HD),
        out_shape=jax.ShapeDtypeStruct((B, T, B_HEADS * B_HD), BF16),
        compiler_params=_cparams(("parallel",)),
        name="dsa_decode",
    )(qi, proj, qb, ki2, kbb, vbb, ki_past_t, kp_rows, vp_rows)


def _bmm(a, b):
    return jnp.einsum('bij,bjk->bik', a.astype(BF16), b.astype(BF16), preferred_element_type=F32)


def _bmm_nt(a, b):
    return jnp.einsum('bik,bjk->bij', a.astype(BF16), b.astype(BF16), preferred_element_type=F32)


def _gdn_intra_kernel(x_ref, halo_ref, sm_ref, cw_ref, cbuf_ref, alog_ref, dtb_ref,
                      u_o, w_o, qt_o, ktT_o, qkg_o, dec_o, buf_s, *, cc, nck):
    i = pl.program_id(1)
    R = nck * cc
    HW = C_HEADS * C_HD

    @pl.when(i == 0)
    def _():
        buf_s[0:SUBLANES, :] = jnp.zeros((SUBLANES, C_QKV), F32)
        buf_s[SUBLANES - (C_CONV - 1):SUBLANES, :] = cbuf_ref[0]

    @pl.when(i > 0)
    def _():
        buf_s[0:SUBLANES, :] = halo_ref[0]

    buf_s[SUBLANES:SUBLANES + R, :] = x_ref[0]
    y = None
    for jw in range(C_CONV):
        off = SUBLANES - (C_CONV - 1) + jw
        term = buf_s[off:off + R, :] * cw_ref[jw:jw + 1, :]
        y = term if y is None else y + term
    y = y * jax.nn.sigmoid(y)

    sm = sm_ref[0]
    xg = sm + dtb_ref[...]
    softplus = jnp.maximum(xg, 0.0) + jnp.log(1.0 + jnp.exp(-jnp.abs(xg)))
    g = -jnp.exp(alog_ref[...]) * softplus
    beta = jax.nn.sigmoid(sm)

    cshift = cc.bit_length() - 1
    rr = lax.broadcasted_iota(jnp.int32, (R, R), 0)
    rc = lax.broadcasted_iota(jnp.int32, (R, R), 1)
    cum = ((rr >> cshift) == (rc >> cshift)) & (rr >= rc)
    G = jnp.dot(jnp.where(cum, 1.0, 0.0), g, preferred_element_type=F32,
                precision=lax.Precision.HIGHEST)

    qs, ks, vs, gcs, bcs = [], [], [], [], []
    for ck in range(nck):
        rows = slice(ck * cc, (ck + 1) * cc)
        for h in range(C_HEADS):
            qs.append(y[rows, h * C_HD:(h + 1) * C_HD])
            ks.append(y[rows, HW + h * C_HD:HW + (h + 1) * C_HD])
            vs.append(y[rows, 2 * HW + h * C_HD:2 * HW + (h + 1) * C_HD])
            gcs.append(G[rows, CA_LANE + h:CA_LANE + h + 1])
            bcs.append(beta[rows, CB_LANE + h:CB_LANE + h + 1])
    q3 = jnp.stack(qs)
    k3 = jnp.stack(ks)
    v3 = jnp.stack(vs)
    Gc = jnp.stack(gcs)
    bc = jnp.stack(bcs)
    q3 = q3 * lax.rsqrt(jnp.sum(q3 * q3, axis=-1, keepdims=True) + EPS) * (C_HD ** -0.5)
    k3 = k3 * lax.rsqrt(jnp.sum(k3 * k3, axis=-1, keepdims=True) + EPS)

    ri = lax.broadcasted_iota(jnp.int32, (cc, cc), 0)
    ci = lax.broadcasted_iota(jnp.int32, (cc, cc), 1)
    incl = (ri >= ci)[None]
    strict = (ri > ci)[None]
    eye = (ri == ci)[None]
    bdiag = ((ri // SOLVE_BLOCK) == (ci // SOLVE_BLOCK))[None]

    Gr = jnp.sum(jnp.where(eye, Gc, 0.0), axis=1, keepdims=True)
    gam = jnp.where(incl, jnp.exp(jnp.where(incl, Gc - Gr, 0.0)), 0.0)
    k16 = k3.astype(BF16)
    A = jnp.where(strict, bc * _bmm_nt(k16, k16) * gam, 0.0)
    Nd = jnp.where(bdiag, -A, 0.0)
    E = jnp.where(bdiag, 0.0, A)
    N2 = _bmm(Nd, Nd)
    N4 = _bmm(N2, N2)
    N8 = _bmm(N4, N4)
    Q = Nd + N2 + _bmm(Nd, N2)
    Q = Q + N4 + _bmm(Q, N4)
    Q = Q + N8 + _bmm(Q, N8)
    Fm = E + _bmm(Q, E)
    F2 = _bmm(Fm, Fm)
    Rm = F2 - Fm - _bmm(Fm, F2)
    Wm = Rm + Q + _bmm(Rm, Q)
    eG = jnp.exp(Gc)
    rhs = jnp.concatenate([bc * v3, (bc * eG) * k3], axis=-1)
    sol = rhs + _bmm(Wm, rhs)
    qkg = _bmm_nt(q3, k16) * gam
    gl = Gc[:, cc - 1:cc, :]
    qt = q3 * eG
    kt = k3 * jnp.exp(gl - Gc)
    dec = jnp.exp(gl)

    for ck in range(nck):
        rows = slice(ck * cc, (ck + 1) * cc)
        for h in range(C_HEADS):
            b = ck * C_HEADS + h
            cols = slice(h * C_HD, (h + 1) * C_HD)
            u_o[0, rows, cols] = sol[b, :, 0:C_HD]
            w_o[0, rows, cols] = sol[b, :, C_HD:2 * C_HD].astype(BF16)
            qt_o[0, rows, cols] = qt[b].astype(BF16)
            ktT_o[0, ck, h] = kt[b].T.astype(BF16)
            qkg_o[0, ck, h] = qkg[b].astype(BF16)
            dec_o[0, ck, h] = jnp.broadcast_to(dec[b], (1, C_HD))


def _gdn_inter_kernel(u_ref, w_ref, qt_ref, ktT_ref, qkg_ref, dec_ref, z_ref, s0_ref, gn_ref,
                      o_ref, sfin_ref, st_s, *, cc, nck):
    c = pl.program_id(1)

    @pl.when(c == 0)
    def _():
        st_s[...] = s0_ref[0]

    S = st_s[...]
    for ck in range(nck):
        rows = slice(ck * cc, (ck + 1) * cc)

        def heads(ref):
            return jnp.stack([ref[0, rows, h * C_HD:(h + 1) * C_HD] for h in range(C_HEADS)])

        S16 = S.astype(BF16)
        v_new = heads(u_ref) - _bmm(heads(w_ref), S16)
        vn16 = v_new.astype(BF16)
        o = _bmm(heads(qt_ref), S16) + _bmm(qkg_ref[0, ck], vn16)
        S = dec_ref[0, ck] * S + _bmm(ktT_ref[0, ck], vn16)
        on = o * lax.rsqrt(jnp.mean(o * o, axis=-1, keepdims=True) + EPS) * gn_ref[...]
        for h in range(C_HEADS):
            cols = slice(h * C_HD, (h + 1) * C_HD)
            z = z_ref[0, rows, cols]
            o_ref[0, rows, cols] = (on[h] * (z * jax.nn.sigmoid(z))).astype(o_ref.dtype)
    st_s[...] = S

    @pl.when(c == pl.num_programs(1) - 1)
    def _():
        sfin_ref[0] = S


def _gdn(proj, conv_w, cbuf, S0, a_log, dt_bias, out_norm):
    B, T, _ = proj.shape
    cc = min(CHUNK, T)
    assert T % cc == 0 and cc % SOLVE_BLOCK == 0 and cc // SOLVE_BLOCK <= 4 and cc >= SUBLANES
    assert cc & (cc - 1) == 0
    HW = C_HEADS * C_HD
    NC = T // cc
    nck_a = 2 if NC % 2 == 0 else 1
    nck_b = 4 if NC % 4 == 0 else 1

    def at_ca(p):
        return jnp.zeros((1, LANES), F32).at[0, CA_LANE:CA_LANE + C_HEADS].set(p.astype(F32))

    Ra = nck_a * cc
    hb = Ra // SUBLANES
    u, w, qt, ktT, qkg, dec = pl.pallas_call(
        functools.partial(_gdn_intra_kernel, cc=cc, nck=nck_a),
        grid=(B, NC // nck_a),
        in_specs=[pl.BlockSpec((1, Ra, C_QKV), lambda b, i: (b, i, OFF_CQKV // C_QKV)),
                  pl.BlockSpec((1, SUBLANES, C_QKV),
                               lambda b, i: (b, jnp.maximum(i * hb - 1, 0), OFF_CQKV // C_QKV)),
                  pl.BlockSpec((1, Ra, LANES), lambda b, i: (b, i, OFF_SMALL // LANES)),
                  pl.BlockSpec((C_CONV, C_QKV), lambda b, i: (0, 0)),
                  pl.BlockSpec((1, C_CONV - 1, C_QKV), lambda b, i: (b, 0, 0)),
                  pl.BlockSpec((1, LANES), lambda b, i: (0, 0)),
                  pl.BlockSpec((1, LANES), lambda b, i: (0, 0))],
        out_specs=[pl.BlockSpec((1, Ra, HW), lambda b, i: (b, i, 0)),
                   pl.BlockSpec((1, Ra, HW), lambda b, i: (b, i, 0)),
                   pl.BlockSpec((1, Ra, HW), lambda b, i: (b, i, 0)),
                   pl.BlockSpec((1, nck_a, C_HEADS, C_HD, cc), lambda b, i: (b, i, 0, 0, 0)),
                   pl.BlockSpec((1, nck_a, C_HEADS, cc, cc), lambda b, i: (b, i, 0, 0, 0)),
                   pl.BlockSpec((1, nck_a, C_HEADS, 1, C_HD), lambda b, i: (b, i, 0, 0, 0))],
        out_shape=[jax.ShapeDtypeStruct((B, T, HW), F32),
                   jax.ShapeDtypeStruct((B, T, HW), BF16),
                   jax.ShapeDtypeStruct((B, T, HW), BF16),
                   jax.ShapeDtypeStruct((B, NC, C_HEADS, C_HD, cc), BF16),
                   jax.ShapeDtypeStruct((B, NC, C_HEADS, cc, cc), BF16),
                   jax.ShapeDtypeStruct((B, NC, C_HEADS, 1, C_HD), F32)],
        scratch_shapes=[pltpu.VMEM((Ra + SUBLANES, C_QKV), F32)],
        compiler_params=_cparams(("parallel", "parallel")),
        name="gdn_intra",
    )(proj, proj, proj, conv_w, cbuf, at_ca(a_log), at_ca(dt_bias))

    Rb = nck_b * cc
    return pl.pallas_call(
        functools.partial(_gdn_inter_kernel, cc=cc, nck=nck_b),
        grid=(B, NC // nck_b),
        in_specs=[pl.BlockSpec((1, Rb, HW), lambda b, c: (b, c, 0)),
                  pl.BlockSpec((1, Rb, HW), lambda b, c: (b, c, 0)),
                  pl.BlockSpec((1, Rb, HW), lambda b, c: (b, c, 0)),
                  pl.BlockSpec((1, nck_b, C_HEADS, C_HD, cc), lambda b, c: (b, c, 0, 0, 0)),
                  pl.BlockSpec((1, nck_b, C_HEADS, cc, cc), lambda b, c: (b, c, 0, 0, 0)),
                  pl.BlockSpec((1, nck_b, C_HEADS, 1, C_HD), lambda b, c: (b, c, 0, 0, 0)),
                  pl.BlockSpec((1, Rb, HW), lambda b, c: (b, c, OFF_CZ // HW)),
                  pl.BlockSpec((1, C_HEADS, C_HD, C_HD), lambda b, c: (b, 0, 0, 0)),
                  pl.BlockSpec((1, C_HD), lambda b, c: (0, 0))],
        out_specs=[pl.BlockSpec((1, Rb, HW), lambda b, c: (b, c, 0)),
                   pl.BlockSpec((1, C_HEADS, C_HD, C_HD), lambda b, c: (b, 0, 0, 0))],
        out_shape=[jax.ShapeDtypeStruct((B, T, HW), BF16),
                   jax.ShapeDtypeStruct((B, C_HEADS, C_HD, C_HD), F32)],
        scratch_shapes=[pltpu.VMEM((C_HEADS, C_HD, C_HD), F32)],
        compiler_params=_cparams(("parallel", "arbitrary")),
        name="gdn_inter",
    )(u, w, qt, ktT, qkg, dec, proj, S0, out_norm.reshape(1, C_HD))


def _ffn_up_kernel(x_ref, g_ref, sh_ref, sc_ref, wg_ref, wu_ref, cwg_ref, cwu_ref, fg_ref, fu_ref,
                   act_o, fng_o, fnu_o, h_ref, *scr, split_rows):
    bb, tt, D = x_ref.shape
    tn = wg_ref.shape[1]
    i = pl.program_id(1)
    j = pl.program_id(2)
    lo = SUBLANES - (FFN_CONV - 1)

    @pl.when(j == 0)
    def _():
        x = x_ref[...]
        y = x * lax.rsqrt(jnp.mean(x * x, axis=-1, keepdims=True) + EPS) * g_ref[...]
        h = y * (1.0 + sc_ref[...]) + sh_ref[...]
        h_ref[...] = h.reshape(bb * tt, D).astype(BF16)

    if split_rows:
        (carry_s,) = scr
        halves = ((wg_ref, cwg_ref, fg_ref, fng_o, 0), (wu_ref, cwu_ref, fu_ref, fnu_o, 1))
        for _, _, f_ref, _, idx in halves:
            @pl.when(i == 0)
            def _(f_ref=f_ref, idx=idx):
                carry_s[idx, j, 0, lo:SUBLANES, :] = f_ref[0]
        prev = [carry_s[idx, j, 0] for idx in range(2)]
        row8 = lax.broadcasted_iota(jnp.int32, (SUBLANES, tn), 0)
        rc = min(tt, FFN_ROW_CHUNK)
        for r in range(tt // rc):
            rows = slice(r * rc, (r + 1) * rc)
            ys = []
            for w_ref, cw_ref, _, _, idx in halves:
                u = _dot(h_ref[rows, :], w_ref[...])
                y = u * cw_ref[FFN_CONV - 1:FFN_CONV, :]
                for k in range(1, FFN_CONV):
                    s = pltpu.roll(u, k, 0)
                    head = jnp.where(row8 < k, pltpu.roll(prev[idx], k, 0), s[0:SUBLANES])
                    s = jnp.concatenate([head, s[SUBLANES:]], axis=0)
                    y = y + s * cw_ref[FFN_CONV - 1 - k:FFN_CONV - k, :]
                prev[idx] = u[rc - SUBLANES:rc]
                ys.append(y)
            act_o[0, rows, :] = (ys[0] * jax.nn.sigmoid(ys[0]) * ys[1]).astype(act_o.dtype)
        for _, _, _, fn_o, idx in halves:
            carry_s[idx, j, 0] = prev[idx]
            fn_o[...] = prev[idx][lo:SUBLANES][None]
    else:
        bg_s, bu_s = scr

        def half(w_ref, cw_ref, f_ref, fn_o, buf):
            u = _dot(h_ref[...], w_ref[...]).reshape(bb, tt, tn)
            buf[:, lo:SUBLANES, :] = f_ref[...]
            buf[:, SUBLANES:SUBLANES + tt, :] = u
            fn_o[...] = u[:, tt - (FFN_CONV - 1):tt, :]
            y = None
            for jw in range(FFN_CONV):
                term = buf[:, lo + jw:lo + jw + tt, :] * cw_ref[jw:jw + 1, :]
                y = term if y is None else y + term
            return y

        yg = half(wg_ref, cwg_ref, fg_ref, fng_o, bg_s)
        yu = half(wu_ref, cwu_ref, fu_ref, fnu_o, bu_s)
        act_o[...] = (yg * jax.nn.sigmoid(yg) * yu).astype(act_o.dtype)


def _ffn_up(x, g, shift, scale, w_up, layer, conv_w, fbuf):
    B, T, D = x.shape
    FF = w_up.shape[2] // 2
    bb, tt = _row_tiles(B, T)
    tn = COL_TILE
    assert FF % tn == 0 and tt >= SUBLANES
    nj = FF // tn
    split_rows = T > tt
    assert not split_rows or bb == 1
    nf = FFN_CONV - 1

    def cols(off, rows, lead=()):
        return pl.BlockSpec((None,) * len(lead) + rows + (tn,),
                            lambda b, i, j: lead + (0,) * len(rows) + (off + j,))

    def fb(off):
        return pl.BlockSpec((bb, nf, tn), lambda b, i, j: (b, 0, off + j))

    act, fng, fnu = pl.pallas_call(
        functools.partial(_ffn_up_kernel, split_rows=split_rows),
        grid=(B // bb, T // tt, nj),
        in_specs=[pl.BlockSpec((bb, tt, D), lambda b, i, j: (b, i, 0)),
                  pl.BlockSpec((1, 1, D), lambda b, i, j: (0, 0, 0)),
                  pl.BlockSpec((bb, 1, D), lambda b, i, j: (b, 0, 0)),
                  pl.BlockSpec((bb, 1, D), lambda b, i, j: (b, 0, 0)),
                  cols(0, (D,), (layer,)), cols(nj, (D,), (layer,)),
                  cols(0, (FFN_CONV,)), cols(nj, (FFN_CONV,)),
                  fb(0), fb(nj)],
        out_specs=[pl.BlockSpec((bb, tt, tn), lambda b, i, j: (b, i, j)),
                   pl.BlockSpec((bb, None, nf, tn), lambda b, i, j: (b, i, 0, j)),
                   pl.BlockSpec((bb, None, nf, tn), lambda b, i, j: (b, i, 0, j))],
        out_shape=[jax.ShapeDtypeStruct((B, T, FF), BF16),
                   jax.ShapeDtypeStruct((B, T // tt, nf, FF), F32),
                   jax.ShapeDtypeStruct((B, T // tt, nf, FF), F32)],
        scratch_shapes=[pltpu.VMEM((bb * tt, D), BF16)] + (
            [pltpu.VMEM((2, nj, 1, SUBLANES, tn), F32)] if split_rows else
            [pltpu.VMEM((bb, tt + SUBLANES, tn), F32), pltpu.VMEM((bb, tt + SUBLANES, tn), F32)]),
        compiler_params=_cparams(("parallel", "arbitrary", "arbitrary")),
        name="ffn_up",
    )(x, g.reshape(1, 1, D), shift, scale, w_up, w_up, conv_w, conv_w, fbuf, fbuf)
    return act, jnp.concatenate([fng[:, -1], fnu[:, -1]], axis=-1)


def _prep_w_in(w_in):
    depth, D = w_in.shape[0], w_in.shape[1]
    sizes = (1024, 1024, 1024, 1024, 256, 256, 512, 64, 8, C_QKV, 8, 8, 1024, N_BRANCH * D)
    offs = np.concatenate([[0], np.cumsum(sizes)])
    (aq, ak, av, bq, bk, bv, bqi, bki, bwi, cqkv, ca, cb, cz, gt) = [
        w_in[:, :, int(offs[k]):int(offs[k + 1])] for k in range(len(sizes))]

    def z(n):
        return jnp.zeros((depth, D, n), w_in.dtype)

    cols = [cqkv, cz, aq, ak, av, bq, bk, bv, bqi, bki, z(LANES - I_HD), bwi, ca, cb, z(LANES - 24),
            z(OFF_GT - OFF_SMALL - LANES), gt]
    w = jnp.concatenate(cols, axis=2).astype(BF16)
    assert w.shape[2] == N_PROJ
    return w


def _layer(x, mod, past, lw, layer_idx):
    kA_p, vA_p, kB_p, vB_p, kI_p, cbuf, S0, fbuf = past
    B, T, D = x.shape
    P = 0 if kI_p is None else kI_p.shape[3]
    sh1, sc1, g1, sh2, sc2, g2 = [m.reshape(B, 1, D) for m in jnp.split(mod, 6, axis=-1)]

    proj = _mm_norm(x, lw['norm_mix'], sh1, sc1, lw['w_in'], layer_idx)
    rope = _rope_table(P + jnp.arange(T, dtype=jnp.int32))
    (qa, ka, kab, va, vab, qb, kb, kbb, vb, vbb, qi, ki, ki2) = _prep(
        proj, rope, lw['a_q_norm'], lw['a_k_norm'], lw['b_q_norm'], lw['b_k_norm'])

    lam_init = 0.8 - 0.6 * math.exp(-0.3 * layer_idx)
    if P:
        oa = _diff_decode(qa, kab, vab, kA_p, vA_p, layer_idx, lw['a_lambda'], lw['a_subln'], P, lam_init)
        ob = _dsa_decode(qi, proj, qb, ki2, kbb, vbb, kI_p, kB_p, vB_p, layer_idx)
    else:
        oa = _diff_attn(qa, kab, vab, lw['a_lambda'], lw['a_subln'], lam_init)
        ob = _dsa(qi, proj, qb, ki2, kbb, vbb, 0, T, min(T, 512))

    oc, S_fin = _gdn(proj, lw['c_conv'], cbuf, S0, lw['c_a_log'], lw['c_dt_bias'], lw['c_out_norm'])
    assert T >= C_CONV - 1 and T >= FFN_CONV - 1
    cbuf_new = proj[:, T - (C_CONV - 1):, OFF_CQKV:OFF_CQKV + C_QKV]

    mixed = _merge(oa, ob, oc, lw['w_branch'], layer_idx, proj, D)
    x = _mm_res(mixed, lw['w_out'], layer_idx, x, g1)

    act, fbuf_new = _ffn_up(x, lw['norm_ffn'], sh2, sc2, lw['w_up'], layer_idx, lw['ffn_conv'], fbuf)
    x = _mm_res(act, lw['w_down'], layer_idx, x, g2, row_target=512)

    va_out = va.reshape(B, T, A_VD // LANES, A_HEADS, LANES).transpose(0, 1, 3, 2, 4)
    new_state = (ka.reshape(B, T, A_HEADS, 2, A_HD), va_out.reshape(B, T, A_HEADS, A_VD),
                 kb.reshape(B, T, B_KV, B_HD), vb.reshape(B, T, B_KV, B_HD), ki,
                 cbuf_new, S_fin, fbuf_new)
    return x, new_state


def kernel(x_prompt, x_sample, c_prompt, c_sample, cache_diff_k, cache_diff_v, cache_dsa_k, cache_dsa_v, cache_dsa_kidx, state_gdn_conv, state_gdn, state_ffn_conv, w_ada, b_ada, norm_mix, w_in, a_q_norm, a_k_norm, a_lambda, a_subln, b_q_norm, b_k_norm, c_conv, c_a_log, c_dt_bias, c_out_norm, w_branch, w_out, norm_ffn, w_up, ffn_conv, w_down):
    depth = w_in.shape[0]
    bp, bs = x_prompt.shape[0], x_sample.shape[0]
    d_ff2 = w_up.shape[2]
    dt_ = x_prompt.dtype
    prompt_past = (None, None, None, None, None, jnp.zeros((bp, C_CONV - 1, C_QKV), dt_),
                   jnp.zeros((bp, C_HEADS, C_HD, C_HD), dt_), jnp.zeros((bp, FFN_CONV - 1, d_ff2), dt_))
    nrow = -(-(bp + bs) // SUBLANES) * SUBLANES
    c_all = jnp.concatenate([c_prompt, c_sample, jnp.zeros((nrow - bp - bs, c_prompt.shape[1]), dt_)], axis=0)
    past_len = cache_diff_k.shape[2]
    diff_k_rows = cache_diff_k.reshape(depth, bs, past_len * A_HEADS * 2, A_HD)
    diff_v_rows = cache_diff_v.reshape(depth, bs, past_len, A_HEADS, A_VD // LANES, LANES)
    diff_v_rows = diff_v_rows.transpose(0, 1, 2, 4, 3, 5).reshape(depth, bs, past_len * A_HEADS * 2, LANES)
    dsa_k_rows = cache_dsa_k.reshape(depth, bs, past_len * B_KV, B_HD)
    dsa_v_rows = cache_dsa_v.reshape(depth, bs, past_len * B_KV, B_HD)
    dsa_kidx_t = jnp.swapaxes(cache_dsa_kidx, 2, 3)
    xp, xs = x_prompt, x_sample
    prompt_states, sample_states = [], []
    big = dict(w_in=_prep_w_in(w_in), w_branch=w_branch.astype(BF16), w_out=w_out.astype(BF16),
               w_up=w_up.astype(BF16), w_down=w_down.astype(BF16))
    for l in range(depth):
        lw = dict(big, norm_mix=norm_mix[l],
                  a_q_norm=a_q_norm[l], a_k_norm=a_k_norm[l], a_lambda=a_lambda[l], a_subln=a_subln[l],
                  b_q_norm=b_q_norm[l], b_k_norm=b_k_norm[l], c_conv=c_conv[l], c_a_log=c_a_log[l],
                  c_dt_bias=c_dt_bias[l], c_out_norm=c_out_norm[l], norm_ffn=norm_ffn[l],
                  ffn_conv=ffn_conv[l])
        mod = _ada(c_all, w_ada, b_ada, l)
        xp, st_p = _layer(xp, mod[:bp], prompt_past, lw, l)
        sample_past = (diff_k_rows, diff_v_rows, dsa_k_rows, dsa_v_rows, dsa_kidx_t,
                       state_gdn_conv[l], state_gdn[l], state_ffn_conv[l])
        xs, st_s = _layer(xs, mod[bp:bp + bs], sample_past, lw, l)
        prompt_states.append(st_p)
        sample_states.append(st_s)
    p_out = [jnp.stack(s, axis=0) for s in zip(*prompt_states)]
    s_out = [jnp.stack(s, axis=0) for s in zip(*sample_states)]
    return (xp, xs, *p_out, *s_out)
```

```python
import functools
import math

import jax
import jax.numpy as jnp
import numpy as np
from jax import lax
from jax.experimental import pallas as pl
from jax.experimental.pallas import tpu as pltpu

F32 = jnp.float32
BF16 = jnp.bfloat16

CHUNK = 64
CHUNK_SHIFT = 6
ROPE_THETA = 10000.0
EPS = 1e-6
A_HEADS, A_HD, A_VD = 4, 128, 256
B_HEADS, B_KV, B_HD = 8, 2, 128
I_HEADS, I_HD = 8, 64
I_HD_SHIFT = 6
TOPK_MAX = 256
C_HEADS, C_HD, C_CONV = 8, 128, 4
C_QKV = 3 * C_HEADS * C_HD
BRANCH_W = 1024
N_BRANCH = 3
FFN_CONV = 3

LANES = 128
SUBLANES = 8
VMEM_LIMIT = 56 * 1024 * 1024
ROW_TILE = 1024
COL_TILE = 512
SOLVE_BLOCK = 16
TILE_UNROLL = 4
FFN_ROW_CHUNK = 256
NEG = -1e30
KEY_MIN = -(2 ** 31)

OFF_CQKV = 0
OFF_CZ = 3072
OFF_AQ = 4096
OFF_AK = 5120
OFF_AV = 6144
OFF_BQ = 7168
OFF_BK = 8192
OFF_BV = 8448
OFF_BQI = 8704
OFF_BKI = 9216
OFF_SMALL = 9344
CA_LANE = 8
CB_LANE = 16
OFF_GT = 9728
N_PROJ = 15872


def _cparams(sem):
    return pltpu.CompilerParams(dimension_semantics=sem, vmem_limit_bytes=VMEM_LIMIT)


def _row_tiles(B, T, target=ROW_TILE):
    tt = min(T, target)
    assert T % tt == 0 and tt % SUBLANES == 0
    bb = max(1, min(B, target // tt))
    while B % bb:
        bb -= 1
    return bb, tt


def _dot(a, b):
    return jnp.dot(a, b, preferred_element_type=F32)


def _dot_nt(a, b):
    return lax.dot_general(a, b, (((1,), (1,)), ((), ())), preferred_element_type=F32)


def _bdot(a, b):
    return _dot(a.astype(BF16), b.astype(BF16))


def _tile_loop(n, body, init, unroll):
    nb = n // unroll

    def trip(tb, c):
        for u in range(unroll):
            c = body(tb * unroll + u, c)
        return c

    c = lax.fori_loop(0, nb, trip, init)
    return lax.fori_loop(nb * unroll, n, body, c)


def _ada_kernel(c_ref, w_ref, b_ref, o_ref):
    c = c_ref[...]
    s = (c * jax.nn.sigmoid(c)).astype(BF16)
    o_ref[...] = _dot(s, w_ref[...].astype(BF16)) + b_ref[...]


def _ada(c, w, b, layer):
    M, D = c.shape
    N = w.shape[2]
    tn = 1024
    return pl.pallas_call(
        _ada_kernel,
        grid=(N // tn,),
        in_specs=[pl.BlockSpec((M, D), lambda j: (0, 0)),
                  pl.BlockSpec((None, D, tn), lambda j: (layer, 0, j)),
                  pl.BlockSpec((None, 1, tn), lambda j: (layer, 0, j))],
        out_specs=pl.BlockSpec((M, tn), lambda j: (0, j)),
        out_shape=jax.ShapeDtypeStruct((M, N), F32),
        compiler_params=_cparams(("parallel",)),
        name="ada",
    )(c, w, b.reshape(-1, 1, N))


def _mm_norm_kernel(x_ref, g_ref, sh_ref, sc_ref, w_ref, o_ref, h_ref):
    bb, tt, D = x_ref.shape
    tn = w_ref.shape[1]

    @pl.when(pl.program_id(2) == 0)
    def _():
        x = x_ref[...]
        y = x * lax.rsqrt(jnp.mean(x * x, axis=-1, keepdims=True) + EPS) * g_ref[...]
        h = y * (1.0 + sc_ref[...]) + sh_ref[...]
        h_ref[...] = h.reshape(bb * tt, D).astype(BF16)

    acc = _dot(h_ref[...], w_ref[...])
    o_ref[...] = acc.reshape(bb, tt, tn).astype(o_ref.dtype)


def _mm_norm(x, g, shift, scale, w, layer, out_dtype=F32):
    B, T, D = x.shape
    N = w.shape[2]
    bb, tt = _row_tiles(B, T)
    tn = COL_TILE
    assert N % tn == 0
    return pl.pallas_call(
        _mm_norm_kernel,
        grid=(B // bb, T // tt, N // tn),
        in_specs=[pl.BlockSpec((bb, tt, D), lambda b, i, j: (b, i, 0)),
                  pl.BlockSpec((1, 1, D), lambda b, i, j: (0, 0, 0)),
                  pl.BlockSpec((bb, 1, D), lambda b, i, j: (b, 0, 0)),
                  pl.BlockSpec((bb, 1, D), lambda b, i, j: (b, 0, 0)),
                  pl.BlockSpec((None, D, tn), lambda b, i, j: (layer, 0, j))],
        out_specs=pl.BlockSpec((bb, tt, tn), lambda b, i, j: (b, i, j)),
        out_shape=jax.ShapeDtypeStruct((B, T, N), out_dtype),
        scratch_shapes=[pltpu.VMEM((bb * tt, D), BF16)],
        compiler_params=_cparams(("parallel", "parallel", "arbitrary")),
        name="mm_norm",
    )(x, g.reshape(1, 1, D), shift, scale, w)


def _mm_res_kernel(a_ref, w_ref, res_ref, g_ref, o_ref):
    bb, tt, K = a_ref.shape
    tn = w_ref.shape[1]
    acc = _dot(a_ref[...].reshape(bb * tt, K), w_ref[...])
    o_ref[...] = res_ref[...] + g_ref[...] * acc.reshape(bb, tt, tn)


def _mm_res(a, w, layer, res, gate, row_target=ROW_TILE):
    B, T, K = a.shape
    N = w.shape[2]
    bb, tt = _row_tiles(B, T, row_target)
    tn = COL_TILE
    assert N % tn == 0
    return pl.pallas_call(
        _mm_res_kernel,
        grid=(B // bb, T // tt, N // tn),
        in_specs=[pl.BlockSpec((bb, tt, K), lambda b, i, j: (b, i, 0)),
                  pl.BlockSpec((None, K, tn), lambda b, i, j: (layer, 0, j)),
                  pl.BlockSpec((bb, tt, tn), lambda b, i, j: (b, i, j)),
                  pl.BlockSpec((bb, 1, tn), lambda b, i, j: (b, 0, j))],
        out_specs=pl.BlockSpec((bb, tt, tn), lambda b, i, j: (b, i, j)),
        out_shape=jax.ShapeDtypeStruct((B, T, N), F32),
        compiler_params=_cparams(("parallel", "parallel", "arbitrary")),
        name="mm_res",
    )(a, w, res, gate)


def _merge_kernel(b0_ref, b1_ref, b2_ref, w_ref, g0_ref, g1_ref, g2_ref, o_ref):
    bb, tt, W = b0_ref.shape
    tn = w_ref.shape[2]
    acc = None
    for n, (br, gr) in enumerate(((b0_ref, g0_ref), (b1_ref, g1_ref), (b2_ref, g2_ref))):
        y = _dot(br[...].reshape(bb * tt, W), w_ref[n])
        t = jax.nn.sigmoid(gr[...].reshape(bb * tt, tn)) * y
        acc = t if acc is None else acc + t
    o_ref[...] = acc.reshape(bb, tt, tn).astype(o_ref.dtype)


def _merge(oa, ob, oc, w_branch, layer, proj, D):
    B, T, W = oa.shape
    bb, tt = _row_tiles(B, T)
    tn = COL_TILE
    gblk = OFF_GT // tn
    nper = D // tn
    br_spec = pl.BlockSpec((bb, tt, W), lambda b, i, j: (b, i, 0))

    def gate_spec(n):
        return pl.BlockSpec((bb, tt, tn), lambda b, i, j: (b, i, gblk + n * nper + j))

    return pl.pallas_call(
        _merge_kernel,
        grid=(B // bb, T // tt, D // tn),
        in_specs=[br_spec, br_spec, br_spec,
                  pl.BlockSpec((None, N_BRANCH, W, tn), lambda b, i, j: (layer, 0, 0, j)),
                  gate_spec(0), gate_spec(1), gate_spec(2)],
        out_specs=pl.BlockSpec((bb, tt, tn), lambda b, i, j: (b, i, j)),
        out_shape=jax.ShapeDtypeStruct((B, T, D), BF16),
        compiler_params=_cparams(("parallel", "parallel", "arbitrary")),
        name="merge",
    )(oa, ob, oc, w_branch, proj, proj, proj)


def _prep_kernel(aqk_ref, av_ref, bq_ref, bkv_ref, bqi_ref, bki_ref, rope_ref,
                 gaq_ref, gak_ref, gbq_ref, gbk_ref,
                 qa_o, ka_o, kab_o, va_o, vab_o, qb_o, kb_o, kbb_o, vb_o, vbb_o,
                 qi_o, ki_o, ki2_o):
    c1 = rope_ref[:, 0:LANES]
    s1 = rope_ref[:, LANES:2 * LANES]
    c2 = rope_ref[:, 2 * LANES:3 * LANES]
    s2 = rope_ref[:, 3 * LANES:4 * LANES]
    tt = c1.shape[0]
    lane = lax.broadcasted_iota(jnp.int32, (tt, LANES), 1)
    low_half = (lane & (I_HD - 1)) < (I_HD // 2)

    def norm_rope(x, g):
        y = x * lax.rsqrt(jnp.mean(x * x, axis=-1, keepdims=True) + EPS) * g
        return y * c1 + pltpu.roll(y, A_HD // 2, 1) * s1

    def rope64(x):
        r = jnp.where(low_half, pltpu.roll(x, LANES - I_HD // 2, 1), pltpu.roll(x, I_HD // 2, 1))
        return x * c2 + r * s2

    gaq, gak, gbq, gbk = gaq_ref[...], gak_ref[...], gbq_ref[...], gbk_ref[...]
    for hm in range(2 * A_HEADS):
        sl = slice(hm * LANES, (hm + 1) * LANES)
        q = norm_rope(aqk_ref[0, :, sl], gaq) * (A_HD ** -0.5)
        qa_o[0, :, sl] = q.astype(BF16)
        k = norm_rope(aqk_ref[0, :, 2 * A_HEADS * LANES + hm * LANES:2 * A_HEADS * LANES + (hm + 1) * LANES], gak)
        ka_o[0, pl.ds(hm, tt, stride=2 * A_HEADS), :] = k
        kab_o[0, :, sl] = k.astype(BF16)
    va = av_ref[0]
    for h in range(A_HEADS):
        for c in range(A_VD // LANES):
            va_o[0, pl.ds(c * A_HEADS + h, tt, stride=2 * A_HEADS), :] = (
                va[:, h * A_VD + c * LANES:h * A_VD + (c + 1) * LANES])
    vab_o[0] = va.astype(BF16)
    for h in range(B_HEADS):
        sl = slice(h * LANES, (h + 1) * LANES)
        q = norm_rope(bq_ref[0, :, sl], gbq) * (B_HD ** -0.5)
        qb_o[0, :, sl] = q.astype(BF16)
    for n in range(B_KV):
        sl = slice(n * LANES, (n + 1) * LANES)
        k = norm_rope(bkv_ref[0, :, sl], gbk)
        kb_o[0, pl.ds(n, tt, stride=B_KV), :] = k
        kbb_o[0, :, sl] = k.astype(BF16)
    vb = bkv_ref[0, :, B_KV * LANES:2 * B_KV * LANES]
    for n in range(B_KV):
        vb_o[0, pl.ds(n, tt, stride=B_KV), :] = vb[:, n * LANES:(n + 1) * LANES]
    vbb_o[0] = vb.astype(BF16)
    for p in range(I_HEADS * I_HD // LANES):
        sl = slice(p * LANES, (p + 1) * LANES)
        qi_o[0, :, sl] = (rope64(bqi_ref[0, :, sl]) * (I_HD ** -0.5)).astype(BF16)
    ki = rope64(bki_ref[0, :, 0:LANES])
    ki_o[0] = ki[:, 0:I_HD]
    ki2_o[0] = (ki + pltpu.roll(ki, I_HD, 1)).astype(BF16)


def _rope_table(pos):
    def tab(half):
        inv = jnp.power(ROPE_THETA, -jnp.arange(half, dtype=F32) / half)
        ang = pos.astype(F32)[:, None] * inv[None, :]
        return jnp.cos(ang), jnp.sin(ang)

    c, s = tab(A_HD // 2)
    ci, si = tab(I_HD // 2)
    return jnp.concatenate([c, c, -s, s, ci, ci, ci, ci, -si, si, -si, si], axis=1)


def _prep(proj, rope, gaq, gak, gbq, gbk):
    B, T, _ = proj.shape
    tt = min(T, 256)
    assert T % tt == 0

    def pspec(width, off):
        assert off % width == 0
        return pl.BlockSpec((1, tt, width), lambda b, i: (b, i, off // width))

    gspec = pl.BlockSpec((1, LANES), lambda b, i: (0, 0))
    ra, rb = 2 * A_HEADS, B_KV
    outs = [(1, 1024, BF16), (ra, LANES, F32), (1, 1024, BF16), (ra, LANES, F32), (1, 1024, BF16),
            (1, 1024, BF16), (rb, LANES, F32), (1, 256, BF16), (rb, LANES, F32), (1, 256, BF16),
            (1, 512, BF16), (1, I_HD, F32), (1, LANES, BF16)]
    return pl.pallas_call(
        _prep_kernel,
        grid=(B, T // tt),
        in_specs=[pspec(2048, OFF_AQ), pspec(1024, OFF_AV), pspec(1024, OFF_BQ), pspec(512, OFF_BK),
                  pspec(512, OFF_BQI), pspec(256, OFF_BKI),
                  pl.BlockSpec((tt, 4 * LANES), lambda b, i: (i, 0)),
                  gspec, gspec, gspec, gspec],
        out_specs=[pl.BlockSpec((1, tt * r, w), lambda b, i: (b, i, 0)) for r, w, _ in outs],
        out_shape=[jax.ShapeDtypeStruct((B, T * r, w), d) for r, w, d in outs],
        compiler_params=_cparams(("parallel", "parallel")),
        name="prep",
    )(proj, proj, proj, proj, proj, proj, rope,
      gaq.reshape(1, LANES), gak.reshape(1, LANES), gbq.reshape(1, LANES), gbk.reshape(1, LANES))


def _diff_kernel(q_ref, kn_ref, vn_ref, lam_ref, sub_ref, o_ref, acc0_s, acc1_s, sn_s, *, tq, lam_init):
    acc = (acc0_s, acc1_s)
    i = pl.program_id(2)
    q_pos0 = i * tq
    q = [q_ref[0, :, m * A_HD:(m + 1) * A_HD] for m in range(2)]
    msl = [slice(m * A_HD, (m + 1) * A_HD) for m in range(2)]
    wide = tq % LANES == 0

    def fold(x, op):
        f = x[:, 0:LANES]
        for c in range(1, x.shape[1] // LANES):
            f = op(f, x[:, c * LANES:(c + 1) * LANES])
        return f

    neg = tuple(jnp.full((tq, LANES), NEG, F32) for _ in range(2))
    zero = tuple(jnp.zeros((tq, LANES), F32) for _ in range(2))

    def scores(t, mr):
        r0 = pl.multiple_of(t * tq, tq)
        k = kn_ref[0, pl.ds(r0, tq), :]
        out = []
        for m in range(2):
            s = _dot_nt(q[m], k[:, msl[m]])
            sn_s[m, t] = s
            out.append(jnp.maximum(mr[m], fold(s, jnp.maximum)) if wide else mr[m])
        return tuple(out)
    mn = _tile_loop(i, scores, neg, TILE_UNROLL)

    r0 = pl.multiple_of(i * tq, tq)
    qpos = q_pos0 + lax.broadcasted_iota(jnp.int32, (tq, 1), 0)
    kpos = q_pos0 + lax.broadcasted_iota(jnp.int32, (1, tq), 1)
    vis = (kpos >> CHUNK_SHIFT) <= (qpos >> CHUNK_SHIFT)
    kd = kn_ref[0, pl.ds(r0, tq), :]
    mrow = []
    for m in range(2):
        s = jnp.where(vis, _dot_nt(q[m], kd[:, msl[m]]), NEG)
        sn_s[m, i] = s
        mx = jnp.max(s, axis=-1, keepdims=True)
        if wide:
            mx = jnp.maximum(mx, jnp.max(mn[m], axis=-1, keepdims=True))
        if wide:
            mx = pltpu.repeat(jnp.broadcast_to(mx, (tq, LANES)), tq // LANES, axis=1)
        mrow.append(mx)

    for m in range(2):
        acc[m][...] = jnp.zeros(acc[m].shape, F32)

    def weigh(t, lr):
        r0 = pl.multiple_of(t * tq, tq)
        v = vn_ref[0, pl.ds(r0, tq), :]
        out = []
        for m in range(2):
            p = jnp.exp(sn_s[m, t] - mrow[m])
            if wide:
                out.append(lr[m] + fold(p, jnp.add))
            else:
                out.append(lr[m] + jnp.sum(p, axis=-1, keepdims=True))
            acc[m][...] += _dot(p.astype(BF16), v)
        return tuple(out)
    ln = _tile_loop(i + 1, weigh, zero if wide else tuple(jnp.zeros((tq, 1), F32) for _ in range(2)),
                    TILE_UNROLL)
    lrow = [jnp.sum(ln[m], axis=-1, keepdims=True) for m in range(2)]

    lp = lam_ref[...]
    lam = (jnp.exp(jnp.sum(lp[0:1] * lp[1:2], axis=-1, keepdims=True))
           - jnp.exp(jnp.sum(lp[2:3] * lp[3:4], axis=-1, keepdims=True)) + lam_init)
    o = acc[0][...] / lrow[0] - lam * (acc[1][...] / lrow[1])
    o = o * lax.rsqrt(jnp.mean(o * o, axis=-1, keepdims=True) + EPS) * sub_ref[...]
    o_ref[0] = (o * (1.0 - lam_init)).astype(o_ref.dtype)


def _diff_attn(qa, kn, vn, lam_p, subln, lam_init):
    B, T, _ = qa.shape
    tq = min(T, 256)
    assert T % tq == 0 and (tq % CHUNK == 0 or tq == T)
    W = 2 * A_HD
    return pl.pallas_call(
        functools.partial(_diff_kernel, tq=tq, lam_init=lam_init),
        grid=(B, A_HEADS, T // tq),
        in_specs=[pl.BlockSpec((1, tq, W), lambda b, h, i: (b, i, h)),
                  pl.BlockSpec((1, T, W), lambda b, h, i: (b, 0, h)),
                  pl.BlockSpec((1, T, W), lambda b, h, i: (b, 0, h)),
                  pl.BlockSpec((4, A_HD), lambda b, h, i: (0, 0)),
                  pl.BlockSpec((1, A_VD), lambda b, h, i: (0, 0))],
        out_specs=pl.BlockSpec((1, tq, W), lambda b, h, i: (b, i, h)),
        out_shape=jax.ShapeDtypeStruct((B, T, A_HEADS * A_VD), BF16),
        scratch_shapes=[pltpu.VMEM((tq, A_VD), F32), pltpu.VMEM((tq, A_VD), F32),
                        pltpu.VMEM((2, T // tq, tq, tq), F32)],
        compiler_params=_cparams(("parallel", "parallel", "arbitrary")),
        name="diff_attn",
    )(qa, kn, vn, lam_p, subln.reshape(1, A_VD))


def _diff_decode_kernel(q_ref, kn_ref, vn_ref, kp_ref, vp_ref, lam_ref, sub_ref, o_ref, *, P, lam_init):
    T = q_ref.shape[1]
    per_pos = A_HEADS * 2
    qpos = P + lax.broadcasted_iota(jnp.int32, (T, 1), 0)
    kpos = P + lax.broadcasted_iota(jnp.int32, (1, T), 1)
    vis = (kpos >> CHUNK_SHIFT) <= (qpos >> CHUNK_SHIFT)
    lp = lam_ref[...]
    lam = (jnp.exp(jnp.sum(lp[0:1] * lp[1:2], axis=-1, keepdims=True))
           - jnp.exp(jnp.sum(lp[2:3] * lp[3:4], axis=-1, keepdims=True)) + lam_init)
    for h in range(A_HEADS):
        v_past = jnp.concatenate(
            [vp_ref[pl.ds(c * A_HEADS + h, P, stride=per_pos), :] for c in range(A_VD // LANES)],
            axis=-1).astype(BF16)
        v_new = vn_ref[0, :, h * A_VD:(h + 1) * A_VD]
        outs = []
        for m in range(2):
            hm = h * 2 + m
            q = q_ref[0, :, hm * A_HD:(hm + 1) * A_HD]
            k_past = kp_ref[pl.ds(hm, P, stride=per_pos), :].astype(BF16)
            s_p = _dot_nt(q, k_past)
            s_n = jnp.where(vis, _dot_nt(q, kn_ref[0, :, hm * A_HD:(hm + 1) * A_HD]), NEG)
            mx = jnp.maximum(jnp.max(s_p, axis=-1, keepdims=True), jnp.max(s_n, axis=-1, keepdims=True))
            p_p = jnp.exp(s_p - mx)
            p_n = jnp.exp(s_n - mx)
            l = jnp.sum(p_p, axis=-1, keepdims=True) + jnp.sum(p_n, axis=-1, keepdims=True)
            outs.append((_dot(p_p.astype(BF16), v_past) + _dot(p_n.astype(BF16), v_new)) / l)
        o = outs[0] - lam * outs[1]
        o = o * lax.rsqrt(jnp.mean(o * o, axis=-1, keepdims=True) + EPS) * sub_ref[...]
        o_ref[0, :, h * A_VD:(h + 1) * A_VD] = (o * (1.0 - lam_init)).astype(o_ref.dtype)


def _diff_decode(qa, kn, vn, kp_rows, vp_rows, layer, lam_p, subln, P, lam_init):
    B, T, W = qa.shape
    assert P % CHUNK == 0 and kp_rows.shape[2] == P * A_HEADS * 2 and vp_rows.shape[2] == P * A_HEADS * 2
    new_spec = pl.BlockSpec((1, T, W), lambda b: (b, 0, 0))
    cache_spec = pl.BlockSpec((None, None, P * A_HEADS * 2, LANES), lambda b: (layer, b, 0, 0))
    return pl.pallas_call(
        functools.partial(_diff_decode_kernel, P=P, lam_init=lam_init),
        grid=(B,),
        in_specs=[new_spec, new_spec, new_spec, cache_spec, cache_spec,
                  pl.BlockSpec((4, A_HD), lambda b: (0, 0)),
                  pl.BlockSpec((1, A_VD), lambda b: (0, 0))],
        out_specs=new_spec,
        out_shape=jax.ShapeDtypeStruct((B, T, W), BF16),
        compiler_params=_cparams(("parallel",)),
        name="diff_decode",
    )(qa, kn, vn, kp_rows, vp_rows, lam_p, subln.reshape(1, A_VD))


def _dsa_kernel(qi_ref, wi_ref, qb_ref, ki_ref, kb_ref, vb_ref, o_ref,
                key_s, bias_s, s_s, *, P, L, tq, tk, topk):
    i = pl.program_id(1)
    q_pos0 = P + i * tq
    Lp = ki_ref.shape[1]
    ncol = jnp.minimum(L, ((q_pos0 + tq - 1) // CHUNK + 1) * CHUNK)
    nt = (ncol + tk - 1) // tk
    qpos = q_pos0 + lax.broadcasted_iota(jnp.int32, (tq, 1), 0)
    grp = B_HEADS // B_KV
    lane = lax.broadcasted_iota(jnp.int32, (tq, LANES), 1)
    wi = wi_ref[0] * (I_HEADS ** -0.5)
    w_lanes = [jnp.broadcast_to(wi[:, h:h + 1], (tq, LANES)) for h in range(I_HEADS)]

    def vis_of(t):
        kpos = t * tk + lax.broadcasted_iota(jnp.int32, (1, tk), 1)
        return ((kpos >> CHUNK_SHIFT) <= (qpos >> CHUNK_SHIFT)) & (kpos < L)

    def score_tile(t, carry):
        c0 = pl.multiple_of(t * tk, tk)
        kit = ki_ref[0, pl.ds(c0, tk), :]
        sc = jnp.zeros((tq, tk), F32)
        for pr in range(I_HEADS * I_HD // LANES):
            qpair = qi_ref[0, :, pr * LANES:(pr + 1) * LANES]
            for half in range(LANES // I_HD):
                h = pr * (LANES // I_HD) + half
                qh = jnp.where((lane >> I_HD_SHIFT) == half, qpair, jnp.zeros_like(qpair))
                isc = _dot_nt(qh, kit)
                sc = sc + jnp.maximum(isc, 0.0) * pltpu.repeat(w_lanes[h], tk // LANES, axis=1)
        sc = jnp.where(vis_of(t), sc, -jnp.inf)
        key_s[t] = _float_order_key(sc)
        return carry

    lax.fori_loop(0, nt, score_tile, 0)

    def counts(pred_fns):
        def body(t, accs):
            kk = key_s[t]
            out = []
            for pred_fn, acc in zip(pred_fns, accs):
                hit = jnp.where(pred_fn(kk), 1, 0)
                part = hit[:, 0:LANES]
                for c in range(1, tk // LANES):
                    part = part + hit[:, c * LANES:(c + 1) * LANES]
                out.append(acc + part)
            return tuple(out)
        accs = _tile_loop(nt, body, tuple(jnp.zeros((tq, LANES), jnp.int32) for _ in pred_fns), 2)
        return [jnp.sum(a.astype(F32), axis=-1, keepdims=True) for a in accs]

    def select(_):
        def bit_body(bi, thr):
            cand = thr + jnp.left_shift(jnp.int32(1), 31 - bi)
            (cnt,) = counts([lambda kk: kk >= cand])
            return jnp.where(cnt >= topk, cand, thr)
        return lax.fori_loop(0, 32, bit_body, jnp.full((tq, 1), KEY_MIN, jnp.int32))

    thr = lax.cond(ncol > topk, select, lambda _: jnp.full((tq, 1), KEY_MIN, jnp.int32), 0)
    n_gt, n_eq = counts([lambda kk: kk > thr, lambda kk: kk == thr])
    need = topk - n_gt
    neg_inf_key = jnp.int32(KEY_MIN + 0x7FFFFF)
    tie_break = jnp.max(jnp.where((n_eq > need) & (thr != neg_inf_key), 1.0, 0.0)) > 0.5

    def bias_fast(_):
        def body(t, c):
            sel = (key_s[t] >= thr) & vis_of(t)
            bias_s[t] = jnp.where(sel, 0.0, NEG)
            return c
        lax.fori_loop(0, nt, body, 0)
        return 0

    def bias_ties(_):
        r = lax.broadcasted_iota(jnp.int32, (LANES, LANES), 0)
        c = lax.broadcasted_iota(jnp.int32, (LANES, LANES), 1)
        before = jnp.where(r < c, 1.0, 0.0).astype(BF16)

        def body(t, seen):
            kk = key_s[t]
            vis = vis_of(t)
            for cb in range(tk // LANES):
                sl = slice(cb * LANES, (cb + 1) * LANES)
                eq = kk[:, sl] == thr
                eqf = jnp.where(eq, 1.0, 0.0)
                rank = seen + _dot(eqf.astype(BF16), before)
                sel = ((kk[:, sl] > thr) | (eq & (rank < need.astype(F32)))) & vis[:, sl]
                bias_s[t, :, sl] = jnp.where(sel, 0.0, NEG)
                seen = seen + jnp.sum(eqf, axis=-1, keepdims=True)
            return seen
        lax.fori_loop(0, nt, body, jnp.zeros((tq, 1), F32))
        return 0

    lax.cond(tie_break, bias_ties, bias_fast, 0)

    def fold(x, op):
        f = x[:, 0:LANES]
        for c in range(1, tk // LANES):
            f = op(f, x[:, c * LANES:(c + 1) * LANES])
        return f

    for n in range(B_KV):
        kv = slice(n * B_HD, (n + 1) * B_HD)

        def scores(t, mrun, kv=kv, n=n):
            c0 = pl.multiple_of(t * tk, tk)
            bias = bias_s[t]
            k = kb_ref[0, pl.ds(c0, tk), kv]
            out = []
            for g in range(grp):
                h = n * grp + g
                s = _dot_nt(qb_ref[0, :, h * B_HD:(h + 1) * B_HD], k) + bias
                s_s[g, t] = s
                out.append(jnp.maximum(mrun[g], fold(s, jnp.maximum)))
            return tuple(out)

        mrun = _tile_loop(nt, scores, tuple(jnp.full((tq, LANES), NEG, F32) for _ in range(grp)), 2)
        mrow = [pltpu.repeat(jnp.broadcast_to(jnp.max(m, axis=-1, keepdims=True), (tq, LANES)),
                             tk // LANES, axis=1) for m in mrun]

        def weigh(t, carry, kv=kv, mrow=mrow):
            lrun, acc = carry
            c0 = pl.multiple_of(t * tk, tk)
            v = vb_ref[0, pl.ds(c0, tk), kv]
            lnew, anew = [], []
            for g in range(grp):
                p = jnp.exp(s_s[g, t] - mrow[g])
                lnew.append(lrun[g] + fold(p, jnp.add))
                anew.append(acc[g] + _dot(p.astype(BF16), v))
            return tuple(lnew), tuple(anew)

        lrun, acc = _tile_loop(
            nt, weigh,
            (tuple(jnp.zeros((tq, LANES), F32) for _ in range(grp)),
             tuple(jnp.zeros((tq, B_HD), F32) for _ in range(grp))), 2)
        for g in range(grp):
            h = n * grp + g
            l = jnp.sum(lrun[g], axis=-1, keepdims=True)
            o_ref[0, :, h * B_HD:(h + 1) * B_HD] = (acc[g] / l).astype(o_ref.dtype)


def _dsa(qi, proj, qb, ki2, kb, vb, P, L, tk):
    B, T, _ = qb.shape
    Lp = ki2.shape[1]
    tq = min(T, 128)
    assert T % tq == 0 and Lp % tk == 0 and tk % LANES == 0
    topk = min(TOPK_MAX, L // 4)
    kern = functools.partial(_dsa_kernel, P=P, L=L, tq=tq, tk=tk, topk=topk)
    return pl.pallas_call(
        kern,
        grid=(B, T // tq),
        in_specs=[pl.BlockSpec((1, tq, I_HEADS * I_HD), lambda b, i: (b, i, 0)),
                  pl.BlockSpec((1, tq, LANES), lambda b, i: (b, i, OFF_SMALL // LANES)),
                  pl.BlockSpec((1, tq, B_HEADS * B_HD), lambda b, i: (b, i, 0)),
                  pl.BlockSpec((1, Lp, LANES), lambda b, i: (b, 0, 0)),
                  pl.BlockSpec((1, Lp, B_KV * B_HD), lambda b, i: (b, 0, 0)),
                  pl.BlockSpec((1, Lp, B_KV * B_HD), lambda b, i: (b, 0, 0))],
        out_specs=pl.BlockSpec((1, tq, B_HEADS * B_HD), lambda b, i: (b, i, 0)),
        out_shape=jax.ShapeDtypeStruct((B, T, B_HEADS * B_HD), BF16),
        scratch_shapes=[pltpu.VMEM((Lp // tk, tq, tk), jnp.int32), pltpu.VMEM((Lp // tk, tq, tk), F32),
                        pltpu.VMEM((B_HEADS // B_KV, Lp // tk, tq, tk), F32)],
        compiler_params=_cparams(("parallel", "arbitrary")),
        name="dsa",
    )(qi, proj, qb, ki2, kb, vb)


def _float_order_key(x):
    bits = pltpu.bitcast(x, jnp.int32)
    bits = jnp.where(bits == KEY_MIN, 0, bits)
    return bits ^ ((bits >> 31) & 0x7FFFFFFF)


def _dsa_decode_kernel(qi_ref, wi_ref, qb_ref, kin_ref, kbn_ref, vbn_ref, kip_ref, kp_ref, vp_ref, o_ref,
                       *, P, topk):
    T = qi_ref.shape[1]
    grp = B_HEADS // B_KV
    qpos = P + lax.broadcasted_iota(jnp.int32, (T, 1), 0)
    kpos = P + lax.broadcasted_iota(jnp.int32, (1, T), 1)
    vis_n = (kpos >> CHUNK_SHIFT) <= (qpos >> CHUNK_SHIFT)
    wi = wi_ref[0] * (I_HEADS ** -0.5)
    ki_pT = kip_ref[...].astype(BF16)
    ki_n = kin_ref[0, :, 0:I_HD]

    sc_p = jnp.zeros((T, P), F32)
    sc_n = jnp.zeros((T, T), F32)
    for h in range(I_HEADS):
        qh = qi_ref[0, :, h * I_HD:(h + 1) * I_HD]
        w = wi[:, h:h + 1]
        sc_p = sc_p + jnp.maximum(_dot(qh, ki_pT), 0.0) * w
        sc_n = sc_n + jnp.maximum(_dot_nt(qh, ki_n), 0.0) * w
    key_p = _float_order_key(sc_p)
    key_n = _float_order_key(jnp.where(vis_n, sc_n, -jnp.inf))

    def count(pred):
        return (jnp.sum(jnp.where(pred(key_p), 1.0, 0.0), axis=-1, keepdims=True)
                + jnp.sum(jnp.where(pred(key_n), 1.0, 0.0), axis=-1, keepdims=True))

    def bit_body(bi, thr):
        cand = thr + jnp.left_shift(jnp.int32(1), 31 - bi)
        return jnp.where(count(lambda kk: kk >= cand) >= topk, cand, thr)

    thr = lax.fori_loop(0, 32, bit_body, jnp.full((T, 1), KEY_MIN, jnp.int32))
    need = topk - count(lambda kk: kk > thr)

    def before(n):
        r = lax.broadcasted_iota(jnp.int32, (n, n), 0)
        c = lax.broadcasted_iota(jnp.int32, (n, n), 1)
        return jnp.where(r < c, 1.0, 0.0).astype(BF16)

    def select(kk, seen, pre):
        eq = kk == thr
        eqf = jnp.where(eq, 1.0, 0.0)
        rank = seen + _dot(eqf.astype(BF16), pre)
        sel = (kk > thr) | (eq & (rank < need))
        return sel, seen + jnp.sum(eqf, axis=-1, keepdims=True)

    pre_l = before(LANES)
    seen = jnp.zeros((T, 1), F32)
    bias_blocks = []
    for cb in range(P // LANES):
        sel, seen = select(key_p[:, cb * LANES:(cb + 1) * LANES], seen, pre_l)
        bias_blocks.append(jnp.where(sel, 0.0, NEG))
    bias_p = jnp.concatenate(bias_blocks, axis=-1)
    sel, _ = select(key_n, seen, before(T))
    bias_n = jnp.where(sel & vis_n, 0.0, NEG)

    for n in range(B_KV):
        kv = slice(n * B_HD, (n + 1) * B_HD)
        k_p = kp_ref[pl.ds(n, P, stride=B_KV), :].astype(BF16)
        v_p = vp_ref[pl.ds(n, P, stride=B_KV), :].astype(BF16)
        k_n = kbn_ref[0, :, kv]
        v_n = vbn_ref[0, :, kv]
        for g in range(grp):
            h = n * grp + g
            q = qb_ref[0, :, h * B_HD:(h + 1) * B_HD]
            s_p = _dot_nt(q, k_p) + bias_p
            s_n = _dot_nt(q, k_n) + bias_n
            mx = jnp.maximum(jnp.max(s_p, axis=-1, keepdims=True), jnp.max(s_n, axis=-1, keepdims=True))
            p_p = jnp.exp(s_p - mx)
            p_n = jnp.exp(s_n - mx)
            l = jnp.sum(p_p, axis=-1, keepdims=True) + jnp.sum(p_n, axis=-1, keepdims=True)
            o = _dot(p_p.astype(BF16), v_p) + _dot(p_n.astype(BF16), v_n)
            o_ref[0, :, h * B_HD:(h + 1) * B_HD] = (o / l).astype(o_ref.dtype)


def _dsa_decode(qi, proj, qb, ki2, kbb, vbb, ki_past_t, kp_rows, vp_rows, layer):
    B, T, _ = qb.shape
    P = ki_past_t.shape[3]
    assert P % CHUNK == 0 and P % LANES == 0 and kp_rows.shape[2] == P * B_KV
    topk = min(TOPK_MAX, (P + T) // 4)

    def new(width, blk=0):
        return pl.BlockSpec((1, T, width), lambda b: (b, 0, blk))

    rows = pl.BlockSpec((None, None, P * B_KV, LANES), lambda b: (layer, b, 0, 0))
    return pl.pallas_call(
        functools.partial(_dsa_decode_kernel, P=P, topk=topk),
        grid=(B,),
        in_specs=[new(I_HEADS * I_HD), new(LANES, OFF_SMALL // LANES), new(B_HEADS * B_HD),
                  new(LANES), new(B_KV * B_HD), new(B_KV * B_HD),
                  pl.BlockSpec((None, None, I_HD, P), lambda b: (layer, b, 0, 0)), rows, rows],
        out_specs=new(B_HEADS * B_HD),
        out_shape=jax.ShapeDtypeStruct((B, T, B_HEADS * B_HD), BF16),
        compiler_params=_cparams(("parallel",)),
        name="dsa_decode",
    )(qi, proj, qb, ki2, kbb, vbb, ki_past_t, kp_rows, vp_rows)


def _bmm(a, b):
    return jnp.einsum('bij,bjk->bik', a.astype(BF16), b.astype(BF16), preferred_element_type=F32)


def _bmm_nt(a, b):
    return jnp.einsum('bik,bjk->bij', a.astype(BF16), b.astype(BF16), preferred_element_type=F32)


def _gdn_intra_kernel(x_ref, halo_ref, sm_ref, cw_ref, cbuf_ref, alog_ref, dtb_ref,
                      u_o, w_o, qt_o, ktT_o, qkg_o, dec_o, buf_s, *, cc, nck):
    i = pl.program_id(1)
    R = nck * cc
    HW = C_HEADS * C_HD

    @pl.when(i == 0)
    def _():
        buf_s[0:SUBLANES, :] = jnp.zeros((SUBLANES, C_QKV), F32)
        buf_s[SUBLANES - (C_CONV - 1):SUBLANES, :] = cbuf_ref[0]

    @pl.when(i > 0)
    def _():
        buf_s[0:SUBLANES, :] = halo_ref[0]

    buf_s[SUBLANES:SUBLANES + R, :] = x_ref[0]
    y = None
    for jw in range(C_CONV):
        off = SUBLANES - (C_CONV - 1) + jw
        term = buf_s[off:off + R, :] * cw_ref[jw:jw + 1, :]
        y = term if y is None else y + term
    y = y * jax.nn.sigmoid(y)

    sm = sm_ref[0]
    xg = sm + dtb_ref[...]
    softplus = jnp.maximum(xg, 0.0) + jnp.log(1.0 + jnp.exp(-jnp.abs(xg)))
    g = -jnp.exp(alog_ref[...]) * softplus
    beta = jax.nn.sigmoid(sm)

    cshift = cc.bit_length() - 1
    rr = lax.broadcasted_iota(jnp.int32, (R, R), 0)
    rc = lax.broadcasted_iota(jnp.int32, (R, R), 1)
    cum = ((rr >> cshift) == (rc >> cshift)) & (rr >= rc)
    G = jnp.dot(jnp.where(cum, 1.0, 0.0), g, preferred_element_type=F32,
                precision=lax.Precision.HIGHEST)

    qs, ks, vs, gcs, bcs = [], [], [], [], []
    for ck in range(nck):
        rows = slice(ck * cc, (ck + 1) * cc)
        for h in range(C_HEADS):
            qs.append(y[rows, h * C_HD:(h + 1) * C_HD])
            ks.append(y[rows, HW + h * C_HD:HW + (h + 1) * C_HD])
            vs.append(y[rows, 2 * HW + h * C_HD:2 * HW + (h + 1) * C_HD])
            gcs.append(G[rows, CA_LANE + h:CA_LANE + h + 1])
            bcs.append(beta[rows, CB_LANE + h:CB_LANE + h + 1])
    q3 = jnp.stack(qs)
    k3 = jnp.stack(ks)
    v3 = jnp.stack(vs)
    Gc = jnp.stack(gcs)
    bc = jnp.stack(bcs)
    q3 = q3 * lax.rsqrt(jnp.sum(q3 * q3, axis=-1, keepdims=True) + EPS) * (C_HD ** -0.5)
    k3 = k3 * lax.rsqrt(jnp.sum(k3 * k3, axis=-1, keepdims=True) + EPS)

    ri = lax.broadcasted_iota(jnp.int32, (cc, cc), 0)
    ci = lax.broadcasted_iota(jnp.int32, (cc, cc), 1)
    incl = (ri >= ci)[None]
    strict = (ri > ci)[None]
    eye = (ri == ci)[None]
    bdiag = ((ri // SOLVE_BLOCK) == (ci // SOLVE_BLOCK))[None]

    Gr = jnp.sum(jnp.where(eye, Gc, 0.0), axis=1, keepdims=True)
    gam = jnp.where(incl, jnp.exp(jnp.where(incl, Gc - Gr, 0.0)), 0.0)
    k16 = k3.astype(BF16)
    A = jnp.where(strict, bc * _bmm_nt(k16, k16) * gam, 0.0)
    Nd = jnp.where(bdiag, -A, 0.0)
    E = jnp.where(bdiag, 0.0, A)
    N2 = _bmm(Nd, Nd)
    N4 = _bmm(N2, N2)
    N8 = _bmm(N4, N4)
    Q = Nd + N2 + _bmm(Nd, N2)
    Q = Q + N4 + _bmm(Q, N4)
    Q = Q + N8 + _bmm(Q, N8)
    Fm = E + _bmm(Q, E)
    F2 = _bmm(Fm, Fm)
    Rm = F2 - Fm - _bmm(Fm, F2)
    Wm = Rm + Q + _bmm(Rm, Q)
    eG = jnp.exp(Gc)
    rhs = jnp.concatenate([bc * v3, (bc * eG) * k3], axis=-1)
    sol = rhs + _bmm(Wm, rhs)
    qkg = _bmm_nt(q3, k16) * gam
    gl = Gc[:, cc - 1:cc, :]
    qt = q3 * eG
    kt = k3 * jnp.exp(gl - Gc)
    dec = jnp.exp(gl)

    for ck in range(nck):
        rows = slice(ck * cc, (ck + 1) * cc)
        for h in range(C_HEADS):
            b = ck * C_HEADS + h
            cols = slice(h * C_HD, (h + 1) * C_HD)
            u_o[0, rows, cols] = sol[b, :, 0:C_HD]
            w_o[0, rows, cols] = sol[b, :, C_HD:2 * C_HD].astype(BF16)
            qt_o[0, rows, cols] = qt[b].astype(BF16)
            ktT_o[0, ck, h] = kt[b].T.astype(BF16)
            qkg_o[0, ck, h] = qkg[b].astype(BF16)
            dec_o[0, ck, h] = jnp.broadcast_to(dec[b], (1, C_HD))


def _gdn_inter_kernel(u_ref, w_ref, qt_ref, ktT_ref, qkg_ref, dec_ref, z_ref, s0_ref, gn_ref,
                      o_ref, sfin_ref, st_s, *, cc, nck):
    c = pl.program_id(1)

    @pl.when(c == 0)
    def _():
        st_s[...] = s0_ref[0]

    S = st_s[...]
    for ck in range(nck):
        rows = slice(ck * cc, (ck + 1) * cc)

        def heads(ref):
            return jnp.stack([ref[0, rows, h * C_HD:(h + 1) * C_HD] for h in range(C_HEADS)])

        S16 = S.astype(BF16)
        v_new = heads(u_ref) - _bmm(heads(w_ref), S16)
        vn16 = v_new.astype(BF16)
        o = _bmm(heads(qt_ref), S16) + _bmm(qkg_ref[0, ck], vn16)
        S = dec_ref[0, ck] * S + _bmm(ktT_ref[0, ck], vn16)
        on = o * lax.rsqrt(jnp.mean(o * o, axis=-1, keepdims=True) + EPS) * gn_ref[...]
        for h in range(C_HEADS):
            cols = slice(h * C_HD, (h + 1) * C_HD)
            z = z_ref[0, rows, cols]
            o_ref[0, rows, cols] = (on[h] * (z * jax.nn.sigmoid(z))).astype(o_ref.dtype)
    st_s[...] = S

    @pl.when(c == pl.num_programs(1) - 1)
    def _():
        sfin_ref[0] = S


def _gdn(proj, conv_w, cbuf, S0, a_log, dt_bias, out_norm):
    B, T, _ = proj.shape
    cc = min(CHUNK, T)
    assert T % cc == 0 and cc % SOLVE_BLOCK == 0 and cc // SOLVE_BLOCK <= 4 and cc >= SUBLANES
    assert cc & (cc - 1) == 0
    HW = C_HEADS * C_HD
    NC = T // cc
    nck_a = 2 if NC % 2 == 0 else 1
    nck_b = 4 if NC % 4 == 0 else 1

    def at_ca(p):
        return jnp.zeros((1, LANES), F32).at[0, CA_LANE:CA_LANE + C_HEADS].set(p.astype(F32))

    Ra = nck_a * cc
    hb = Ra // SUBLANES
    u, w, qt, ktT, qkg, dec = pl.pallas_call(
        functools.partial(_gdn_intra_kernel, cc=cc, nck=nck_a),
        grid=(B, NC // nck_a),
        in_specs=[pl.BlockSpec((1, Ra, C_QKV), lambda b, i: (b, i, OFF_CQKV // C_QKV)),
                  pl.BlockSpec((1, SUBLANES, C_QKV),
                               lambda b, i: (b, jnp.maximum(i * hb - 1, 0), OFF_CQKV // C_QKV)),
                  pl.BlockSpec((1, Ra, LANES), lambda b, i: (b, i, OFF_SMALL // LANES)),
                  pl.BlockSpec((C_CONV, C_QKV), lambda b, i: (0, 0)),
                  pl.BlockSpec((1, C_CONV - 1, C_QKV), lambda b, i: (b, 0, 0)),
                  pl.BlockSpec((1, LANES), lambda b, i: (0, 0)),
                  pl.BlockSpec((1, LANES), lambda b, i: (0, 0))],
        out_specs=[pl.BlockSpec((1, Ra, HW), lambda b, i: (b, i, 0)),
                   pl.BlockSpec((1, Ra, HW), lambda b, i: (b, i, 0)),
                   pl.BlockSpec((1, Ra, HW), lambda b, i: (b, i, 0)),
                   pl.BlockSpec((1, nck_a, C_HEADS, C_HD, cc), lambda b, i: (b, i, 0, 0, 0)),
                   pl.BlockSpec((1, nck_a, C_HEADS, cc, cc), lambda b, i: (b, i, 0, 0, 0)),
                   pl.BlockSpec((1, nck_a, C_HEADS, 1, C_HD), lambda b, i: (b, i, 0, 0, 0))],
        out_shape=[jax.ShapeDtypeStruct((B, T, HW), F32),
                   jax.ShapeDtypeStruct((B, T, HW), BF16),
                   jax.ShapeDtypeStruct((B, T, HW), BF16),
                   jax.ShapeDtypeStruct((B, NC, C_HEADS, C_HD, cc), BF16),
                   jax.ShapeDtypeStruct((B, NC, C_HEADS, cc, cc), BF16),
                   jax.ShapeDtypeStruct((B, NC, C_HEADS, 1, C_HD), F32)],
        scratch_shapes=[pltpu.VMEM((Ra + SUBLANES, C_QKV), F32)],
        compiler_params=_cparams(("parallel", "parallel")),
        name="gdn_intra",
    )(proj, proj, proj, conv_w, cbuf, at_ca(a_log), at_ca(dt_bias))

    Rb = nck_b * cc
    return pl.pallas_call(
        functools.partial(_gdn_inter_kernel, cc=cc, nck=nck_b),
        grid=(B, NC // nck_b),
        in_specs=[pl.BlockSpec((1, Rb, HW), lambda b, c: (b, c, 0)),
                  pl.BlockSpec((1, Rb, HW), lambda b, c: (b, c, 0)),
                  pl.BlockSpec((1, Rb, HW), lambda b, c: (b, c, 0)),
                  pl.BlockSpec((1, nck_b, C_HEADS, C_HD, cc), lambda b, c: (b, c, 0, 0, 0)),
                  pl.BlockSpec((1, nck_b, C_HEADS, cc, cc), lambda b, c: (b, c, 0, 0, 0)),
                  pl.BlockSpec((1, nck_b, C_HEADS, 1, C_HD), lambda b, c: (b, c, 0, 0, 0)),
                  pl.BlockSpec((1, Rb, HW), lambda b, c: (b, c, OFF_CZ // HW)),
                  pl.BlockSpec((1, C_HEADS, C_HD, C_HD), lambda b, c: (b, 0, 0, 0)),
                  pl.BlockSpec((1, C_HD), lambda b, c: (0, 0))],
        out_specs=[pl.BlockSpec((1, Rb, HW), lambda b, c: (b, c, 0)),
                   pl.BlockSpec((1, C_HEADS, C_HD, C_HD), lambda b, c: (b, 0, 0, 0))],
        out_shape=[jax.ShapeDtypeStruct((B, T, HW), BF16),
                   jax.ShapeDtypeStruct((B, C_HEADS, C_HD, C_HD), F32)],
        scratch_shapes=[pltpu.VMEM((C_HEADS, C_HD, C_HD), F32)],
        compiler_params=_cparams(("parallel", "arbitrary")),
        name="gdn_inter",
    )(u, w, qt, ktT, qkg, dec, proj, S0, out_norm.reshape(1, C_HD))


def _ffn_up_kernel(x_ref, g_ref, sh_ref, sc_ref, wg_ref, wu_ref, cwg_ref, cwu_ref, fg_ref, fu_ref,
                   act_o, fng_o, fnu_o, h_ref, *scr, split_rows):
    bb, tt, D = x_ref.shape
    tn = wg_ref.shape[1]
    i = pl.program_id(1)
    j = pl.program_id(2)
    lo = SUBLANES - (FFN_CONV - 1)

    @pl.when(j == 0)
    def _():
        x = x_ref[...]
        y = x * lax.rsqrt(jnp.mean(x * x, axis=-1, keepdims=True) + EPS) * g_ref[...]
        h = y * (1.0 + sc_ref[...]) + sh_ref[...]
        h_ref[...] = h.reshape(bb * tt, D).astype(BF16)

    if split_rows:
        (carry_s,) = scr
        halves = ((wg_ref, cwg_ref, fg_ref, fng_o, 0), (wu_ref, cwu_ref, fu_ref, fnu_o, 1))
        for _, _, f_ref, _, idx in halves:
            @pl.when(i == 0)
            def _(f_ref=f_ref, idx=idx):
                carry_s[idx, j, 0, lo:SUBLANES, :] = f_ref[0]
        prev = [carry_s[idx, j, 0] for idx in range(2)]
        row8 = lax.broadcasted_iota(jnp.int32, (SUBLANES, tn), 0)
        rc = min(tt, FFN_ROW_CHUNK)
        for r in range(tt // rc):
            rows = slice(r * rc, (r + 1) * rc)
            ys = []
            for w_ref, cw_ref, _, _, idx in halves:
                u = _dot(h_ref[rows, :], w_ref[...])
                y = u * cw_ref[FFN_CONV - 1:FFN_CONV, :]
                for k in range(1, FFN_CONV):
                    s = pltpu.roll(u, k, 0)
                    head = jnp.where(row8 < k, pltpu.roll(prev[idx], k, 0), s[0:SUBLANES])
                    s = jnp.concatenate([head, s[SUBLANES:]], axis=0)
                    y = y + s * cw_ref[FFN_CONV - 1 - k:FFN_CONV - k, :]
                prev[idx] = u[rc - SUBLANES:rc]
                ys.append(y)
            act_o[0, rows, :] = (ys[0] * jax.nn.sigmoid(ys[0]) * ys[1]).astype(act_o.dtype)
        for _, _, _, fn_o, idx in halves:
            carry_s[idx, j, 0] = prev[idx]
            fn_o[...] = prev[idx][lo:SUBLANES][None]
    else:
        bg_s, bu_s = scr

        def half(w_ref, cw_ref, f_ref, fn_o, buf):
            u = _dot(h_ref[...], w_ref[...]).reshape(bb, tt, tn)
            buf[:, lo:SUBLANES, :] = f_ref[...]
            buf[:, SUBLANES:SUBLANES + tt, :] = u
            fn_o[...] = u[:, tt - (FFN_CONV - 1):tt, :]
            y = None
            for jw in range(FFN_CONV):
                term = buf[:, lo + jw:lo + jw + tt, :] * cw_ref[jw:jw + 1, :]
                y = term if y is None else y + term
            return y

        yg = half(wg_ref, cwg_ref, fg_ref, fng_o, bg_s)
        yu = half(wu_ref, cwu_ref, fu_ref, fnu_o, bu_s)
        act_o[...] = (yg * jax.nn.sigmoid(yg) * yu).astype(act_o.dtype)


def _ffn_up(x, g, shift, scale, w_up, layer, conv_w, fbuf):
    B, T, D = x.shape
    FF = w_up.shape[2] // 2
    bb, tt = _row_tiles(B, T)
    tn = COL_TILE
    assert FF % tn == 0 and tt >= SUBLANES
    nj = FF // tn
    split_rows = T > tt
    assert not split_rows or bb == 1
    nf = FFN_CONV - 1

    def cols(off, rows, lead=()):
        return pl.BlockSpec((None,) * len(lead) + rows + (tn,),
                            lambda b, i, j: lead + (0,) * len(rows) + (off + j,))

    def fb(off):
        return pl.BlockSpec((bb, nf, tn), lambda b, i, j: (b, 0, off + j))

    act, fng, fnu = pl.pallas_call(
        functools.partial(_ffn_up_kernel, split_rows=split_rows),
        grid=(B // bb, T // tt, nj),
        in_specs=[pl.BlockSpec((bb, tt, D), lambda b, i, j: (b, i, 0)),
                  pl.BlockSpec((1, 1, D), lambda b, i, j: (0, 0, 0)),
                  pl.BlockSpec((bb, 1, D), lambda b, i, j: (b, 0, 0)),
                  pl.BlockSpec((bb, 1, D), lambda b, i, j: (b, 0, 0)),
                  cols(0, (D,), (layer,)), cols(nj, (D,), (layer,)),
                  cols(0, (FFN_CONV,)), cols(nj, (FFN_CONV,)),
                  fb(0), fb(nj)],
        out_specs=[pl.BlockSpec((bb, tt, tn), lambda b, i, j: (b, i, j)),
                   pl.BlockSpec((bb, None, nf, tn), lambda b, i, j: (b, i, 0, j)),
                   pl.BlockSpec((bb, None, nf, tn), lambda b, i, j: (b, i, 0, j))],
        out_shape=[jax.ShapeDtypeStruct((B, T, FF), BF16),
                   jax.ShapeDtypeStruct((B, T // tt, nf, FF), F32),
                   jax.ShapeDtypeStruct((B, T // tt, nf, FF), F32)],
        scratch_shapes=[pltpu.VMEM((bb * tt, D), BF16)] + (
            [pltpu.VMEM((2, nj, 1, SUBLANES, tn), F32)] if split_rows else
            [pltpu.VMEM((bb, tt + SUBLANES, tn), F32), pltpu.VMEM((bb, tt + SUBLANES, tn), F32)]),
        compiler_params=_cparams(("parallel", "arbitrary", "arbitrary")),
        name="ffn_up",
    )(x, g.reshape(1, 1, D), shift, scale, w_up, w_up, conv_w, conv_w, fbuf, fbuf)
    return act, jnp.concatenate([fng[:, -1], fnu[:, -1]], axis=-1)


def _prep_w_in(w_in):
    depth, D = w_in.shape[0], w_in.shape[1]
    w_in = w_in.astype(BF16)
    sizes = (1024, 1024, 1024, 1024, 256, 256, 512, 64, 8, C_QKV, 8, 8, 1024, N_BRANCH * D)
    offs = np.concatenate([[0], np.cumsum(sizes)])
    (aq, ak, av, bq, bk, bv, bqi, bki, bwi, cqkv, ca, cb, cz, gt) = [
        w_in[:, :, int(offs[k]):int(offs[k + 1])] for k in range(len(sizes))]

    def z(n):
        return jnp.zeros((depth, D, n), w_in.dtype)

    cols = [cqkv, cz, aq, ak, av, bq, bk, bv, bqi, bki, z(LANES - I_HD), bwi, ca, cb, z(LANES - 24),
            z(OFF_GT - OFF_SMALL - LANES), gt]
    w = jnp.concatenate(cols, axis=2)
    assert w.shape[2] == N_PROJ
    return w


def _layer(x, mod, past, lw, layer_idx):
    kA_p, vA_p, kB_p, vB_p, kI_p, cbuf, S0, fbuf = past
    B, T, D = x.shape
    P = 0 if kI_p is None else kI_p.shape[3]
    sh1, sc1, g1, sh2, sc2, g2 = [m.reshape(B, 1, D) for m in jnp.split(mod, 6, axis=-1)]

    proj = _mm_norm(x, lw['norm_mix'], sh1, sc1, lw['w_in'], layer_idx)
    rope = _rope_table(P + jnp.arange(T, dtype=jnp.int32))
    (qa, ka, kab, va, vab, qb, kb, kbb, vb, vbb, qi, ki, ki2) = _prep(
        proj, rope, lw['a_q_norm'], lw['a_k_norm'], lw['b_q_norm'], lw['b_k_norm'])

    lam_init = 0.8 - 0.6 * math.exp(-0.3 * layer_idx)
    if P:
        oa = _diff_decode(qa, kab, vab, kA_p, vA_p, layer_idx, lw['a_lambda'], lw['a_subln'], P, lam_init)
        ob = _dsa_decode(qi, proj, qb, ki2, kbb, vbb, kI_p, kB_p, vB_p, layer_idx)
    else:
        oa = _diff_attn(qa, kab, vab, lw['a_lambda'], lw['a_subln'], lam_init)
        ob = _dsa(qi, proj, qb, ki2, kbb, vbb, 0, T, min(T, 512))

    oc, S_fin = _gdn(proj, lw['c_conv'], cbuf, S0, lw['c_a_log'], lw['c_dt_bias'], lw['c_out_norm'])
    assert T >= C_CONV - 1 and T >= FFN_CONV - 1
    cbuf_new = proj[:, T - (C_CONV - 1):, OFF_CQKV:OFF_CQKV + C_QKV]

    mixed = _merge(oa, ob, oc, lw['w_branch'], layer_idx, proj, D)
    x = _mm_res(mixed, lw['w_out'], layer_idx, x, g1)

    act, fbuf_new = _ffn_up(x, lw['norm_ffn'], sh2, sc2, lw['w_up'], layer_idx, lw['ffn_conv'], fbuf)
    x = _mm_res(act, lw['w_down'], layer_idx, x, g2, row_target=512)

    va_out = va.reshape(B, T, A_VD // LANES, A_HEADS, LANES).transpose(0, 1, 3, 2, 4)
    new_state = (ka.reshape(B, T, A_HEADS, 2, A_HD), va_out.reshape(B, T, A_HEADS, A_VD),
                 kb.reshape(B, T, B_KV, B_HD), vb.reshape(B, T, B_KV, B_HD), ki,
                 cbuf_new, S_fin, fbuf_new)
    return x, new_state


def kernel(x_prompt, x_sample, c_prompt, c_sample, cache_diff_k, cache_diff_v, cache_dsa_k, cache_dsa_v, cache_dsa_kidx, state_gdn_conv, state_gdn, state_ffn_conv, w_ada, b_ada, norm_mix, w_in, a_q_norm, a_k_norm, a_lambda, a_subln, b_q_norm, b_k_norm, c_conv, c_a_log, c_dt_bias, c_out_norm, w_branch, w_out, norm_ffn, w_up, ffn_conv, w_down):
    depth = w_in.shape[0]
    bp, bs = x_prompt.shape[0], x_sample.shape[0]
    d_ff2 = w_up.shape[2]
    dt_ = x_prompt.dtype
    prompt_past = (None, None, None, None, None, jnp.zeros((bp, C_CONV - 1, C_QKV), dt_),
                   jnp.zeros((bp, C_HEADS, C_HD, C_HD), dt_), jnp.zeros((bp, FFN_CONV - 1, d_ff2), dt_))
    nrow = -(-(bp + bs) // SUBLANES) * SUBLANES
    c_all = jnp.concatenate([c_prompt, c_sample, jnp.zeros((nrow - bp - bs, c_prompt.shape[1]), dt_)], axis=0)
    past_len = cache_diff_k.shape[2]
    diff_k_rows = cache_diff_k.reshape(depth, bs, past_len * A_HEADS * 2, A_HD)
    diff_v_rows = cache_diff_v.reshape(depth, bs, past_len, A_HEADS, A_VD // LANES, LANES)
    diff_v_rows = diff_v_rows.transpose(0, 1, 2, 4, 3, 5).reshape(depth, bs, past_len * A_HEADS * 2, LANES)
    dsa_k_rows = cache_dsa_k.reshape(depth, bs, past_len * B_KV, B_HD)
    dsa_v_rows = cache_dsa_v.reshape(depth, bs, past_len * B_KV, B_HD)
    dsa_kidx_t = jnp.swapaxes(cache_dsa_kidx, 2, 3)
    xp, xs = x_prompt, x_sample
    prompt_states, sample_states = [], []
    big = dict(w_in=_prep_w_in(w_in), w_branch=w_branch.astype(BF16), w_out=w_out.astype(BF16),
               w_up=w_up.astype(BF16), w_down=w_down.astype(BF16))
    for l in range(depth):
        lw = dict(big, norm_mix=norm_mix[l],
                  a_q_norm=a_q_norm[l], a_k_norm=a_k_norm[l], a_lambda=a_lambda[l], a_subln=a_subln[l],
                  b_q_norm=b_q_norm[l], b_k_norm=b_k_norm[l], c_conv=c_conv[l], c_a_log=c_a_log[l],
                  c_dt_bias=c_dt_bias[l], c_out_norm=c_out_norm[l], norm_ffn=norm_ffn[l],
                  ffn_conv=ffn_conv[l])
        mod = _ada(c_all, w_ada, b_ada, l)
        xp, st_p = _layer(xp, mod[:bp], prompt_past, lw, l)
        sample_past = (diff_k_rows, diff_v_rows, dsa_k_rows, dsa_v_rows, dsa_kidx_t,
                       state_gdn_conv[l], state_gdn[l], state_ffn_conv[l])
        xs, st_s = _layer(xs, mod[bp:bp + bs], sample_past, lw, l)
        prompt_states.append(st_p)
        sample_states.append(st_s)
    p_out = [jnp.stack(s, axis=0) for s in zip(*prompt_states)]
    s_out = [jnp.stack(s, axis=0) for s in zip(*sample_states)]
    return (xp, xs, *p_out, *s_out)
```

```python
import functools
import math

import jax
import jax.numpy as jnp
import numpy as np
from jax import lax
from jax.experimental import pallas as pl
from jax.experimental.pallas import tpu as pltpu

F32 = jnp.float32
BF16 = jnp.bfloat16

CHUNK = 64
CHUNK_SHIFT = 6
ROPE_THETA = 10000.0
EPS = 1e-6
A_HEADS, A_HD, A_VD = 4, 128, 256
B_HEADS, B_KV, B_HD = 8, 2, 128
I_HEADS, I_HD = 8, 64
I_HD_SHIFT = 6
TOPK_MAX = 256
C_HEADS, C_HD, C_CONV = 8, 128, 4
C_QKV = 3 * C_HEADS * C_HD
BRANCH_W = 1024
N_BRANCH = 3
FFN_CONV = 3

LANES = 128
SUBLANES = 8
VMEM_LIMIT = 56 * 1024 * 1024
ROW_TILE = 1024
COL_TILE = 512
SOLVE_BLOCK = 16
TILE_UNROLL = 4
FFN_ROW_CHUNK = 256
NEG = -1e30
KEY_MIN = -(2 ** 31)

OFF_CQKV = 0
OFF_CZ = 3072
OFF_AQ = 4096
OFF_AK = 5120
OFF_AV = 6144
OFF_BQ = 7168
OFF_BK = 8192
OFF_BV = 8448
OFF_BQI = 8704
OFF_BKI = 9216
OFF_SMALL = 9344
CA_LANE = 8
CB_LANE = 16
OFF_GT = 9728
N_PROJ = 15872


def _cparams(sem):
    return pltpu.CompilerParams(dimension_semantics=sem, vmem_limit_bytes=VMEM_LIMIT)


def _row_tiles(B, T, target=ROW_TILE):
    tt = min(T, target)
    assert T % tt == 0 and tt % SUBLANES == 0
    bb = max(1, min(B, target // tt))
    while B % bb:
        bb -= 1
    return bb, tt


def _dot(a, b):
    return jnp.dot(a, b, preferred_element_type=F32)


def _dot_nt(a, b):
    return lax.dot_general(a, b, (((1,), (1,)), ((), ())), preferred_element_type=F32)


def _bdot(a, b):
    return _dot(a.astype(BF16), b.astype(BF16))


def _tile_loop(n, body, init, unroll):
    nb = n // unroll

    def trip(tb, c):
        for u in range(unroll):
            c = body(tb * unroll + u, c)
        return c

    c = lax.fori_loop(0, nb, trip, init)
    return lax.fori_loop(nb * unroll, n, body, c)


def _ada_kernel(c_ref, w_ref, b_ref, o_ref):
    c = c_ref[...]
    s = (c * jax.nn.sigmoid(c)).astype(BF16)
    o_ref[...] = _dot(s, w_ref[...].astype(BF16)) + b_ref[...]


def _ada(c, w, b, layer):
    M, D = c.shape
    N = w.shape[2]
    tn = 1024
    return pl.pallas_call(
        _ada_kernel,
        grid=(N // tn,),
        in_specs=[pl.BlockSpec((M, D), lambda j: (0, 0)),
                  pl.BlockSpec((None, D, tn), lambda j: (layer, 0, j)),
                  pl.BlockSpec((None, 1, tn), lambda j: (layer, 0, j))],
        out_specs=pl.BlockSpec((M, tn), lambda j: (0, j)),
        out_shape=jax.ShapeDtypeStruct((M, N), F32),
        compiler_params=_cparams(("parallel",)),
        name="ada",
    )(c, w, b.reshape(-1, 1, N))


def _mm_norm_kernel(x_ref, g_ref, sh_ref, sc_ref, w_ref, o_ref, h_ref):
    bb, tt, D = x_ref.shape
    tn = w_ref.shape[1]

    @pl.when(pl.program_id(2) == 0)
    def _():
        x = x_ref[...]
        y = x * lax.rsqrt(jnp.mean(x * x, axis=-1, keepdims=True) + EPS) * g_ref[...]
        h = y * (1.0 + sc_ref[...]) + sh_ref[...]
        h_ref[...] = h.reshape(bb * tt, D).astype(BF16)

    acc = _dot(h_ref[...], w_ref[...])
    o_ref[...] = acc.reshape(bb, tt, tn).astype(o_ref.dtype)


def _mm_norm(x, g, shift, scale, w, layer, out_dtype=F32):
    B, T, D = x.shape
    N = w.shape[2]
    bb, tt = _row_tiles(B, T)
    tn = COL_TILE
    assert N % tn == 0
    return pl.pallas_call(
        _mm_norm_kernel,
        grid=(B // bb, T // tt, N // tn),
        in_specs=[pl.BlockSpec((bb, tt, D), lambda b, i, j: (b, i, 0)),
                  pl.BlockSpec((1, 1, D), lambda b, i, j: (0, 0, 0)),
                  pl.BlockSpec((bb, 1, D), lambda b, i, j: (b, 0, 0)),
                  pl.BlockSpec((bb, 1, D), lambda b, i, j: (b, 0, 0)),
                  pl.BlockSpec((None, D, tn), lambda b, i, j: (layer, 0, j))],
        out_specs=pl.BlockSpec((bb, tt, tn), lambda b, i, j: (b, i, j)),
        out_shape=jax.ShapeDtypeStruct((B, T, N), out_dtype),
        scratch_shapes=[pltpu.VMEM((bb * tt, D), BF16)],
        compiler_params=_cparams(("parallel", "parallel", "arbitrary")),
        name="mm_norm",
    )(x, g.reshape(1, 1, D), shift, scale, w)


def _mm_res_kernel(a_ref, w_ref, res_ref, g_ref, o_ref):
    bb, tt, K = a_ref.shape
    tn = w_ref.shape[1]
    acc = _dot(a_ref[...].reshape(bb * tt, K), w_ref[...])
    o_ref[...] = res_ref[...] + g_ref[...] * acc.reshape(bb, tt, tn)


def _mm_res(a, w, layer, res, gate, row_target=ROW_TILE):
    B, T, K = a.shape
    N = w.shape[2]
    bb, tt = _row_tiles(B, T, row_target)
    tn = COL_TILE
    assert N % tn == 0
    return pl.pallas_call(
        _mm_res_kernel,
        grid=(B // bb, T // tt, N // tn),
        in_specs=[pl.BlockSpec((bb, tt, K), lambda b, i, j: (b, i, 0)),
                  pl.BlockSpec((None, K, tn), lambda b, i, j: (layer, 0, j)),
                  pl.BlockSpec((bb, tt, tn), lambda b, i, j: (b, i, j)),
                  pl.BlockSpec((bb, 1, tn), lambda b, i, j: (b, 0, j))],
        out_specs=pl.BlockSpec((bb, tt, tn), lambda b, i, j: (b, i, j)),
        out_shape=jax.ShapeDtypeStruct((B, T, N), F32),
        compiler_params=_cparams(("parallel", "parallel", "arbitrary")),
        name="mm_res",
    )(a, w, res, gate)


def _merge_kernel(b0_ref, b1_ref, b2_ref, w_ref, g0_ref, g1_ref, g2_ref, o_ref):
    bb, tt, W = b0_ref.shape
    tn = w_ref.shape[2]
    acc = None
    for n, (br, gr) in enumerate(((b0_ref, g0_ref), (b1_ref, g1_ref), (b2_ref, g2_ref))):
        y = _dot(br[...].reshape(bb * tt, W), w_ref[n])
        t = jax.nn.sigmoid(gr[...].reshape(bb * tt, tn)) * y
        acc = t if acc is None else acc + t
    o_ref[...] = acc.reshape(bb, tt, tn).astype(o_ref.dtype)


def _merge(oa, ob, oc, w_branch, layer, proj, D):
    B, T, W = oa.shape
    bb, tt = _row_tiles(B, T)
    tn = COL_TILE
    gblk = OFF_GT // tn
    nper = D // tn
    br_spec = pl.BlockSpec((bb, tt, W), lambda b, i, j: (b, i, 0))

    def gate_spec(n):
        return pl.BlockSpec((bb, tt, tn), lambda b, i, j: (b, i, gblk + n * nper + j))

    return pl.pallas_call(
        _merge_kernel,
        grid=(B // bb, T // tt, D // tn),
        in_specs=[br_spec, br_spec, br_spec,
                  pl.BlockSpec((None, N_BRANCH, W, tn), lambda b, i, j: (layer, 0, 0, j)),
                  gate_spec(0), gate_spec(1), gate_spec(2)],
        out_specs=pl.BlockSpec((bb, tt, tn), lambda b, i, j: (b, i, j)),
        out_shape=jax.ShapeDtypeStruct((B, T, D), BF16),
        compiler_params=_cparams(("parallel", "parallel", "arbitrary")),
        name="merge",
    )(oa, ob, oc, w_branch, proj, proj, proj)


def _prep_kernel(aqk_ref, av_ref, bq_ref, bkv_ref, bqi_ref, bki_ref, rope_ref,
                 gaq_ref, gak_ref, gbq_ref, gbk_ref,
                 qa_o, ka_o, kab_o, va_o, vab_o, qb_o, kb_o, kbb_o, vb_o, vbb_o,
                 qi_o, ki_o, ki2_o):
    c1 = rope_ref[:, 0:LANES]
    s1 = rope_ref[:, LANES:2 * LANES]
    c2 = rope_ref[:, 2 * LANES:3 * LANES]
    s2 = rope_ref[:, 3 * LANES:4 * LANES]
    tt = c1.shape[0]
    lane = lax.broadcasted_iota(jnp.int32, (tt, LANES), 1)
    low_half = (lane & (I_HD - 1)) < (I_HD // 2)

    def norm_rope(x, g):
        y = x * lax.rsqrt(jnp.mean(x * x, axis=-1, keepdims=True) + EPS) * g
        return y * c1 + pltpu.roll(y, A_HD // 2, 1) * s1

    def rope64(x):
        r = jnp.where(low_half, pltpu.roll(x, LANES - I_HD // 2, 1), pltpu.roll(x, I_HD // 2, 1))
        return x * c2 + r * s2

    gaq, gak, gbq, gbk = gaq_ref[...], gak_ref[...], gbq_ref[...], gbk_ref[...]
    for hm in range(2 * A_HEADS):
        sl = slice(hm * LANES, (hm + 1) * LANES)
        q = norm_rope(aqk_ref[0, :, sl], gaq) * (A_HD ** -0.5)
        qa_o[0, :, sl] = q.astype(BF16)
        k = norm_rope(aqk_ref[0, :, 2 * A_HEADS * LANES + hm * LANES:2 * A_HEADS * LANES + (hm + 1) * LANES], gak)
        ka_o[0, pl.ds(hm, tt, stride=2 * A_HEADS), :] = k
        kab_o[0, :, sl] = k.astype(BF16)
    va = av_ref[0]
    for h in range(A_HEADS):
        for c in range(A_VD // LANES):
            va_o[0, pl.ds(c * A_HEADS + h, tt, stride=2 * A_HEADS), :] = (
                va[:, h * A_VD + c * LANES:h * A_VD + (c + 1) * LANES])
    vab_o[0] = va.astype(BF16)
    for h in range(B_HEADS):
        sl = slice(h * LANES, (h + 1) * LANES)
        q = norm_rope(bq_ref[0, :, sl], gbq) * (B_HD ** -0.5)
        qb_o[0, :, sl] = q.astype(BF16)
    for n in range(B_KV):
        sl = slice(n * LANES, (n + 1) * LANES)
        k = norm_rope(bkv_ref[0, :, sl], gbk)
        kb_o[0, pl.ds(n, tt, stride=B_KV), :] = k
        kbb_o[0, :, sl] = k.astype(BF16)
    vb = bkv_ref[0, :, B_KV * LANES:2 * B_KV * LANES]
    for n in range(B_KV):
        vb_o[0, pl.ds(n, tt, stride=B_KV), :] = vb[:, n * LANES:(n + 1) * LANES]
    vbb_o[0] = vb.astype(BF16)
    for p in range(I_HEADS * I_HD // LANES):
        sl = slice(p * LANES, (p + 1) * LANES)
        qi_o[0, :, sl] = (rope64(bqi_ref[0, :, sl]) * (I_HD ** -0.5)).astype(BF16)
    ki = rope64(bki_ref[0, :, 0:LANES])
    ki_o[0] = ki[:, 0:I_HD]
    ki2_o[0] = (ki + pltpu.roll(ki, I_HD, 1)).astype(BF16)


def _rope_table(pos):
    def tab(half):
        inv = jnp.power(ROPE_THETA, -jnp.arange(half, dtype=F32) / half)
        ang = pos.astype(F32)[:, None] * inv[None, :]
        return jnp.cos(ang), jnp.sin(ang)

    c, s = tab(A_HD // 2)
    ci, si = tab(I_HD // 2)
    return jnp.concatenate([c, c, -s, s, ci, ci, ci, ci, -si, si, -si, si], axis=1)


def _prep(proj, rope, gaq, gak, gbq, gbk):
    B, T, _ = proj.shape
    tt = min(T, 256)
    assert T % tt == 0

    def pspec(width, off):
        assert off % width == 0
        return pl.BlockSpec((1, tt, width), lambda b, i: (b, i, off // width))

    gspec = pl.BlockSpec((1, LANES), lambda b, i: (0, 0))
    ra, rb = 2 * A_HEADS, B_KV
    outs = [(1, 1024, BF16), (ra, LANES, F32), (1, 1024, BF16), (ra, LANES, F32), (1, 1024, BF16),
            (1, 1024, BF16), (rb, LANES, F32), (1, 256, BF16), (rb, LANES, F32), (1, 256, BF16),
            (1, 512, BF16), (1, I_HD, F32), (1, LANES, BF16)]
    return pl.pallas_call(
        _prep_kernel,
        grid=(B, T // tt),
        in_specs=[pspec(2048, OFF_AQ), pspec(1024, OFF_AV), pspec(1024, OFF_BQ), pspec(512, OFF_BK),
                  pspec(512, OFF_BQI), pspec(256, OFF_BKI),
                  pl.BlockSpec((tt, 4 * LANES), lambda b, i: (i, 0)),
                  gspec, gspec, gspec, gspec],
        out_specs=[pl.BlockSpec((1, tt * r, w), lambda b, i: (b, i, 0)) for r, w, _ in outs],
        out_shape=[jax.ShapeDtypeStruct((B, T * r, w), d) for r, w, d in outs],
        compiler_params=_cparams(("parallel", "parallel")),
        name="prep",
    )(proj, proj, proj, proj, proj, proj, rope,
      gaq.reshape(1, LANES), gak.reshape(1, LANES), gbq.reshape(1, LANES), gbk.reshape(1, LANES))


def _diff_kernel(q_ref, kn_ref, vn_ref, lam_ref, sub_ref, o_ref, acc0_s, acc1_s, sn_s, *, tq, lam_init):
    acc = (acc0_s, acc1_s)
    i = pl.program_id(2)
    q_pos0 = i * tq
    q = [q_ref[0, :, m * A_HD:(m + 1) * A_HD] for m in range(2)]
    msl = [slice(m * A_HD, (m + 1) * A_HD) for m in range(2)]
    wide = tq % LANES == 0

    def fold(x, op):
        f = x[:, 0:LANES]
        for c in range(1, x.shape[1] // LANES):
            f = op(f, x[:, c * LANES:(c + 1) * LANES])
        return f

    neg = tuple(jnp.full((tq, LANES), NEG, F32) for _ in range(2))
    zero = tuple(jnp.zeros((tq, LANES), F32) for _ in range(2))

    def scores(t, mr):
        r0 = pl.multiple_of(t * tq, tq)
        k = kn_ref[0, pl.ds(r0, tq), :]
        out = []
        for m in range(2):
            s = _dot_nt(q[m], k[:, msl[m]])
            sn_s[m, t] = s
            out.append(jnp.maximum(mr[m], fold(s, jnp.maximum)) if wide else mr[m])
        return tuple(out)
    mn = _tile_loop(i, scores, neg, TILE_UNROLL)

    r0 = pl.multiple_of(i * tq, tq)
    qpos = q_pos0 + lax.broadcasted_iota(jnp.int32, (tq, 1), 0)
    kpos = q_pos0 + lax.broadcasted_iota(jnp.int32, (1, tq), 1)
    vis = (kpos >> CHUNK_SHIFT) <= (qpos >> CHUNK_SHIFT)
    kd = kn_ref[0, pl.ds(r0, tq), :]
    mrow = []
    for m in range(2):
        s = jnp.where(vis, _dot_nt(q[m], kd[:, msl[m]]), NEG)
        sn_s[m, i] = s
        mx = jnp.max(s, axis=-1, keepdims=True)
        if wide:
            mx = jnp.maximum(mx, jnp.max(mn[m], axis=-1, keepdims=True))
        if wide:
            mx = pltpu.repeat(jnp.broadcast_to(mx, (tq, LANES)), tq // LANES, axis=1)
        mrow.append(mx)

    for m in range(2):
        acc[m][...] = jnp.zeros(acc[m].shape, F32)

    def weigh(t, lr):
        r0 = pl.multiple_of(t * tq, tq)
        v = vn_ref[0, pl.ds(r0, tq), :]
        out = []
        for m in range(2):
            p = jnp.exp(sn_s[m, t] - mrow[m])
            if wide:
                out.append(lr[m] + fold(p, jnp.add))
            else:
                out.append(lr[m] + jnp.sum(p, axis=-1, keepdims=True))
            acc[m][...] += _dot(p.astype(BF16), v)
        return tuple(out)
    ln = _tile_loop(i + 1, weigh, zero if wide else tuple(jnp.zeros((tq, 1), F32) for _ in range(2)),
                    TILE_UNROLL)
    lrow = [jnp.sum(ln[m], axis=-1, keepdims=True) for m in range(2)]

    lp = lam_ref[...]
    lam = (jnp.exp(jnp.sum(lp[0:1] * lp[1:2], axis=-1, keepdims=True))
           - jnp.exp(jnp.sum(lp[2:3] * lp[3:4], axis=-1, keepdims=True)) + lam_init)
    o = acc[0][...] / lrow[0] - lam * (acc[1][...] / lrow[1])
    o = o * lax.rsqrt(jnp.mean(o * o, axis=-1, keepdims=True) + EPS) * sub_ref[...]
    o_ref[0] = (o * (1.0 - lam_init)).astype(o_ref.dtype)


def _diff_attn(qa, kn, vn, lam_p, subln, lam_init):
    B, T, _ = qa.shape
    tq = min(T, 512)
    assert T % tq == 0 and (tq % CHUNK == 0 or tq == T)
    W = 2 * A_HD
    return pl.pallas_call(
        functools.partial(_diff_kernel, tq=tq, lam_init=lam_init),
        grid=(B, A_HEADS, T // tq),
        in_specs=[pl.BlockSpec((1, tq, W), lambda b, h, i: (b, i, h)),
                  pl.BlockSpec((1, T, W), lambda b, h, i: (b, 0, h)),
                  pl.BlockSpec((1, T, W), lambda b, h, i: (b, 0, h)),
                  pl.BlockSpec((4, A_HD), lambda b, h, i: (0, 0)),
                  pl.BlockSpec((1, A_VD), lambda b, h, i: (0, 0))],
        out_specs=pl.BlockSpec((1, tq, W), lambda b, h, i: (b, i, h)),
        out_shape=jax.ShapeDtypeStruct((B, T, A_HEADS * A_VD), BF16),
        scratch_shapes=[pltpu.VMEM((tq, A_VD), F32), pltpu.VMEM((tq, A_VD), F32),
                        pltpu.VMEM((2, T // tq, tq, tq), F32)],
        compiler_params=_cparams(("parallel", "parallel", "arbitrary")),
        name="diff_attn",
    )(qa, kn, vn, lam_p, subln.reshape(1, A_VD))


def _diff_decode_kernel(q_ref, kn_ref, vn_ref, kp_ref, vp_ref, lam_ref, sub_ref, o_ref, *, P, lam_init):
    T = q_ref.shape[1]
    per_pos = A_HEADS * 2
    qpos = P + lax.broadcasted_iota(jnp.int32, (T, 1), 0)
    kpos = P + lax.broadcasted_iota(jnp.int32, (1, T), 1)
    vis = (kpos >> CHUNK_SHIFT) <= (qpos >> CHUNK_SHIFT)
    lp = lam_ref[...]
    lam = (jnp.exp(jnp.sum(lp[0:1] * lp[1:2], axis=-1, keepdims=True))
           - jnp.exp(jnp.sum(lp[2:3] * lp[3:4], axis=-1, keepdims=True)) + lam_init)
    for h in range(A_HEADS):
        v_past = jnp.concatenate(
            [vp_ref[pl.ds(c * A_HEADS + h, P, stride=per_pos), :] for c in range(A_VD // LANES)],
            axis=-1).astype(BF16)
        v_new = vn_ref[0, :, h * A_VD:(h + 1) * A_VD]
        outs = []
        for m in range(2):
            hm = h * 2 + m
            q = q_ref[0, :, hm * A_HD:(hm + 1) * A_HD]
            k_past = kp_ref[pl.ds(hm, P, stride=per_pos), :].astype(BF16)
            s_p = _dot_nt(q, k_past)
            s_n = jnp.where(vis, _dot_nt(q, kn_ref[0, :, hm * A_HD:(hm + 1) * A_HD]), NEG)
            mx = jnp.maximum(jnp.max(s_p, axis=-1, keepdims=True), jnp.max(s_n, axis=-1, keepdims=True))
            p_p = jnp.exp(s_p - mx)
            p_n = jnp.exp(s_n - mx)
            l = jnp.sum(p_p, axis=-1, keepdims=True) + jnp.sum(p_n, axis=-1, keepdims=True)
            outs.append((_dot(p_p.astype(BF16), v_past) + _dot(p_n.astype(BF16), v_new)) / l)
        o = outs[0] - lam * outs[1]
        o = o * lax.rsqrt(jnp.mean(o * o, axis=-1, keepdims=True) + EPS) * sub_ref[...]
        o_ref[0, :, h * A_VD:(h + 1) * A_VD] = (o * (1.0 - lam_init)).astype(o_ref.dtype)


def _diff_decode(qa, kn, vn, kp_rows, vp_rows, layer, lam_p, subln, P, lam_init):
    B, T, W = qa.shape
    assert P % CHUNK == 0 and kp_rows.shape[2] == P * A_HEADS * 2 and vp_rows.shape[2] == P * A_HEADS * 2
    new_spec = pl.BlockSpec((1, T, W), lambda b: (b, 0, 0))
    cache_spec = pl.BlockSpec((None, None, P * A_HEADS * 2, LANES), lambda b: (layer, b, 0, 0))
    return pl.pallas_call(
        functools.partial(_diff_decode_kernel, P=P, lam_init=lam_init),
        grid=(B,),
        in_specs=[new_spec, new_spec, new_spec, cache_spec, cache_spec,
                  pl.BlockSpec((4, A_HD), lambda b: (0, 0)),
                  pl.BlockSpec((1, A_VD), lambda b: (0, 0))],
        out_specs=new_spec,
        out_shape=jax.ShapeDtypeStruct((B, T, W), BF16),
        compiler_params=_cparams(("parallel",)),
        name="diff_decode",
    )(qa, kn, vn, kp_rows, vp_rows, lam_p, subln.reshape(1, A_VD))


def _dsa_kernel(qi_ref, wi_ref, qb_ref, ki_ref, kb_ref, vb_ref, o_ref,
                key_s, bias_s, s_s, *, P, L, tq, tk, topk):
    i = pl.program_id(1)
    q_pos0 = P + i * tq
    Lp = ki_ref.shape[1]
    ncol = jnp.minimum(L, ((q_pos0 + tq - 1) // CHUNK + 1) * CHUNK)
    nt = (ncol + tk - 1) // tk
    qpos = q_pos0 + lax.broadcasted_iota(jnp.int32, (tq, 1), 0)
    grp = B_HEADS // B_KV
    lane = lax.broadcasted_iota(jnp.int32, (tq, LANES), 1)
    wi = wi_ref[0] * (I_HEADS ** -0.5)
    w_lanes = [jnp.broadcast_to(wi[:, h:h + 1], (tq, LANES)) for h in range(I_HEADS)]

    def vis_of(t):
        kpos = t * tk + lax.broadcasted_iota(jnp.int32, (1, tk), 1)
        return ((kpos >> CHUNK_SHIFT) <= (qpos >> CHUNK_SHIFT)) & (kpos < L)

    def score_tile(t, carry):
        c0 = pl.multiple_of(t * tk, tk)
        kit = ki_ref[0, pl.ds(c0, tk), :]
        sc = jnp.zeros((tq, tk), F32)
        for pr in range(I_HEADS * I_HD // LANES):
            qpair = qi_ref[0, :, pr * LANES:(pr + 1) * LANES]
            for half in range(LANES // I_HD):
                h = pr * (LANES // I_HD) + half
                qh = jnp.where((lane >> I_HD_SHIFT) == half, qpair, jnp.zeros_like(qpair))
                isc = _dot_nt(qh, kit)
                sc = sc + jnp.maximum(isc, 0.0) * pltpu.repeat(w_lanes[h], tk // LANES, axis=1)
        sc = jnp.where(vis_of(t), sc, -jnp.inf)
        key_s[t] = _float_order_key(sc)
        return carry

    lax.fori_loop(0, nt, score_tile, 0)

    def counts(pred_fns):
        def body(t, accs):
            kk = key_s[t]
            out = []
            for pred_fn, acc in zip(pred_fns, accs):
                hit = jnp.where(pred_fn(kk), 1, 0)
                part = hit[:, 0:LANES]
                for c in range(1, tk // LANES):
                    part = part + hit[:, c * LANES:(c + 1) * LANES]
                out.append(acc + part)
            return tuple(out)
        accs = _tile_loop(nt, body, tuple(jnp.zeros((tq, LANES), jnp.int32) for _ in pred_fns), 2)
        return [jnp.sum(a.astype(F32), axis=-1, keepdims=True) for a in accs]

    def select(_):
        def bit_body(bi, thr):
            cand = thr + jnp.left_shift(jnp.int32(1), 31 - bi)
            (cnt,) = counts([lambda kk: kk >= cand])
            return jnp.where(cnt >= topk, cand, thr)
        return lax.fori_loop(0, 32, bit_body, jnp.full((tq, 1), KEY_MIN, jnp.int32))

    thr = lax.cond(ncol > topk, select, lambda _: jnp.full((tq, 1), KEY_MIN, jnp.int32), 0)
    n_gt, n_eq = counts([lambda kk: kk > thr, lambda kk: kk == thr])
    need = topk - n_gt
    neg_inf_key = jnp.int32(KEY_MIN + 0x7FFFFF)
    tie_break = jnp.max(jnp.where((n_eq > need) & (thr != neg_inf_key), 1.0, 0.0)) > 0.5

    def bias_fast(_):
        def body(t, c):
            sel = (key_s[t] >= thr) & vis_of(t)
            bias_s[t] = jnp.where(sel, 0.0, NEG)
            return c
        lax.fori_loop(0, nt, body, 0)
        return 0

    def bias_ties(_):
        r = lax.broadcasted_iota(jnp.int32, (LANES, LANES), 0)
        c = lax.broadcasted_iota(jnp.int32, (LANES, LANES), 1)
        before = jnp.where(r < c, 1.0, 0.0).astype(BF16)

        def body(t, seen):
            kk = key_s[t]
            vis = vis_of(t)
            for cb in range(tk // LANES):
                sl = slice(cb * LANES, (cb + 1) * LANES)
                eq = kk[:, sl] == thr
                eqf = jnp.where(eq, 1.0, 0.0)
                rank = seen + _dot(eqf.astype(BF16), before)
                sel = ((kk[:, sl] > thr) | (eq & (rank < need.astype(F32)))) & vis[:, sl]
                bias_s[t, :, sl] = jnp.where(sel, 0.0, NEG)
                seen = seen + jnp.sum(eqf, axis=-1, keepdims=True)
            return seen
        lax.fori_loop(0, nt, body, jnp.zeros((tq, 1), F32))
        return 0

    lax.cond(tie_break, bias_ties, bias_fast, 0)

    def fold(x, op):
        f = x[:, 0:LANES]
        for c in range(1, tk // LANES):
            f = op(f, x[:, c * LANES:(c + 1) * LANES])
        return f

    for n in range(B_KV):
        kv = slice(n * B_HD, (n + 1) * B_HD)

        def scores(t, mrun, kv=kv, n=n):
            c0 = pl.multiple_of(t * tk, tk)
            bias = bias_s[t]
            k = kb_ref[0, pl.ds(c0, tk), kv]
            out = []
            for g in range(grp):
                h = n * grp + g
                s = _dot_nt(qb_ref[0, :, h * B_HD:(h + 1) * B_HD], k) + bias
                s_s[g, t] = s
                out.append(jnp.maximum(mrun[g], fold(s, jnp.maximum)))
            return tuple(out)

        mrun = _tile_loop(nt, scores, tuple(jnp.full((tq, LANES), NEG, F32) for _ in range(grp)), 2)
        mrow = [pltpu.repeat(jnp.broadcast_to(jnp.max(m, axis=-1, keepdims=True), (tq, LANES)),
                             tk // LANES, axis=1) for m in mrun]

        def weigh(t, carry, kv=kv, mrow=mrow):
            lrun, acc = carry
            c0 = pl.multiple_of(t * tk, tk)
            v = vb_ref[0, pl.ds(c0, tk), kv]
            lnew, anew = [], []
            for g in range(grp):
                p = jnp.exp(s_s[g, t] - mrow[g])
                lnew.append(lrun[g] + fold(p, jnp.add))
                anew.append(acc[g] + _dot(p.astype(BF16), v))
            return tuple(lnew), tuple(anew)

        lrun, acc = _tile_loop(
            nt, weigh,
            (tuple(jnp.zeros((tq, LANES), F32) for _ in range(grp)),
             tuple(jnp.zeros((tq, B_HD), F32) for _ in range(grp))), 2)
        for g in range(grp):
            h = n * grp + g
            l = jnp.sum(lrun[g], axis=-1, keepdims=True)
            o_ref[0, :, h * B_HD:(h + 1) * B_HD] = (acc[g] / l).astype(o_ref.dtype)


def _dsa(qi, proj, qb, ki2, kb, vb, P, L, tk):
    B, T, _ = qb.shape
    Lp = ki2.shape[1]
    tq = min(T, 256)
    assert T % tq == 0 and Lp % tk == 0 and tk % LANES == 0
    topk = min(TOPK_MAX, L // 4)
    kern = functools.partial(_dsa_kernel, P=P, L=L, tq=tq, tk=tk, topk=topk)
    return pl.pallas_call(
        kern,
        grid=(B, T // tq),
        in_specs=[pl.BlockSpec((1, tq, I_HEADS * I_HD), lambda b, i: (b, i, 0)),
                  pl.BlockSpec((1, tq, LANES), lambda b, i: (b, i, OFF_SMALL // LANES)),
                  pl.BlockSpec((1, tq, B_HEADS * B_HD), lambda b, i: (b, i, 0)),
                  pl.BlockSpec((1, Lp, LANES), lambda b, i: (b, 0, 0)),
                  pl.BlockSpec((1, Lp, B_KV * B_HD), lambda b, i: (b, 0, 0)),
                  pl.BlockSpec((1, Lp, B_KV * B_HD), lambda b, i: (b, 0, 0))],
        out_specs=pl.BlockSpec((1, tq, B_HEADS * B_HD), lambda b, i: (b, i, 0)),
        out_shape=jax.ShapeDtypeStruct((B, T, B_HEADS * B_HD), BF16),
        scratch_shapes=[pltpu.VMEM((Lp // tk, tq, tk), jnp.int32), pltpu.VMEM((Lp // tk, tq, tk), F32),
                        pltpu.VMEM((B_HEADS // B_KV, Lp // tk, tq, tk), F32)],
        compiler_params=_cparams(("parallel", "arbitrary")),
        name="dsa",
    )(qi, proj, qb, ki2, kb, vb)


def _float_order_key(x):
    bits = pltpu.bitcast(x, jnp.int32)
    bits = jnp.where(bits == KEY_MIN, 0, bits)
    return bits ^ ((bits >> 31) & 0x7FFFFFFF)


def _dsa_decode_kernel(qi_ref, wi_ref, qb_ref, kin_ref, kbn_ref, vbn_ref, kip_ref, kp_ref, vp_ref, o_ref,
                       *, P, topk):
    T = qi_ref.shape[1]
    grp = B_HEADS // B_KV
    qpos = P + lax.broadcasted_iota(jnp.int32, (T, 1), 0)
    kpos = P + lax.broadcasted_iota(jnp.int32, (1, T), 1)
    vis_n = (kpos >> CHUNK_SHIFT) <= (qpos >> CHUNK_SHIFT)
    wi = wi_ref[0] * (I_HEADS ** -0.5)
    ki_pT = kip_ref[...].astype(BF16)
    ki_n = kin_ref[0, :, 0:I_HD]

    sc_p = jnp.zeros((T, P), F32)
    sc_n = jnp.zeros((T, T), F32)
    for h in range(I_HEADS):
        qh = qi_ref[0, :, h * I_HD:(h + 1) * I_HD]
        w = wi[:, h:h + 1]
        sc_p = sc_p + jnp.maximum(_dot(qh, ki_pT), 0.0) * w
        sc_n = sc_n + jnp.maximum(_dot_nt(qh, ki_n), 0.0) * w
    key_p = _float_order_key(sc_p)
    key_n = _float_order_key(jnp.where(vis_n, sc_n, -jnp.inf))

    def count(pred):
        return (jnp.sum(jnp.where(pred(key_p), 1.0, 0.0), axis=-1, keepdims=True)
                + jnp.sum(jnp.where(pred(key_n), 1.0, 0.0), axis=-1, keepdims=True))

    def bit_body(bi, thr):
        cand = thr + jnp.left_shift(jnp.int32(1), 31 - bi)
        return jnp.where(count(lambda kk: kk >= cand) >= topk, cand, thr)

    thr = lax.fori_loop(0, 32, bit_body, jnp.full((T, 1), KEY_MIN, jnp.int32))
    need = topk - count(lambda kk: kk > thr)

    def before(n):
        r = lax.broadcasted_iota(jnp.int32, (n, n), 0)
        c = lax.broadcasted_iota(jnp.int32, (n, n), 1)
        return jnp.where(r < c, 1.0, 0.0).astype(BF16)

    def select(kk, seen, pre):
        eq = kk == thr
        eqf = jnp.where(eq, 1.0, 0.0)
        rank = seen + _dot(eqf.astype(BF16), pre)
        sel = (kk > thr) | (eq & (rank < need))
        return sel, seen + jnp.sum(eqf, axis=-1, keepdims=True)

    pre_l = before(LANES)
    seen = jnp.zeros((T, 1), F32)
    bias_blocks = []
    for cb in range(P // LANES):
        sel, seen = select(key_p[:, cb * LANES:(cb + 1) * LANES], seen, pre_l)
        bias_blocks.append(jnp.where(sel, 0.0, NEG))
    bias_p = jnp.concatenate(bias_blocks, axis=-1)
    sel, _ = select(key_n, seen, before(T))
    bias_n = jnp.where(sel & vis_n, 0.0, NEG)

    for n in range(B_KV):
        kv = slice(n * B_HD, (n + 1) * B_HD)
        k_p = kp_ref[pl.ds(n, P, stride=B_KV), :].astype(BF16)
        v_p = vp_ref[pl.ds(n, P, stride=B_KV), :].astype(BF16)
        k_n = kbn_ref[0, :, kv]
        v_n = vbn_ref[0, :, kv]
        for g in range(grp):
            h = n * grp + g
            q = qb_ref[0, :, h * B_HD:(h + 1) * B_HD]
            s_p = _dot_nt(q, k_p) + bias_p
            s_n = _dot_nt(q, k_n) + bias_n
            mx = jnp.maximum(jnp.max(s_p, axis=-1, keepdims=True), jnp.max(s_n, axis=-1, keepdims=True))
            p_p = jnp.exp(s_p - mx)
            p_n = jnp.exp(s_n - mx)
            l = jnp.sum(p_p, axis=-1, keepdims=True) + jnp.sum(p_n, axis=-1, keepdims=True)
            o = _dot(p_p.astype(BF16), v_p) + _dot(p_n.astype(BF16), v_n)
            o_ref[0, :, h * B_HD:(h + 1) * B_HD] = (o / l).astype(o_ref.dtype)


def _dsa_decode(qi, proj, qb, ki2, kbb, vbb, ki_past_t, kp_rows, vp_rows, layer):
    B, T, _ = qb.shape
    P = ki_past_t.shape[3]
    assert P % CHUNK == 0 and P % LANES == 0 and kp_rows.shape[2] == P * B_KV
    topk = min(TOPK_MAX, (P + T) // 4)

    def new(width, blk=0):
        return pl.BlockSpec((1, T, width), lambda b: (b, 0, blk))

    rows = pl.BlockSpec((None, None, P * B_KV, LANES), lambda b: (layer, b, 0, 0))
    return pl.pallas_call(
        functools.partial(_dsa_decode_kernel, P=P, topk=topk),
        grid=(B,),
        in_specs=[new(I_HEADS * I_HD), new(LANES, OFF_SMALL // LANES), new(B_HEADS * B_HD),
                  new(LANES), new(B_KV * B_HD), new(B_KV * B_HD),
                  pl.BlockSpec((None, None, I_HD, P), lambda b: (layer, b, 0, 0)), rows, rows],
        out_specs=new(B_HEADS * B_HD),
        out_shape=jax.ShapeDtypeStruct((B, T, B_HEADS * B_HD), BF16),
        compiler_params=_cparams(("parallel",)),
        name="dsa_decode",
    )(qi, proj, qb, ki2, kbb, vbb, ki_past_t, kp_rows, vp_rows)


def _bmm(a, b):
    return jnp.einsum('bij,bjk->bik', a.astype(BF16), b.astype(BF16), preferred_element_type=F32)


def _bmm_nt(a, b):
    return jnp.einsum('bik,bjk->bij', a.astype(BF16), b.astype(BF16), preferred_element_type=F32)


def _gdn_intra_kernel(x_ref, halo_ref, sm_ref, cw_ref, cbuf_ref, alog_ref, dtb_ref,
                      u_o, w_o, qt_o, ktT_o, qkg_o, dec_o, buf_s, *, cc, nck):
    i = pl.program_id(1)
    R = nck * cc
    HW = C_HEADS * C_HD

    @pl.when(i == 0)
    def _():
        buf_s[0:SUBLANES, :] = jnp.zeros((SUBLANES, C_QKV), F32)
        buf_s[SUBLANES - (C_CONV - 1):SUBLANES, :] = cbuf_ref[0]

    @pl.when(i > 0)
    def _():
        buf_s[0:SUBLANES, :] = halo_ref[0]

    x = x_ref[0]
    prev = buf_s[...]
    row8 = lax.broadcasted_iota(jnp.int32, (SUBLANES, C_QKV), 0)
    y = x * cw_ref[C_CONV - 1:C_CONV, :]
    s = x
    for k in range(1, C_CONV):
        s = pltpu.roll(s, 1, 0)
        head = jnp.where(row8 < k, pltpu.roll(prev, k, 0), s[0:SUBLANES])
        y = y + jnp.concatenate([head, s[SUBLANES:]], axis=0) * cw_ref[C_CONV - 1 - k:C_CONV - k, :]
    y = y * jax.nn.sigmoid(y)

    sm = sm_ref[0]
    xg = sm + dtb_ref[...]
    softplus = jnp.maximum(xg, 0.0) + jnp.log(1.0 + jnp.exp(-jnp.abs(xg)))
    g = -jnp.exp(alog_ref[...]) * softplus
    beta = jax.nn.sigmoid(sm)

    cshift = cc.bit_length() - 1
    rr = lax.broadcasted_iota(jnp.int32, (R, R), 0)
    rc = lax.broadcasted_iota(jnp.int32, (R, R), 1)
    cum = ((rr >> cshift) == (rc >> cshift)) & (rr >= rc)
    G = jnp.dot(jnp.where(cum, 1.0, 0.0), g, preferred_element_type=F32,
                precision=lax.Precision.HIGHEST)

    qs, ks, vs, gcs, bcs = [], [], [], [], []
    for ck in range(nck):
        rows = slice(ck * cc, (ck + 1) * cc)
        for h in range(C_HEADS):
            qs.append(y[rows, h * C_HD:(h + 1) * C_HD])
            ks.append(y[rows, HW + h * C_HD:HW + (h + 1) * C_HD])
            vs.append(y[rows, 2 * HW + h * C_HD:2 * HW + (h + 1) * C_HD])
            gcs.append(G[rows, CA_LANE + h:CA_LANE + h + 1])
            bcs.append(beta[rows, CB_LANE + h:CB_LANE + h + 1])
    q3 = jnp.stack(qs)
    k3 = jnp.stack(ks)
    v3 = jnp.stack(vs)
    Gc = jnp.stack(gcs)
    bc = jnp.stack(bcs)
    q3 = q3 * lax.rsqrt(jnp.sum(q3 * q3, axis=-1, keepdims=True) + EPS) * (C_HD ** -0.5)
    k3 = k3 * lax.rsqrt(jnp.sum(k3 * k3, axis=-1, keepdims=True) + EPS)

    ri = lax.broadcasted_iota(jnp.int32, (cc, cc), 0)
    ci = lax.broadcasted_iota(jnp.int32, (cc, cc), 1)
    incl = (ri >= ci)[None]
    strict = (ri > ci)[None]
    eye = (ri == ci)[None]
    bdiag = ((ri // SOLVE_BLOCK) == (ci // SOLVE_BLOCK))[None]

    Gr = jnp.sum(jnp.where(eye, Gc, 0.0), axis=1, keepdims=True)
    gam = jnp.where(incl, jnp.exp(jnp.where(incl, Gc - Gr, 0.0)), 0.0)
    k16 = k3.astype(BF16)
    A = jnp.where(strict, bc * _bmm_nt(k16, k16) * gam, 0.0)
    Nd = jnp.where(bdiag, -A, 0.0)
    E = jnp.where(bdiag, 0.0, A)
    N2 = _bmm(Nd, Nd)
    N4 = _bmm(N2, N2)
    N8 = _bmm(N4, N4)
    Q = Nd + N2 + _bmm(Nd, N2)
    Q = Q + N4 + _bmm(Q, N4)
    Q = Q + N8 + _bmm(Q, N8)
    Fm = E + _bmm(Q, E)
    F2 = _bmm(Fm, Fm)
    Rm = F2 - Fm - _bmm(Fm, F2)
    Wm = Rm + Q + _bmm(Rm, Q)
    eG = jnp.exp(Gc)
    rhs = jnp.concatenate([bc * v3, (bc * eG) * k3], axis=-1)
    sol = rhs + _bmm(Wm, rhs)
    qkg = _bmm_nt(q3, k16) * gam
    gl = Gc[:, cc - 1:cc, :]
    qt = q3 * eG
    kt = k3 * jnp.exp(gl - Gc)
    dec = jnp.exp(gl)

    for ck in range(nck):
        rows = slice(ck * cc, (ck + 1) * cc)
        for h in range(C_HEADS):
            b = ck * C_HEADS + h
            cols = slice(h * C_HD, (h + 1) * C_HD)
            u_o[0, rows, cols] = sol[b, :, 0:C_HD]
            w_o[0, rows, cols] = sol[b, :, C_HD:2 * C_HD].astype(BF16)
            qt_o[0, rows, cols] = qt[b].astype(BF16)
            ktT_o[0, ck, h] = kt[b].T.astype(BF16)
            qkg_o[0, ck, h] = qkg[b].astype(BF16)
            dec_o[0, ck, h] = jnp.broadcast_to(dec[b], (1, C_HD))


def _gdn_inter_kernel(u_ref, w_ref, qt_ref, ktT_ref, qkg_ref, dec_ref, z_ref, s0_ref, gn_ref,
                      o_ref, sfin_ref, st_s, *, cc, nck):
    c = pl.program_id(1)

    @pl.when(c == 0)
    def _():
        st_s[...] = s0_ref[0]

    S = st_s[...]
    for ck in range(nck):
        rows = slice(ck * cc, (ck + 1) * cc)

        def heads(ref):
            return jnp.stack([ref[0, rows, h * C_HD:(h + 1) * C_HD] for h in range(C_HEADS)])

        S16 = S.astype(BF16)
        v_new = heads(u_ref) - _bmm(heads(w_ref), S16)
        vn16 = v_new.astype(BF16)
        o = _bmm(heads(qt_ref), S16) + _bmm(qkg_ref[0, ck], vn16)
        S = dec_ref[0, ck] * S + _bmm(ktT_ref[0, ck], vn16)
        on = o * lax.rsqrt(jnp.mean(o * o, axis=-1, keepdims=True) + EPS) * gn_ref[...]
        for h in range(C_HEADS):
            cols = slice(h * C_HD, (h + 1) * C_HD)
            z = z_ref[0, rows, cols]
            o_ref[0, rows, cols] = (on[h] * (z * jax.nn.sigmoid(z))).astype(o_ref.dtype)
    st_s[...] = S

    @pl.when(c == pl.num_programs(1) - 1)
    def _():
        sfin_ref[0] = S


def _gdn(proj, conv_w, cbuf, S0, a_log, dt_bias, out_norm):
    B, T, _ = proj.shape
    cc = min(CHUNK, T)
    assert T % cc == 0 and cc % SOLVE_BLOCK == 0 and cc // SOLVE_BLOCK <= 4 and cc >= SUBLANES
    assert cc & (cc - 1) == 0
    HW = C_HEADS * C_HD
    NC = T // cc
    nck_a = 2 if NC % 2 == 0 else 1
    nck_b = 8 if NC % 8 == 0 else 1

    def at_ca(p):
        return jnp.zeros((1, LANES), F32).at[0, CA_LANE:CA_LANE + C_HEADS].set(p.astype(F32))

    Ra = nck_a * cc
    hb = Ra // SUBLANES
    u, w, qt, ktT, qkg, dec = pl.pallas_call(
        functools.partial(_gdn_intra_kernel, cc=cc, nck=nck_a),
        grid=(B, NC // nck_a),
        in_specs=[pl.BlockSpec((1, Ra, C_QKV), lambda b, i: (b, i, OFF_CQKV // C_QKV)),
                  pl.BlockSpec((1, SUBLANES, C_QKV),
                               lambda b, i: (b, jnp.maximum(i * hb - 1, 0), OFF_CQKV // C_QKV)),
                  pl.BlockSpec((1, Ra, LANES), lambda b, i: (b, i, OFF_SMALL // LANES)),
                  pl.BlockSpec((C_CONV, C_QKV), lambda b, i: (0, 0)),
                  pl.BlockSpec((1, C_CONV - 1, C_QKV), lambda b, i: (b, 0, 0)),
                  pl.BlockSpec((1, LANES), lambda b, i: (0, 0)),
                  pl.BlockSpec((1, LANES), lambda b, i: (0, 0))],
        out_specs=[pl.BlockSpec((1, Ra, HW), lambda b, i: (b, i, 0)),
                   pl.BlockSpec((1, Ra, HW), lambda b, i: (b, i, 0)),
                   pl.BlockSpec((1, Ra, HW), lambda b, i: (b, i, 0)),
                   pl.BlockSpec((1, nck_a, C_HEADS, C_HD, cc), lambda b, i: (b, i, 0, 0, 0)),
                   pl.BlockSpec((1, nck_a, C_HEADS, cc, cc), lambda b, i: (b, i, 0, 0, 0)),
                   pl.BlockSpec((1, nck_a, C_HEADS, 1, C_HD), lambda b, i: (b, i, 0, 0, 0))],
        out_shape=[jax.ShapeDtypeStruct((B, T, HW), F32),
                   jax.ShapeDtypeStruct((B, T, HW), BF16),
                   jax.ShapeDtypeStruct((B, T, HW), BF16),
                   jax.ShapeDtypeStruct((B, NC, C_HEADS, C_HD, cc), BF16),
                   jax.ShapeDtypeStruct((B, NC, C_HEADS, cc, cc), BF16),
                   jax.ShapeDtypeStruct((B, NC, C_HEADS, 1, C_HD), F32)],
        scratch_shapes=[pltpu.VMEM((SUBLANES, C_QKV), F32)],
        compiler_params=_cparams(("parallel", "parallel")),
        name="gdn_intra",
    )(proj, proj, proj, conv_w, cbuf, at_ca(a_log), at_ca(dt_bias))

    Rb = nck_b * cc
    return pl.pallas_call(
        functools.partial(_gdn_inter_kernel, cc=cc, nck=nck_b),
        grid=(B, NC // nck_b),
        in_specs=[pl.BlockSpec((1, Rb, HW), lambda b, c: (b, c, 0)),
                  pl.BlockSpec((1, Rb, HW), lambda b, c: (b, c, 0)),
                  pl.BlockSpec((1, Rb, HW), lambda b, c: (b, c, 0)),
                  pl.BlockSpec((1, nck_b, C_HEADS, C_HD, cc), lambda b, c: (b, c, 0, 0, 0)),
                  pl.BlockSpec((1, nck_b, C_HEADS, cc, cc), lambda b, c: (b, c, 0, 0, 0)),
                  pl.BlockSpec((1, nck_b, C_HEADS, 1, C_HD), lambda b, c: (b, c, 0, 0, 0)),
                  pl.BlockSpec((1, Rb, HW), lambda b, c: (b, c, OFF_CZ // HW)),
                  pl.BlockSpec((1, C_HEADS, C_HD, C_HD), lambda b, c: (b, 0, 0, 0)),
                  pl.BlockSpec((1, C_HD), lambda b, c: (0, 0))],
        out_specs=[pl.BlockSpec((1, Rb, HW), lambda b, c: (b, c, 0)),
                   pl.BlockSpec((1, C_HEADS, C_HD, C_HD), lambda b, c: (b, 0, 0, 0))],
        out_shape=[jax.ShapeDtypeStruct((B, T, HW), BF16),
                   jax.ShapeDtypeStruct((B, C_HEADS, C_HD, C_HD), F32)],
        scratch_shapes=[pltpu.VMEM((C_HEADS, C_HD, C_HD), F32)],
        compiler_params=_cparams(("parallel", "arbitrary")),
        name="gdn_inter",
    )(u, w, qt, ktT, qkg, dec, proj, S0, out_norm.reshape(1, C_HD))


def _ffn_up_kernel(x_ref, g_ref, sh_ref, sc_ref, wg_ref, wu_ref, cwg_ref, cwu_ref, fg_ref, fu_ref,
                   act_o, fng_o, fnu_o, h_ref, *scr, split_rows):
    bb, tt, D = x_ref.shape
    tn = wg_ref.shape[1]
    i = pl.program_id(1)
    j = pl.program_id(2)
    lo = SUBLANES - (FFN_CONV - 1)

    @pl.when(j == 0)
    def _():
        x = x_ref[...]
        y = x * lax.rsqrt(jnp.mean(x * x, axis=-1, keepdims=True) + EPS) * g_ref[...]
        h = y * (1.0 + sc_ref[...]) + sh_ref[...]
        h_ref[...] = h.reshape(bb * tt, D).astype(BF16)

    if split_rows:
        (carry_s,) = scr
        halves = ((wg_ref, cwg_ref, fg_ref, fng_o, 0), (wu_ref, cwu_ref, fu_ref, fnu_o, 1))
        for _, _, f_ref, _, idx in halves:
            @pl.when(i == 0)
            def _(f_ref=f_ref, idx=idx):
                carry_s[idx, j, 0, lo:SUBLANES, :] = f_ref[0]
        prev = [carry_s[idx, j, 0] for idx in range(2)]
        row8 = lax.broadcasted_iota(jnp.int32, (SUBLANES, tn), 0)
        rc = min(tt, FFN_ROW_CHUNK)
        for r in range(tt // rc):
            rows = slice(r * rc, (r + 1) * rc)
            ys = []
            for w_ref, cw_ref, _, _, idx in halves:
                u = _dot(h_ref[rows, :], w_ref[...])
                y = u * cw_ref[FFN_CONV - 1:FFN_CONV, :]
                for k in range(1, FFN_CONV):
                    s = pltpu.roll(u, k, 0)
                    head = jnp.where(row8 < k, pltpu.roll(prev[idx], k, 0), s[0:SUBLANES])
                    s = jnp.concatenate([head, s[SUBLANES:]], axis=0)
                    y = y + s * cw_ref[FFN_CONV - 1 - k:FFN_CONV - k, :]
                prev[idx] = u[rc - SUBLANES:rc]
                ys.append(y)
            act_o[0, rows, :] = (ys[0] * jax.nn.sigmoid(ys[0]) * ys[1]).astype(act_o.dtype)
        for _, _, _, fn_o, idx in halves:
            carry_s[idx, j, 0] = prev[idx]
            fn_o[...] = prev[idx][lo:SUBLANES][None]
    else:
        bg_s, bu_s = scr

        def half(w_ref, cw_ref, f_ref, fn_o, buf):
            u = _dot(h_ref[...], w_ref[...]).reshape(bb, tt, tn)
            buf[:, lo:SUBLANES, :] = f_ref[...]
            buf[:, SUBLANES:SUBLANES + tt, :] = u
            fn_o[...] = u[:, tt - (FFN_CONV - 1):tt, :]
            y = None
            for jw in range(FFN_CONV):
                term = buf[:, lo + jw:lo + jw + tt, :] * cw_ref[jw:jw + 1, :]
                y = term if y is None else y + term
            return y

        yg = half(wg_ref, cwg_ref, fg_ref, fng_o, bg_s)
        yu = half(wu_ref, cwu_ref, fu_ref, fnu_o, bu_s)
        act_o[...] = (yg * jax.nn.sigmoid(yg) * yu).astype(act_o.dtype)


def _ffn_up(x, g, shift, scale, w_up, layer, conv_w, fbuf):
    B, T, D = x.shape
    FF = w_up.shape[2] // 2
    bb, tt = _row_tiles(B, T)
    tn = COL_TILE
    assert FF % tn == 0 and tt >= SUBLANES
    nj = FF // tn
    split_rows = T > tt
    assert not split_rows or bb == 1
    nf = FFN_CONV - 1

    def cols(off, rows, lead=()):
        return pl.BlockSpec((None,) * len(lead) + rows + (tn,),
                            lambda b, i, j: lead + (0,) * len(rows) + (off + j,))

    def fb(off):
        return pl.BlockSpec((bb, nf, tn), lambda b, i, j: (b, 0, off + j))

    act, fng, fnu = pl.pallas_call(
        functools.partial(_ffn_up_kernel, split_rows=split_rows),
        grid=(B // bb, T // tt, nj),
        in_specs=[pl.BlockSpec((bb, tt, D), lambda b, i, j: (b, i, 0)),
                  pl.BlockSpec((1, 1, D), lambda b, i, j: (0, 0, 0)),
                  pl.BlockSpec((bb, 1, D), lambda b, i, j: (b, 0, 0)),
                  pl.BlockSpec((bb, 1, D), lambda b, i, j: (b, 0, 0)),
                  cols(0, (D,), (layer,)), cols(nj, (D,), (layer,)),
                  cols(0, (FFN_CONV,)), cols(nj, (FFN_CONV,)),
                  fb(0), fb(nj)],
        out_specs=[pl.BlockSpec((bb, tt, tn), lambda b, i, j: (b, i, j)),
                   pl.BlockSpec((bb, None, nf, tn), lambda b, i, j: (b, i, 0, j)),
                   pl.BlockSpec((bb, None, nf, tn), lambda b, i, j: (b, i, 0, j))],
        out_shape=[jax.ShapeDtypeStruct((B, T, FF), BF16),
                   jax.ShapeDtypeStruct((B, T // tt, nf, FF), F32),
                   jax.ShapeDtypeStruct((B, T // tt, nf, FF), F32)],
        scratch_shapes=[pltpu.VMEM((bb * tt, D), BF16)] + (
            [pltpu.VMEM((2, nj, 1, SUBLANES, tn), F32)] if split_rows else
            [pltpu.VMEM((bb, tt + SUBLANES, tn), F32), pltpu.VMEM((bb, tt + SUBLANES, tn), F32)]),
        compiler_params=_cparams(("parallel", "arbitrary", "arbitrary")),
        name="ffn_up",
    )(x, g.reshape(1, 1, D), shift, scale, w_up, w_up, conv_w, conv_w, fbuf, fbuf)
    return act, jnp.concatenate([fng[:, -1], fnu[:, -1]], axis=-1)


def _prep_w_in(w_in):
    depth, D = w_in.shape[0], w_in.shape[1]
    w_in = w_in.astype(BF16)
    sizes = (1024, 1024, 1024, 1024, 256, 256, 512, 64, 8, C_QKV, 8, 8, 1024, N_BRANCH * D)
    offs = np.concatenate([[0], np.cumsum(sizes)])
    (aq, ak, av, bq, bk, bv, bqi, bki, bwi, cqkv, ca, cb, cz, gt) = [
        w_in[:, :, int(offs[k]):int(offs[k + 1])] for k in range(len(sizes))]

    def z(n):
        return jnp.zeros((depth, D, n), w_in.dtype)

    cols = [cqkv, cz, aq, ak, av, bq, bk, bv, bqi, bki, z(LANES - I_HD), bwi, ca, cb, z(LANES - 24),
            z(OFF_GT - OFF_SMALL - LANES), gt]
    w = jnp.concatenate(cols, axis=2)
    assert w.shape[2] == N_PROJ
    return w


def _layer(x, mod, past, lw, layer_idx):
    kA_p, vA_p, kB_p, vB_p, kI_p, cbuf, S0, fbuf = past
    B, T, D = x.shape
    P = 0 if kI_p is None else kI_p.shape[3]
    sh1, sc1, g1, sh2, sc2, g2 = [m.reshape(B, 1, D) for m in jnp.split(mod, 6, axis=-1)]

    proj = _mm_norm(x, lw['norm_mix'], sh1, sc1, lw['w_in'], layer_idx)
    rope = _rope_table(P + jnp.arange(T, dtype=jnp.int32))
    (qa, ka, kab, va, vab, qb, kb, kbb, vb, vbb, qi, ki, ki2) = _prep(
        proj, rope, lw['a_q_norm'], lw['a_k_norm'], lw['b_q_norm'], lw['b_k_norm'])

    lam_init = 0.8 - 0.6 * math.exp(-0.3 * layer_idx)
    if P:
        oa = _diff_decode(qa, kab, vab, kA_p, vA_p, layer_idx, lw['a_lambda'], lw['a_subln'], P, lam_init)
        ob = _dsa_decode(qi, proj, qb, ki2, kbb, vbb, kI_p, kB_p, vB_p, layer_idx)
    else:
        oa = _diff_attn(qa, kab, vab, lw['a_lambda'], lw['a_subln'], lam_init)
        ob = _dsa(qi, proj, qb, ki2, kbb, vbb, 0, T, min(T, 512))

    oc, S_fin = _gdn(proj, lw['c_conv'], cbuf, S0, lw['c_a_log'], lw['c_dt_bias'], lw['c_out_norm'])
    assert T >= C_CONV - 1 and T >= FFN_CONV - 1
    cbuf_new = proj[:, T - (C_CONV - 1):, OFF_CQKV:OFF_CQKV + C_QKV]

    mixed = _merge(oa, ob, oc, lw['w_branch'], layer_idx, proj, D)
    x = _mm_res(mixed, lw['w_out'], layer_idx, x, g1)

    act, fbuf_new = _ffn_up(x, lw['norm_ffn'], sh2, sc2, lw['w_up'], layer_idx, lw['ffn_conv'], fbuf)
    x = _mm_res(act, lw['w_down'], layer_idx, x, g2, row_target=512)

    va_out = va.reshape(B, T, A_VD // LANES, A_HEADS, LANES).transpose(0, 1, 3, 2, 4)
    new_state = (ka.reshape(B, T, A_HEADS, 2, A_HD), va_out.reshape(B, T, A_HEADS, A_VD),
                 kb.reshape(B, T, B_KV, B_HD), vb.reshape(B, T, B_KV, B_HD), ki,
                 cbuf_new, S_fin, fbuf_new)
    return x, new_state


def kernel(x_prompt, x_sample, c_prompt, c_sample, cache_diff_k, cache_diff_v, cache_dsa_k, cache_dsa_v, cache_dsa_kidx, state_gdn_conv, state_gdn, state_ffn_conv, w_ada, b_ada, norm_mix, w_in, a_q_norm, a_k_norm, a_lambda, a_subln, b_q_norm, b_k_norm, c_conv, c_a_log, c_dt_bias, c_out_norm, w_branch, w_out, norm_ffn, w_up, ffn_conv, w_down):
    depth = w_in.shape[0]
    bp, bs = x_prompt.shape[0], x_sample.shape[0]
    d_ff2 = w_up.shape[2]
    dt_ = x_prompt.dtype
    prompt_past = (None, None, None, None, None, jnp.zeros((bp, C_CONV - 1, C_QKV), dt_),
                   jnp.zeros((bp, C_HEADS, C_HD, C_HD), dt_), jnp.zeros((bp, FFN_CONV - 1, d_ff2), dt_))
    nrow = -(-(bp + bs) // SUBLANES) * SUBLANES
    c_all = jnp.concatenate([c_prompt, c_sample, jnp.zeros((nrow - bp - bs, c_prompt.shape[1]), dt_)], axis=0)
    past_len = cache_diff_k.shape[2]
    diff_k_rows = cache_diff_k.reshape(depth, bs, past_len * A_HEADS * 2, A_HD)
    diff_v_rows = cache_diff_v.reshape(depth, bs, past_len, A_HEADS, A_VD // LANES, LANES)
    diff_v_rows = diff_v_rows.transpose(0, 1, 2, 4, 3, 5).reshape(depth, bs, past_len * A_HEADS * 2, LANES)
    dsa_k_rows = cache_dsa_k.reshape(depth, bs, past_len * B_KV, B_HD)
    dsa_v_rows = cache_dsa_v.reshape(depth, bs, past_len * B_KV, B_HD)
    dsa_kidx_t = jnp.swapaxes(cache_dsa_kidx, 2, 3)
    xp, xs = x_prompt, x_sample
    prompt_states, sample_states = [], []
    big = dict(w_in=_prep_w_in(w_in), w_branch=w_branch.astype(BF16), w_out=w_out.astype(BF16),
               w_up=w_up.astype(BF16), w_down=w_down.astype(BF16))
    for l in range(depth):
        lw = dict(big, norm_mix=norm_mix[l],
                  a_q_norm=a_q_norm[l], a_k_norm=a_k_norm[l], a_lambda=a_lambda[l], a_subln=a_subln[l],
                  b_q_norm=b_q_norm[l], b_k_norm=b_k_norm[l], c_conv=c_conv[l], c_a_log=c_a_log[l],
                  c_dt_bias=c_dt_bias[l], c_out_norm=c_out_norm[l], norm_ffn=norm_ffn[l],
                  ffn_conv=ffn_conv[l])
        mod = _ada(c_all, w_ada, b_ada, l)
        xp, st_p = _layer(xp, mod[:bp], prompt_past, lw, l)
        sample_past = (diff_k_rows, diff_v_rows, dsa_k_rows, dsa_v_rows, dsa_kidx_t,
                       state_gdn_conv[l], state_gdn[l], state_ffn_conv[l])
        xs, st_s = _layer(xs, mod[bp:bp + bs], sample_past, lw, l)
        prompt_states.append(st_p)
        sample_states.append(st_s)
    p_out = [jnp.stack(s, axis=0) for s in zip(*prompt_states)]
    s_out = [jnp.stack(s, axis=0) for s in zip(*sample_states)]
    return (xp, xs, *p_out, *s_out)
```

```python
import functools
import math

import jax
import jax.numpy as jnp
import numpy as np
from jax import lax
from jax.experimental import pallas as pl
from jax.experimental.pallas import tpu as pltpu

F32 = jnp.float32
BF16 = jnp.bfloat16

CHUNK = 64
CHUNK_SHIFT = 6
ROPE_THETA = 10000.0
EPS = 1e-6
LOG2E = 1.4426950408889634
A_HEADS, A_HD, A_VD = 4, 128, 256
B_HEADS, B_KV, B_HD = 8, 2, 128
I_HEADS, I_HD = 8, 64
I_HD_SHIFT = 6
TOPK_MAX = 256
C_HEADS, C_HD, C_CONV = 8, 128, 4
C_QKV = 3 * C_HEADS * C_HD
BRANCH_W = 1024
N_BRANCH = 3
FFN_CONV = 3

LANES = 128
SUBLANES = 8
VMEM_LIMIT = 56 * 1024 * 1024
ROW_TILE = 1024
COL_TILE = 512
SOLVE_BLOCK = 16
TILE_UNROLL = 4
FFN_ROW_CHUNK = 256
NEG = -1e30
KEY_MIN = -(2 ** 31)

OFF_CQKV = 0
OFF_CZ = 3072
OFF_AQ = 4096
OFF_AK = 5120
OFF_AV = 6144
OFF_BQ = 7168
OFF_BK = 8192
OFF_BV = 8448
OFF_BQI = 8704
OFF_BKI = 9216
OFF_SMALL = 9344
CA_LANE = 8
CB_LANE = 16
OFF_GT = 9728
N_PROJ = 15872


def _cparams(sem):
    return pltpu.CompilerParams(dimension_semantics=sem, vmem_limit_bytes=VMEM_LIMIT)


def _row_tiles(B, T, target=ROW_TILE):
    tt = min(T, target)
    assert T % tt == 0 and tt % SUBLANES == 0
    bb = max(1, min(B, target // tt))
    while B % bb:
        bb -= 1
    return bb, tt


def _dot(a, b):
    return jnp.dot(a, b, preferred_element_type=F32)


def _dot_nt(a, b):
    return lax.dot_general(a, b, (((1,), (1,)), ((), ())), preferred_element_type=F32)


def _bdot(a, b):
    return _dot(a.astype(BF16), b.astype(BF16))


def _tile_loop(n, body, init, unroll):
    nb = n // unroll

    def trip(tb, c):
        for u in range(unroll):
            c = body(tb * unroll + u, c)
        return c

    c = lax.fori_loop(0, nb, trip, init)
    return lax.fori_loop(nb * unroll, n, body, c)


def _ada_kernel(c_ref, w_ref, b_ref, o_ref):
    c = c_ref[...]
    s = (c * jax.nn.sigmoid(c)).astype(BF16)
    o_ref[...] = _dot(s, w_ref[...].astype(BF16)) + b_ref[...]


def _ada(c, w, b, layer):
    M, D = c.shape
    N = w.shape[2]
    tn = 1024
    return pl.pallas_call(
        _ada_kernel,
        grid=(N // tn,),
        in_specs=[pl.BlockSpec((M, D), lambda j: (0, 0)),
                  pl.BlockSpec((None, D, tn), lambda j: (layer, 0, j)),
                  pl.BlockSpec((None, 1, tn), lambda j: (layer, 0, j))],
        out_specs=pl.BlockSpec((M, tn), lambda j: (0, j)),
        out_shape=jax.ShapeDtypeStruct((M, N), F32),
        compiler_params=_cparams(("parallel",)),
        name="ada",
    )(c, w, b.reshape(-1, 1, N))


def _mm_norm_kernel(x_ref, g_ref, sh_ref, sc_ref, w_ref, o_ref, h_ref):
    bb, tt, D = x_ref.shape
    tn = w_ref.shape[1]

    @pl.when(pl.program_id(2) == 0)
    def _():
        x = x_ref[...]
        y = x * lax.rsqrt(jnp.mean(x * x, axis=-1, keepdims=True) + EPS) * g_ref[...]
        h = y * (1.0 + sc_ref[...]) + sh_ref[...]
        h_ref[...] = h.reshape(bb * tt, D).astype(BF16)

    acc = _dot(h_ref[...], w_ref[...])
    o_ref[...] = acc.reshape(bb, tt, tn).astype(o_ref.dtype)


def _mm_norm(x, g, shift, scale, w, layer, out_dtype=F32):
    B, T, D = x.shape
    N = w.shape[2]
    bb, tt = _row_tiles(B, T)
    tn = COL_TILE
    assert N % tn == 0
    return pl.pallas_call(
        _mm_norm_kernel,
        grid=(B // bb, T // tt, N // tn),
        in_specs=[pl.BlockSpec((bb, tt, D), lambda b, i, j: (b, i, 0)),
                  pl.BlockSpec((1, 1, D), lambda b, i, j: (0, 0, 0)),
                  pl.BlockSpec((bb, 1, D), lambda b, i, j: (b, 0, 0)),
                  pl.BlockSpec((bb, 1, D), lambda b, i, j: (b, 0, 0)),
                  pl.BlockSpec((None, D, tn), lambda b, i, j: (layer, 0, j))],
        out_specs=pl.BlockSpec((bb, tt, tn), lambda b, i, j: (b, i, j)),
        out_shape=jax.ShapeDtypeStruct((B, T, N), out_dtype),
        scratch_shapes=[pltpu.VMEM((bb * tt, D), BF16)],
        compiler_params=_cparams(("parallel", "parallel", "arbitrary")),
        name="mm_norm",
    )(x, g.reshape(1, 1, D), shift, scale, w)


def _mm_res_kernel(a_ref, w_ref, res_ref, g_ref, o_ref):
    bb, tt, K = a_ref.shape
    tn = w_ref.shape[1]
    acc = _dot(a_ref[...].reshape(bb * tt, K), w_ref[...])
    o_ref[...] = res_ref[...] + g_ref[...] * acc.reshape(bb, tt, tn)


def _mm_res(a, w, layer, res, gate, row_target=ROW_TILE):
    B, T, K = a.shape
    N = w.shape[2]
    bb, tt = _row_tiles(B, T, row_target)
    tn = COL_TILE
    assert N % tn == 0
    return pl.pallas_call(
        _mm_res_kernel,
        grid=(B // bb, T // tt, N // tn),
        in_specs=[pl.BlockSpec((bb, tt, K), lambda b, i, j: (b, i, 0)),
                  pl.BlockSpec((None, K, tn), lambda b, i, j: (layer, 0, j)),
                  pl.BlockSpec((bb, tt, tn), lambda b, i, j: (b, i, j)),
                  pl.BlockSpec((bb, 1, tn), lambda b, i, j: (b, 0, j))],
        out_specs=pl.BlockSpec((bb, tt, tn), lambda b, i, j: (b, i, j)),
        out_shape=jax.ShapeDtypeStruct((B, T, N), F32),
        compiler_params=_cparams(("parallel", "parallel", "arbitrary")),
        name="mm_res",
    )(a, w, res, gate)


def _merge_kernel(b0_ref, b1_ref, b2_ref, w_ref, g0_ref, g1_ref, g2_ref, o_ref):
    bb, tt, W = b0_ref.shape
    tn = w_ref.shape[2]
    acc = None
    for n, (br, gr) in enumerate(((b0_ref, g0_ref), (b1_ref, g1_ref), (b2_ref, g2_ref))):
        y = _dot(br[...].reshape(bb * tt, W), w_ref[n])
        t = jax.nn.sigmoid(gr[...].reshape(bb * tt, tn)) * y
        acc = t if acc is None else acc + t
    o_ref[...] = acc.reshape(bb, tt, tn).astype(o_ref.dtype)


def _merge(oa, ob, oc, w_branch, layer, proj, D):
    B, T, W = oa.shape
    bb, tt = _row_tiles(B, T)
    tn = COL_TILE
    gblk = OFF_GT // tn
    nper = D // tn
    br_spec = pl.BlockSpec((bb, tt, W), lambda b, i, j: (b, i, 0))

    def gate_spec(n):
        return pl.BlockSpec((bb, tt, tn), lambda b, i, j: (b, i, gblk + n * nper + j))

    return pl.pallas_call(
        _merge_kernel,
        grid=(B // bb, T // tt, D // tn),
        in_specs=[br_spec, br_spec, br_spec,
                  pl.BlockSpec((None, N_BRANCH, W, tn), lambda b, i, j: (layer, 0, 0, j)),
                  gate_spec(0), gate_spec(1), gate_spec(2)],
        out_specs=pl.BlockSpec((bb, tt, tn), lambda b, i, j: (b, i, j)),
        out_shape=jax.ShapeDtypeStruct((B, T, D), BF16),
        compiler_params=_cparams(("parallel", "parallel", "arbitrary")),
        name="merge",
    )(oa, ob, oc, w_branch, proj, proj, proj)


def _prep_kernel(aqk_ref, av_ref, bq_ref, bkv_ref, bqi_ref, bki_ref, rope_ref,
                 gaq_ref, gak_ref, gbq_ref, gbk_ref,
                 qa_o, ka_o, kab_o, va_o, vab_o, qb_o, kb_o, kbb_o, vb_o, vbb_o,
                 qi_o, ki_o, ki2_o):
    c1 = rope_ref[:, 0:LANES]
    s1 = rope_ref[:, LANES:2 * LANES]
    c2 = rope_ref[:, 2 * LANES:3 * LANES]
    s2 = rope_ref[:, 3 * LANES:4 * LANES]
    tt = c1.shape[0]
    lane = lax.broadcasted_iota(jnp.int32, (tt, LANES), 1)
    low_half = (lane & (I_HD - 1)) < (I_HD // 2)

    def norm_rope(x, g):
        y = x * lax.rsqrt(jnp.mean(x * x, axis=-1, keepdims=True) + EPS) * g
        return y * c1 + pltpu.roll(y, A_HD // 2, 1) * s1

    def rope64(x):
        r = jnp.where(low_half, pltpu.roll(x, LANES - I_HD // 2, 1), pltpu.roll(x, I_HD // 2, 1))
        return x * c2 + r * s2

    gaq, gak, gbq, gbk = gaq_ref[...], gak_ref[...], gbq_ref[...], gbk_ref[...]
    for hm in range(2 * A_HEADS):
        sl = slice(hm * LANES, (hm + 1) * LANES)
        q = norm_rope(aqk_ref[0, :, sl], gaq) * (A_HD ** -0.5 * LOG2E)
        qa_o[0, :, sl] = q.astype(BF16)
        k = norm_rope(aqk_ref[0, :, 2 * A_HEADS * LANES + hm * LANES:2 * A_HEADS * LANES + (hm + 1) * LANES], gak)
        ka_o[0, pl.ds(hm, tt, stride=2 * A_HEADS), :] = k
        kab_o[0, :, sl] = k.astype(BF16)
    va = av_ref[0]
    for h in range(A_HEADS):
        for c in range(A_VD // LANES):
            va_o[0, pl.ds(c * A_HEADS + h, tt, stride=2 * A_HEADS), :] = (
                va[:, h * A_VD + c * LANES:h * A_VD + (c + 1) * LANES])
    vab_o[0] = va.astype(BF16)
    for h in range(B_HEADS):
        sl = slice(h * LANES, (h + 1) * LANES)
        q = norm_rope(bq_ref[0, :, sl], gbq) * (B_HD ** -0.5 * LOG2E)
        qb_o[0, :, sl] = q.astype(BF16)
    for n in range(B_KV):
        sl = slice(n * LANES, (n + 1) * LANES)
        k = norm_rope(bkv_ref[0, :, sl], gbk)
        kb_o[0, pl.ds(n, tt, stride=B_KV), :] = k
        kbb_o[0, :, sl] = k.astype(BF16)
    vb = bkv_ref[0, :, B_KV * LANES:2 * B_KV * LANES]
    for n in range(B_KV):
        vb_o[0, pl.ds(n, tt, stride=B_KV), :] = vb[:, n * LANES:(n + 1) * LANES]
    vbb_o[0] = vb.astype(BF16)
    for p in range(I_HEADS * I_HD // LANES):
        sl = slice(p * LANES, (p + 1) * LANES)
        qi_o[0, :, sl] = (rope64(bqi_ref[0, :, sl]) * (I_HD ** -0.5)).astype(BF16)
    ki = rope64(bki_ref[0, :, 0:LANES])
    ki_o[0] = ki[:, 0:I_HD]
    ki2_o[0] = (ki + pltpu.roll(ki, I_HD, 1)).astype(BF16)


def _rope_table(pos):
    def tab(half):
        inv = jnp.power(ROPE_THETA, -jnp.arange(half, dtype=F32) / half)
        ang = pos.astype(F32)[:, None] * inv[None, :]
        return jnp.cos(ang), jnp.sin(ang)

    c, s = tab(A_HD // 2)
    ci, si = tab(I_HD // 2)
    return jnp.concatenate([c, c, -s, s, ci, ci, ci, ci, -si, si, -si, si], axis=1)


def _prep(proj, rope, gaq, gak, gbq, gbk):
    B, T, _ = proj.shape
    tt = min(T, 256)
    assert T % tt == 0

    def pspec(width, off):
        assert off % width == 0
        return pl.BlockSpec((1, tt, width), lambda b, i: (b, i, off // width))

    gspec = pl.BlockSpec((1, LANES), lambda b, i: (0, 0))
    ra, rb = 2 * A_HEADS, B_KV
    outs = [(1, 1024, BF16), (ra, LANES, F32), (1, 1024, BF16), (ra, LANES, F32), (1, 1024, BF16),
            (1, 1024, BF16), (rb, LANES, F32), (1, 256, BF16), (rb, LANES, F32), (1, 256, BF16),
            (1, 512, BF16), (1, I_HD, F32), (1, LANES, BF16)]
    return pl.pallas_call(
        _prep_kernel,
        grid=(B, T // tt),
        in_specs=[pspec(2048, OFF_AQ), pspec(1024, OFF_AV), pspec(1024, OFF_BQ), pspec(512, OFF_BK),
                  pspec(512, OFF_BQI), pspec(256, OFF_BKI),
                  pl.BlockSpec((tt, 4 * LANES), lambda b, i: (i, 0)),
                  gspec, gspec, gspec, gspec],
        out_specs=[pl.BlockSpec((1, tt * r, w), lambda b, i: (b, i, 0)) for r, w, _ in outs],
        out_shape=[jax.ShapeDtypeStruct((B, T * r, w), d) for r, w, d in outs],
        compiler_params=_cparams(("parallel", "parallel")),
        name="prep",
    )(proj, proj, proj, proj, proj, proj, rope,
      gaq.reshape(1, LANES), gak.reshape(1, LANES), gbq.reshape(1, LANES), gbk.reshape(1, LANES))


def _diff_kernel(q_ref, kn_ref, vn_ref, lam_ref, sub_ref, o_ref, acc0_s, acc1_s, sn_s, *, tq, lam_init):
    acc = (acc0_s, acc1_s)
    i = pl.program_id(2)
    q_pos0 = i * tq
    q = [q_ref[0, :, m * A_HD:(m + 1) * A_HD] for m in range(2)]
    msl = [slice(m * A_HD, (m + 1) * A_HD) for m in range(2)]
    wide = tq % LANES == 0

    def fold(x, op):
        f = x[:, 0:LANES]
        for c in range(1, x.shape[1] // LANES):
            f = op(f, x[:, c * LANES:(c + 1) * LANES])
        return f

    neg = tuple(jnp.full((tq, LANES), NEG, F32) for _ in range(2))
    zero = tuple(jnp.zeros((tq, LANES), F32) for _ in range(2))

    def scores(t, mr):
        r0 = pl.multiple_of(t * tq, tq)
        k = kn_ref[0, pl.ds(r0, tq), :]
        out = []
        for m in range(2):
            s = _dot_nt(q[m], k[:, msl[m]])
            sn_s[m, t] = s
            out.append(jnp.maximum(mr[m], fold(s, jnp.maximum)) if wide else mr[m])
        return tuple(out)
    mn = _tile_loop(i, scores, neg, TILE_UNROLL)

    r0 = pl.multiple_of(i * tq, tq)
    qpos = q_pos0 + lax.broadcasted_iota(jnp.int32, (tq, 1), 0)
    kpos = q_pos0 + lax.broadcasted_iota(jnp.int32, (1, tq), 1)
    vis = (kpos >> CHUNK_SHIFT) <= (qpos >> CHUNK_SHIFT)
    kd = kn_ref[0, pl.ds(r0, tq), :]
    mrow = []
    for m in range(2):
        s = jnp.where(vis, _dot_nt(q[m], kd[:, msl[m]]), NEG)
        sn_s[m, i] = s
        mx = jnp.max(s, axis=-1, keepdims=True)
        if wide:
            mx = jnp.maximum(mx, jnp.max(mn[m], axis=-1, keepdims=True))
        if wide:
            mx = pltpu.repeat(jnp.broadcast_to(mx, (tq, LANES)), tq // LANES, axis=1)
        mrow.append(mx)

    for m in range(2):
        acc[m][...] = jnp.zeros(acc[m].shape, F32)

    def weigh(t, lr):
        r0 = pl.multiple_of(t * tq, tq)
        v = vn_ref[0, pl.ds(r0, tq), :]
        out = []
        for m in range(2):
            p = jnp.exp2(sn_s[m, t] - mrow[m])
            if wide:
                out.append(lr[m] + fold(p, jnp.add))
            else:
                out.append(lr[m] + jnp.sum(p, axis=-1, keepdims=True))
            acc[m][...] += _dot(p.astype(BF16), v)
        return tuple(out)
    ln = _tile_loop(i + 1, weigh, zero if wide else tuple(jnp.zeros((tq, 1), F32) for _ in range(2)),
                    TILE_UNROLL)
    lrow = [jnp.sum(ln[m], axis=-1, keepdims=True) for m in range(2)]

    lp = lam_ref[...]
    lam = (jnp.exp(jnp.sum(lp[0:1] * lp[1:2], axis=-1, keepdims=True))
           - jnp.exp(jnp.sum(lp[2:3] * lp[3:4], axis=-1, keepdims=True)) + lam_init)
    o = acc[0][...] / lrow[0] - lam * (acc[1][...] / lrow[1])
    o = o * lax.rsqrt(jnp.mean(o * o, axis=-1, keepdims=True) + EPS) * sub_ref[...]
    o_ref[0] = (o * (1.0 - lam_init)).astype(o_ref.dtype)


def _diff_attn(qa, kn, vn, lam_p, subln, lam_init):
    B, T, _ = qa.shape
    tq = min(T, 512)
    assert T % tq == 0 and (tq % CHUNK == 0 or tq == T)
    W = 2 * A_HD
    return pl.pallas_call(
        functools.partial(_diff_kernel, tq=tq, lam_init=lam_init),
        grid=(B, A_HEADS, T // tq),
        in_specs=[pl.BlockSpec((1, tq, W), lambda b, h, i: (b, i, h)),
                  pl.BlockSpec((1, T, W), lambda b, h, i: (b, 0, h)),
                  pl.BlockSpec((1, T, W), lambda b, h, i: (b, 0, h)),
                  pl.BlockSpec((4, A_HD), lambda b, h, i: (0, 0)),
                  pl.BlockSpec((1, A_VD), lambda b, h, i: (0, 0))],
        out_specs=pl.BlockSpec((1, tq, W), lambda b, h, i: (b, i, h)),
        out_shape=jax.ShapeDtypeStruct((B, T, A_HEADS * A_VD), BF16),
        scratch_shapes=[pltpu.VMEM((tq, A_VD), F32), pltpu.VMEM((tq, A_VD), F32),
                        pltpu.VMEM((2, T // tq, tq, tq), F32)],
        compiler_params=_cparams(("parallel", "parallel", "arbitrary")),
        name="diff_attn",
    )(qa, kn, vn, lam_p, subln.reshape(1, A_VD))


def _diff_decode_kernel(q_ref, kn_ref, vn_ref, kp_ref, vp_ref, lam_ref, sub_ref, o_ref, *, P, lam_init):
    T = q_ref.shape[1]
    per_pos = A_HEADS * 2
    qpos = P + lax.broadcasted_iota(jnp.int32, (T, 1), 0)
    kpos = P + lax.broadcasted_iota(jnp.int32, (1, T), 1)
    vis = (kpos >> CHUNK_SHIFT) <= (qpos >> CHUNK_SHIFT)
    lp = lam_ref[...]
    lam = (jnp.exp(jnp.sum(lp[0:1] * lp[1:2], axis=-1, keepdims=True))
           - jnp.exp(jnp.sum(lp[2:3] * lp[3:4], axis=-1, keepdims=True)) + lam_init)
    for h in range(A_HEADS):
        v_past = jnp.concatenate(
            [vp_ref[pl.ds(c * A_HEADS + h, P, stride=per_pos), :] for c in range(A_VD // LANES)],
            axis=-1).astype(BF16)
        v_new = vn_ref[0, :, h * A_VD:(h + 1) * A_VD]
        outs = []
        for m in range(2):
            hm = h * 2 + m
            q = q_ref[0, :, hm * A_HD:(hm + 1) * A_HD]
            k_past = kp_ref[pl.ds(hm, P, stride=per_pos), :].astype(BF16)
            s_p = _dot_nt(q, k_past)
            s_n = jnp.where(vis, _dot_nt(q, kn_ref[0, :, hm * A_HD:(hm + 1) * A_HD]), NEG)
            mx = jnp.maximum(jnp.max(s_p, axis=-1, keepdims=True), jnp.max(s_n, axis=-1, keepdims=True))
            p_p = jnp.exp2(s_p - mx)
            p_n = jnp.exp2(s_n - mx)
            l = jnp.sum(p_p, axis=-1, keepdims=True) + jnp.sum(p_n, axis=-1, keepdims=True)
            outs.append((_dot(p_p.astype(BF16), v_past) + _dot(p_n.astype(BF16), v_new)) / l)
        o = outs[0] - lam * outs[1]
        o = o * lax.rsqrt(jnp.mean(o * o, axis=-1, keepdims=True) + EPS) * sub_ref[...]
        o_ref[0, :, h * A_VD:(h + 1) * A_VD] = (o * (1.0 - lam_init)).astype(o_ref.dtype)


def _diff_decode(qa, kn, vn, kp_rows, vp_rows, layer, lam_p, subln, P, lam_init):
    B, T, W = qa.shape
    assert P % CHUNK == 0 and kp_rows.shape[2] == P * A_HEADS * 2 and vp_rows.shape[2] == P * A_HEADS * 2
    new_spec = pl.BlockSpec((1, T, W), lambda b: (b, 0, 0))
    cache_spec = pl.BlockSpec((None, None, P * A_HEADS * 2, LANES), lambda b: (layer, b, 0, 0))
    return pl.pallas_call(
        functools.partial(_diff_decode_kernel, P=P, lam_init=lam_init),
        grid=(B,),
        in_specs=[new_spec, new_spec, new_spec, cache_spec, cache_spec,
                  pl.BlockSpec((4, A_HD), lambda b: (0, 0)),
                  pl.BlockSpec((1, A_VD), lambda b: (0, 0))],
        out_specs=new_spec,
        out_shape=jax.ShapeDtypeStruct((B, T, W), BF16),
        compiler_params=_cparams(("parallel",)),
        name="diff_decode",
    )(qa, kn, vn, kp_rows, vp_rows, lam_p, subln.reshape(1, A_VD))


def _dsa_kernel(qi_ref, wi_ref, qb_ref, ki_ref, kb_ref, vb_ref, o_ref,
                key_s, bias_s, s_s, *, P, L, tq, tk, topk):
    i = pl.program_id(1)
    q_pos0 = P + i * tq
    Lp = ki_ref.shape[1]
    ncol = jnp.minimum(L, ((q_pos0 + tq - 1) // CHUNK + 1) * CHUNK)
    nt = (ncol + tk - 1) // tk
    qpos = q_pos0 + lax.broadcasted_iota(jnp.int32, (tq, 1), 0)
    grp = B_HEADS // B_KV
    lane = lax.broadcasted_iota(jnp.int32, (tq, LANES), 1)
    wi = wi_ref[0] * (I_HEADS ** -0.5)
    w_lanes = [jnp.broadcast_to(wi[:, h:h + 1], (tq, LANES)) for h in range(I_HEADS)]

    def vis_of(t):
        kpos = t * tk + lax.broadcasted_iota(jnp.int32, (1, tk), 1)
        return ((kpos >> CHUNK_SHIFT) <= (qpos >> CHUNK_SHIFT)) & (kpos < L)

    def score_tile(t, carry):
        c0 = pl.multiple_of(t * tk, tk)
        kit = ki_ref[0, pl.ds(c0, tk), :]
        sc = jnp.zeros((tq, tk), F32)
        for pr in range(I_HEADS * I_HD // LANES):
            qpair = qi_ref[0, :, pr * LANES:(pr + 1) * LANES]
            for half in range(LANES // I_HD):
                h = pr * (LANES // I_HD) + half
                qh = jnp.where((lane >> I_HD_SHIFT) == half, qpair, jnp.zeros_like(qpair))
                isc = _dot_nt(qh, kit)
                sc = sc + jnp.maximum(isc, 0.0) * pltpu.repeat(w_lanes[h], tk // LANES, axis=1)
        sc = jnp.where(vis_of(t), sc, -jnp.inf)
        key_s[t] = _float_order_key(sc)
        return carry

    lax.fori_loop(0, nt, score_tile, 0)

    def counts(pred_fns):
        def body(t, accs):
            kk = key_s[t]
            out = []
            for pred_fn, acc in zip(pred_fns, accs):
                hit = jnp.where(pred_fn(kk), 1, 0)
                part = hit[:, 0:LANES]
                for c in range(1, tk // LANES):
                    part = part + hit[:, c * LANES:(c + 1) * LANES]
                out.append(acc + part)
            return tuple(out)
        accs = _tile_loop(nt, body, tuple(jnp.zeros((tq, LANES), jnp.int32) for _ in pred_fns), 2)
        return [jnp.sum(a.astype(F32), axis=-1, keepdims=True) for a in accs]

    def select(_):
        def bit_body(bi, thr):
            cand = thr + jnp.left_shift(jnp.int32(1), 31 - bi)
            (cnt,) = counts([lambda kk: kk >= cand])
            return jnp.where(cnt >= topk, cand, thr)
        return lax.fori_loop(0, 32, bit_body, jnp.full((tq, 1), KEY_MIN, jnp.int32))

    thr = lax.cond(ncol > topk, select, lambda _: jnp.full((tq, 1), KEY_MIN, jnp.int32), 0)
    n_gt, n_eq = counts([lambda kk: kk > thr, lambda kk: kk == thr])
    need = topk - n_gt
    neg_inf_key = jnp.int32(KEY_MIN + 0x7FFFFF)
    tie_break = jnp.max(jnp.where((n_eq > need) & (thr != neg_inf_key), 1.0, 0.0)) > 0.5

    def bias_fast(_):
        def body(t, c):
            sel = (key_s[t] >= thr) & vis_of(t)
            bias_s[t] = jnp.where(sel, 0.0, NEG)
            return c
        lax.fori_loop(0, nt, body, 0)
        return 0

    def bias_ties(_):
        r = lax.broadcasted_iota(jnp.int32, (LANES, LANES), 0)
        c = lax.broadcasted_iota(jnp.int32, (LANES, LANES), 1)
        before = jnp.where(r < c, 1.0, 0.0).astype(BF16)

        def body(t, seen):
            kk = key_s[t]
            vis = vis_of(t)
            for cb in range(tk // LANES):
                sl = slice(cb * LANES, (cb + 1) * LANES)
                eq = kk[:, sl] == thr
                eqf = jnp.where(eq, 1.0, 0.0)
                rank = seen + _dot(eqf.astype(BF16), before)
                sel = ((kk[:, sl] > thr) | (eq & (rank < need.astype(F32)))) & vis[:, sl]
                bias_s[t, :, sl] = jnp.where(sel, 0.0, NEG)
                seen = seen + jnp.sum(eqf, axis=-1, keepdims=True)
            return seen
        lax.fori_loop(0, nt, body, jnp.zeros((tq, 1), F32))
        return 0

    lax.cond(tie_break, bias_ties, bias_fast, 0)

    def fold(x, op):
        f = x[:, 0:LANES]
        for c in range(1, tk // LANES):
            f = op(f, x[:, c * LANES:(c + 1) * LANES])
        return f

    for n in range(B_KV):
        kv = slice(n * B_HD, (n + 1) * B_HD)

        def scores(t, mrun, kv=kv, n=n):
            c0 = pl.multiple_of(t * tk, tk)
            bias = bias_s[t]
            k = kb_ref[0, pl.ds(c0, tk), kv]
            out = []
            for g in range(grp):
                h = n * grp + g
                s = _dot_nt(qb_ref[0, :, h * B_HD:(h + 1) * B_HD], k) + bias
                s_s[g, t] = s
                out.append(jnp.maximum(mrun[g], fold(s, jnp.maximum)))
            return tuple(out)

        mrun = _tile_loop(nt, scores, tuple(jnp.full((tq, LANES), NEG, F32) for _ in range(grp)), 2)
        mrow = [pltpu.repeat(jnp.broadcast_to(jnp.max(m, axis=-1, keepdims=True), (tq, LANES)),
                             tk // LANES, axis=1) for m in mrun]

        ones = jnp.ones((tk, LANES), BF16)

        def weigh(t, acc, kv=kv, mrow=mrow):
            c0 = pl.multiple_of(t * tk, tk)
            v1 = jnp.concatenate([vb_ref[0, pl.ds(c0, tk), kv], ones], axis=-1)
            return tuple(acc[g] + _dot(jnp.exp2(s_s[g, t] - mrow[g]).astype(BF16), v1)
                         for g in range(grp))

        acc = _tile_loop(nt, weigh, tuple(jnp.zeros((tq, B_HD + LANES), F32) for _ in range(grp)), 2)
        for g in range(grp):
            h = n * grp + g
            o_ref[0, :, h * B_HD:(h + 1) * B_HD] = (
                acc[g][:, 0:B_HD] / acc[g][:, B_HD:B_HD + LANES]).astype(o_ref.dtype)


def _dsa(qi, proj, qb, ki2, kb, vb, P, L, tk):
    B, T, _ = qb.shape
    Lp = ki2.shape[1]
    tq = min(T, 256)
    assert T % tq == 0 and Lp % tk == 0 and tk % LANES == 0
    topk = min(TOPK_MAX, L // 4)
    kern = functools.partial(_dsa_kernel, P=P, L=L, tq=tq, tk=tk, topk=topk)
    return pl.pallas_call(
        kern,
        grid=(B, T // tq),
        in_specs=[pl.BlockSpec((1, tq, I_HEADS * I_HD), lambda b, i: (b, i, 0)),
                  pl.BlockSpec((1, tq, LANES), lambda b, i: (b, i, OFF_SMALL // LANES)),
                  pl.BlockSpec((1, tq, B_HEADS * B_HD), lambda b, i: (b, i, 0)),
                  pl.BlockSpec((1, Lp, LANES), lambda b, i: (b, 0, 0)),
                  pl.BlockSpec((1, Lp, B_KV * B_HD), lambda b, i: (b, 0, 0)),
                  pl.BlockSpec((1, Lp, B_KV * B_HD), lambda b, i: (b, 0, 0))],
        out_specs=pl.BlockSpec((1, tq, B_HEADS * B_HD), lambda b, i: (b, i, 0)),
        out_shape=jax.ShapeDtypeStruct((B, T, B_HEADS * B_HD), BF16),
        scratch_shapes=[pltpu.VMEM((Lp // tk, tq, tk), jnp.int32), pltpu.VMEM((Lp // tk, tq, tk), F32),
                        pltpu.VMEM((B_HEADS // B_KV, Lp // tk, tq, tk), F32)],
        compiler_params=_cparams(("parallel", "arbitrary")),
        name="dsa",
    )(qi, proj, qb, ki2, kb, vb)


def _float_order_key(x):
    bits = pltpu.bitcast(x, jnp.int32)
    bits = jnp.where(bits == KEY_MIN, 0, bits)
    return bits ^ ((bits >> 31) & 0x7FFFFFFF)


def _dsa_decode_kernel(qi_ref, wi_ref, qb_ref, kin_ref, kbn_ref, vbn_ref, kip_ref, kp_ref, vp_ref, o_ref,
                       *, P, topk):
    T = qi_ref.shape[1]
    grp = B_HEADS // B_KV
    qpos = P + lax.broadcasted_iota(jnp.int32, (T, 1), 0)
    kpos = P + lax.broadcasted_iota(jnp.int32, (1, T), 1)
    vis_n = (kpos >> CHUNK_SHIFT) <= (qpos >> CHUNK_SHIFT)
    wi = wi_ref[0] * (I_HEADS ** -0.5)
    ki_pT = kip_ref[...].astype(BF16)
    ki_n = kin_ref[0, :, 0:I_HD]

    sc_p = jnp.zeros((T, P), F32)
    sc_n = jnp.zeros((T, T), F32)
    for h in range(I_HEADS):
        qh = qi_ref[0, :, h * I_HD:(h + 1) * I_HD]
        w = wi[:, h:h + 1]
        sc_p = sc_p + jnp.maximum(_dot(qh, ki_pT), 0.0) * w
        sc_n = sc_n + jnp.maximum(_dot_nt(qh, ki_n), 0.0) * w
    key_p = _float_order_key(sc_p)
    key_n = _float_order_key(jnp.where(vis_n, sc_n, -jnp.inf))

    def count(pred):
        return (jnp.sum(jnp.where(pred(key_p), 1.0, 0.0), axis=-1, keepdims=True)
                + jnp.sum(jnp.where(pred(key_n), 1.0, 0.0), axis=-1, keepdims=True))

    def bit_body(bi, thr):
        cand = thr + jnp.left_shift(jnp.int32(1), 31 - bi)
        return jnp.where(count(lambda kk: kk >= cand) >= topk, cand, thr)

    thr = lax.fori_loop(0, 32, bit_body, jnp.full((T, 1), KEY_MIN, jnp.int32))
    need = topk - count(lambda kk: kk > thr)

    def before(n):
        r = lax.broadcasted_iota(jnp.int32, (n, n), 0)
        c = lax.broadcasted_iota(jnp.int32, (n, n), 1)
        return jnp.where(r < c, 1.0, 0.0).astype(BF16)

    def select(kk, seen, pre):
        eq = kk == thr
        eqf = jnp.where(eq, 1.0, 0.0)
        rank = seen + _dot(eqf.astype(BF16), pre)
        sel = (kk > thr) | (eq & (rank < need))
        return sel, seen + jnp.sum(eqf, axis=-1, keepdims=True)

    pre_l = before(LANES)
    seen = jnp.zeros((T, 1), F32)
    bias_blocks = []
    for cb in range(P // LANES):
        sel, seen = select(key_p[:, cb * LANES:(cb + 1) * LANES], seen, pre_l)
        bias_blocks.append(jnp.where(sel, 0.0, NEG))
    bias_p = jnp.concatenate(bias_blocks, axis=-1)
    sel, _ = select(key_n, seen, before(T))
    bias_n = jnp.where(sel & vis_n, 0.0, NEG)

    for n in range(B_KV):
        kv = slice(n * B_HD, (n + 1) * B_HD)
        k_p = kp_ref[pl.ds(n, P, stride=B_KV), :].astype(BF16)
        v_p = vp_ref[pl.ds(n, P, stride=B_KV), :].astype(BF16)
        k_n = kbn_ref[0, :, kv]
        v_n = vbn_ref[0, :, kv]
        for g in range(grp):
            h = n * grp + g
            q = qb_ref[0, :, h * B_HD:(h + 1) * B_HD]
            s_p = _dot_nt(q, k_p) + bias_p
            s_n = _dot_nt(q, k_n) + bias_n
            mx = jnp.maximum(jnp.max(s_p, axis=-1, keepdims=True), jnp.max(s_n, axis=-1, keepdims=True))
            p_p = jnp.exp2(s_p - mx)
            p_n = jnp.exp2(s_n - mx)
            l = jnp.sum(p_p, axis=-1, keepdims=True) + jnp.sum(p_n, axis=-1, keepdims=True)
            o = _dot(p_p.astype(BF16), v_p) + _dot(p_n.astype(BF16), v_n)
            o_ref[0, :, h * B_HD:(h + 1) * B_HD] = (o / l).astype(o_ref.dtype)


def _dsa_decode(qi, proj, qb, ki2, kbb, vbb, ki_past_t, kp_rows, vp_rows, layer):
    B, T, _ = qb.shape
    P = ki_past_t.shape[3]
    assert P % CHUNK == 0 and P % LANES == 0 and kp_rows.shape[2] == P * B_KV
    topk = min(TOPK_MAX, (P + T) // 4)

    def new(width, blk=0):
        return pl.BlockSpec((1, T, width), lambda b: (b, 0, blk))

    rows = pl.BlockSpec((None, None, P * B_KV, LANES), lambda b: (layer, b, 0, 0))
    return pl.pallas_call(
        functools.partial(_dsa_decode_kernel, P=P, topk=topk),
        grid=(B,),
        in_specs=[new(I_HEADS * I_HD), new(LANES, OFF_SMALL // LANES), new(B_HEADS * B_HD),
                  new(LANES), new(B_KV * B_HD), new(B_KV * B_HD),
                  pl.BlockSpec((None, None, I_HD, P), lambda b: (layer, b, 0, 0)), rows, rows],
        out_specs=new(B_HEADS * B_HD),
        out_shape=jax.ShapeDtypeStruct((B, T, B_HEADS * B_HD), BF16),
        compiler_params=_cparams(("parallel",)),
        name="dsa_decode",
    )(qi, proj, qb, ki2, kbb, vbb, ki_past_t, kp_rows, vp_rows)


def _bmm(a, b):
    return jnp.einsum('bij,bjk->bik', a.astype(BF16), b.astype(BF16), preferred_element_type=F32)


def _bmm_nt(a, b):
    return jnp.einsum('bik,bjk->bij', a.astype(BF16), b.astype(BF16), preferred_element_type=F32)


def _gdn_intra_kernel(x_ref, halo_ref, sm_ref, cw_ref, cbuf_ref, alog_ref, dtb_ref,
                      u_o, w_o, qt_o, ktT_o, qkg_o, dec_o, buf_s, *, cc, nck):
    i = pl.program_id(1)
    R = nck * cc
    HW = C_HEADS * C_HD

    @pl.when(i == 0)
    def _():
        buf_s[0:SUBLANES, :] = jnp.zeros((SUBLANES, C_QKV), F32)
        buf_s[SUBLANES - (C_CONV - 1):SUBLANES, :] = cbuf_ref[0]

    @pl.when(i > 0)
    def _():
        buf_s[0:SUBLANES, :] = halo_ref[0]

    x = x_ref[0]
    prev = buf_s[...]
    row8 = lax.broadcasted_iota(jnp.int32, (SUBLANES, C_QKV), 0)
    y = x * cw_ref[C_CONV - 1:C_CONV, :]
    s = x
    for k in range(1, C_CONV):
        s = pltpu.roll(s, 1, 0)
        head = jnp.where(row8 < k, pltpu.roll(prev, k, 0), s[0:SUBLANES])
        y = y + jnp.concatenate([head, s[SUBLANES:]], axis=0) * cw_ref[C_CONV - 1 - k:C_CONV - k, :]
    y = y * jax.nn.sigmoid(y)

    sm = sm_ref[0]
    xg = sm + dtb_ref[...]
    softplus = jnp.maximum(xg, 0.0) + jnp.log(1.0 + jnp.exp(-jnp.abs(xg)))
    g = -jnp.exp(alog_ref[...]) * softplus
    beta = jax.nn.sigmoid(sm)

    cshift = cc.bit_length() - 1
    rr = lax.broadcasted_iota(jnp.int32, (R, R), 0)
    rc = lax.broadcasted_iota(jnp.int32, (R, R), 1)
    cum = ((rr >> cshift) == (rc >> cshift)) & (rr >= rc)
    G = jnp.dot(jnp.where(cum, 1.0, 0.0), g, preferred_element_type=F32,
                precision=lax.Precision.HIGHEST)

    qs, ks, vs, gcs, bcs = [], [], [], [], []
    for ck in range(nck):
        rows = slice(ck * cc, (ck + 1) * cc)
        for h in range(C_HEADS):
            qs.append(y[rows, h * C_HD:(h + 1) * C_HD])
            ks.append(y[rows, HW + h * C_HD:HW + (h + 1) * C_HD])
            vs.append(y[rows, 2 * HW + h * C_HD:2 * HW + (h + 1) * C_HD])
            gcs.append(G[rows, CA_LANE + h:CA_LANE + h + 1])
            bcs.append(beta[rows, CB_LANE + h:CB_LANE + h + 1])
    q3 = jnp.stack(qs)
    k3 = jnp.stack(ks)
    v3 = jnp.stack(vs)
    Gc = jnp.stack(gcs)
    bc = jnp.stack(bcs)
    q3 = q3 * lax.rsqrt(jnp.sum(q3 * q3, axis=-1, keepdims=True) + EPS) * (C_HD ** -0.5)
    k3 = k3 * lax.rsqrt(jnp.sum(k3 * k3, axis=-1, keepdims=True) + EPS)

    ri = lax.broadcasted_iota(jnp.int32, (cc, cc), 0)
    ci = lax.broadcasted_iota(jnp.int32, (cc, cc), 1)
    incl = (ri >= ci)[None]
    strict = (ri > ci)[None]
    eye = (ri == ci)[None]
    bdiag = ((ri // SOLVE_BLOCK) == (ci // SOLVE_BLOCK))[None]

    Gr = jnp.sum(jnp.where(eye, Gc, 0.0), axis=1, keepdims=True)
    gam = jnp.where(incl, jnp.exp(jnp.where(incl, Gc - Gr, 0.0)), 0.0)
    k16 = k3.astype(BF16)
    A = jnp.where(strict, bc * _bmm_nt(k16, k16) * gam, 0.0)
    Nd = jnp.where(bdiag, -A, 0.0)
    E = jnp.where(bdiag, 0.0, A)
    N2 = _bmm(Nd, Nd)
    N4 = _bmm(N2, N2)
    N8 = _bmm(N4, N4)
    Q = Nd + N2 + _bmm(Nd, N2)
    Q = Q + N4 + _bmm(Q, N4)
    Q = Q + N8 + _bmm(Q, N8)
    Fm = E + _bmm(Q, E)
    F2 = _bmm(Fm, Fm)
    Rm = F2 - Fm - _bmm(Fm, F2)
    Wm = Rm + Q + _bmm(Rm, Q)
    eG = jnp.exp(Gc)
    rhs = jnp.concatenate([bc * v3, (bc * eG) * k3], axis=-1)
    sol = rhs + _bmm(Wm, rhs)
    qkg = _bmm_nt(q3, k16) * gam
    gl = Gc[:, cc - 1:cc, :]
    qt = q3 * eG
    kt = k3 * jnp.exp(gl - Gc)
    dec = jnp.exp(gl)

    for ck in range(nck):
        rows = slice(ck * cc, (ck + 1) * cc)
        for h in range(C_HEADS):
            b = ck * C_HEADS + h
            cols = slice(h * C_HD, (h + 1) * C_HD)
            u_o[0, rows, cols] = sol[b, :, 0:C_HD]
            w_o[0, rows, cols] = sol[b, :, C_HD:2 * C_HD].astype(BF16)
            qt_o[0, rows, cols] = qt[b].astype(BF16)
            ktT_o[0, ck, h] = kt[b].T.astype(BF16)
            qkg_o[0, ck, h] = qkg[b].astype(BF16)
            dec_o[0, ck, h] = jnp.broadcast_to(dec[b], (1, C_HD))


def _gdn_inter_kernel(u_ref, w_ref, qt_ref, ktT_ref, qkg_ref, dec_ref, z_ref, s0_ref, gn_ref,
                      o_ref, sfin_ref, st_s, *, cc, nck):
    c = pl.program_id(1)

    @pl.when(c == 0)
    def _():
        st_s[...] = s0_ref[0]

    S = st_s[...]
    for ck in range(nck):
        rows = slice(ck * cc, (ck + 1) * cc)

        def heads(ref):
            return jnp.stack([ref[0, rows, h * C_HD:(h + 1) * C_HD] for h in range(C_HEADS)])

        S16 = S.astype(BF16)
        v_new = heads(u_ref) - _bmm(heads(w_ref), S16)
        vn16 = v_new.astype(BF16)
        o = _bmm(heads(qt_ref), S16) + _bmm(qkg_ref[0, ck], vn16)
        S = dec_ref[0, ck] * S + _bmm(ktT_ref[0, ck], vn16)
        on = o * lax.rsqrt(jnp.mean(o * o, axis=-1, keepdims=True) + EPS) * gn_ref[...]
        for h in range(C_HEADS):
            cols = slice(h * C_HD, (h + 1) * C_HD)
            z = z_ref[0, rows, cols]
            o_ref[0, rows, cols] = (on[h] * (z * jax.nn.sigmoid(z))).astype(o_ref.dtype)
    st_s[...] = S

    @pl.when(c == pl.num_programs(1) - 1)
    def _():
        sfin_ref[0] = S


def _gdn(proj, conv_w, cbuf, S0, a_log, dt_bias, out_norm):
    B, T, _ = proj.shape
    cc = min(CHUNK, T)
    assert T % cc == 0 and cc % SOLVE_BLOCK == 0 and cc // SOLVE_BLOCK <= 4 and cc >= SUBLANES
    assert cc & (cc - 1) == 0
    HW = C_HEADS * C_HD
    NC = T // cc
    nck_a = 2 if NC % 2 == 0 else 1
    nck_b = 8 if NC % 8 == 0 else 1

    def at_ca(p):
        return jnp.zeros((1, LANES), F32).at[0, CA_LANE:CA_LANE + C_HEADS].set(p.astype(F32))

    Ra = nck_a * cc
    hb = Ra // SUBLANES
    u, w, qt, ktT, qkg, dec = pl.pallas_call(
        functools.partial(_gdn_intra_kernel, cc=cc, nck=nck_a),
        grid=(B, NC // nck_a),
        in_specs=[pl.BlockSpec((1, Ra, C_QKV), lambda b, i: (b, i, OFF_CQKV // C_QKV)),
                  pl.BlockSpec((1, SUBLANES, C_QKV),
                               lambda b, i: (b, jnp.maximum(i * hb - 1, 0), OFF_CQKV // C_QKV)),
                  pl.BlockSpec((1, Ra, LANES), lambda b, i: (b, i, OFF_SMALL // LANES)),
                  pl.BlockSpec((C_CONV, C_QKV), lambda b, i: (0, 0)),
                  pl.BlockSpec((1, C_CONV - 1, C_QKV), lambda b, i: (b, 0, 0)),
                  pl.BlockSpec((1, LANES), lambda b, i: (0, 0)),
                  pl.BlockSpec((1, LANES), lambda b, i: (0, 0))],
        out_specs=[pl.BlockSpec((1, Ra, HW), lambda b, i: (b, i, 0)),
                   pl.BlockSpec((1, Ra, HW), lambda b, i: (b, i, 0)),
                   pl.BlockSpec((1, Ra, HW), lambda b, i: (b, i, 0)),
                   pl.BlockSpec((1, nck_a, C_HEADS, C_HD, cc), lambda b, i: (b, i, 0, 0, 0)),
                   pl.BlockSpec((1, nck_a, C_HEADS, cc, cc), lambda b, i: (b, i, 0, 0, 0)),
                   pl.BlockSpec((1, nck_a, C_HEADS, 1, C_HD), lambda b, i: (b, i, 0, 0, 0))],
        out_shape=[jax.ShapeDtypeStruct((B, T, HW), F32),
                   jax.ShapeDtypeStruct((B, T, HW), BF16),
                   jax.ShapeDtypeStruct((B, T, HW), BF16),
                   jax.ShapeDtypeStruct((B, NC, C_HEADS, C_HD, cc), BF16),
                   jax.ShapeDtypeStruct((B, NC, C_HEADS, cc, cc), BF16),
                   jax.ShapeDtypeStruct((B, NC, C_HEADS, 1, C_HD), F32)],
        scratch_shapes=[pltpu.VMEM((SUBLANES, C_QKV), F32)],
        compiler_params=_cparams(("parallel", "parallel")),
        name="gdn_intra",
    )(proj, proj, proj, conv_w, cbuf, at_ca(a_log), at_ca(dt_bias))

    Rb = nck_b * cc
    return pl.pallas_call(
        functools.partial(_gdn_inter_kernel, cc=cc, nck=nck_b),
        grid=(B, NC // nck_b),
        in_specs=[pl.BlockSpec((1, Rb, HW), lambda b, c: (b, c, 0)),
                  pl.BlockSpec((1, Rb, HW), lambda b, c: (b, c, 0)),
                  pl.BlockSpec((1, Rb, HW), lambda b, c: (b, c, 0)),
                  pl.BlockSpec((1, nck_b, C_HEADS, C_HD, cc), lambda b, c: (b, c, 0, 0, 0)),
                  pl.BlockSpec((1, nck_b, C_HEADS, cc, cc), lambda b, c: (b, c, 0, 0, 0)),
                  pl.BlockSpec((1, nck_b, C_HEADS, 1, C_HD), lambda b, c: (b, c, 0, 0, 0)),
                  pl.BlockSpec((1, Rb, HW), lambda b, c: (b, c, OFF_CZ // HW)),
                  pl.BlockSpec((1, C_HEADS, C_HD, C_HD), lambda b, c: (b, 0, 0, 0)),
                  pl.BlockSpec((1, C_HD), lambda b, c: (0, 0))],
        out_specs=[pl.BlockSpec((1, Rb, HW), lambda b, c: (b, c, 0)),
                   pl.BlockSpec((1, C_HEADS, C_HD, C_HD), lambda b, c: (b, 0, 0, 0))],
        out_shape=[jax.ShapeDtypeStruct((B, T, HW), BF16),
                   jax.ShapeDtypeStruct((B, C_HEADS, C_HD, C_HD), F32)],
        scratch_shapes=[pltpu.VMEM((C_HEADS, C_HD, C_HD), F32)],
        compiler_params=_cparams(("parallel", "arbitrary")),
        name="gdn_inter",
    )(u, w, qt, ktT, qkg, dec, proj, S0, out_norm.reshape(1, C_HD))


def _ffn_up_kernel(x_ref, g_ref, sh_ref, sc_ref, wg_ref, wu_ref, cwg_ref, cwu_ref, fg_ref, fu_ref,
                   act_o, fng_o, fnu_o, h_ref, *scr, split_rows):
    bb, tt, D = x_ref.shape
    tn = wg_ref.shape[1]
    i = pl.program_id(1)
    j = pl.program_id(2)
    lo = SUBLANES - (FFN_CONV - 1)

    @pl.when(j == 0)
    def _():
        x = x_ref[...]
        y = x * lax.rsqrt(jnp.mean(x * x, axis=-1, keepdims=True) + EPS) * g_ref[...]
        h = y * (1.0 + sc_ref[...]) + sh_ref[...]
        h_ref[...] = h.reshape(bb * tt, D).astype(BF16)

    if split_rows:
        (carry_s,) = scr
        halves = ((wg_ref, cwg_ref, fg_ref, fng_o, 0), (wu_ref, cwu_ref, fu_ref, fnu_o, 1))
        for _, _, f_ref, _, idx in halves:
            @pl.when(i == 0)
            def _(f_ref=f_ref, idx=idx):
                carry_s[idx, j, 0, lo:SUBLANES, :] = f_ref[0]
        prev = [carry_s[idx, j, 0] for idx in range(2)]
        row8 = lax.broadcasted_iota(jnp.int32, (SUBLANES, tn), 0)
        rc = min(tt, FFN_ROW_CHUNK)
        for r in range(tt // rc):
            rows = slice(r * rc, (r + 1) * rc)
            ys = []
            for w_ref, cw_ref, _, _, idx in halves:
                u = _dot(h_ref[rows, :], w_ref[...])
                y = u * cw_ref[FFN_CONV - 1:FFN_CONV, :]
                for k in range(1, FFN_CONV):
                    s = pltpu.roll(u, k, 0)
                    head = jnp.where(row8 < k, pltpu.roll(prev[idx], k, 0), s[0:SUBLANES])
                    s = jnp.concatenate([head, s[SUBLANES:]], axis=0)
                    y = y + s * cw_ref[FFN_CONV - 1 - k:FFN_CONV - k, :]
                prev[idx] = u[rc - SUBLANES:rc]
                ys.append(y)
            act_o[0, rows, :] = (ys[0] * jax.nn.sigmoid(ys[0]) * ys[1]).astype(act_o.dtype)
        for _, _, _, fn_o, idx in halves:
            carry_s[idx, j, 0] = prev[idx]
            fn_o[...] = prev[idx][lo:SUBLANES][None]
    else:
        bg_s, bu_s = scr

        def half(w_ref, cw_ref, f_ref, fn_o, buf):
            u = _dot(h_ref[...], w_ref[...]).reshape(bb, tt, tn)
            buf[:, lo:SUBLANES, :] = f_ref[...]
            buf[:, SUBLANES:SUBLANES + tt, :] = u
            fn_o[...] = u[:, tt - (FFN_CONV - 1):tt, :]
            y = None
            for jw in range(FFN_CONV):
                term = buf[:, lo + jw:lo + jw + tt, :] * cw_ref[jw:jw + 1, :]
                y = term if y is None else y + term
            return y

        yg = half(wg_ref, cwg_ref, fg_ref, fng_o, bg_s)
        yu = half(wu_ref, cwu_ref, fu_ref, fnu_o, bu_s)
        act_o[...] = (yg * jax.nn.sigmoid(yg) * yu).astype(act_o.dtype)


def _ffn_up(x, g, shift, scale, w_up, layer, conv_w, fbuf):
    B, T, D = x.shape
    FF = w_up.shape[2] // 2
    bb, tt = _row_tiles(B, T)
    tn = COL_TILE
    assert FF % tn == 0 and tt >= SUBLANES
    nj = FF // tn
    split_rows = T > tt
    assert not split_rows or bb == 1
    nf = FFN_CONV - 1

    def cols(off, rows, lead=()):
        return pl.BlockSpec((None,) * len(lead) + rows + (tn,),
                            lambda b, i, j: lead + (0,) * len(rows) + (off + j,))

    def fb(off):
        return pl.BlockSpec((bb, nf, tn), lambda b, i, j: (b, 0, off + j))

    act, fng, fnu = pl.pallas_call(
        functools.partial(_ffn_up_kernel, split_rows=split_rows),
        grid=(B // bb, T // tt, nj),
        in_specs=[pl.BlockSpec((bb, tt, D), lambda b, i, j: (b, i, 0)),
                  pl.BlockSpec((1, 1, D), lambda b, i, j: (0, 0, 0)),
                  pl.BlockSpec((bb, 1, D), lambda b, i, j: (b, 0, 0)),
                  pl.BlockSpec((bb, 1, D), lambda b, i, j: (b, 0, 0)),
                  cols(0, (D,), (layer,)), cols(nj, (D,), (layer,)),
                  cols(0, (FFN_CONV,)), cols(nj, (FFN_CONV,)),
                  fb(0), fb(nj)],
        out_specs=[pl.BlockSpec((bb, tt, tn), lambda b, i, j: (b, i, j)),
                   pl.BlockSpec((bb, None, nf, tn), lambda b, i, j: (b, i, 0, j)),
                   pl.BlockSpec((bb, None, nf, tn), lambda b, i, j: (b, i, 0, j))],
        out_shape=[jax.ShapeDtypeStruct((B, T, FF), BF16),
                   jax.ShapeDtypeStruct((B, T // tt, nf, FF), F32),
                   jax.ShapeDtypeStruct((B, T // tt, nf, FF), F32)],
        scratch_shapes=[pltpu.VMEM((bb * tt, D), BF16)] + (
            [pltpu.VMEM((2, nj, 1, SUBLANES, tn), F32)] if split_rows else
            [pltpu.VMEM((bb, tt + SUBLANES, tn), F32), pltpu.VMEM((bb, tt + SUBLANES, tn), F32)]),
        compiler_params=_cparams(("parallel", "arbitrary", "arbitrary")),
        name="ffn_up",
    )(x, g.reshape(1, 1, D), shift, scale, w_up, w_up, conv_w, conv_w, fbuf, fbuf)
    return act, jnp.concatenate([fng[:, -1], fnu[:, -1]], axis=-1)


def _prep_w_in(w_in):
    depth, D = w_in.shape[0], w_in.shape[1]
    w_in = w_in.astype(BF16)
    sizes = (1024, 1024, 1024, 1024, 256, 256, 512, 64, 8, C_QKV, 8, 8, 1024, N_BRANCH * D)
    offs = np.concatenate([[0], np.cumsum(sizes)])
    (aq, ak, av, bq, bk, bv, bqi, bki, bwi, cqkv, ca, cb, cz, gt) = [
        w_in[:, :, int(offs[k]):int(offs[k + 1])] for k in range(len(sizes))]

    def z(n):
        return jnp.zeros((depth, D, n), w_in.dtype)

    cols = [cqkv, cz, aq, ak, av, bq, bk, bv, bqi, bki, z(LANES - I_HD), bwi, ca, cb, z(LANES - 24),
            z(OFF_GT - OFF_SMALL - LANES), gt]
    w = jnp.concatenate(cols, axis=2)
    assert w.shape[2] == N_PROJ
    return w


def _layer(x, mod, past, lw, layer_idx):
    kA_p, vA_p, kB_p, vB_p, kI_p, cbuf, S0, fbuf = past
    B, T, D = x.shape
    P = 0 if kI_p is None else kI_p.shape[3]
    sh1, sc1, g1, sh2, sc2, g2 = [m.reshape(B, 1, D) for m in jnp.split(mod, 6, axis=-1)]

    proj = _mm_norm(x, lw['norm_mix'], sh1, sc1, lw['w_in'], layer_idx)
    rope = _rope_table(P + jnp.arange(T, dtype=jnp.int32))
    (qa, ka, kab, va, vab, qb, kb, kbb, vb, vbb, qi, ki, ki2) = _prep(
        proj, rope, lw['a_q_norm'], lw['a_k_norm'], lw['b_q_norm'], lw['b_k_norm'])

    lam_init = 0.8 - 0.6 * math.exp(-0.3 * layer_idx)
    if P:
        oa = _diff_decode(qa, kab, vab, kA_p, vA_p, layer_idx, lw['a_lambda'], lw['a_subln'], P, lam_init)
        ob = _dsa_decode(qi, proj, qb, ki2, kbb, vbb, kI_p, kB_p, vB_p, layer_idx)
    else:
        oa = _diff_attn(qa, kab, vab, lw['a_lambda'], lw['a_subln'], lam_init)
        ob = _dsa(qi, proj, qb, ki2, kbb, vbb, 0, T, min(T, 512))

    oc, S_fin = _gdn(proj, lw['c_conv'], cbuf, S0, lw['c_a_log'], lw['c_dt_bias'], lw['c_out_norm'])
    assert T >= C_CONV - 1 and T >= FFN_CONV - 1
    cbuf_new = proj[:, T - (C_CONV - 1):, OFF_CQKV:OFF_CQKV + C_QKV]

    mixed = _merge(oa, ob, oc, lw['w_branch'], layer_idx, proj, D)
    x = _mm_res(mixed, lw['w_out'], layer_idx, x, g1)

    act, fbuf_new = _ffn_up(x, lw['norm_ffn'], sh2, sc2, lw['w_up'], layer_idx, lw['ffn_conv'], fbuf)
    x = _mm_res(act, lw['w_down'], layer_idx, x, g2, row_target=512)

    va_out = va.reshape(B, T, A_VD // LANES, A_HEADS, LANES).transpose(0, 1, 3, 2, 4)
    new_state = (ka.reshape(B, T, A_HEADS, 2, A_HD), va_out.reshape(B, T, A_HEADS, A_VD),
                 kb.reshape(B, T, B_KV, B_HD), vb.reshape(B, T, B_KV, B_HD), ki,
                 cbuf_new, S_fin, fbuf_new)
    return x, new_state


def kernel(x_prompt, x_sample, c_prompt, c_sample, cache_diff_k, cache_diff_v, cache_dsa_k, cache_dsa_v, cache_dsa_kidx, state_gdn_conv, state_gdn, state_ffn_conv, w_ada, b_ada, norm_mix, w_in, a_q_norm, a_k_norm, a_lambda, a_subln, b_q_norm, b_k_norm, c_conv, c_a_log, c_dt_bias, c_out_norm, w_branch, w_out, norm_ffn, w_up, ffn_conv, w_down):
    depth = w_in.shape[0]
    bp, bs = x_prompt.shape[0], x_sample.shape[0]
    d_ff2 = w_up.shape[2]
    dt_ = x_prompt.dtype
    prompt_past = (None, None, None, None, None, jnp.zeros((bp, C_CONV - 1, C_QKV), dt_),
                   jnp.zeros((bp, C_HEADS, C_HD, C_HD), dt_), jnp.zeros((bp, FFN_CONV - 1, d_ff2), dt_))
    nrow = -(-(bp + bs) // SUBLANES) * SUBLANES
    c_all = jnp.concatenate([c_prompt, c_sample, jnp.zeros((nrow - bp - bs, c_prompt.shape[1]), dt_)], axis=0)
    past_len = cache_diff_k.shape[2]
    diff_k_rows = cache_diff_k.reshape(depth, bs, past_len * A_HEADS * 2, A_HD)
    diff_v_rows = cache_diff_v.reshape(depth, bs, past_len, A_HEADS, A_VD // LANES, LANES)
    diff_v_rows = diff_v_rows.transpose(0, 1, 2, 4, 3, 5).reshape(depth, bs, past_len * A_HEADS * 2, LANES)
    dsa_k_rows = cache_dsa_k.reshape(depth, bs, past_len * B_KV, B_HD)
    dsa_v_rows = cache_dsa_v.reshape(depth, bs, past_len * B_KV, B_HD)
    dsa_kidx_t = jnp.swapaxes(cache_dsa_kidx, 2, 3)
    xp, xs = x_prompt, x_sample
    prompt_states, sample_states = [], []
    big = dict(w_in=_prep_w_in(w_in), w_branch=w_branch.astype(BF16), w_out=w_out.astype(BF16),
               w_up=w_up.astype(BF16), w_down=w_down.astype(BF16))
    for l in range(depth):
        lw = dict(big, norm_mix=norm_mix[l],
                  a_q_norm=a_q_norm[l], a_k_norm=a_k_norm[l], a_lambda=a_lambda[l], a_subln=a_subln[l],
                  b_q_norm=b_q_norm[l], b_k_norm=b_k_norm[l], c_conv=c_conv[l], c_a_log=c_a_log[l],
                  c_dt_bias=c_dt_bias[l], c_out_norm=c_out_norm[l], norm_ffn=norm_ffn[l],
                  ffn_conv=ffn_conv[l])
        mod = _ada(c_all, w_ada, b_ada, l)
        xp, st_p = _layer(xp, mod[:bp], prompt_past, lw, l)
        sample_past = (diff_k_rows, diff_v_rows, dsa_k_rows, dsa_v_rows, dsa_kidx_t,
                       state_gdn_conv[l], state_gdn[l], state_ffn_conv[l])
        xs, st_s = _layer(xs, mod[bp:bp + bs], sample_past, lw, l)
        prompt_states.append(st_p)
        sample_states.append(st_s)
    p_out = [jnp.stack(s, axis=0) for s in zip(*prompt_states)]
    s_out = [jnp.stack(s, axis=0) for s in zip(*sample_states)]
    return (xp, xs, *p_out, *s_out)
```

```python
import functools
import math

import jax
import jax.numpy as jnp
import numpy as np
from jax import lax
from jax.experimental import pallas as pl
from jax.experimental.pallas import tpu as pltpu

F32 = jnp.float32
BF16 = jnp.bfloat16

CHUNK = 64
CHUNK_SHIFT = 6
ROPE_THETA = 10000.0
EPS = 1e-6
LOG2E = 1.4426950408889634
A_HEADS, A_HD, A_VD = 4, 128, 256
B_HEADS, B_KV, B_HD = 8, 2, 128
I_HEADS, I_HD = 8, 64
I_HD_SHIFT = 6
TOPK_MAX = 256
C_HEADS, C_HD, C_CONV = 8, 128, 4
C_QKV = 3 * C_HEADS * C_HD
N_BRANCH = 3
FFN_CONV = 3

LANES = 128
SUBLANES = 8
VMEM_LIMIT = 56 * 1024 * 1024
ROW_TILE = 1024
COL_TILE = 512
SOLVE_BLOCK = 16
TILE_UNROLL = 4
FFN_ROW_CHUNK = 256
NEG = -1e30
KEY_MIN = -(2 ** 31)

OFF_CQKV = 0
OFF_CZ = 3072
OFF_AQ = 4096
OFF_AK = 5120
OFF_AV = 6144
OFF_BQ = 7168
OFF_BK = 8192
OFF_BV = 8448
OFF_BQI = 8704
OFF_BKI = 9216
OFF_SMALL = 9344
CA_LANE = 8
CB_LANE = 16
OFF_GT = 9728
N_PROJ = 15872


def _cparams(sem):
    return pltpu.CompilerParams(dimension_semantics=sem, vmem_limit_bytes=VMEM_LIMIT)


def _row_tiles(B, T, target=ROW_TILE):
    tt = min(T, target)
    assert T % tt == 0 and tt % SUBLANES == 0
    bb = max(1, min(B, target // tt))
    while B % bb:
        bb -= 1
    return bb, tt


def _dot(a, b):
    return jnp.dot(a, b, preferred_element_type=F32)


def _dot_nt(a, b):
    return lax.dot_general(a, b, (((1,), (1,)), ((), ())), preferred_element_type=F32)


def _tile_loop(n, body, init, unroll):
    nb = n // unroll

    def trip(tb, c):
        for u in range(unroll):
            c = body(tb * unroll + u, c)
        return c

    c = lax.fori_loop(0, nb, trip, init)
    return lax.fori_loop(nb * unroll, n, body, c)


def _ada_kernel(c_ref, w_ref, b_ref, o_ref):
    c = c_ref[...]
    s = (c * jax.nn.sigmoid(c)).astype(BF16)
    o_ref[...] = _dot(s, w_ref[...].astype(BF16)) + b_ref[...]


def _ada(c, w, b, layer):
    M, D = c.shape
    N = w.shape[2]
    tn = 1024
    return pl.pallas_call(
        _ada_kernel,
        grid=(N // tn,),
        in_specs=[pl.BlockSpec((M, D), lambda j: (0, 0)),
                  pl.BlockSpec((None, D, tn), lambda j: (layer, 0, j)),
                  pl.BlockSpec((None, 1, tn), lambda j: (layer, 0, j))],
        out_specs=pl.BlockSpec((M, tn), lambda j: (0, j)),
        out_shape=jax.ShapeDtypeStruct((M, N), F32),
        compiler_params=_cparams(("parallel",)),
        name="ada",
    )(c, w, b.reshape(-1, 1, N))


def _mm_norm_kernel(x_ref, g_ref, sh_ref, sc_ref, w_ref, o_ref, h_ref):
    bb, tt, D = x_ref.shape
    tn = w_ref.shape[1]

    @pl.when(pl.program_id(2) == 0)
    def _():
        x = x_ref[...]
        y = x * lax.rsqrt(jnp.mean(x * x, axis=-1, keepdims=True) + EPS) * g_ref[...]
        h = y * (1.0 + sc_ref[...]) + sh_ref[...]
        h_ref[...] = h.reshape(bb * tt, D).astype(BF16)

    acc = _dot(h_ref[...], w_ref[...])
    o_ref[...] = acc.reshape(bb, tt, tn).astype(o_ref.dtype)


def _mm_norm(x, g, shift, scale, w, layer, out_dtype=F32):
    B, T, D = x.shape
    N = w.shape[2]
    bb, tt = _row_tiles(B, T)
    tn = COL_TILE
    assert N % tn == 0
    return pl.pallas_call(
        _mm_norm_kernel,
        grid=(B // bb, T // tt, N // tn),
        in_specs=[pl.BlockSpec((bb, tt, D), lambda b, i, j: (b, i, 0)),
                  pl.BlockSpec((1, 1, D), lambda b, i, j: (0, 0, 0)),
                  pl.BlockSpec((bb, 1, D), lambda b, i, j: (b, 0, 0)),
                  pl.BlockSpec((bb, 1, D), lambda b, i, j: (b, 0, 0)),
                  pl.BlockSpec((None, D, tn), lambda b, i, j: (layer, 0, j))],
        out_specs=pl.BlockSpec((bb, tt, tn), lambda b, i, j: (b, i, j)),
        out_shape=jax.ShapeDtypeStruct((B, T, N), out_dtype),
        scratch_shapes=[pltpu.VMEM((bb * tt, D), BF16)],
        compiler_params=_cparams(("parallel", "parallel", "arbitrary")),
        name="mm_norm",
    )(x, g.reshape(1, 1, D), shift, scale, w)


def _mm_res_kernel(a_ref, w_ref, res_ref, g_ref, o_ref):
    bb, tt, K = a_ref.shape
    tn = w_ref.shape[1]
    acc = _dot(a_ref[...].reshape(bb * tt, K), w_ref[...])
    o_ref[...] = res_ref[...] + g_ref[...] * acc.reshape(bb, tt, tn)


def _mm_res(a, w, layer, res, gate, row_target=ROW_TILE):
    B, T, K = a.shape
    N = w.shape[2]
    bb, tt = _row_tiles(B, T, row_target)
    tn = COL_TILE
    assert N % tn == 0
    return pl.pallas_call(
        _mm_res_kernel,
        grid=(B // bb, T // tt, N // tn),
        in_specs=[pl.BlockSpec((bb, tt, K), lambda b, i, j: (b, i, 0)),
                  pl.BlockSpec((None, K, tn), lambda b, i, j: (layer, 0, j)),
                  pl.BlockSpec((bb, tt, tn), lambda b, i, j: (b, i, j)),
                  pl.BlockSpec((bb, 1, tn), lambda b, i, j: (b, 0, j))],
        out_specs=pl.BlockSpec((bb, tt, tn), lambda b, i, j: (b, i, j)),
        out_shape=jax.ShapeDtypeStruct((B, T, N), F32),
        compiler_params=_cparams(("parallel", "parallel", "arbitrary")),
        name="mm_res",
    )(a, w, res, gate)


def _merge_kernel(b0_ref, b1_ref, b2_ref, w_ref, g0_ref, g1_ref, g2_ref, o_ref):
    bb, tt, W = b0_ref.shape
    tn = w_ref.shape[2]
    acc = None
    for n, (br, gr) in enumerate(((b0_ref, g0_ref), (b1_ref, g1_ref), (b2_ref, g2_ref))):
        y = _dot(br[...].reshape(bb * tt, W), w_ref[n])
        t = jax.nn.sigmoid(gr[...].reshape(bb * tt, tn)) * y
        acc = t if acc is None else acc + t
    o_ref[...] = acc.reshape(bb, tt, tn).astype(o_ref.dtype)


def _merge(oa, ob, oc, w_branch, layer, proj, D):
    B, T, W = oa.shape
    bb, tt = _row_tiles(B, T)
    tn = COL_TILE
    gblk = OFF_GT // tn
    nper = D // tn
    br_spec = pl.BlockSpec((bb, tt, W), lambda b, i, j: (b, i, 0))

    def gate_spec(n):
        return pl.BlockSpec((bb, tt, tn), lambda b, i, j: (b, i, gblk + n * nper + j))

    return pl.pallas_call(
        _merge_kernel,
        grid=(B // bb, T // tt, D // tn),
        in_specs=[br_spec, br_spec, br_spec,
                  pl.BlockSpec((None, N_BRANCH, W, tn), lambda b, i, j: (layer, 0, 0, j)),
                  gate_spec(0), gate_spec(1), gate_spec(2)],
        out_specs=pl.BlockSpec((bb, tt, tn), lambda b, i, j: (b, i, j)),
        out_shape=jax.ShapeDtypeStruct((B, T, D), BF16),
        compiler_params=_cparams(("parallel", "parallel", "arbitrary")),
        name="merge",
    )(oa, ob, oc, w_branch, proj, proj, proj)


def _prep_kernel(aqk_ref, av_ref, bq_ref, bkv_ref, bqi_ref, bki_ref, rope_ref,
                 gaq_ref, gak_ref, gbq_ref, gbk_ref,
                 qa_o, ka_o, kab_o, va_o, vab_o, qb_o, kb_o, kbb_o, vb_o, vbb_o,
                 qi_o, ki_o, ki2_o):
    c1 = rope_ref[:, 0:LANES]
    s1 = rope_ref[:, LANES:2 * LANES]
    c2 = rope_ref[:, 2 * LANES:3 * LANES]
    s2 = rope_ref[:, 3 * LANES:4 * LANES]
    tt = c1.shape[0]
    lane = lax.broadcasted_iota(jnp.int32, (tt, LANES), 1)
    low_half = (lane & (I_HD - 1)) < (I_HD // 2)

    def norm_rope(x, g):
        y = x * lax.rsqrt(jnp.mean(x * x, axis=-1, keepdims=True) + EPS) * g
        return y * c1 + pltpu.roll(y, A_HD // 2, 1) * s1

    def rope64(x):
        r = jnp.where(low_half, pltpu.roll(x, LANES - I_HD // 2, 1), pltpu.roll(x, I_HD // 2, 1))
        return x * c2 + r * s2

    gaq, gak, gbq, gbk = gaq_ref[...], gak_ref[...], gbq_ref[...], gbk_ref[...]
    for hm in range(2 * A_HEADS):
        sl = slice(hm * LANES, (hm + 1) * LANES)
        q = norm_rope(aqk_ref[0, :, sl], gaq) * (A_HD ** -0.5 * LOG2E)
        qa_o[0, :, sl] = q.astype(BF16)
        k = norm_rope(aqk_ref[0, :, 2 * A_HEADS * LANES + hm * LANES:2 * A_HEADS * LANES + (hm + 1) * LANES], gak)
        ka_o[0, pl.ds(hm, tt, stride=2 * A_HEADS), :] = k
        kab_o[0, :, sl] = k.astype(BF16)
    va = av_ref[0]
    for h in range(A_HEADS):
        for c in range(A_VD // LANES):
            va_o[0, pl.ds(c * A_HEADS + h, tt, stride=2 * A_HEADS), :] = (
                va[:, h * A_VD + c * LANES:h * A_VD + (c + 1) * LANES])
    vab_o[0] = va.astype(BF16)
    for h in range(B_HEADS):
        sl = slice(h * LANES, (h + 1) * LANES)
        q = norm_rope(bq_ref[0, :, sl], gbq) * (B_HD ** -0.5 * LOG2E)
        qb_o[0, :, sl] = q.astype(BF16)
    for n in range(B_KV):
        sl = slice(n * LANES, (n + 1) * LANES)
        k = norm_rope(bkv_ref[0, :, sl], gbk)
        kb_o[0, pl.ds(n, tt, stride=B_KV), :] = k
        kbb_o[0, :, sl] = k.astype(BF16)
    vb = bkv_ref[0, :, B_KV * LANES:2 * B_KV * LANES]
    for n in range(B_KV):
        vb_o[0, pl.ds(n, tt, stride=B_KV), :] = vb[:, n * LANES:(n + 1) * LANES]
    vbb_o[0] = vb.astype(BF16)
    for p in range(I_HEADS * I_HD // LANES):
        sl = slice(p * LANES, (p + 1) * LANES)
        qi_o[0, :, sl] = (rope64(bqi_ref[0, :, sl]) * (I_HD ** -0.5)).astype(BF16)
    ki = rope64(bki_ref[0, :, 0:LANES])
    ki_o[0] = ki[:, 0:I_HD]
    ki2_o[0] = (ki + pltpu.roll(ki, I_HD, 1)).astype(BF16)


def _rope_table(pos):
    def tab(half):
        inv = jnp.power(ROPE_THETA, -jnp.arange(half, dtype=F32) / half)
        ang = pos.astype(F32)[:, None] * inv[None, :]
        return jnp.cos(ang), jnp.sin(ang)

    c, s = tab(A_HD // 2)
    ci, si = tab(I_HD // 2)
    return jnp.concatenate([c, c, -s, s, ci, ci, ci, ci, -si, si, -si, si], axis=1)


def _prep(proj, rope, gaq, gak, gbq, gbk):
    B, T, _ = proj.shape
    tt = min(T, 512)
    assert T % tt == 0

    def pspec(width, off):
        assert off % width == 0
        return pl.BlockSpec((1, tt, width), lambda b, i: (b, i, off // width))

    gspec = pl.BlockSpec((1, LANES), lambda b, i: (0, 0))
    ra, rb = 2 * A_HEADS, B_KV
    outs = [(1, 1024, BF16), (ra, LANES, F32), (1, 1024, BF16), (ra, LANES, F32), (1, 1024, BF16),
            (1, 1024, BF16), (rb, LANES, F32), (1, 256, BF16), (rb, LANES, F32), (1, 256, BF16),
            (1, 512, BF16), (1, I_HD, F32), (1, LANES, BF16)]
    return pl.pallas_call(
        _prep_kernel,
        grid=(B, T // tt),
        in_specs=[pspec(2048, OFF_AQ), pspec(1024, OFF_AV), pspec(1024, OFF_BQ), pspec(512, OFF_BK),
                  pspec(512, OFF_BQI), pspec(256, OFF_BKI),
                  pl.BlockSpec((tt, 4 * LANES), lambda b, i: (i, 0)),
                  gspec, gspec, gspec, gspec],
        out_specs=[pl.BlockSpec((1, tt * r, w), lambda b, i: (b, i, 0)) for r, w, _ in outs],
        out_shape=[jax.ShapeDtypeStruct((B, T * r, w), d) for r, w, d in outs],
        compiler_params=_cparams(("parallel", "parallel")),
        name="prep",
    )(proj, proj, proj, proj, proj, proj, rope,
      gaq.reshape(1, LANES), gak.reshape(1, LANES), gbq.reshape(1, LANES), gbk.reshape(1, LANES))


def _diff_kernel(q_ref, kn_ref, vn_ref, lam_ref, sub_ref, o_ref, acc0_s, acc1_s, sn_s, *, tq, lam_init):
    acc = (acc0_s, acc1_s)
    i = pl.program_id(2)
    q_pos0 = i * tq
    q = [q_ref[0, :, m * A_HD:(m + 1) * A_HD] for m in range(2)]
    msl = [slice(m * A_HD, (m + 1) * A_HD) for m in range(2)]
    wide = tq % LANES == 0

    def fold(x, op):
        f = x[:, 0:LANES]
        for c in range(1, x.shape[1] // LANES):
            f = op(f, x[:, c * LANES:(c + 1) * LANES])
        return f

    neg = tuple(jnp.full((tq, LANES), NEG, F32) for _ in range(2))
    zero = tuple(jnp.zeros((tq, LANES), F32) for _ in range(2))

    def scores(t, mr):
        r0 = pl.multiple_of(t * tq, tq)
        k = kn_ref[0, pl.ds(r0, tq), :]
        out = []
        for m in range(2):
            s = _dot_nt(q[m], k[:, msl[m]])
            sn_s[m, t] = s
            out.append(jnp.maximum(mr[m], fold(s, jnp.maximum)) if wide else mr[m])
        return tuple(out)
    mn = _tile_loop(i, scores, neg, TILE_UNROLL)

    r0 = pl.multiple_of(i * tq, tq)
    qpos = q_pos0 + lax.broadcasted_iota(jnp.int32, (tq, 1), 0)
    kpos = q_pos0 + lax.broadcasted_iota(jnp.int32, (1, tq), 1)
    vis = (kpos >> CHUNK_SHIFT) <= (qpos >> CHUNK_SHIFT)
    kd = kn_ref[0, pl.ds(r0, tq), :]
    mrow = []
    for m in range(2):
        s = jnp.where(vis, _dot_nt(q[m], kd[:, msl[m]]), NEG)
        sn_s[m, i] = s
        mx = jnp.max(s, axis=-1, keepdims=True)
        if wide:
            mx = jnp.maximum(mx, jnp.max(mn[m], axis=-1, keepdims=True))
        if wide:
            mx = pltpu.repeat(jnp.broadcast_to(mx, (tq, LANES)), tq // LANES, axis=1)
        mrow.append(mx)

    for m in range(2):
        acc[m][...] = jnp.zeros(acc[m].shape, F32)

    def weigh(t, lr):
        r0 = pl.multiple_of(t * tq, tq)
        v = vn_ref[0, pl.ds(r0, tq), :]
        out = []
        for m in range(2):
            p = jnp.exp2(sn_s[m, t] - mrow[m])
            if wide:
                out.append(lr[m] + fold(p, jnp.add))
            else:
                out.append(lr[m] + jnp.sum(p, axis=-1, keepdims=True))
            acc[m][...] += _dot(p.astype(BF16), v)
        return tuple(out)
    ln = _tile_loop(i + 1, weigh, zero if wide else tuple(jnp.zeros((tq, 1), F32) for _ in range(2)),
                    TILE_UNROLL)
    lrow = [jnp.sum(ln[m], axis=-1, keepdims=True) for m in range(2)]

    lp = lam_ref[...]
    lam = (jnp.exp(jnp.sum(lp[0:1] * lp[1:2], axis=-1, keepdims=True))
           - jnp.exp(jnp.sum(lp[2:3] * lp[3:4], axis=-1, keepdims=True)) + lam_init)
    o = acc[0][...] / lrow[0] - lam * (acc[1][...] / lrow[1])
    o = o * lax.rsqrt(jnp.mean(o * o, axis=-1, keepdims=True) + EPS) * sub_ref[...]
    o_ref[0] = (o * (1.0 - lam_init)).astype(o_ref.dtype)


def _diff_attn(qa, kn, vn, lam_p, subln, lam_init):
    B, T, _ = qa.shape
    tq = min(T, 512)
    assert T % tq == 0 and (tq % CHUNK == 0 or tq == T)
    W = 2 * A_HD
    return pl.pallas_call(
        functools.partial(_diff_kernel, tq=tq, lam_init=lam_init),
        grid=(B, A_HEADS, T // tq),
        in_specs=[pl.BlockSpec((1, tq, W), lambda b, h, i: (b, i, h)),
                  pl.BlockSpec((1, T, W), lambda b, h, i: (b, 0, h)),
                  pl.BlockSpec((1, T, W), lambda b, h, i: (b, 0, h)),
                  pl.BlockSpec((4, A_HD), lambda b, h, i: (0, 0)),
                  pl.BlockSpec((1, A_VD), lambda b, h, i: (0, 0))],
        out_specs=pl.BlockSpec((1, tq, W), lambda b, h, i: (b, i, h)),
        out_shape=jax.ShapeDtypeStruct((B, T, A_HEADS * A_VD), BF16),
        scratch_shapes=[pltpu.VMEM((tq, A_VD), F32), pltpu.VMEM((tq, A_VD), F32),
                        pltpu.VMEM((2, T // tq, tq, tq), F32)],
        compiler_params=_cparams(("parallel", "parallel", "arbitrary")),
        name="diff_attn",
    )(qa, kn, vn, lam_p, subln.reshape(1, A_VD))


def _diff_decode_kernel(q_ref, kn_ref, vn_ref, kp_ref, vp_ref, lam_ref, sub_ref, o_ref, *, P, lam_init):
    T = q_ref.shape[1]
    per_pos = A_HEADS * 2
    qpos = P + lax.broadcasted_iota(jnp.int32, (T, 1), 0)
    kpos = P + lax.broadcasted_iota(jnp.int32, (1, T), 1)
    vis = (kpos >> CHUNK_SHIFT) <= (qpos >> CHUNK_SHIFT)
    lp = lam_ref[...]
    lam = (jnp.exp(jnp.sum(lp[0:1] * lp[1:2], axis=-1, keepdims=True))
           - jnp.exp(jnp.sum(lp[2:3] * lp[3:4], axis=-1, keepdims=True)) + lam_init)
    for h in range(A_HEADS):
        v_past = jnp.concatenate(
            [vp_ref[pl.ds(c * A_HEADS + h, P, stride=per_pos), :] for c in range(A_VD // LANES)],
            axis=-1).astype(BF16)
        v_new = vn_ref[0, :, h * A_VD:(h + 1) * A_VD]
        outs = []
        for m in range(2):
            hm = h * 2 + m
            q = q_ref[0, :, hm * A_HD:(hm + 1) * A_HD]
            k_past = kp_ref[pl.ds(hm, P, stride=per_pos), :].astype(BF16)
            s_p = _dot_nt(q, k_past)
            s_n = jnp.where(vis, _dot_nt(q, kn_ref[0, :, hm * A_HD:(hm + 1) * A_HD]), NEG)
            mx = jnp.maximum(jnp.max(s_p, axis=-1, keepdims=True), jnp.max(s_n, axis=-1, keepdims=True))
            p_p = jnp.exp2(s_p - mx)
            p_n = jnp.exp2(s_n - mx)
            l = jnp.sum(p_p, axis=-1, keepdims=True) + jnp.sum(p_n, axis=-1, keepdims=True)
            outs.append((_dot(p_p.astype(BF16), v_past) + _dot(p_n.astype(BF16), v_new)) / l)
        o = outs[0] - lam * outs[1]
        o = o * lax.rsqrt(jnp.mean(o * o, axis=-1, keepdims=True) + EPS) * sub_ref[...]
        o_ref[0, :, h * A_VD:(h + 1) * A_VD] = (o * (1.0 - lam_init)).astype(o_ref.dtype)


def _diff_decode(qa, kn, vn, kp_rows, vp_rows, layer, lam_p, subln, P, lam_init):
    B, T, W = qa.shape
    assert P % CHUNK == 0 and kp_rows.shape[2] == P * A_HEADS * 2 and vp_rows.shape[2] == P * A_HEADS * 2
    new_spec = pl.BlockSpec((1, T, W), lambda b: (b, 0, 0))
    cache_spec = pl.BlockSpec((None, None, P * A_HEADS * 2, LANES), lambda b: (layer, b, 0, 0))
    return pl.pallas_call(
        functools.partial(_diff_decode_kernel, P=P, lam_init=lam_init),
        grid=(B,),
        in_specs=[new_spec, new_spec, new_spec, cache_spec, cache_spec,
                  pl.BlockSpec((4, A_HD), lambda b: (0, 0)),
                  pl.BlockSpec((1, A_VD), lambda b: (0, 0))],
        out_specs=new_spec,
        out_shape=jax.ShapeDtypeStruct((B, T, W), BF16),
        compiler_params=_cparams(("parallel",)),
        name="diff_decode",
    )(qa, kn, vn, kp_rows, vp_rows, lam_p, subln.reshape(1, A_VD))


def _dsa_kernel(qi_ref, wi_ref, qb_ref, ki_ref, kb_ref, vb_ref, o_ref,
                key_s, bias_s, s_s, *, P, L, tq, tk, topk):
    i = pl.program_id(1)
    q_pos0 = P + i * tq
    ncol = jnp.minimum(L, ((q_pos0 + tq - 1) // CHUNK + 1) * CHUNK)
    nt = (ncol + tk - 1) // tk
    qpos = q_pos0 + lax.broadcasted_iota(jnp.int32, (tq, 1), 0)
    grp = B_HEADS // B_KV
    lane = lax.broadcasted_iota(jnp.int32, (tq, LANES), 1)
    wi = wi_ref[0] * (I_HEADS ** -0.5)
    w_lanes = [jnp.broadcast_to(wi[:, h:h + 1], (tq, LANES)) for h in range(I_HEADS)]

    def vis_of(t):
        kpos = t * tk + lax.broadcasted_iota(jnp.int32, (1, tk), 1)
        return ((kpos >> CHUNK_SHIFT) <= (qpos >> CHUNK_SHIFT)) & (kpos < L)

    def score_tile(t, carry):
        c0 = pl.multiple_of(t * tk, tk)
        kit = ki_ref[0, pl.ds(c0, tk), :]
        sc = jnp.zeros((tq, tk), F32)
        for pr in range(I_HEADS * I_HD // LANES):
            qpair = qi_ref[0, :, pr * LANES:(pr + 1) * LANES]
            for half in range(LANES // I_HD):
                h = pr * (LANES // I_HD) + half
                qh = jnp.where((lane >> I_HD_SHIFT) == half, qpair, jnp.zeros_like(qpair))
                isc = _dot_nt(qh, kit)
                sc = sc + jnp.maximum(isc, 0.0) * pltpu.repeat(w_lanes[h], tk // LANES, axis=1)
        sc = jnp.where(vis_of(t), sc, -jnp.inf)
        key_s[t] = _float_order_key(sc)
        return carry

    lax.fori_loop(0, nt, score_tile, 0)

    def counts(pred_fns):
        def body(t, accs):
            kk = key_s[t]
            out = []
            for pred_fn, acc in zip(pred_fns, accs):
                hit = jnp.where(pred_fn(kk), 1, 0)
                part = hit[:, 0:LANES]
                for c in range(1, tk // LANES):
                    part = part + hit[:, c * LANES:(c + 1) * LANES]
                out.append(acc + part)
            return tuple(out)
        accs = _tile_loop(nt, body, tuple(jnp.zeros((tq, LANES), jnp.int32) for _ in pred_fns), 2)
        return [jnp.sum(a.astype(F32), axis=-1, keepdims=True) for a in accs]

    def select(_):
        def bit_body(bi, thr):
            cand = thr + jnp.left_shift(jnp.int32(1), 31 - bi)
            (cnt,) = counts([lambda kk: kk >= cand])
            return jnp.where(cnt >= topk, cand, thr)
        return lax.fori_loop(0, 32, bit_body, jnp.full((tq, 1), KEY_MIN, jnp.int32))

    thr = lax.cond(ncol > topk, select, lambda _: jnp.full((tq, 1), KEY_MIN, jnp.int32), 0)
    n_gt, n_eq = counts([lambda kk: kk > thr, lambda kk: kk == thr])
    need = topk - n_gt
    neg_inf_key = jnp.int32(KEY_MIN + 0x7FFFFF)
    tie_break = jnp.max(jnp.where((n_eq > need) & (thr != neg_inf_key), 1.0, 0.0)) > 0.5

    def bias_fast(_):
        def body(t, c):
            sel = (key_s[t] >= thr) & vis_of(t)
            bias_s[t] = jnp.where(sel, 0.0, NEG)
            return c
        lax.fori_loop(0, nt, body, 0)
        return 0

    def bias_ties(_):
        r = lax.broadcasted_iota(jnp.int32, (LANES, LANES), 0)
        c = lax.broadcasted_iota(jnp.int32, (LANES, LANES), 1)
        before = jnp.where(r < c, 1.0, 0.0).astype(BF16)

        def body(t, seen):
            kk = key_s[t]
            vis = vis_of(t)
            for cb in range(tk // LANES):
                sl = slice(cb * LANES, (cb + 1) * LANES)
                eq = kk[:, sl] == thr
                eqf = jnp.where(eq, 1.0, 0.0)
                rank = seen + _dot(eqf.astype(BF16), before)
                sel = ((kk[:, sl] > thr) | (eq & (rank < need.astype(F32)))) & vis[:, sl]
                bias_s[t, :, sl] = jnp.where(sel, 0.0, NEG)
                seen = seen + jnp.sum(eqf, axis=-1, keepdims=True)
            return seen
        lax.fori_loop(0, nt, body, jnp.zeros((tq, 1), F32))
        return 0

    lax.cond(tie_break, bias_ties, bias_fast, 0)

    def fold(x, op):
        f = x[:, 0:LANES]
        for c in range(1, tk // LANES):
            f = op(f, x[:, c * LANES:(c + 1) * LANES])
        return f

    for n in range(B_KV):
        kv = slice(n * B_HD, (n + 1) * B_HD)

        def scores(t, mrun, kv=kv, n=n):
            c0 = pl.multiple_of(t * tk, tk)
            bias = bias_s[t]
            k = kb_ref[0, pl.ds(c0, tk), kv]
            out = []
            for g in range(grp):
                h = n * grp + g
                s = _dot_nt(qb_ref[0, :, h * B_HD:(h + 1) * B_HD], k) + bias
                s_s[g, t] = s
                out.append(jnp.maximum(mrun[g], fold(s, jnp.maximum)))
            return tuple(out)

        mrun = _tile_loop(nt, scores, tuple(jnp.full((tq, LANES), NEG, F32) for _ in range(grp)), 2)
        mrow = [pltpu.repeat(jnp.broadcast_to(jnp.max(m, axis=-1, keepdims=True), (tq, LANES)),
                             tk // LANES, axis=1) for m in mrun]

        ones = jnp.ones((tk, LANES), BF16)

        def weigh(t, acc, kv=kv, mrow=mrow):
            c0 = pl.multiple_of(t * tk, tk)
            v1 = jnp.concatenate([vb_ref[0, pl.ds(c0, tk), kv], ones], axis=-1)
            return tuple(acc[g] + _dot(jnp.exp2(s_s[g, t] - mrow[g]).astype(BF16), v1)
                         for g in range(grp))

        acc = _tile_loop(nt, weigh, tuple(jnp.zeros((tq, B_HD + LANES), F32) for _ in range(grp)), 2)
        for g in range(grp):
            h = n * grp + g
            o_ref[0, :, h * B_HD:(h + 1) * B_HD] = (
                acc[g][:, 0:B_HD] / acc[g][:, B_HD:B_HD + LANES]).astype(o_ref.dtype)


def _dsa(qi, proj, qb, ki2, kb, vb, P, L, tk):
    B, T, _ = qb.shape
    Lp = ki2.shape[1]
    tq = min(T, 256)
    assert T % tq == 0 and Lp % tk == 0 and tk % LANES == 0
    topk = min(TOPK_MAX, L // 4)
    kern = functools.partial(_dsa_kernel, P=P, L=L, tq=tq, tk=tk, topk=topk)
    return pl.pallas_call(
        kern,
        grid=(B, T // tq),
        in_specs=[pl.BlockSpec((1, tq, I_HEADS * I_HD), lambda b, i: (b, i, 0)),
                  pl.BlockSpec((1, tq, LANES), lambda b, i: (b, i, OFF_SMALL // LANES)),
                  pl.BlockSpec((1, tq, B_HEADS * B_HD), lambda b, i: (b, i, 0)),
                  pl.BlockSpec((1, Lp, LANES), lambda b, i: (b, 0, 0)),
                  pl.BlockSpec((1, Lp, B_KV * B_HD), lambda b, i: (b, 0, 0)),
                  pl.BlockSpec((1, Lp, B_KV * B_HD), lambda b, i: (b, 0, 0))],
        out_specs=pl.BlockSpec((1, tq, B_HEADS * B_HD), lambda b, i: (b, i, 0)),
        out_shape=jax.ShapeDtypeStruct((B, T, B_HEADS * B_HD), BF16),
        scratch_shapes=[pltpu.VMEM((Lp // tk, tq, tk), jnp.int32), pltpu.VMEM((Lp // tk, tq, tk), F32),
                        pltpu.VMEM((B_HEADS // B_KV, Lp // tk, tq, tk), F32)],
        compiler_params=_cparams(("parallel", "arbitrary")),
        name="dsa",
    )(qi, proj, qb, ki2, kb, vb)


def _float_order_key(x):
    bits = pltpu.bitcast(x, jnp.int32)
    bits = jnp.where(bits == KEY_MIN, 0, bits)
    return bits ^ ((bits >> 31) & 0x7FFFFFFF)


def _dsa_decode_kernel(qi_ref, wi_ref, qb_ref, kin_ref, kbn_ref, vbn_ref, kip_ref, kp_ref, vp_ref, o_ref,
                       *, P, topk):
    T = qi_ref.shape[1]
    grp = B_HEADS // B_KV
    qpos = P + lax.broadcasted_iota(jnp.int32, (T, 1), 0)
    kpos = P + lax.broadcasted_iota(jnp.int32, (1, T), 1)
    vis_n = (kpos >> CHUNK_SHIFT) <= (qpos >> CHUNK_SHIFT)
    wi = wi_ref[0] * (I_HEADS ** -0.5)
    ki_pT = kip_ref[...].astype(BF16)
    ki_n = kin_ref[0, :, 0:I_HD]

    sc_p = jnp.zeros((T, P), F32)
    sc_n = jnp.zeros((T, T), F32)
    for h in range(I_HEADS):
        qh = qi_ref[0, :, h * I_HD:(h + 1) * I_HD]
        w = wi[:, h:h + 1]
        sc_p = sc_p + jnp.maximum(_dot(qh, ki_pT), 0.0) * w
        sc_n = sc_n + jnp.maximum(_dot_nt(qh, ki_n), 0.0) * w
    key_p = _float_order_key(sc_p)
    key_n = _float_order_key(jnp.where(vis_n, sc_n, -jnp.inf))

    def count(pred):
        return (jnp.sum(jnp.where(pred(key_p), 1.0, 0.0), axis=-1, keepdims=True)
                + jnp.sum(jnp.where(pred(key_n), 1.0, 0.0), axis=-1, keepdims=True))

    def bit_body(bi, thr):
        cand = thr + jnp.left_shift(jnp.int32(1), 31 - bi)
        return jnp.where(count(lambda kk: kk >= cand) >= topk, cand, thr)

    thr = lax.fori_loop(0, 32, bit_body, jnp.full((T, 1), KEY_MIN, jnp.int32))
    need = topk - count(lambda kk: kk > thr)

    def before(n):
        r = lax.broadcasted_iota(jnp.int32, (n, n), 0)
        c = lax.broadcasted_iota(jnp.int32, (n, n), 1)
        return jnp.where(r < c, 1.0, 0.0).astype(BF16)

    def select(kk, seen, pre):
        eq = kk == thr
        eqf = jnp.where(eq, 1.0, 0.0)
        rank = seen + _dot(eqf.astype(BF16), pre)
        sel = (kk > thr) | (eq & (rank < need))
        return sel, seen + jnp.sum(eqf, axis=-1, keepdims=True)

    pre_l = before(LANES)
    seen = jnp.zeros((T, 1), F32)
    bias_blocks = []
    for cb in range(P // LANES):
        sel, seen = select(key_p[:, cb * LANES:(cb + 1) * LANES], seen, pre_l)
        bias_blocks.append(jnp.where(sel, 0.0, NEG))
    bias_p = jnp.concatenate(bias_blocks, axis=-1)
    sel, _ = select(key_n, seen, before(T))
    bias_n = jnp.where(sel & vis_n, 0.0, NEG)

    for n in range(B_KV):
        kv = slice(n * B_HD, (n + 1) * B_HD)
        k_p = kp_ref[pl.ds(n, P, stride=B_KV), :].astype(BF16)
        v_p = vp_ref[pl.ds(n, P, stride=B_KV), :].astype(BF16)
        k_n = kbn_ref[0, :, kv]
        v_n = vbn_ref[0, :, kv]
        for g in range(grp):
            h = n * grp + g
            q = qb_ref[0, :, h * B_HD:(h + 1) * B_HD]
            s_p = _dot_nt(q, k_p) + bias_p
            s_n = _dot_nt(q, k_n) + bias_n
            mx = jnp.maximum(jnp.max(s_p, axis=-1, keepdims=True), jnp.max(s_n, axis=-1, keepdims=True))
            p_p = jnp.exp2(s_p - mx)
            p_n = jnp.exp2(s_n - mx)
            l = jnp.sum(p_p, axis=-1, keepdims=True) + jnp.sum(p_n, axis=-1, keepdims=True)
            o = _dot(p_p.astype(BF16), v_p) + _dot(p_n.astype(BF16), v_n)
            o_ref[0, :, h * B_HD:(h + 1) * B_HD] = (o / l).astype(o_ref.dtype)


def _dsa_decode(qi, proj, qb, ki2, kbb, vbb, ki_past_t, kp_rows, vp_rows, layer):
    B, T, _ = qb.shape
    P = ki_past_t.shape[3]
    assert P % CHUNK == 0 and P % LANES == 0 and kp_rows.shape[2] == P * B_KV
    topk = min(TOPK_MAX, (P + T) // 4)

    def new(width, blk=0):
        return pl.BlockSpec((1, T, width), lambda b: (b, 0, blk))

    rows = pl.BlockSpec((None, None, P * B_KV, LANES), lambda b: (layer, b, 0, 0))
    return pl.pallas_call(
        functools.partial(_dsa_decode_kernel, P=P, topk=topk),
        grid=(B,),
        in_specs=[new(I_HEADS * I_HD), new(LANES, OFF_SMALL // LANES), new(B_HEADS * B_HD),
                  new(LANES), new(B_KV * B_HD), new(B_KV * B_HD),
                  pl.BlockSpec((None, None, I_HD, P), lambda b: (layer, b, 0, 0)), rows, rows],
        out_specs=new(B_HEADS * B_HD),
        out_shape=jax.ShapeDtypeStruct((B, T, B_HEADS * B_HD), BF16),
        compiler_params=_cparams(("parallel",)),
        name="dsa_decode",
    )(qi, proj, qb, ki2, kbb, vbb, ki_past_t, kp_rows, vp_rows)


def _bmm(a, b):
    return jnp.einsum('bij,bjk->bik', a.astype(BF16), b.astype(BF16), preferred_element_type=F32)


def _bmm_nt(a, b):
    return jnp.einsum('bik,bjk->bij', a.astype(BF16), b.astype(BF16), preferred_element_type=F32)


def _gdn_intra_kernel(x_ref, halo_ref, sm_ref, cw_ref, cbuf_ref, alog_ref, dtb_ref,
                      u_o, w_o, qt_o, ktT_o, qkg_o, dec_o, buf_s, *, cc, nck):
    i = pl.program_id(1)
    R = nck * cc
    HW = C_HEADS * C_HD

    @pl.when(i == 0)
    def _():
        buf_s[0:SUBLANES, :] = jnp.zeros((SUBLANES, C_QKV), F32)
        buf_s[SUBLANES - (C_CONV - 1):SUBLANES, :] = cbuf_ref[0]

    @pl.when(i > 0)
    def _():
        buf_s[0:SUBLANES, :] = halo_ref[0]

    x = x_ref[0]
    prev = buf_s[...]
    row8 = lax.broadcasted_iota(jnp.int32, (SUBLANES, C_QKV), 0)
    y = x * cw_ref[C_CONV - 1:C_CONV, :]
    s = x
    for k in range(1, C_CONV):
        s = pltpu.roll(s, 1, 0)
        head = jnp.where(row8 < k, pltpu.roll(prev, k, 0), s[0:SUBLANES])
        y = y + jnp.concatenate([head, s[SUBLANES:]], axis=0) * cw_ref[C_CONV - 1 - k:C_CONV - k, :]
    y = y * jax.nn.sigmoid(y)

    sm = sm_ref[0]
    xg = sm + dtb_ref[...]
    softplus = jnp.maximum(xg, 0.0) + jnp.log(1.0 + jnp.exp(-jnp.abs(xg)))
    g = -jnp.exp(alog_ref[...]) * softplus
    beta = jax.nn.sigmoid(sm)

    cshift = cc.bit_length() - 1
    rr = lax.broadcasted_iota(jnp.int32, (R, R), 0)
    rc = lax.broadcasted_iota(jnp.int32, (R, R), 1)
    cum = ((rr >> cshift) == (rc >> cshift)) & (rr >= rc)
    G = jnp.dot(jnp.where(cum, 1.0, 0.0), g, preferred_element_type=F32,
                precision=lax.Precision.HIGHEST)

    qs, ks, vs, gcs, bcs = [], [], [], [], []
    for ck in range(nck):
        rows = slice(ck * cc, (ck + 1) * cc)
        for h in range(C_HEADS):
            qs.append(y[rows, h * C_HD:(h + 1) * C_HD])
            ks.append(y[rows, HW + h * C_HD:HW + (h + 1) * C_HD])
            vs.append(y[rows, 2 * HW + h * C_HD:2 * HW + (h + 1) * C_HD])
            gcs.append(G[rows, CA_LANE + h:CA_LANE + h + 1])
            bcs.append(beta[rows, CB_LANE + h:CB_LANE + h + 1])
    q3 = jnp.stack(qs)
    k3 = jnp.stack(ks)
    v3 = jnp.stack(vs)
    Gc = jnp.stack(gcs)
    bc = jnp.stack(bcs)
    q3 = q3 * lax.rsqrt(jnp.sum(q3 * q3, axis=-1, keepdims=True) + EPS) * (C_HD ** -0.5)
    k3 = k3 * lax.rsqrt(jnp.sum(k3 * k3, axis=-1, keepdims=True) + EPS)

    ri = lax.broadcasted_iota(jnp.int32, (cc, cc), 0)
    ci = lax.broadcasted_iota(jnp.int32, (cc, cc), 1)
    incl = (ri >= ci)[None]
    strict = (ri > ci)[None]
    eye = (ri == ci)[None]
    bdiag = ((ri // SOLVE_BLOCK) == (ci // SOLVE_BLOCK))[None]

    Gr = jnp.sum(jnp.where(eye, Gc, 0.0), axis=1, keepdims=True)
    gam = jnp.where(incl, jnp.exp(jnp.where(incl, Gc - Gr, 0.0)), 0.0)
    k16 = k3.astype(BF16)
    A = jnp.where(strict, bc * _bmm_nt(k16, k16) * gam, 0.0)
    Nd = jnp.where(bdiag, -A, 0.0)
    E = jnp.where(bdiag, 0.0, A)
    N2 = _bmm(Nd, Nd)
    N4 = _bmm(N2, N2)
    N8 = _bmm(N4, N4)
    Q = Nd + N2 + _bmm(Nd, N2)
    Q = Q + N4 + _bmm(Q, N4)
    Q = Q + N8 + _bmm(Q, N8)
    Fm = E + _bmm(Q, E)
    F2 = _bmm(Fm, Fm)
    Rm = F2 - Fm - _bmm(Fm, F2)
    Wm = Rm + Q + _bmm(Rm, Q)
    eG = jnp.exp(Gc)
    rhs = jnp.concatenate([bc * v3, (bc * eG) * k3], axis=-1)
    sol = rhs + _bmm(Wm, rhs)
    qkg = _bmm_nt(q3, k16) * gam
    gl = Gc[:, cc - 1:cc, :]
    qt = q3 * eG
    kt = k3 * jnp.exp(gl - Gc)
    dec = jnp.exp(gl)

    for ck in range(nck):
        rows = slice(ck * cc, (ck + 1) * cc)
        for h in range(C_HEADS):
            b = ck * C_HEADS + h
            cols = slice(h * C_HD, (h + 1) * C_HD)
            u_o[0, rows, cols] = sol[b, :, 0:C_HD]
            w_o[0, rows, cols] = sol[b, :, C_HD:2 * C_HD].astype(BF16)
            qt_o[0, rows, cols] = qt[b].astype(BF16)
            ktT_o[0, ck, h] = kt[b].T.astype(BF16)
            qkg_o[0, ck, h] = qkg[b].astype(BF16)
            dec_o[0, ck, h] = jnp.broadcast_to(dec[b], (1, C_HD))


def _gdn_inter_kernel(u_ref, w_ref, qt_ref, ktT_ref, qkg_ref, dec_ref, z_ref, s0_ref, gn_ref,
                      o_ref, sfin_ref, st_s, *, cc, nck):
    c = pl.program_id(1)

    @pl.when(c == 0)
    def _():
        st_s[...] = s0_ref[0]

    S = st_s[...]
    for ck in range(nck):
        rows = slice(ck * cc, (ck + 1) * cc)

        def heads(ref):
            return jnp.stack([ref[0, rows, h * C_HD:(h + 1) * C_HD] for h in range(C_HEADS)])

        S16 = S.astype(BF16)
        v_new = heads(u_ref) - _bmm(heads(w_ref), S16)
        vn16 = v_new.astype(BF16)
        o = _bmm(heads(qt_ref), S16) + _bmm(qkg_ref[0, ck], vn16)
        S = dec_ref[0, ck] * S + _bmm(ktT_ref[0, ck], vn16)
        on = o * lax.rsqrt(jnp.mean(o * o, axis=-1, keepdims=True) + EPS) * gn_ref[...]
        for h in range(C_HEADS):
            cols = slice(h * C_HD, (h + 1) * C_HD)
            z = z_ref[0, rows, cols]
            o_ref[0, rows, cols] = (on[h] * (z * jax.nn.sigmoid(z))).astype(o_ref.dtype)
    st_s[...] = S

    @pl.when(c == pl.num_programs(1) - 1)
    def _():
        sfin_ref[0] = S


def _gdn(proj, conv_w, cbuf, S0, a_log, dt_bias, out_norm):
    B, T, _ = proj.shape
    cc = min(CHUNK, T)
    assert T % cc == 0 and cc % SOLVE_BLOCK == 0 and cc // SOLVE_BLOCK <= 4 and cc >= SUBLANES
    assert cc & (cc - 1) == 0
    HW = C_HEADS * C_HD
    NC = T // cc
    nck_a = 2 if NC % 2 == 0 else 1
    nck_b = 8 if NC % 8 == 0 else 1

    def at_ca(p):
        return jnp.zeros((1, LANES), F32).at[0, CA_LANE:CA_LANE + C_HEADS].set(p.astype(F32))

    Ra = nck_a * cc
    hb = Ra // SUBLANES
    u, w, qt, ktT, qkg, dec = pl.pallas_call(
        functools.partial(_gdn_intra_kernel, cc=cc, nck=nck_a),
        grid=(B, NC // nck_a),
        in_specs=[pl.BlockSpec((1, Ra, C_QKV), lambda b, i: (b, i, OFF_CQKV // C_QKV)),
                  pl.BlockSpec((1, SUBLANES, C_QKV),
                               lambda b, i: (b, jnp.maximum(i * hb - 1, 0), OFF_CQKV // C_QKV)),
                  pl.BlockSpec((1, Ra, LANES), lambda b, i: (b, i, OFF_SMALL // LANES)),
                  pl.BlockSpec((C_CONV, C_QKV), lambda b, i: (0, 0)),
                  pl.BlockSpec((1, C_CONV - 1, C_QKV), lambda b, i: (b, 0, 0)),
                  pl.BlockSpec((1, LANES), lambda b, i: (0, 0)),
                  pl.BlockSpec((1, LANES), lambda b, i: (0, 0))],
        out_specs=[pl.BlockSpec((1, Ra, HW), lambda b, i: (b, i, 0)),
                   pl.BlockSpec((1, Ra, HW), lambda b, i: (b, i, 0)),
                   pl.BlockSpec((1, Ra, HW), lambda b, i: (b, i, 0)),
                   pl.BlockSpec((1, nck_a, C_HEADS, C_HD, cc), lambda b, i: (b, i, 0, 0, 0)),
                   pl.BlockSpec((1, nck_a, C_HEADS, cc, cc), lambda b, i: (b, i, 0, 0, 0)),
                   pl.BlockSpec((1, nck_a, C_HEADS, 1, C_HD), lambda b, i: (b, i, 0, 0, 0))],
        out_shape=[jax.ShapeDtypeStruct((B, T, HW), F32),
                   jax.ShapeDtypeStruct((B, T, HW), BF16),
                   jax.ShapeDtypeStruct((B, T, HW), BF16),
                   jax.ShapeDtypeStruct((B, NC, C_HEADS, C_HD, cc), BF16),
                   jax.ShapeDtypeStruct((B, NC, C_HEADS, cc, cc), BF16),
                   jax.ShapeDtypeStruct((B, NC, C_HEADS, 1, C_HD), F32)],
        scratch_shapes=[pltpu.VMEM((SUBLANES, C_QKV), F32)],
        compiler_params=_cparams(("parallel", "parallel")),
        name="gdn_intra",
    )(proj, proj, proj, conv_w, cbuf, at_ca(a_log), at_ca(dt_bias))

    Rb = nck_b * cc
    return pl.pallas_call(
        functools.partial(_gdn_inter_kernel, cc=cc, nck=nck_b),
        grid=(B, NC // nck_b),
        in_specs=[pl.BlockSpec((1, Rb, HW), lambda b, c: (b, c, 0)),
                  pl.BlockSpec((1, Rb, HW), lambda b, c: (b, c, 0)),
                  pl.BlockSpec((1, Rb, HW), lambda b, c: (b, c, 0)),
                  pl.BlockSpec((1, nck_b, C_HEADS, C_HD, cc), lambda b, c: (b, c, 0, 0, 0)),
                  pl.BlockSpec((1, nck_b, C_HEADS, cc, cc), lambda b, c: (b, c, 0, 0, 0)),
                  pl.BlockSpec((1, nck_b, C_HEADS, 1, C_HD), lambda b, c: (b, c, 0, 0, 0)),
                  pl.BlockSpec((1, Rb, HW), lambda b, c: (b, c, OFF_CZ // HW)),
                  pl.BlockSpec((1, C_HEADS, C_HD, C_HD), lambda b, c: (b, 0, 0, 0)),
                  pl.BlockSpec((1, C_HD), lambda b, c: (0, 0))],
        out_specs=[pl.BlockSpec((1, Rb, HW), lambda b, c: (b, c, 0)),
                   pl.BlockSpec((1, C_HEADS, C_HD, C_HD), lambda b, c: (b, 0, 0, 0))],
        out_shape=[jax.ShapeDtypeStruct((B, T, HW), BF16),
                   jax.ShapeDtypeStruct((B, C_HEADS, C_HD, C_HD), F32)],
        scratch_shapes=[pltpu.VMEM((C_HEADS, C_HD, C_HD), F32)],
        compiler_params=_cparams(("parallel", "arbitrary")),
        name="gdn_inter",
    )(u, w, qt, ktT, qkg, dec, proj, S0, out_norm.reshape(1, C_HD))


def _ffn_up_kernel(x_ref, g_ref, sh_ref, sc_ref, wg_ref, wu_ref, cwg_ref, cwu_ref, fg_ref, fu_ref,
                   act_o, fng_o, fnu_o, h_ref, *scr, split_rows):
    bb, tt, D = x_ref.shape
    tn = wg_ref.shape[1]
    i = pl.program_id(1)
    j = pl.program_id(2)
    lo = SUBLANES - (FFN_CONV - 1)

    @pl.when(j == 0)
    def _():
        x = x_ref[...]
        y = x * lax.rsqrt(jnp.mean(x * x, axis=-1, keepdims=True) + EPS) * g_ref[...]
        h = y * (1.0 + sc_ref[...]) + sh_ref[...]
        h_ref[...] = h.reshape(bb * tt, D).astype(BF16)

    if split_rows:
        (carry_s,) = scr
        halves = ((wg_ref, cwg_ref, fg_ref, fng_o, 0), (wu_ref, cwu_ref, fu_ref, fnu_o, 1))
        for _, _, f_ref, _, idx in halves:
            @pl.when(i == 0)
            def _(f_ref=f_ref, idx=idx):
                carry_s[idx, j, 0, lo:SUBLANES, :] = f_ref[0]
        prev = [carry_s[idx, j, 0] for idx in range(2)]
        row8 = lax.broadcasted_iota(jnp.int32, (SUBLANES, tn), 0)
        rc = min(tt, FFN_ROW_CHUNK)
        for r in range(tt // rc):
            rows = slice(r * rc, (r + 1) * rc)
            ys = []
            for w_ref, cw_ref, _, _, idx in halves:
                u = _dot(h_ref[rows, :], w_ref[...])
                y = u * cw_ref[FFN_CONV - 1:FFN_CONV, :]
                for k in range(1, FFN_CONV):
                    s = pltpu.roll(u, k, 0)
                    head = jnp.where(row8 < k, pltpu.roll(prev[idx], k, 0), s[0:SUBLANES])
                    s = jnp.concatenate([head, s[SUBLANES:]], axis=0)
                    y = y + s * cw_ref[FFN_CONV - 1 - k:FFN_CONV - k, :]
                prev[idx] = u[rc - SUBLANES:rc]
                ys.append(y)
            act_o[0, rows, :] = (ys[0] * jax.nn.sigmoid(ys[0]) * ys[1]).astype(act_o.dtype)
        for _, _, _, fn_o, idx in halves:
            carry_s[idx, j, 0] = prev[idx]
            fn_o[...] = prev[idx][lo:SUBLANES][None]
    else:
        bg_s, bu_s = scr

        def half(w_ref, cw_ref, f_ref, fn_o, buf):
            u = _dot(h_ref[...], w_ref[...]).reshape(bb, tt, tn)
            buf[:, lo:SUBLANES, :] = f_ref[...]
            buf[:, SUBLANES:SUBLANES + tt, :] = u
            fn_o[...] = u[:, tt - (FFN_CONV - 1):tt, :]
            y = None
            for jw in range(FFN_CONV):
                term = buf[:, lo + jw:lo + jw + tt, :] * cw_ref[jw:jw + 1, :]
                y = term if y is None else y + term
            return y

        yg = half(wg_ref, cwg_ref, fg_ref, fng_o, bg_s)
        yu = half(wu_ref, cwu_ref, fu_ref, fnu_o, bu_s)
        act_o[...] = (yg * jax.nn.sigmoid(yg) * yu).astype(act_o.dtype)


def _ffn_up(x, g, shift, scale, w_up, layer, conv_w, fbuf):
    B, T, D = x.shape
    FF = w_up.shape[2] // 2
    bb, tt = _row_tiles(B, T)
    tn = COL_TILE
    assert FF % tn == 0 and tt >= SUBLANES
    nj = FF // tn
    split_rows = T > tt
    assert not split_rows or bb == 1
    nf = FFN_CONV - 1

    def cols(off, rows, lead=()):
        return pl.BlockSpec((None,) * len(lead) + rows + (tn,),
                            lambda b, i, j: lead + (0,) * len(rows) + (off + j,))

    def fb(off):
        return pl.BlockSpec((bb, nf, tn), lambda b, i, j: (b, 0, off + j))

    act, fng, fnu = pl.pallas_call(
        functools.partial(_ffn_up_kernel, split_rows=split_rows),
        grid=(B // bb, T // tt, nj),
        in_specs=[pl.BlockSpec((bb, tt, D), lambda b, i, j: (b, i, 0)),
                  pl.BlockSpec((1, 1, D), lambda b, i, j: (0, 0, 0)),
                  pl.BlockSpec((bb, 1, D), lambda b, i, j: (b, 0, 0)),
                  pl.BlockSpec((bb, 1, D), lambda b, i, j: (b, 0, 0)),
                  cols(0, (D,), (layer,)), cols(nj, (D,), (layer,)),
                  cols(0, (FFN_CONV,)), cols(nj, (FFN_CONV,)),
                  fb(0), fb(nj)],
        out_specs=[pl.BlockSpec((bb, tt, tn), lambda b, i, j: (b, i, j)),
                   pl.BlockSpec((bb, None, nf, tn), lambda b, i, j: (b, i, 0, j)),
                   pl.BlockSpec((bb, None, nf, tn), lambda b, i, j: (b, i, 0, j))],
        out_shape=[jax.ShapeDtypeStruct((B, T, FF), BF16),
                   jax.ShapeDtypeStruct((B, T // tt, nf, FF), F32),
                   jax.ShapeDtypeStruct((B, T // tt, nf, FF), F32)],
        scratch_shapes=[pltpu.VMEM((bb * tt, D), BF16)] + (
            [pltpu.VMEM((2, nj, 1, SUBLANES, tn), F32)] if split_rows else
            [pltpu.VMEM((bb, tt + SUBLANES, tn), F32), pltpu.VMEM((bb, tt + SUBLANES, tn), F32)]),
        compiler_params=_cparams(("parallel", "arbitrary", "arbitrary")),
        name="ffn_up",
    )(x, g.reshape(1, 1, D), shift, scale, w_up, w_up, conv_w, conv_w, fbuf, fbuf)
    return act, jnp.concatenate([fng[:, -1], fnu[:, -1]], axis=-1)


def _prep_w_in(w_in):
    depth, D = w_in.shape[0], w_in.shape[1]
    w_in = w_in.astype(BF16)
    sizes = (1024, 1024, 1024, 1024, 256, 256, 512, 64, 8, C_QKV, 8, 8, 1024, N_BRANCH * D)
    offs = np.concatenate([[0], np.cumsum(sizes)])
    (aq, ak, av, bq, bk, bv, bqi, bki, bwi, cqkv, ca, cb, cz, gt) = [
        w_in[:, :, int(offs[k]):int(offs[k + 1])] for k in range(len(sizes))]

    def z(n):
        return jnp.zeros((depth, D, n), w_in.dtype)

    cols = [cqkv, cz, aq, ak, av, bq, bk, bv, bqi, bki, z(LANES - I_HD), bwi, ca, cb, z(LANES - 24),
            z(OFF_GT - OFF_SMALL - LANES), gt]
    w = jnp.concatenate(cols, axis=2)
    assert w.shape[2] == N_PROJ
    return w


def _layer(x, mod, past, lw, layer_idx):
    kA_p, vA_p, kB_p, vB_p, kI_p, cbuf, S0, fbuf = past
    B, T, D = x.shape
    P = 0 if kI_p is None else kI_p.shape[3]
    sh1, sc1, g1, sh2, sc2, g2 = [m.reshape(B, 1, D) for m in jnp.split(mod, 6, axis=-1)]

    proj = _mm_norm(x, lw['norm_mix'], sh1, sc1, lw['w_in'], layer_idx)
    rope = _rope_table(P + jnp.arange(T, dtype=jnp.int32))
    (qa, ka, kab, va, vab, qb, kb, kbb, vb, vbb, qi, ki, ki2) = _prep(
        proj, rope, lw['a_q_norm'], lw['a_k_norm'], lw['b_q_norm'], lw['b_k_norm'])

    lam_init = 0.8 - 0.6 * math.exp(-0.3 * layer_idx)
    if P:
        oa = _diff_decode(qa, kab, vab, kA_p, vA_p, layer_idx, lw['a_lambda'], lw['a_subln'], P, lam_init)
        ob = _dsa_decode(qi, proj, qb, ki2, kbb, vbb, kI_p, kB_p, vB_p, layer_idx)
    else:
        oa = _diff_attn(qa, kab, vab, lw['a_lambda'], lw['a_subln'], lam_init)
        ob = _dsa(qi, proj, qb, ki2, kbb, vbb, 0, T, min(T, 512))

    oc, S_fin = _gdn(proj, lw['c_conv'], cbuf, S0, lw['c_a_log'], lw['c_dt_bias'], lw['c_out_norm'])
    assert T >= C_CONV - 1 and T >= FFN_CONV - 1
    cbuf_new = proj[:, T - (C_CONV - 1):, OFF_CQKV:OFF_CQKV + C_QKV]

    mixed = _merge(oa, ob, oc, lw['w_branch'], layer_idx, proj, D)
    x = _mm_res(mixed, lw['w_out'], layer_idx, x, g1)

    act, fbuf_new = _ffn_up(x, lw['norm_ffn'], sh2, sc2, lw['w_up'], layer_idx, lw['ffn_conv'], fbuf)
    x = _mm_res(act, lw['w_down'], layer_idx, x, g2, row_target=512)

    va_out = va.reshape(B, T, A_VD // LANES, A_HEADS, LANES).transpose(0, 1, 3, 2, 4)
    new_state = (ka.reshape(B, T, A_HEADS, 2, A_HD), va_out.reshape(B, T, A_HEADS, A_VD),
                 kb.reshape(B, T, B_KV, B_HD), vb.reshape(B, T, B_KV, B_HD), ki,
                 cbuf_new, S_fin, fbuf_new)
    return x, new_state


def kernel(x_prompt, x_sample, c_prompt, c_sample, cache_diff_k, cache_diff_v, cache_dsa_k, cache_dsa_v, cache_dsa_kidx, state_gdn_conv, state_gdn, state_ffn_conv, w_ada, b_ada, norm_mix, w_in, a_q_norm, a_k_norm, a_lambda, a_subln, b_q_norm, b_k_norm, c_conv, c_a_log, c_dt_bias, c_out_norm, w_branch, w_out, norm_ffn, w_up, ffn_conv, w_down):
    depth = w_in.shape[0]
    bp, bs = x_prompt.shape[0], x_sample.shape[0]
    d_ff2 = w_up.shape[2]
    dt_ = x_prompt.dtype
    prompt_past = (None, None, None, None, None, jnp.zeros((bp, C_CONV - 1, C_QKV), dt_),
                   jnp.zeros((bp, C_HEADS, C_HD, C_HD), dt_), jnp.zeros((bp, FFN_CONV - 1, d_ff2), dt_))
    nrow = -(-(bp + bs) // SUBLANES) * SUBLANES
    c_all = jnp.concatenate([c_prompt, c_sample, jnp.zeros((nrow - bp - bs, c_prompt.shape[1]), dt_)], axis=0)
    past_len = cache_diff_k.shape[2]
    diff_k_rows = cache_diff_k.reshape(depth, bs, past_len * A_HEADS * 2, A_HD)
    diff_v_rows = cache_diff_v.reshape(depth, bs, past_len, A_HEADS, A_VD // LANES, LANES)
    diff_v_rows = diff_v_rows.transpose(0, 1, 2, 4, 3, 5).reshape(depth, bs, past_len * A_HEADS * 2, LANES)
    dsa_k_rows = cache_dsa_k.reshape(depth, bs, past_len * B_KV, B_HD)
    dsa_v_rows = cache_dsa_v.reshape(depth, bs, past_len * B_KV, B_HD)
    dsa_kidx_t = jnp.swapaxes(cache_dsa_kidx, 2, 3)
    xp, xs = x_prompt, x_sample
    prompt_states, sample_states = [], []
    big = dict(w_in=_prep_w_in(w_in), w_branch=w_branch.astype(BF16), w_out=w_out.astype(BF16),
               w_up=w_up.astype(BF16), w_down=w_down.astype(BF16))
    for l in range(depth):
        lw = dict(big, norm_mix=norm_mix[l],
                  a_q_norm=a_q_norm[l], a_k_norm=a_k_norm[l], a_lambda=a_lambda[l], a_subln=a_subln[l],
                  b_q_norm=b_q_norm[l], b_k_norm=b_k_norm[l], c_conv=c_conv[l], c_a_log=c_a_log[l],
                  c_dt_bias=c_dt_bias[l], c_out_norm=c_out_norm[l], norm_ffn=norm_ffn[l],
                  ffn_conv=ffn_conv[l])
        mod = _ada(c_all, w_ada, b_ada, l)
        xp, st_p = _layer(xp, mod[:bp], prompt_past, lw, l)
        sample_past = (diff_k_rows, diff_v_rows, dsa_k_rows, dsa_v_rows, dsa_kidx_t,
                       state_gdn_conv[l], state_gdn[l], state_ffn_conv[l])
        xs, st_s = _layer(xs, mod[bp:bp + bs], sample_past, lw, l)
        prompt_states.append(st_p)
        sample_states.append(st_s)
    p_out = [jnp.stack(s, axis=0) for s in zip(*prompt_states)]
    s_out = [jnp.stack(s, axis=0) for s in zip(*sample_states)]
    return (xp, xs, *p_out, *s_out)
```

```python
import functools
import math

import jax
import jax.numpy as jnp
import numpy as np
from jax import lax
from jax.experimental import pallas as pl
from jax.experimental.pallas import tpu as pltpu

F32 = jnp.float32
BF16 = jnp.bfloat16

CHUNK = 64
CHUNK_SHIFT = 6
ROPE_THETA = 10000.0
EPS = 1e-6
LOG2E = 1.4426950408889634
A_HEADS, A_HD, A_VD = 4, 128, 256
B_HEADS, B_KV, B_HD = 8, 2, 128
I_HEADS, I_HD = 8, 64
I_HD_SHIFT = 6
TOPK_MAX = 256
C_HEADS, C_HD, C_CONV = 8, 128, 4
C_QKV = 3 * C_HEADS * C_HD
N_BRANCH = 3
FFN_CONV = 3

LANES = 128
SUBLANES = 8
VMEM_LIMIT = 56 * 1024 * 1024
ROW_TILE = 1024
COL_TILE = 512
SOLVE_BLOCK = 16
TILE_UNROLL = 4
FFN_ROW_CHUNK = 256
NEG = -1e30
KEY_MIN = -(2 ** 31)

OFF_CQKV = 0
OFF_CZ = 3072
OFF_AQ = 4096
OFF_AK = 5120
OFF_AV = 6144
OFF_BQ = 7168
OFF_BK = 8192
OFF_BV = 8448
OFF_BQI = 8704
OFF_BKI = 9216
OFF_SMALL = 9344
CA_LANE = 8
CB_LANE = 16
OFF_GT = 9728
N_PROJ = 15872


def _cparams(sem):
    return pltpu.CompilerParams(dimension_semantics=sem, vmem_limit_bytes=VMEM_LIMIT)


def _row_tiles(B, T, target=ROW_TILE):
    tt = min(T, target)
    assert T % tt == 0 and tt % SUBLANES == 0
    bb = max(1, min(B, target // tt))
    while B % bb:
        bb -= 1
    return bb, tt


def _dot(a, b):
    return jnp.dot(a, b, preferred_element_type=F32)


def _dot_nt(a, b):
    return lax.dot_general(a, b, (((1,), (1,)), ((), ())), preferred_element_type=F32)


def _tile_loop(n, body, init, unroll):
    assert unroll & (unroll - 1) == 0
    c, start, u = init, 0, unroll
    while u >= 1:
        def trip(tb, c, u=u, start=start):
            for k in range(u):
                c = body(start + tb * u + k, c)
            return c

        nb = (n - start) // u
        c = lax.fori_loop(0, nb, trip, c)
        start = start + nb * u
        u //= 2
    return c


def _ada_kernel(c_ref, w_ref, b_ref, o_ref):
    c = c_ref[...]
    s = (c * jax.nn.sigmoid(c)).astype(BF16)
    o_ref[...] = _dot(s, w_ref[...].astype(BF16)) + b_ref[...]


def _ada(c, w, b, layer):
    M, D = c.shape
    N = w.shape[2]
    tn = 1024
    return pl.pallas_call(
        _ada_kernel,
        grid=(N // tn,),
        in_specs=[pl.BlockSpec((M, D), lambda j: (0, 0)),
                  pl.BlockSpec((None, D, tn), lambda j: (layer, 0, j)),
                  pl.BlockSpec((None, 1, tn), lambda j: (layer, 0, j))],
        out_specs=pl.BlockSpec((M, tn), lambda j: (0, j)),
        out_shape=jax.ShapeDtypeStruct((M, N), F32),
        compiler_params=_cparams(("parallel",)),
        name="ada",
    )(c, w, b.reshape(-1, 1, N))


def _mm_norm_kernel(x_ref, g_ref, sh_ref, sc_ref, w_ref, o_ref, h_ref):
    bb, tt, D = x_ref.shape
    tn = w_ref.shape[1]

    @pl.when(pl.program_id(2) == 0)
    def _():
        x = x_ref[...]
        y = x * lax.rsqrt(jnp.mean(x * x, axis=-1, keepdims=True) + EPS) * g_ref[...]
        h = y * (1.0 + sc_ref[...]) + sh_ref[...]
        h_ref[...] = h.reshape(bb * tt, D).astype(BF16)

    acc = _dot(h_ref[...], w_ref[...])
    o_ref[...] = acc.reshape(bb, tt, tn).astype(o_ref.dtype)


def _mm_norm(x, g, shift, scale, w, layer, out_dtype=F32):
    B, T, D = x.shape
    N = w.shape[2]
    bb, tt = _row_tiles(B, T)
    tn = COL_TILE
    assert N % tn == 0
    return pl.pallas_call(
        _mm_norm_kernel,
        grid=(B // bb, T // tt, N // tn),
        in_specs=[pl.BlockSpec((bb, tt, D), lambda b, i, j: (b, i, 0)),
                  pl.BlockSpec((1, 1, D), lambda b, i, j: (0, 0, 0)),
                  pl.BlockSpec((bb, 1, D), lambda b, i, j: (b, 0, 0)),
                  pl.BlockSpec((bb, 1, D), lambda b, i, j: (b, 0, 0)),
                  pl.BlockSpec((None, D, tn), lambda b, i, j: (layer, 0, j))],
        out_specs=pl.BlockSpec((bb, tt, tn), lambda b, i, j: (b, i, j)),
        out_shape=jax.ShapeDtypeStruct((B, T, N), out_dtype),
        scratch_shapes=[pltpu.VMEM((bb * tt, D), BF16)],
        compiler_params=_cparams(("parallel", "parallel", "arbitrary")),
        name="mm_norm",
    )(x, g.reshape(1, 1, D), shift, scale, w)


def _mm_res_kernel(a_ref, w_ref, res_ref, g_ref, o_ref):
    bb, tt, K = a_ref.shape
    tn = w_ref.shape[1]
    acc = _dot(a_ref[...].reshape(bb * tt, K), w_ref[...])
    o_ref[...] = res_ref[...] + g_ref[...] * acc.reshape(bb, tt, tn)


def _mm_res(a, w, layer, res, gate, row_target=ROW_TILE):
    B, T, K = a.shape
    N = w.shape[2]
    bb, tt = _row_tiles(B, T, row_target)
    tn = COL_TILE
    assert N % tn == 0
    return pl.pallas_call(
        _mm_res_kernel,
        grid=(B // bb, T // tt, N // tn),
        in_specs=[pl.BlockSpec((bb, tt, K), lambda b, i, j: (b, i, 0)),
                  pl.BlockSpec((None, K, tn), lambda b, i, j: (layer, 0, j)),
                  pl.BlockSpec((bb, tt, tn), lambda b, i, j: (b, i, j)),
                  pl.BlockSpec((bb, 1, tn), lambda b, i, j: (b, 0, j))],
        out_specs=pl.BlockSpec((bb, tt, tn), lambda b, i, j: (b, i, j)),
        out_shape=jax.ShapeDtypeStruct((B, T, N), F32),
        compiler_params=_cparams(("parallel", "parallel", "arbitrary")),
        name="mm_res",
    )(a, w, res, gate)


def _merge_kernel(b0_ref, b1_ref, b2_ref, w_ref, g0_ref, g1_ref, g2_ref, o_ref):
    bb, tt, W = b0_ref.shape
    tn = w_ref.shape[2]
    acc = None
    for n, (br, gr) in enumerate(((b0_ref, g0_ref), (b1_ref, g1_ref), (b2_ref, g2_ref))):
        y = _dot(br[...].reshape(bb * tt, W), w_ref[n])
        t = jax.nn.sigmoid(gr[...].reshape(bb * tt, tn)) * y
        acc = t if acc is None else acc + t
    o_ref[...] = acc.reshape(bb, tt, tn).astype(o_ref.dtype)


def _merge(oa, ob, oc, w_branch, layer, proj, D):
    B, T, W = oa.shape
    bb, tt = _row_tiles(B, T)
    tn = COL_TILE
    gblk = OFF_GT // tn
    nper = D // tn
    br_spec = pl.BlockSpec((bb, tt, W), lambda b, i, j: (b, i, 0))

    def gate_spec(n):
        return pl.BlockSpec((bb, tt, tn), lambda b, i, j: (b, i, gblk + n * nper + j))

    return pl.pallas_call(
        _merge_kernel,
        grid=(B // bb, T // tt, D // tn),
        in_specs=[br_spec, br_spec, br_spec,
                  pl.BlockSpec((None, N_BRANCH, W, tn), lambda b, i, j: (layer, 0, 0, j)),
                  gate_spec(0), gate_spec(1), gate_spec(2)],
        out_specs=pl.BlockSpec((bb, tt, tn), lambda b, i, j: (b, i, j)),
        out_shape=jax.ShapeDtypeStruct((B, T, D), BF16),
        compiler_params=_cparams(("parallel", "parallel", "arbitrary")),
        name="merge",
    )(oa, ob, oc, w_branch, proj, proj, proj)


def _prep_kernel(aqk_ref, av_ref, bq_ref, bkv_ref, bqi_ref, bki_ref, rope_ref,
                 gaq_ref, gak_ref, gbq_ref, gbk_ref,
                 qa_o, ka_o, kab_o, va_o, vab_o, qb_o, kb_o, kbb_o, vb_o, vbb_o,
                 qi_o, ki_o, ki2_o):
    c1 = rope_ref[:, 0:LANES]
    s1 = rope_ref[:, LANES:2 * LANES]
    c2 = rope_ref[:, 2 * LANES:3 * LANES]
    s2 = rope_ref[:, 3 * LANES:4 * LANES]
    tt = c1.shape[0]
    lane = lax.broadcasted_iota(jnp.int32, (tt, LANES), 1)
    low_half = (lane & (I_HD - 1)) < (I_HD // 2)

    def norm_rope(x, g):
        y = x * lax.rsqrt(jnp.mean(x * x, axis=-1, keepdims=True) + EPS) * g
        return y * c1 + pltpu.roll(y, A_HD // 2, 1) * s1

    def rope64(x):
        r = jnp.where(low_half, pltpu.roll(x, LANES - I_HD // 2, 1), pltpu.roll(x, I_HD // 2, 1))
        return x * c2 + r * s2

    gaq, gak, gbq, gbk = gaq_ref[...], gak_ref[...], gbq_ref[...], gbk_ref[...]
    for hm in range(2 * A_HEADS):
        sl = slice(hm * LANES, (hm + 1) * LANES)
        q = norm_rope(aqk_ref[0, :, sl], gaq) * (A_HD ** -0.5 * LOG2E)
        qa_o[0, :, sl] = q.astype(BF16)
        k = norm_rope(aqk_ref[0, :, 2 * A_HEADS * LANES + hm * LANES:2 * A_HEADS * LANES + (hm + 1) * LANES], gak)
        ka_o[0, pl.ds(hm, tt, stride=2 * A_HEADS), :] = k
        kab_o[0, :, sl] = k.astype(BF16)
    va = av_ref[0]
    for h in range(A_HEADS):
        for c in range(A_VD // LANES):
            va_o[0, pl.ds(c * A_HEADS + h, tt, stride=2 * A_HEADS), :] = (
                va[:, h * A_VD + c * LANES:h * A_VD + (c + 1) * LANES])
    vab_o[0] = va.astype(BF16)
    for h in range(B_HEADS):
        sl = slice(h * LANES, (h + 1) * LANES)
        q = norm_rope(bq_ref[0, :, sl], gbq) * (B_HD ** -0.5 * LOG2E)
        qb_o[0, :, sl] = q.astype(BF16)
    for n in range(B_KV):
        sl = slice(n * LANES, (n + 1) * LANES)
        k = norm_rope(bkv_ref[0, :, sl], gbk)
        kb_o[0, pl.ds(n, tt, stride=B_KV), :] = k
        kbb_o[0, :, sl] = k.astype(BF16)
    vb = bkv_ref[0, :, B_KV * LANES:2 * B_KV * LANES]
    for n in range(B_KV):
        vb_o[0, pl.ds(n, tt, stride=B_KV), :] = vb[:, n * LANES:(n + 1) * LANES]
    vbb_o[0] = vb.astype(BF16)
    for p in range(I_HEADS * I_HD // LANES):
        sl = slice(p * LANES, (p + 1) * LANES)
        qi_o[0, :, sl] = (rope64(bqi_ref[0, :, sl]) * (I_HD ** -0.5)).astype(BF16)
    ki = rope64(bki_ref[0, :, 0:LANES])
    ki_o[0] = ki[:, 0:I_HD]
    ki2_o[0] = (ki + pltpu.roll(ki, I_HD, 1)).astype(BF16)


def _rope_table(pos):
    def tab(half):
        inv = jnp.power(ROPE_THETA, -jnp.arange(half, dtype=F32) / half)
        ang = pos.astype(F32)[:, None] * inv[None, :]
        return jnp.cos(ang), jnp.sin(ang)

    c, s = tab(A_HD // 2)
    ci, si = tab(I_HD // 2)
    return jnp.concatenate([c, c, -s, s, ci, ci, ci, ci, -si, si, -si, si], axis=1)


def _prep(proj, rope, gaq, gak, gbq, gbk):
    B, T, _ = proj.shape
    tt = min(T, 512)
    assert T % tt == 0

    def pspec(width, off):
        assert off % width == 0
        return pl.BlockSpec((1, tt, width), lambda b, i: (b, i, off // width))

    gspec = pl.BlockSpec((1, LANES), lambda b, i: (0, 0))
    ra, rb = 2 * A_HEADS, B_KV
    outs = [(1, 1024, BF16), (ra, LANES, F32), (1, 1024, BF16), (ra, LANES, F32), (1, 1024, BF16),
            (1, 1024, BF16), (rb, LANES, F32), (1, 256, BF16), (rb, LANES, F32), (1, 256, BF16),
            (1, 512, BF16), (1, I_HD, F32), (1, LANES, BF16)]
    return pl.pallas_call(
        _prep_kernel,
        grid=(B, T // tt),
        in_specs=[pspec(2048, OFF_AQ), pspec(1024, OFF_AV), pspec(1024, OFF_BQ), pspec(512, OFF_BK),
                  pspec(512, OFF_BQI), pspec(256, OFF_BKI),
                  pl.BlockSpec((tt, 4 * LANES), lambda b, i: (i, 0)),
                  gspec, gspec, gspec, gspec],
        out_specs=[pl.BlockSpec((1, tt * r, w), lambda b, i: (b, i, 0)) for r, w, _ in outs],
        out_shape=[jax.ShapeDtypeStruct((B, T * r, w), d) for r, w, d in outs],
        compiler_params=_cparams(("parallel", "parallel")),
        name="prep",
    )(proj, proj, proj, proj, proj, proj, rope,
      gaq.reshape(1, LANES), gak.reshape(1, LANES), gbq.reshape(1, LANES), gbk.reshape(1, LANES))


def _diff_kernel(q_ref, kn_ref, vn_ref, lam_ref, sub_ref, o_ref, acc0_s, acc1_s, sn_s, *, tq, lam_init):
    acc = (acc0_s, acc1_s)
    i = pl.program_id(2)
    q_pos0 = i * tq
    q = [q_ref[0, :, m * A_HD:(m + 1) * A_HD] for m in range(2)]
    msl = [slice(m * A_HD, (m + 1) * A_HD) for m in range(2)]
    wide = tq % LANES == 0

    def fold(x, op):
        f = x[:, 0:LANES]
        for c in range(1, x.shape[1] // LANES):
            f = op(f, x[:, c * LANES:(c + 1) * LANES])
        return f

    neg = tuple(jnp.full((tq, LANES), NEG, F32) for _ in range(2))
    zero = tuple(jnp.zeros((tq, LANES), F32) for _ in range(2))

    def scores(t, mr):
        r0 = pl.multiple_of(t * tq, tq)
        k = kn_ref[0, pl.ds(r0, tq), :]
        out = []
        for m in range(2):
            s = _dot_nt(q[m], k[:, msl[m]])
            sn_s[m, t] = s
            out.append(jnp.maximum(mr[m], fold(s, jnp.maximum)) if wide else mr[m])
        return tuple(out)
    mn = _tile_loop(i, scores, neg, TILE_UNROLL)

    r0 = pl.multiple_of(i * tq, tq)
    qpos = q_pos0 + lax.broadcasted_iota(jnp.int32, (tq, 1), 0)
    kpos = q_pos0 + lax.broadcasted_iota(jnp.int32, (1, tq), 1)
    vis = (kpos >> CHUNK_SHIFT) <= (qpos >> CHUNK_SHIFT)
    kd = kn_ref[0, pl.ds(r0, tq), :]
    mrow = []
    for m in range(2):
        s = jnp.where(vis, _dot_nt(q[m], kd[:, msl[m]]), NEG)
        sn_s[m, i] = s
        mx = jnp.max(s, axis=-1, keepdims=True)
        if wide:
            mx = jnp.maximum(mx, jnp.max(mn[m], axis=-1, keepdims=True))
        if wide:
            mx = pltpu.repeat(jnp.broadcast_to(mx, (tq, LANES)), tq // LANES, axis=1)
        mrow.append(mx)

    for m in range(2):
        acc[m][...] = jnp.zeros(acc[m].shape, F32)

    def weigh(t, lr):
        r0 = pl.multiple_of(t * tq, tq)
        v = vn_ref[0, pl.ds(r0, tq), :]
        out = []
        for m in range(2):
            p = jnp.exp2(sn_s[m, t] - mrow[m])
            if wide:
                out.append(lr[m] + fold(p, jnp.add))
            else:
                out.append(lr[m] + jnp.sum(p, axis=-1, keepdims=True))
            acc[m][...] += _dot(p.astype(BF16), v)
        return tuple(out)
    ln = _tile_loop(i + 1, weigh, zero if wide else tuple(jnp.zeros((tq, 1), F32) for _ in range(2)),
                    TILE_UNROLL)
    lrow = [jnp.sum(ln[m], axis=-1, keepdims=True) for m in range(2)]

    lp = lam_ref[...]
    lam = (jnp.exp(jnp.sum(lp[0:1] * lp[1:2], axis=-1, keepdims=True))
           - jnp.exp(jnp.sum(lp[2:3] * lp[3:4], axis=-1, keepdims=True)) + lam_init)
    o = acc[0][...] / lrow[0] - lam * (acc[1][...] / lrow[1])
    o = o * lax.rsqrt(jnp.mean(o * o, axis=-1, keepdims=True) + EPS) * sub_ref[...]
    o_ref[0] = (o * (1.0 - lam_init)).astype(o_ref.dtype)


def _diff_attn(qa, kn, vn, lam_p, subln, lam_init):
    B, T, _ = qa.shape
    tq = min(T, 512)
    assert T % tq == 0 and (tq % CHUNK == 0 or tq == T)
    W = 2 * A_HD
    return pl.pallas_call(
        functools.partial(_diff_kernel, tq=tq, lam_init=lam_init),
        grid=(B, A_HEADS, T // tq),
        in_specs=[pl.BlockSpec((1, tq, W), lambda b, h, i: (b, i, h)),
                  pl.BlockSpec((1, T, W), lambda b, h, i: (b, 0, h)),
                  pl.BlockSpec((1, T, W), lambda b, h, i: (b, 0, h)),
                  pl.BlockSpec((4, A_HD), lambda b, h, i: (0, 0)),
                  pl.BlockSpec((1, A_VD), lambda b, h, i: (0, 0))],
        out_specs=pl.BlockSpec((1, tq, W), lambda b, h, i: (b, i, h)),
        out_shape=jax.ShapeDtypeStruct((B, T, A_HEADS * A_VD), BF16),
        scratch_shapes=[pltpu.VMEM((tq, A_VD), F32), pltpu.VMEM((tq, A_VD), F32),
                        pltpu.VMEM((2, T // tq, tq, tq), F32)],
        compiler_params=_cparams(("parallel", "parallel", "arbitrary")),
        name="diff_attn",
    )(qa, kn, vn, lam_p, subln.reshape(1, A_VD))


def _diff_decode_kernel(q_ref, kn_ref, vn_ref, kp_ref, vp_ref, lam_ref, sub_ref, o_ref, *, P, lam_init):
    T = q_ref.shape[1]
    per_pos = A_HEADS * 2
    qpos = P + lax.broadcasted_iota(jnp.int32, (T, 1), 0)
    kpos = P + lax.broadcasted_iota(jnp.int32, (1, T), 1)
    vis = (kpos >> CHUNK_SHIFT) <= (qpos >> CHUNK_SHIFT)
    lp = lam_ref[...]
    lam = (jnp.exp(jnp.sum(lp[0:1] * lp[1:2], axis=-1, keepdims=True))
           - jnp.exp(jnp.sum(lp[2:3] * lp[3:4], axis=-1, keepdims=True)) + lam_init)
    for h in range(A_HEADS):
        v_past = jnp.concatenate(
            [vp_ref[pl.ds(c * A_HEADS + h, P, stride=per_pos), :] for c in range(A_VD // LANES)],
            axis=-1).astype(BF16)
        v_new = vn_ref[0, :, h * A_VD:(h + 1) * A_VD]
        outs = []
        for m in range(2):
            hm = h * 2 + m
            q = q_ref[0, :, hm * A_HD:(hm + 1) * A_HD]
            k_past = kp_ref[pl.ds(hm, P, stride=per_pos), :].astype(BF16)
            s_p = _dot_nt(q, k_past)
            s_n = jnp.where(vis, _dot_nt(q, kn_ref[0, :, hm * A_HD:(hm + 1) * A_HD]), NEG)
            mx = jnp.maximum(jnp.max(s_p, axis=-1, keepdims=True), jnp.max(s_n, axis=-1, keepdims=True))
            p_p = jnp.exp2(s_p - mx)
            p_n = jnp.exp2(s_n - mx)
            l = jnp.sum(p_p, axis=-1, keepdims=True) + jnp.sum(p_n, axis=-1, keepdims=True)
            outs.append((_dot(p_p.astype(BF16), v_past) + _dot(p_n.astype(BF16), v_new)) / l)
        o = outs[0] - lam * outs[1]
        o = o * lax.rsqrt(jnp.mean(o * o, axis=-1, keepdims=True) + EPS) * sub_ref[...]
        o_ref[0, :, h * A_VD:(h + 1) * A_VD] = (o * (1.0 - lam_init)).astype(o_ref.dtype)


def _diff_decode(qa, kn, vn, kp_rows, vp_rows, layer, lam_p, subln, P, lam_init):
    B, T, W = qa.shape
    assert P % CHUNK == 0 and kp_rows.shape[2] == P * A_HEADS * 2 and vp_rows.shape[2] == P * A_HEADS * 2
    new_spec = pl.BlockSpec((1, T, W), lambda b: (b, 0, 0))
    cache_spec = pl.BlockSpec((None, None, P * A_HEADS * 2, LANES), lambda b: (layer, b, 0, 0))
    return pl.pallas_call(
        functools.partial(_diff_decode_kernel, P=P, lam_init=lam_init),
        grid=(B,),
        in_specs=[new_spec, new_spec, new_spec, cache_spec, cache_spec,
                  pl.BlockSpec((4, A_HD), lambda b: (0, 0)),
                  pl.BlockSpec((1, A_VD), lambda b: (0, 0))],
        out_specs=new_spec,
        out_shape=jax.ShapeDtypeStruct((B, T, W), BF16),
        compiler_params=_cparams(("parallel",)),
        name="diff_decode",
    )(qa, kn, vn, kp_rows, vp_rows, lam_p, subln.reshape(1, A_VD))


def _dsa_kernel(qi_ref, wi_ref, qb_ref, ki_ref, kb_ref, vb_ref, o_ref,
                key_s, bias_s, s_s, *, P, L, tq, tk, topk):
    i = pl.program_id(1)
    q_pos0 = P + i * tq
    ncol = jnp.minimum(L, ((q_pos0 + tq - 1) // CHUNK + 1) * CHUNK)
    nt = (ncol + tk - 1) // tk
    qpos = q_pos0 + lax.broadcasted_iota(jnp.int32, (tq, 1), 0)
    grp = B_HEADS // B_KV
    lane = lax.broadcasted_iota(jnp.int32, (tq, LANES), 1)
    wi = wi_ref[0] * (I_HEADS ** -0.5)
    w_lanes = [jnp.broadcast_to(wi[:, h:h + 1], (tq, LANES)) for h in range(I_HEADS)]

    def vis_of(t):
        kpos = t * tk + lax.broadcasted_iota(jnp.int32, (1, tk), 1)
        return ((kpos >> CHUNK_SHIFT) <= (qpos >> CHUNK_SHIFT)) & (kpos < L)

    def score_tile(t, carry):
        c0 = pl.multiple_of(t * tk, tk)
        kit = ki_ref[0, pl.ds(c0, tk), :]
        sc = jnp.zeros((tq, tk), F32)
        for pr in range(I_HEADS * I_HD // LANES):
            qpair = qi_ref[0, :, pr * LANES:(pr + 1) * LANES]
            for half in range(LANES // I_HD):
                h = pr * (LANES // I_HD) + half
                qh = jnp.where((lane >> I_HD_SHIFT) == half, qpair, jnp.zeros_like(qpair))
                isc = _dot_nt(qh, kit)
                sc = sc + jnp.maximum(isc, 0.0) * pltpu.repeat(w_lanes[h], tk // LANES, axis=1)
        sc = jnp.where(vis_of(t), sc, -jnp.inf)
        key_s[t] = _float_order_key(sc)
        return carry

    lax.fori_loop(0, nt, score_tile, 0)

    def counts(pred_fns):
        def body(t, accs):
            kk = key_s[t]
            out = []
            for pred_fn, acc in zip(pred_fns, accs):
                hit = jnp.where(pred_fn(kk), 1, 0)
                part = hit[:, 0:LANES]
                for c in range(1, tk // LANES):
                    part = part + hit[:, c * LANES:(c + 1) * LANES]
                out.append(acc + part)
            return tuple(out)
        accs = _tile_loop(nt, body, tuple(jnp.zeros((tq, LANES), jnp.int32) for _ in pred_fns), 2)
        return [jnp.sum(a.astype(F32), axis=-1, keepdims=True) for a in accs]

    def select(_):
        def bit_body(bi, thr):
            cand = thr + jnp.left_shift(jnp.int32(1), 31 - bi)
            (cnt,) = counts([lambda kk: kk >= cand])
            return jnp.where(cnt >= topk, cand, thr)
        return lax.fori_loop(0, 32, bit_body, jnp.full((tq, 1), KEY_MIN, jnp.int32))

    thr = lax.cond(ncol > topk, select, lambda _: jnp.full((tq, 1), KEY_MIN, jnp.int32), 0)
    n_gt, n_eq = counts([lambda kk: kk > thr, lambda kk: kk == thr])
    need = topk - n_gt
    neg_inf_key = jnp.int32(KEY_MIN + 0x7FFFFF)
    tie_break = jnp.max(jnp.where((n_eq > need) & (thr != neg_inf_key), 1.0, 0.0)) > 0.5

    def bias_fast(_):
        def body(t, c):
            sel = (key_s[t] >= thr) & vis_of(t)
            bias_s[t] = jnp.where(sel, 0.0, NEG)
            return c
        lax.fori_loop(0, nt, body, 0)
        return 0

    def bias_ties(_):
        r = lax.broadcasted_iota(jnp.int32, (LANES, LANES), 0)
        c = lax.broadcasted_iota(jnp.int32, (LANES, LANES), 1)
        before = jnp.where(r < c, 1.0, 0.0).astype(BF16)

        def body(t, seen):
            kk = key_s[t]
            vis = vis_of(t)
            for cb in range(tk // LANES):
                sl = slice(cb * LANES, (cb + 1) * LANES)
                eq = kk[:, sl] == thr
                eqf = jnp.where(eq, 1.0, 0.0)
                rank = seen + _dot(eqf.astype(BF16), before)
                sel = ((kk[:, sl] > thr) | (eq & (rank < need.astype(F32)))) & vis[:, sl]
                bias_s[t, :, sl] = jnp.where(sel, 0.0, NEG)
                seen = seen + jnp.sum(eqf, axis=-1, keepdims=True)
            return seen
        lax.fori_loop(0, nt, body, jnp.zeros((tq, 1), F32))
        return 0

    lax.cond(tie_break, bias_ties, bias_fast, 0)

    def fold(x, op):
        f = x[:, 0:LANES]
        for c in range(1, tk // LANES):
            f = op(f, x[:, c * LANES:(c + 1) * LANES])
        return f

    for n in range(B_KV):
        kv = slice(n * B_HD, (n + 1) * B_HD)

        def scores(t, mrun, kv=kv, n=n):
            c0 = pl.multiple_of(t * tk, tk)
            bias = bias_s[t]
            k = kb_ref[0, pl.ds(c0, tk), kv]
            out = []
            for g in range(grp):
                h = n * grp + g
                s = _dot_nt(qb_ref[0, :, h * B_HD:(h + 1) * B_HD], k) + bias
                s_s[g, t] = s
                out.append(jnp.maximum(mrun[g], fold(s, jnp.maximum)))
            return tuple(out)

        mrun = _tile_loop(nt, scores, tuple(jnp.full((tq, LANES), NEG, F32) for _ in range(grp)), 2)
        mrow = [pltpu.repeat(jnp.broadcast_to(jnp.max(m, axis=-1, keepdims=True), (tq, LANES)),
                             tk // LANES, axis=1) for m in mrun]

        ones = jnp.ones((tk, LANES), BF16)

        def weigh(t, acc, kv=kv, mrow=mrow):
            c0 = pl.multiple_of(t * tk, tk)
            v1 = jnp.concatenate([vb_ref[0, pl.ds(c0, tk), kv], ones], axis=-1)
            return tuple(acc[g] + _dot(jnp.exp2(s_s[g, t] - mrow[g]).astype(BF16), v1)
                         for g in range(grp))

        acc = _tile_loop(nt, weigh, tuple(jnp.zeros((tq, B_HD + LANES), F32) for _ in range(grp)), 2)
        for g in range(grp):
            h = n * grp + g
            o_ref[0, :, h * B_HD:(h + 1) * B_HD] = (
                acc[g][:, 0:B_HD] / acc[g][:, B_HD:B_HD + LANES]).astype(o_ref.dtype)


def _dsa(qi, proj, qb, ki2, kb, vb, P, L, tk):
    B, T, _ = qb.shape
    Lp = ki2.shape[1]
    tq = min(T, 256)
    assert T % tq == 0 and Lp % tk == 0 and tk % LANES == 0
    topk = min(TOPK_MAX, L // 4)
    kern = functools.partial(_dsa_kernel, P=P, L=L, tq=tq, tk=tk, topk=topk)
    return pl.pallas_call(
        kern,
        grid=(B, T // tq),
        in_specs=[pl.BlockSpec((1, tq, I_HEADS * I_HD), lambda b, i: (b, i, 0)),
                  pl.BlockSpec((1, tq, LANES), lambda b, i: (b, i, OFF_SMALL // LANES)),
                  pl.BlockSpec((1, tq, B_HEADS * B_HD), lambda b, i: (b, i, 0)),
                  pl.BlockSpec((1, Lp, LANES), lambda b, i: (b, 0, 0)),
                  pl.BlockSpec((1, Lp, B_KV * B_HD), lambda b, i: (b, 0, 0)),
                  pl.BlockSpec((1, Lp, B_KV * B_HD), lambda b, i: (b, 0, 0))],
        out_specs=pl.BlockSpec((1, tq, B_HEADS * B_HD), lambda b, i: (b, i, 0)),
        out_shape=jax.ShapeDtypeStruct((B, T, B_HEADS * B_HD), BF16),
        scratch_shapes=[pltpu.VMEM((Lp // tk, tq, tk), jnp.int32), pltpu.VMEM((Lp // tk, tq, tk), F32),
                        pltpu.VMEM((B_HEADS // B_KV, Lp // tk, tq, tk), F32)],
        compiler_params=_cparams(("parallel", "arbitrary")),
        name="dsa",
    )(qi, proj, qb, ki2, kb, vb)


def _float_order_key(x):
    bits = pltpu.bitcast(x, jnp.int32)
    bits = jnp.where(bits == KEY_MIN, 0, bits)
    return bits ^ ((bits >> 31) & 0x7FFFFFFF)


def _dsa_decode_kernel(qi_ref, wi_ref, qb_ref, kin_ref, kbn_ref, vbn_ref, kip_ref, kp_ref, vp_ref, o_ref,
                       *, P, topk):
    T = qi_ref.shape[1]
    grp = B_HEADS // B_KV
    qpos = P + lax.broadcasted_iota(jnp.int32, (T, 1), 0)
    kpos = P + lax.broadcasted_iota(jnp.int32, (1, T), 1)
    vis_n = (kpos >> CHUNK_SHIFT) <= (qpos >> CHUNK_SHIFT)
    wi = wi_ref[0] * (I_HEADS ** -0.5)
    ki_pT = kip_ref[...].astype(BF16)
    ki_n = kin_ref[0, :, 0:I_HD]

    sc_p = jnp.zeros((T, P), F32)
    sc_n = jnp.zeros((T, T), F32)
    for h in range(I_HEADS):
        qh = qi_ref[0, :, h * I_HD:(h + 1) * I_HD]
        w = wi[:, h:h + 1]
        sc_p = sc_p + jnp.maximum(_dot(qh, ki_pT), 0.0) * w
        sc_n = sc_n + jnp.maximum(_dot_nt(qh, ki_n), 0.0) * w
    key_p = _float_order_key(sc_p)
    key_n = _float_order_key(jnp.where(vis_n, sc_n, -jnp.inf))

    def count(pred):
        return (jnp.sum(jnp.where(pred(key_p), 1.0, 0.0), axis=-1, keepdims=True)
                + jnp.sum(jnp.where(pred(key_n), 1.0, 0.0), axis=-1, keepdims=True))

    def bit_body(bi, thr):
        cand = thr + jnp.left_shift(jnp.int32(1), 31 - bi)
        return jnp.where(count(lambda kk: kk >= cand) >= topk, cand, thr)

    thr = lax.fori_loop(0, 32, bit_body, jnp.full((T, 1), KEY_MIN, jnp.int32))
    need = topk - count(lambda kk: kk > thr)

    def before(n):
        r = lax.broadcasted_iota(jnp.int32, (n, n), 0)
        c = lax.broadcasted_iota(jnp.int32, (n, n), 1)
        return jnp.where(r < c, 1.0, 0.0).astype(BF16)

    def select(kk, seen, pre):
        eq = kk == thr
        eqf = jnp.where(eq, 1.0, 0.0)
        rank = seen + _dot(eqf.astype(BF16), pre)
        sel = (kk > thr) | (eq & (rank < need))
        return sel, seen + jnp.sum(eqf, axis=-1, keepdims=True)

    pre_l = before(LANES)
    seen = jnp.zeros((T, 1), F32)
    bias_blocks = []
    for cb in range(P // LANES):
        sel, seen = select(key_p[:, cb * LANES:(cb + 1) * LANES], seen, pre_l)
        bias_blocks.append(jnp.where(sel, 0.0, NEG))
    bias_p = jnp.concatenate(bias_blocks, axis=-1)
    sel, _ = select(key_n, seen, before(T))
    bias_n = jnp.where(sel & vis_n, 0.0, NEG)

    for n in range(B_KV):
        kv = slice(n * B_HD, (n + 1) * B_HD)
        k_p = kp_ref[pl.ds(n, P, stride=B_KV), :].astype(BF16)
        v_p = vp_ref[pl.ds(n, P, stride=B_KV), :].astype(BF16)
        k_n = kbn_ref[0, :, kv]
        v_n = vbn_ref[0, :, kv]
        for g in range(grp):
            h = n * grp + g
            q = qb_ref[0, :, h * B_HD:(h + 1) * B_HD]
            s_p = _dot_nt(q, k_p) + bias_p
            s_n = _dot_nt(q, k_n) + bias_n
            mx = jnp.maximum(jnp.max(s_p, axis=-1, keepdims=True), jnp.max(s_n, axis=-1, keepdims=True))
            p_p = jnp.exp2(s_p - mx)
            p_n = jnp.exp2(s_n - mx)
            l = jnp.sum(p_p, axis=-1, keepdims=True) + jnp.sum(p_n, axis=-1, keepdims=True)
            o = _dot(p_p.astype(BF16), v_p) + _dot(p_n.astype(BF16), v_n)
            o_ref[0, :, h * B_HD:(h + 1) * B_HD] = (o / l).astype(o_ref.dtype)


def _dsa_decode(qi, proj, qb, ki2, kbb, vbb, ki_past_t, kp_rows, vp_rows, layer):
    B, T, _ = qb.shape
    P = ki_past_t.shape[3]
    assert P % CHUNK == 0 and P % LANES == 0 and kp_rows.shape[2] == P * B_KV
    topk = min(TOPK_MAX, (P + T) // 4)

    def new(width, blk=0):
        return pl.BlockSpec((1, T, width), lambda b: (b, 0, blk))

    rows = pl.BlockSpec((None, None, P * B_KV, LANES), lambda b: (layer, b, 0, 0))
    return pl.pallas_call(
        functools.partial(_dsa_decode_kernel, P=P, topk=topk),
        grid=(B,),
        in_specs=[new(I_HEADS * I_HD), new(LANES, OFF_SMALL // LANES), new(B_HEADS * B_HD),
                  new(LANES), new(B_KV * B_HD), new(B_KV * B_HD),
                  pl.BlockSpec((None, None, I_HD, P), lambda b: (layer, b, 0, 0)), rows, rows],
        out_specs=new(B_HEADS * B_HD),
        out_shape=jax.ShapeDtypeStruct((B, T, B_HEADS * B_HD), BF16),
        compiler_params=_cparams(("parallel",)),
        name="dsa_decode",
    )(qi, proj, qb, ki2, kbb, vbb, ki_past_t, kp_rows, vp_rows)


def _bmm(a, b):
    return jnp.einsum('bij,bjk->bik', a.astype(BF16), b.astype(BF16), preferred_element_type=F32)


def _bmm_nt(a, b):
    return jnp.einsum('bik,bjk->bij', a.astype(BF16), b.astype(BF16), preferred_element_type=F32)


def _gdn_intra_kernel(x_ref, halo_ref, sm_ref, cw_ref, cbuf_ref, alog_ref, dtb_ref,
                      u_o, w_o, qt_o, ktT_o, qkg_o, dec_o, buf_s, *, cc, nck):
    i = pl.program_id(1)
    R = nck * cc
    HW = C_HEADS * C_HD

    @pl.when(i == 0)
    def _():
        buf_s[0:SUBLANES, :] = jnp.zeros((SUBLANES, C_QKV), F32)
        buf_s[SUBLANES - (C_CONV - 1):SUBLANES, :] = cbuf_ref[0]

    @pl.when(i > 0)
    def _():
        buf_s[0:SUBLANES, :] = halo_ref[0]

    x = x_ref[0]
    prev = buf_s[...]
    row8 = lax.broadcasted_iota(jnp.int32, (SUBLANES, C_QKV), 0)
    y = x * cw_ref[C_CONV - 1:C_CONV, :]
    s = x
    for k in range(1, C_CONV):
        s = pltpu.roll(s, 1, 0)
        head = jnp.where(row8 < k, pltpu.roll(prev, k, 0), s[0:SUBLANES])
        y = y + jnp.concatenate([head, s[SUBLANES:]], axis=0) * cw_ref[C_CONV - 1 - k:C_CONV - k, :]
    y = y * jax.nn.sigmoid(y)

    sm = sm_ref[0]
    xg = sm + dtb_ref[...]
    softplus = jnp.maximum(xg, 0.0) + jnp.log(1.0 + jnp.exp(-jnp.abs(xg)))
    g = -jnp.exp(alog_ref[...]) * softplus
    beta = jax.nn.sigmoid(sm)

    cshift = cc.bit_length() - 1
    rr = lax.broadcasted_iota(jnp.int32, (R, R), 0)
    rc = lax.broadcasted_iota(jnp.int32, (R, R), 1)
    cum = ((rr >> cshift) == (rc >> cshift)) & (rr >= rc)
    G = jnp.dot(jnp.where(cum, 1.0, 0.0), g, preferred_element_type=F32,
                precision=lax.Precision.HIGHEST)

    qs, ks, vs, gcs, bcs = [], [], [], [], []
    for ck in range(nck):
        rows = slice(ck * cc, (ck + 1) * cc)
        for h in range(C_HEADS):
            qs.append(y[rows, h * C_HD:(h + 1) * C_HD])
            ks.append(y[rows, HW + h * C_HD:HW + (h + 1) * C_HD])
            vs.append(y[rows, 2 * HW + h * C_HD:2 * HW + (h + 1) * C_HD])
            gcs.append(G[rows, CA_LANE + h:CA_LANE + h + 1])
            bcs.append(beta[rows, CB_LANE + h:CB_LANE + h + 1])
    q3 = jnp.stack(qs)
    k3 = jnp.stack(ks)
    v3 = jnp.stack(vs)
    Gc = jnp.stack(gcs)
    bc = jnp.stack(bcs)
    q3 = q3 * lax.rsqrt(jnp.sum(q3 * q3, axis=-1, keepdims=True) + EPS) * (C_HD ** -0.5)
    k3 = k3 * lax.rsqrt(jnp.sum(k3 * k3, axis=-1, keepdims=True) + EPS)

    ri = lax.broadcasted_iota(jnp.int32, (cc, cc), 0)
    ci = lax.broadcasted_iota(jnp.int32, (cc, cc), 1)
    incl = (ri >= ci)[None]
    strict = (ri > ci)[None]
    eye = (ri == ci)[None]
    bdiag = ((ri // SOLVE_BLOCK) == (ci // SOLVE_BLOCK))[None]

    Gr = jnp.sum(jnp.where(eye, Gc, 0.0), axis=1, keepdims=True)
    gam = jnp.where(incl, jnp.exp(jnp.where(incl, Gc - Gr, 0.0)), 0.0)
    k16 = k3.astype(BF16)
    A = jnp.where(strict, bc * _bmm_nt(k16, k16) * gam, 0.0)
    Nd = jnp.where(bdiag, -A, 0.0)
    E = jnp.where(bdiag, 0.0, A)
    N2 = _bmm(Nd, Nd)
    N4 = _bmm(N2, N2)
    N8 = _bmm(N4, N4)
    Q = Nd + N2 + _bmm(Nd, N2)
    Q = Q + N4 + _bmm(Q, N4)
    Q = Q + N8 + _bmm(Q, N8)
    Fm = E + _bmm(Q, E)
    F2 = _bmm(Fm, Fm)
    Rm = F2 - Fm - _bmm(Fm, F2)
    Wm = Rm + Q + _bmm(Rm, Q)
    eG = jnp.exp(Gc)
    rhs = jnp.concatenate([bc * v3, (bc * eG) * k3], axis=-1)
    sol = rhs + _bmm(Wm, rhs)
    qkg = _bmm_nt(q3, k16) * gam
    gl = Gc[:, cc - 1:cc, :]
    qt = q3 * eG
    kt = k3 * jnp.exp(gl - Gc)
    dec = jnp.exp(gl)

    for ck in range(nck):
        rows = slice(ck * cc, (ck + 1) * cc)
        for h in range(C_HEADS):
            b = ck * C_HEADS + h
            cols = slice(h * C_HD, (h + 1) * C_HD)
            u_o[0, rows, cols] = sol[b, :, 0:C_HD]
            w_o[0, rows, cols] = sol[b, :, C_HD:2 * C_HD].astype(BF16)
            qt_o[0, rows, cols] = qt[b].astype(BF16)
            ktT_o[0, ck, h] = kt[b].T.astype(BF16)
            qkg_o[0, ck, h] = qkg[b].astype(BF16)
            dec_o[0, ck, h] = jnp.broadcast_to(dec[b], (1, C_HD))


def _gdn_inter_kernel(u_ref, w_ref, qt_ref, ktT_ref, qkg_ref, dec_ref, z_ref, s0_ref, gn_ref,
                      o_ref, sfin_ref, st_s, *, cc, nck):
    c = pl.program_id(1)

    @pl.when(c == 0)
    def _():
        st_s[...] = s0_ref[0]

    S = st_s[...]
    for ck in range(nck):
        rows = slice(ck * cc, (ck + 1) * cc)

        def heads(ref):
            return jnp.stack([ref[0, rows, h * C_HD:(h + 1) * C_HD] for h in range(C_HEADS)])

        S16 = S.astype(BF16)
        v_new = heads(u_ref) - _bmm(heads(w_ref), S16)
        vn16 = v_new.astype(BF16)
        o = _bmm(heads(qt_ref), S16) + _bmm(qkg_ref[0, ck], vn16)
        S = dec_ref[0, ck] * S + _bmm(ktT_ref[0, ck], vn16)
        on = o * lax.rsqrt(jnp.mean(o * o, axis=-1, keepdims=True) + EPS) * gn_ref[...]
        for h in range(C_HEADS):
            cols = slice(h * C_HD, (h + 1) * C_HD)
            z = z_ref[0, rows, cols]
            o_ref[0, rows, cols] = (on[h] * (z * jax.nn.sigmoid(z))).astype(o_ref.dtype)
    st_s[...] = S

    @pl.when(c == pl.num_programs(1) - 1)
    def _():
        sfin_ref[0] = S


def _gdn(proj, conv_w, cbuf, S0, a_log, dt_bias, out_norm):
    B, T, _ = proj.shape
    cc = min(CHUNK, T)
    assert T % cc == 0 and cc % SOLVE_BLOCK == 0 and cc // SOLVE_BLOCK <= 4 and cc >= SUBLANES
    assert cc & (cc - 1) == 0
    HW = C_HEADS * C_HD
    NC = T // cc
    nck_a = 2 if NC % 2 == 0 else 1
    nck_b = 8 if NC % 8 == 0 else 1

    def at_ca(p):
        return jnp.zeros((1, LANES), F32).at[0, CA_LANE:CA_LANE + C_HEADS].set(p.astype(F32))

    Ra = nck_a * cc
    hb = Ra // SUBLANES
    u, w, qt, ktT, qkg, dec = pl.pallas_call(
        functools.partial(_gdn_intra_kernel, cc=cc, nck=nck_a),
        grid=(B, NC // nck_a),
        in_specs=[pl.BlockSpec((1, Ra, C_QKV), lambda b, i: (b, i, OFF_CQKV // C_QKV)),
                  pl.BlockSpec((1, SUBLANES, C_QKV),
                               lambda b, i: (b, jnp.maximum(i * hb - 1, 0), OFF_CQKV // C_QKV)),
                  pl.BlockSpec((1, Ra, LANES), lambda b, i: (b, i, OFF_SMALL // LANES)),
                  pl.BlockSpec((C_CONV, C_QKV), lambda b, i: (0, 0)),
                  pl.BlockSpec((1, C_CONV - 1, C_QKV), lambda b, i: (b, 0, 0)),
                  pl.BlockSpec((1, LANES), lambda b, i: (0, 0)),
                  pl.BlockSpec((1, LANES), lambda b, i: (0, 0))],
        out_specs=[pl.BlockSpec((1, Ra, HW), lambda b, i: (b, i, 0)),
                   pl.BlockSpec((1, Ra, HW), lambda b, i: (b, i, 0)),
                   pl.BlockSpec((1, Ra, HW), lambda b, i: (b, i, 0)),
                   pl.BlockSpec((1, nck_a, C_HEADS, C_HD, cc), lambda b, i: (b, i, 0, 0, 0)),
                   pl.BlockSpec((1, nck_a, C_HEADS, cc, cc), lambda b, i: (b, i, 0, 0, 0)),
                   pl.BlockSpec((1, nck_a, C_HEADS, 1, C_HD), lambda b, i: (b, i, 0, 0, 0))],
        out_shape=[jax.ShapeDtypeStruct((B, T, HW), F32),
                   jax.ShapeDtypeStruct((B, T, HW), BF16),
                   jax.ShapeDtypeStruct((B, T, HW), BF16),
                   jax.ShapeDtypeStruct((B, NC, C_HEADS, C_HD, cc), BF16),
                   jax.ShapeDtypeStruct((B, NC, C_HEADS, cc, cc), BF16),
                   jax.ShapeDtypeStruct((B, NC, C_HEADS, 1, C_HD), F32)],
        scratch_shapes=[pltpu.VMEM((SUBLANES, C_QKV), F32)],
        compiler_params=_cparams(("parallel", "parallel")),
        name="gdn_intra",
    )(proj, proj, proj, conv_w, cbuf, at_ca(a_log), at_ca(dt_bias))

    Rb = nck_b * cc
    return pl.pallas_call(
        functools.partial(_gdn_inter_kernel, cc=cc, nck=nck_b),
        grid=(B, NC // nck_b),
        in_specs=[pl.BlockSpec((1, Rb, HW), lambda b, c: (b, c, 0)),
                  pl.BlockSpec((1, Rb, HW), lambda b, c: (b, c, 0)),
                  pl.BlockSpec((1, Rb, HW), lambda b, c: (b, c, 0)),
                  pl.BlockSpec((1, nck_b, C_HEADS, C_HD, cc), lambda b, c: (b, c, 0, 0, 0)),
                  pl.BlockSpec((1, nck_b, C_HEADS, cc, cc), lambda b, c: (b, c, 0, 0, 0)),
                  pl.BlockSpec((1, nck_b, C_HEADS, 1, C_HD), lambda b, c: (b, c, 0, 0, 0)),
                  pl.BlockSpec((1, Rb, HW), lambda b, c: (b, c, OFF_CZ // HW)),
                  pl.BlockSpec((1, C_HEADS, C_HD, C_HD), lambda b, c: (b, 0, 0, 0)),
                  pl.BlockSpec((1, C_HD), lambda b, c: (0, 0))],
        out_specs=[pl.BlockSpec((1, Rb, HW), lambda b, c: (b, c, 0)),
                   pl.BlockSpec((1, C_HEADS, C_HD, C_HD), lambda b, c: (b, 0, 0, 0))],
        out_shape=[jax.ShapeDtypeStruct((B, T, HW), BF16),
                   jax.ShapeDtypeStruct((B, C_HEADS, C_HD, C_HD), F32)],
        scratch_shapes=[pltpu.VMEM((C_HEADS, C_HD, C_HD), F32)],
        compiler_params=_cparams(("parallel", "arbitrary")),
        name="gdn_inter",
    )(u, w, qt, ktT, qkg, dec, proj, S0, out_norm.reshape(1, C_HD))


def _ffn_up_kernel(x_ref, g_ref, sh_ref, sc_ref, wg_ref, wu_ref, cwg_ref, cwu_ref, fg_ref, fu_ref,
                   act_o, fng_o, fnu_o, h_ref, *scr, split_rows):
    bb, tt, D = x_ref.shape
    tn = wg_ref.shape[1]
    i = pl.program_id(1)
    j = pl.program_id(2)
    lo = SUBLANES - (FFN_CONV - 1)

    @pl.when(j == 0)
    def _():
        x = x_ref[...]
        y = x * lax.rsqrt(jnp.mean(x * x, axis=-1, keepdims=True) + EPS) * g_ref[...]
        h = y * (1.0 + sc_ref[...]) + sh_ref[...]
        h_ref[...] = h.reshape(bb * tt, D).astype(BF16)

    if split_rows:
        (carry_s,) = scr
        halves = ((wg_ref, cwg_ref, fg_ref, fng_o, 0), (wu_ref, cwu_ref, fu_ref, fnu_o, 1))
        for _, _, f_ref, _, idx in halves:
            @pl.when(i == 0)
            def _(f_ref=f_ref, idx=idx):
                carry_s[idx, j, 0, lo:SUBLANES, :] = f_ref[0]
        prev = [carry_s[idx, j, 0] for idx in range(2)]
        row8 = lax.broadcasted_iota(jnp.int32, (SUBLANES, tn), 0)
        rc = min(tt, FFN_ROW_CHUNK)
        for r in range(tt // rc):
            rows = slice(r * rc, (r + 1) * rc)
            ys = []
            for w_ref, cw_ref, _, _, idx in halves:
                u = _dot(h_ref[rows, :], w_ref[...])
                y = u * cw_ref[FFN_CONV - 1:FFN_CONV, :]
                for k in range(1, FFN_CONV):
                    s = pltpu.roll(u, k, 0)
                    head = jnp.where(row8 < k, pltpu.roll(prev[idx], k, 0), s[0:SUBLANES])
                    s = jnp.concatenate([head, s[SUBLANES:]], axis=0)
                    y = y + s * cw_ref[FFN_CONV - 1 - k:FFN_CONV - k, :]
                prev[idx] = u[rc - SUBLANES:rc]
                ys.append(y)
            act_o[0, rows, :] = (ys[0] * jax.nn.sigmoid(ys[0]) * ys[1]).astype(act_o.dtype)
        for _, _, _, fn_o, idx in halves:
            carry_s[idx, j, 0] = prev[idx]
            fn_o[...] = prev[idx][lo:SUBLANES][None]
    else:
        bg_s, bu_s = scr

        def half(w_ref, cw_ref, f_ref, fn_o, buf):
            u = _dot(h_ref[...], w_ref[...]).reshape(bb, tt, tn)
            buf[:, lo:SUBLANES, :] = f_ref[...]
            buf[:, SUBLANES:SUBLANES + tt, :] = u
            fn_o[...] = u[:, tt - (FFN_CONV - 1):tt, :]
            y = None
            for jw in range(FFN_CONV):
                term = buf[:, lo + jw:lo + jw + tt, :] * cw_ref[jw:jw + 1, :]
                y = term if y is None else y + term
            return y

        yg = half(wg_ref, cwg_ref, fg_ref, fng_o, bg_s)
        yu = half(wu_ref, cwu_ref, fu_ref, fnu_o, bu_s)
        act_o[...] = (yg * jax.nn.sigmoid(yg) * yu).astype(act_o.dtype)


def _ffn_up(x, g, shift, scale, w_up, layer, conv_w, fbuf):
    B, T, D = x.shape
    FF = w_up.shape[2] // 2
    bb, tt = _row_tiles(B, T)
    tn = COL_TILE
    assert FF % tn == 0 and tt >= SUBLANES
    nj = FF // tn
    split_rows = T > tt
    assert not split_rows or bb == 1
    nf = FFN_CONV - 1

    def cols(off, rows, lead=()):
        return pl.BlockSpec((None,) * len(lead) + rows + (tn,),
                            lambda b, i, j: lead + (0,) * len(rows) + (off + j,))

    def fb(off):
        return pl.BlockSpec((bb, nf, tn), lambda b, i, j: (b, 0, off + j))

    act, fng, fnu = pl.pallas_call(
        functools.partial(_ffn_up_kernel, split_rows=split_rows),
        grid=(B // bb, T // tt, nj),
        in_specs=[pl.BlockSpec((bb, tt, D), lambda b, i, j: (b, i, 0)),
                  pl.BlockSpec((1, 1, D), lambda b, i, j: (0, 0, 0)),
                  pl.BlockSpec((bb, 1, D), lambda b, i, j: (b, 0, 0)),
                  pl.BlockSpec((bb, 1, D), lambda b, i, j: (b, 0, 0)),
                  cols(0, (D,), (layer,)), cols(nj, (D,), (layer,)),
                  cols(0, (FFN_CONV,)), cols(nj, (FFN_CONV,)),
                  fb(0), fb(nj)],
        out_specs=[pl.BlockSpec((bb, tt, tn), lambda b, i, j: (b, i, j)),
                   pl.BlockSpec((bb, None, nf, tn), lambda b, i, j: (b, i, 0, j)),
                   pl.BlockSpec((bb, None, nf, tn), lambda b, i, j: (b, i, 0, j))],
        out_shape=[jax.ShapeDtypeStruct((B, T, FF), BF16),
                   jax.ShapeDtypeStruct((B, T // tt, nf, FF), F32),
                   jax.ShapeDtypeStruct((B, T // tt, nf, FF), F32)],
        scratch_shapes=[pltpu.VMEM((bb * tt, D), BF16)] + (
            [pltpu.VMEM((2, nj, 1, SUBLANES, tn), F32)] if split_rows else
            [pltpu.VMEM((bb, tt + SUBLANES, tn), F32), pltpu.VMEM((bb, tt + SUBLANES, tn), F32)]),
        compiler_params=_cparams(("parallel", "arbitrary", "arbitrary")),
        name="ffn_up",
    )(x, g.reshape(1, 1, D), shift, scale, w_up, w_up, conv_w, conv_w, fbuf, fbuf)
    return act, jnp.concatenate([fng[:, -1], fnu[:, -1]], axis=-1)


def _prep_w_in(w_in):
    depth, D = w_in.shape[0], w_in.shape[1]
    w_in = w_in.astype(BF16)
    sizes = (1024, 1024, 1024, 1024, 256, 256, 512, 64, 8, C_QKV, 8, 8, 1024, N_BRANCH * D)
    offs = np.concatenate([[0], np.cumsum(sizes)])
    (aq, ak, av, bq, bk, bv, bqi, bki, bwi, cqkv, ca, cb, cz, gt) = [
        w_in[:, :, int(offs[k]):int(offs[k + 1])] for k in range(len(sizes))]

    def z(n):
        return jnp.zeros((depth, D, n), w_in.dtype)

    cols = [cqkv, cz, aq, ak, av, bq, bk, bv, bqi, bki, z(LANES - I_HD), bwi, ca, cb, z(LANES - 24),
            z(OFF_GT - OFF_SMALL - LANES), gt]
    w = jnp.concatenate(cols, axis=2)
    assert w.shape[2] == N_PROJ
    return w


def _layer(x, mod, past, lw, layer_idx):
    kA_p, vA_p, kB_p, vB_p, kI_p, cbuf, S0, fbuf = past
    B, T, D = x.shape
    P = 0 if kI_p is None else kI_p.shape[3]
    sh1, sc1, g1, sh2, sc2, g2 = [m.reshape(B, 1, D) for m in jnp.split(mod, 6, axis=-1)]

    proj = _mm_norm(x, lw['norm_mix'], sh1, sc1, lw['w_in'], layer_idx)
    rope = _rope_table(P + jnp.arange(T, dtype=jnp.int32))
    (qa, ka, kab, va, vab, qb, kb, kbb, vb, vbb, qi, ki, ki2) = _prep(
        proj, rope, lw['a_q_norm'], lw['a_k_norm'], lw['b_q_norm'], lw['b_k_norm'])

    lam_init = 0.8 - 0.6 * math.exp(-0.3 * layer_idx)
    if P:
        oa = _diff_decode(qa, kab, vab, kA_p, vA_p, layer_idx, lw['a_lambda'], lw['a_subln'], P, lam_init)
        ob = _dsa_decode(qi, proj, qb, ki2, kbb, vbb, kI_p, kB_p, vB_p, layer_idx)
    else:
        oa = _diff_attn(qa, kab, vab, lw['a_lambda'], lw['a_subln'], lam_init)
        ob = _dsa(qi, proj, qb, ki2, kbb, vbb, 0, T, min(T, 512))

    oc, S_fin = _gdn(proj, lw['c_conv'], cbuf, S0, lw['c_a_log'], lw['c_dt_bias'], lw['c_out_norm'])
    assert T >= C_CONV - 1 and T >= FFN_CONV - 1
    cbuf_new = proj[:, T - (C_CONV - 1):, OFF_CQKV:OFF_CQKV + C_QKV]

    mixed = _merge(oa, ob, oc, lw['w_branch'], layer_idx, proj, D)
    x = _mm_res(mixed, lw['w_out'], layer_idx, x, g1)

    act, fbuf_new = _ffn_up(x, lw['norm_ffn'], sh2, sc2, lw['w_up'], layer_idx, lw['ffn_conv'], fbuf)
    x = _mm_res(act, lw['w_down'], layer_idx, x, g2, row_target=512)

    va_out = va.reshape(B, T, A_VD // LANES, A_HEADS, LANES).transpose(0, 1, 3, 2, 4)
    new_state = (ka.reshape(B, T, A_HEADS, 2, A_HD), va_out.reshape(B, T, A_HEADS, A_VD),
                 kb.reshape(B, T, B_KV, B_HD), vb.reshape(B, T, B_KV, B_HD), ki,
                 cbuf_new, S_fin, fbuf_new)
    return x, new_state


def kernel(x_prompt, x_sample, c_prompt, c_sample, cache_diff_k, cache_diff_v, cache_dsa_k, cache_dsa_v, cache_dsa_kidx, state_gdn_conv, state_gdn, state_ffn_conv, w_ada, b_ada, norm_mix, w_in, a_q_norm, a_k_norm, a_lambda, a_subln, b_q_norm, b_k_norm, c_conv, c_a_log, c_dt_bias, c_out_norm, w_branch, w_out, norm_ffn, w_up, ffn_conv, w_down):
    depth = w_in.shape[0]
    bp, bs = x_prompt.shape[0], x_sample.shape[0]
    d_ff2 = w_up.shape[2]
    dt_ = x_prompt.dtype
    prompt_past = (None, None, None, None, None, jnp.zeros((bp, C_CONV - 1, C_QKV), dt_),
                   jnp.zeros((bp, C_HEADS, C_HD, C_HD), dt_), jnp.zeros((bp, FFN_CONV - 1, d_ff2), dt_))
    nrow = -(-(bp + bs) // SUBLANES) * SUBLANES
    c_all = jnp.concatenate([c_prompt, c_sample, jnp.zeros((nrow - bp - bs, c_prompt.shape[1]), dt_)], axis=0)
    past_len = cache_diff_k.shape[2]
    diff_k_rows = cache_diff_k.reshape(depth, bs, past_len * A_HEADS * 2, A_HD)
    diff_v_rows = cache_diff_v.reshape(depth, bs, past_len, A_HEADS, A_VD // LANES, LANES)
    diff_v_rows = diff_v_rows.transpose(0, 1, 2, 4, 3, 5).reshape(depth, bs, past_len * A_HEADS * 2, LANES)
    dsa_k_rows = cache_dsa_k.reshape(depth, bs, past_len * B_KV, B_HD)
    dsa_v_rows = cache_dsa_v.reshape(depth, bs, past_len * B_KV, B_HD)
    dsa_kidx_t = jnp.swapaxes(cache_dsa_kidx, 2, 3)
    xp, xs = x_prompt, x_sample
    prompt_states, sample_states = [], []
    big = dict(w_in=_prep_w_in(w_in), w_branch=w_branch.astype(BF16), w_out=w_out.astype(BF16),
               w_up=w_up.astype(BF16), w_down=w_down.astype(BF16))
    for l in range(depth):
        lw = dict(big, norm_mix=norm_mix[l],
                  a_q_norm=a_q_norm[l], a_k_norm=a_k_norm[l], a_lambda=a_lambda[l], a_subln=a_subln[l],
                  b_q_norm=b_q_norm[l], b_k_norm=b_k_norm[l], c_conv=c_conv[l], c_a_log=c_a_log[l],
                  c_dt_bias=c_dt_bias[l], c_out_norm=c_out_norm[l], norm_ffn=norm_ffn[l],
                  ffn_conv=ffn_conv[l])
        mod = _ada(c_all, w_ada, b_ada, l)
        xp, st_p = _layer(xp, mod[:bp], prompt_past, lw, l)
        sample_past = (diff_k_rows, diff_v_rows, dsa_k_rows, dsa_v_rows, dsa_kidx_t,
                       state_gdn_conv[l], state_gdn[l], state_ffn_conv[l])
        xs, st_s = _layer(xs, mod[bp:bp + bs], sample_past, lw, l)
        prompt_states.append(st_p)
        sample_states.append(st_s)
    p_out = [jnp.stack(s, axis=0) for s in zip(*prompt_states)]
    s_out = [jnp.stack(s, axis=0) for s in zip(*sample_states)]
    return (xp, xs, *p_out, *s_out)
```

```python
import functools
import math

import jax
import jax.numpy as jnp
import numpy as np
from jax import lax
from jax.experimental import pallas as pl
from jax.experimental.pallas import tpu as pltpu

F32 = jnp.float32
BF16 = jnp.bfloat16

CHUNK = 64
CHUNK_SHIFT = 6
ROPE_THETA = 10000.0
EPS = 1e-6
LOG2E = 1.4426950408889634
A_HEADS, A_HD, A_VD = 4, 128, 256
B_HEADS, B_KV, B_HD = 8, 2, 128
I_HEADS, I_HD = 8, 64
I_HD_SHIFT = 6
TOPK_MAX = 256
C_HEADS, C_HD, C_CONV = 8, 128, 4
C_QKV = 3 * C_HEADS * C_HD
N_BRANCH = 3
FFN_CONV = 3

LANES = 128
SUBLANES = 8
VMEM_LIMIT = 56 * 1024 * 1024
ROW_TILE = 1024
COL_TILE = 512
SOLVE_BLOCK = 16
TILE_UNROLL = 4
FFN_ROW_CHUNK = 256
NEG = -1e30
KEY_MIN = -(2 ** 31)

OFF_CQKV = 0
OFF_CZ = 3072
OFF_AQ = 4096
OFF_AK = 5120
OFF_AV = 6144
OFF_BQ = 7168
OFF_BK = 8192
OFF_BV = 8448
OFF_BQI = 8704
OFF_BKI = 9216
OFF_SMALL = 9344
CA_LANE = 8
CB_LANE = 16
OFF_GT = 9728
N_PROJ = 15872


def _cparams(sem):
    return pltpu.CompilerParams(dimension_semantics=sem, vmem_limit_bytes=VMEM_LIMIT)


def _row_tiles(B, T, target=ROW_TILE):
    tt = min(T, target)
    assert T % tt == 0 and tt % SUBLANES == 0
    bb = max(1, min(B, target // tt))
    while B % bb:
        bb -= 1
    return bb, tt


def _dot(a, b):
    return jnp.dot(a, b, preferred_element_type=F32)


def _dot_nt(a, b):
    return lax.dot_general(a, b, (((1,), (1,)), ((), ())), preferred_element_type=F32)


def _tile_loop(n, body, init, unroll):
    assert unroll & (unroll - 1) == 0
    c, start, u = init, 0, unroll
    while u >= 1:
        def trip(tb, c, u=u, start=start):
            for k in range(u):
                c = body(start + tb * u + k, c)
            return c

        nb = (n - start) // u
        c = lax.fori_loop(0, nb, trip, c)
        start = start + nb * u
        u //= 2
    return c


def _ada_kernel(c_ref, w_ref, b_ref, o_ref):
    c = c_ref[...]
    s = (c * jax.nn.sigmoid(c)).astype(BF16)
    o_ref[...] = _dot(s, w_ref[...].astype(BF16)) + b_ref[...]


def _ada(c, w, b, layer):
    M, D = c.shape
    N = w.shape[2]
    tn = 1024
    return pl.pallas_call(
        _ada_kernel,
        grid=(N // tn,),
        in_specs=[pl.BlockSpec((M, D), lambda j: (0, 0)),
                  pl.BlockSpec((None, D, tn), lambda j: (layer, 0, j)),
                  pl.BlockSpec((None, 1, tn), lambda j: (layer, 0, j))],
        out_specs=pl.BlockSpec((M, tn), lambda j: (0, j)),
        out_shape=jax.ShapeDtypeStruct((M, N), F32),
        compiler_params=_cparams(("parallel",)),
        name="ada",
    )(c, w, b.reshape(-1, 1, N))


def _mm_norm_kernel(x_ref, g_ref, sh_ref, sc_ref, w_ref, o_ref, h_ref):
    bb, tt, D = x_ref.shape
    tn = w_ref.shape[1]

    @pl.when(pl.program_id(2) == 0)
    def _():
        x = x_ref[...]
        y = x * lax.rsqrt(jnp.mean(x * x, axis=-1, keepdims=True) + EPS) * g_ref[...]
        h = y * (1.0 + sc_ref[...]) + sh_ref[...]
        h_ref[...] = h.reshape(bb * tt, D).astype(BF16)

    acc = _dot(h_ref[...], w_ref[...])
    o_ref[...] = acc.reshape(bb, tt, tn).astype(o_ref.dtype)


def _mm_norm(x, g, shift, scale, w, layer, out_dtype=F32):
    B, T, D = x.shape
    N = w.shape[2]
    bb, tt = _row_tiles(B, T)
    tn = COL_TILE
    assert N % tn == 0
    return pl.pallas_call(
        _mm_norm_kernel,
        grid=(B // bb, T // tt, N // tn),
        in_specs=[pl.BlockSpec((bb, tt, D), lambda b, i, j: (b, i, 0)),
                  pl.BlockSpec((1, 1, D), lambda b, i, j: (0, 0, 0)),
                  pl.BlockSpec((bb, 1, D), lambda b, i, j: (b, 0, 0)),
                  pl.BlockSpec((bb, 1, D), lambda b, i, j: (b, 0, 0)),
                  pl.BlockSpec((None, D, tn), lambda b, i, j: (layer, 0, j))],
        out_specs=pl.BlockSpec((bb, tt, tn), lambda b, i, j: (b, i, j)),
        out_shape=jax.ShapeDtypeStruct((B, T, N), out_dtype),
        scratch_shapes=[pltpu.VMEM((bb * tt, D), BF16)],
        compiler_params=_cparams(("parallel", "parallel", "arbitrary")),
        name="mm_norm",
    )(x, g.reshape(1, 1, D), shift, scale, w)


def _mm_res_kernel(a_ref, w_ref, res_ref, g_ref, o_ref):
    bb, tt, K = a_ref.shape
    tn = w_ref.shape[1]
    acc = _dot(a_ref[...].reshape(bb * tt, K), w_ref[...])
    o_ref[...] = res_ref[...] + g_ref[...] * acc.reshape(bb, tt, tn)


def _mm_res(a, w, layer, res, gate, row_target=ROW_TILE):
    B, T, K = a.shape
    N = w.shape[2]
    bb, tt = _row_tiles(B, T, row_target)
    tn = COL_TILE
    assert N % tn == 0
    return pl.pallas_call(
        _mm_res_kernel,
        grid=(B // bb, T // tt, N // tn),
        in_specs=[pl.BlockSpec((bb, tt, K), lambda b, i, j: (b, i, 0)),
                  pl.BlockSpec((None, K, tn), lambda b, i, j: (layer, 0, j)),
                  pl.BlockSpec((bb, tt, tn), lambda b, i, j: (b, i, j)),
                  pl.BlockSpec((bb, 1, tn), lambda b, i, j: (b, 0, j))],
        out_specs=pl.BlockSpec((bb, tt, tn), lambda b, i, j: (b, i, j)),
        out_shape=jax.ShapeDtypeStruct((B, T, N), F32),
        compiler_params=_cparams(("parallel", "parallel", "arbitrary")),
        name="mm_res",
    )(a, w, res, gate)


def _merge_kernel(b0_ref, b1_ref, b2_ref, w_ref, g0_ref, g1_ref, g2_ref, o_ref):
    bb, tt, W = b0_ref.shape
    tn = w_ref.shape[2]
    acc = None
    for n, (br, gr) in enumerate(((b0_ref, g0_ref), (b1_ref, g1_ref), (b2_ref, g2_ref))):
        y = _dot(br[...].reshape(bb * tt, W), w_ref[n])
        t = jax.nn.sigmoid(gr[...].reshape(bb * tt, tn)) * y
        acc = t if acc is None else acc + t
    o_ref[...] = acc.reshape(bb, tt, tn).astype(o_ref.dtype)


def _merge(oa, ob, oc, w_branch, layer, proj, D):
    B, T, W = oa.shape
    bb, tt = _row_tiles(B, T)
    tn = COL_TILE
    gblk = OFF_GT // tn
    nper = D // tn
    br_spec = pl.BlockSpec((bb, tt, W), lambda b, i, j: (b, i, 0))

    def gate_spec(n):
        return pl.BlockSpec((bb, tt, tn), lambda b, i, j: (b, i, gblk + n * nper + j))

    return pl.pallas_call(
        _merge_kernel,
        grid=(B // bb, T // tt, D // tn),
        in_specs=[br_spec, br_spec, br_spec,
                  pl.BlockSpec((None, N_BRANCH, W, tn), lambda b, i, j: (layer, 0, 0, j)),
                  gate_spec(0), gate_spec(1), gate_spec(2)],
        out_specs=pl.BlockSpec((bb, tt, tn), lambda b, i, j: (b, i, j)),
        out_shape=jax.ShapeDtypeStruct((B, T, D), BF16),
        compiler_params=_cparams(("parallel", "parallel", "arbitrary")),
        name="merge",
    )(oa, ob, oc, w_branch, proj, proj, proj)


def _prep_kernel(aqk_ref, av_ref, bq_ref, bkv_ref, bqi_ref, bki_ref, rope_ref,
                 gaq_ref, gak_ref, gbq_ref, gbk_ref,
                 qa_o, ka_o, kab_o, va_o, vab_o, qb_o, kb_o, kbb_o, vb_o, vbb_o,
                 qi_o, ki_o, ki2_o):
    c1 = rope_ref[:, 0:LANES]
    s1 = rope_ref[:, LANES:2 * LANES]
    c2 = rope_ref[:, 2 * LANES:3 * LANES]
    s2 = rope_ref[:, 3 * LANES:4 * LANES]
    tt = c1.shape[0]
    lane = lax.broadcasted_iota(jnp.int32, (tt, LANES), 1)
    low_half = (lane & (I_HD - 1)) < (I_HD // 2)

    def norm_rope(x, g):
        y = x * lax.rsqrt(jnp.mean(x * x, axis=-1, keepdims=True) + EPS) * g
        return y * c1 + pltpu.roll(y, A_HD // 2, 1) * s1

    def rope64(x):
        r = jnp.where(low_half, pltpu.roll(x, LANES - I_HD // 2, 1), pltpu.roll(x, I_HD // 2, 1))
        return x * c2 + r * s2

    gaq, gak, gbq, gbk = gaq_ref[...], gak_ref[...], gbq_ref[...], gbk_ref[...]
    for hm in range(2 * A_HEADS):
        sl = slice(hm * LANES, (hm + 1) * LANES)
        q = norm_rope(aqk_ref[0, :, sl], gaq) * (A_HD ** -0.5 * LOG2E)
        qa_o[0, :, sl] = q.astype(BF16)
        k = norm_rope(aqk_ref[0, :, 2 * A_HEADS * LANES + hm * LANES:2 * A_HEADS * LANES + (hm + 1) * LANES], gak)
        ka_o[0, pl.ds(hm, tt, stride=2 * A_HEADS), :] = k
        kab_o[0, :, sl] = k.astype(BF16)
    va = av_ref[0]
    for h in range(A_HEADS):
        for c in range(A_VD // LANES):
            va_o[0, pl.ds(c * A_HEADS + h, tt, stride=2 * A_HEADS), :] = (
                va[:, h * A_VD + c * LANES:h * A_VD + (c + 1) * LANES])
    vab_o[0] = va.astype(BF16)
    for h in range(B_HEADS):
        sl = slice(h * LANES, (h + 1) * LANES)
        q = norm_rope(bq_ref[0, :, sl], gbq) * (B_HD ** -0.5 * LOG2E)
        qb_o[0, :, sl] = q.astype(BF16)
    for n in range(B_KV):
        sl = slice(n * LANES, (n + 1) * LANES)
        k = norm_rope(bkv_ref[0, :, sl], gbk)
        kb_o[0, pl.ds(n, tt, stride=B_KV), :] = k
        kbb_o[0, :, sl] = k.astype(BF16)
    vb = bkv_ref[0, :, B_KV * LANES:2 * B_KV * LANES]
    for n in range(B_KV):
        vb_o[0, pl.ds(n, tt, stride=B_KV), :] = vb[:, n * LANES:(n + 1) * LANES]
    vbb_o[0] = vb.astype(BF16)
    for p in range(I_HEADS * I_HD // LANES):
        sl = slice(p * LANES, (p + 1) * LANES)
        qi_o[0, :, sl] = (rope64(bqi_ref[0, :, sl]) * (I_HD ** -0.5)).astype(BF16)
    ki = rope64(bki_ref[0, :, 0:LANES])
    ki_o[0] = ki[:, 0:I_HD]
    ki2_o[0] = (ki + pltpu.roll(ki, I_HD, 1)).astype(BF16)


def _rope_table(pos):
    def tab(half):
        inv = jnp.power(ROPE_THETA, -jnp.arange(half, dtype=F32) / half)
        ang = pos.astype(F32)[:, None] * inv[None, :]
        return jnp.cos(ang), jnp.sin(ang)

    c, s = tab(A_HD // 2)
    ci, si = tab(I_HD // 2)
    return jnp.concatenate([c, c, -s, s, ci, ci, ci, ci, -si, si, -si, si], axis=1)


def _prep(proj, rope, gaq, gak, gbq, gbk):
    B, T, _ = proj.shape
    tt = min(T, 512)
    assert T % tt == 0

    def pspec(width, off):
        assert off % width == 0
        return pl.BlockSpec((1, tt, width), lambda b, i: (b, i, off // width))

    gspec = pl.BlockSpec((1, LANES), lambda b, i: (0, 0))
    ra, rb = 2 * A_HEADS, B_KV
    outs = [(1, 1024, BF16), (ra, LANES, F32), (1, 1024, BF16), (ra, LANES, F32), (1, 1024, BF16),
            (1, 1024, BF16), (rb, LANES, F32), (1, 256, BF16), (rb, LANES, F32), (1, 256, BF16),
            (1, 512, BF16), (1, I_HD, F32), (1, LANES, BF16)]
    return pl.pallas_call(
        _prep_kernel,
        grid=(B, T // tt),
        in_specs=[pspec(2048, OFF_AQ), pspec(1024, OFF_AV), pspec(1024, OFF_BQ), pspec(512, OFF_BK),
                  pspec(512, OFF_BQI), pspec(256, OFF_BKI),
                  pl.BlockSpec((tt, 4 * LANES), lambda b, i: (i, 0)),
                  gspec, gspec, gspec, gspec],
        out_specs=[pl.BlockSpec((1, tt * r, w), lambda b, i: (b, i, 0)) for r, w, _ in outs],
        out_shape=[jax.ShapeDtypeStruct((B, T * r, w), d) for r, w, d in outs],
        compiler_params=_cparams(("parallel", "parallel")),
        name="prep",
    )(proj, proj, proj, proj, proj, proj, rope,
      gaq.reshape(1, LANES), gak.reshape(1, LANES), gbq.reshape(1, LANES), gbk.reshape(1, LANES))


def _diff_kernel(q_ref, kn_ref, vn_ref, lam_ref, sub_ref, o_ref, acc0_s, acc1_s, sn_s, *, tq, lam_init):
    acc = (acc0_s, acc1_s)
    i = pl.program_id(2)
    q_pos0 = i * tq
    q = [q_ref[0, :, m * A_HD:(m + 1) * A_HD] for m in range(2)]
    msl = [slice(m * A_HD, (m + 1) * A_HD) for m in range(2)]
    wide = tq % LANES == 0

    def fold(x, op):
        f = x[:, 0:LANES]
        for c in range(1, x.shape[1] // LANES):
            f = op(f, x[:, c * LANES:(c + 1) * LANES])
        return f

    neg = tuple(jnp.full((tq, LANES), NEG, F32) for _ in range(2))
    zero = tuple(jnp.zeros((tq, LANES), F32) for _ in range(2))

    def scores(t, mr):
        r0 = pl.multiple_of(t * tq, tq)
        k = kn_ref[0, pl.ds(r0, tq), :]
        out = []
        for m in range(2):
            s = _dot_nt(q[m], k[:, msl[m]])
            sn_s[m, t] = s
            out.append(jnp.maximum(mr[m], fold(s, jnp.maximum)) if wide else mr[m])
        return tuple(out)
    mn = _tile_loop(i, scores, neg, TILE_UNROLL)

    r0 = pl.multiple_of(i * tq, tq)
    qpos = q_pos0 + lax.broadcasted_iota(jnp.int32, (tq, 1), 0)
    kpos = q_pos0 + lax.broadcasted_iota(jnp.int32, (1, tq), 1)
    vis = (kpos >> CHUNK_SHIFT) <= (qpos >> CHUNK_SHIFT)
    kd = kn_ref[0, pl.ds(r0, tq), :]
    mrow = []
    for m in range(2):
        s = jnp.where(vis, _dot_nt(q[m], kd[:, msl[m]]), NEG)
        sn_s[m, i] = s
        mx = jnp.max(s, axis=-1, keepdims=True)
        if wide:
            mx = jnp.maximum(mx, jnp.max(mn[m], axis=-1, keepdims=True))
        if wide:
            mx = pltpu.repeat(jnp.broadcast_to(mx, (tq, LANES)), tq // LANES, axis=1)
        mrow.append(mx)

    for m in range(2):
        acc[m][...] = jnp.zeros(acc[m].shape, F32)

    def weigh(t, lr):
        r0 = pl.multiple_of(t * tq, tq)
        v = vn_ref[0, pl.ds(r0, tq), :]
        out = []
        for m in range(2):
            p = jnp.exp2(sn_s[m, t] - mrow[m])
            if wide:
                out.append(lr[m] + fold(p, jnp.add))
            else:
                out.append(lr[m] + jnp.sum(p, axis=-1, keepdims=True))
            acc[m][...] += _dot(p.astype(BF16), v)
        return tuple(out)
    ln = _tile_loop(i + 1, weigh, zero if wide else tuple(jnp.zeros((tq, 1), F32) for _ in range(2)),
                    TILE_UNROLL)
    lrow = [jnp.sum(ln[m], axis=-1, keepdims=True) for m in range(2)]

    lp = lam_ref[...]
    lam = (jnp.exp(jnp.sum(lp[0:1] * lp[1:2], axis=-1, keepdims=True))
           - jnp.exp(jnp.sum(lp[2:3] * lp[3:4], axis=-1, keepdims=True)) + lam_init)
    o = acc[0][...] / lrow[0] - lam * (acc[1][...] / lrow[1])
    o = o * lax.rsqrt(jnp.mean(o * o, axis=-1, keepdims=True) + EPS) * sub_ref[...]
    o_ref[0] = (o * (1.0 - lam_init)).astype(o_ref.dtype)


def _diff_attn(qa, kn, vn, lam_p, subln, lam_init):
    B, T, _ = qa.shape
    tq = min(T, 512)
    assert T % tq == 0 and (tq % CHUNK == 0 or tq == T)
    W = 2 * A_HD
    return pl.pallas_call(
        functools.partial(_diff_kernel, tq=tq, lam_init=lam_init),
        grid=(B, A_HEADS, T // tq),
        in_specs=[pl.BlockSpec((1, tq, W), lambda b, h, i: (b, i, h)),
                  pl.BlockSpec((1, T, W), lambda b, h, i: (b, 0, h)),
                  pl.BlockSpec((1, T, W), lambda b, h, i: (b, 0, h)),
                  pl.BlockSpec((4, A_HD), lambda b, h, i: (0, 0)),
                  pl.BlockSpec((1, A_VD), lambda b, h, i: (0, 0))],
        out_specs=pl.BlockSpec((1, tq, W), lambda b, h, i: (b, i, h)),
        out_shape=jax.ShapeDtypeStruct((B, T, A_HEADS * A_VD), BF16),
        scratch_shapes=[pltpu.VMEM((tq, A_VD), F32), pltpu.VMEM((tq, A_VD), F32),
                        pltpu.VMEM((2, T // tq, tq, tq), F32)],
        compiler_params=_cparams(("parallel", "parallel", "arbitrary")),
        name="diff_attn",
    )(qa, kn, vn, lam_p, subln.reshape(1, A_VD))


def _diff_decode_kernel(q_ref, kn_ref, vn_ref, kp_ref, vp_ref, lam_ref, sub_ref, o_ref, *, P, lam_init):
    T = q_ref.shape[1]
    per_pos = A_HEADS * 2
    qpos = P + lax.broadcasted_iota(jnp.int32, (T, 1), 0)
    kpos = P + lax.broadcasted_iota(jnp.int32, (1, T), 1)
    vis = (kpos >> CHUNK_SHIFT) <= (qpos >> CHUNK_SHIFT)
    lp = lam_ref[...]
    lam = (jnp.exp(jnp.sum(lp[0:1] * lp[1:2], axis=-1, keepdims=True))
           - jnp.exp(jnp.sum(lp[2:3] * lp[3:4], axis=-1, keepdims=True)) + lam_init)
    for h in range(A_HEADS):
        v_past = jnp.concatenate(
            [vp_ref[pl.ds(c * A_HEADS + h, P, stride=per_pos), :] for c in range(A_VD // LANES)],
            axis=-1).astype(BF16)
        v_new = vn_ref[0, :, h * A_VD:(h + 1) * A_VD]
        outs = []
        for m in range(2):
            hm = h * 2 + m
            q = q_ref[0, :, hm * A_HD:(hm + 1) * A_HD]
            k_past = kp_ref[pl.ds(hm, P, stride=per_pos), :].astype(BF16)
            s_p = _dot_nt(q, k_past)
            s_n = jnp.where(vis, _dot_nt(q, kn_ref[0, :, hm * A_HD:(hm + 1) * A_HD]), NEG)
            mx = jnp.maximum(jnp.max(s_p, axis=-1, keepdims=True), jnp.max(s_n, axis=-1, keepdims=True))
            p_p = jnp.exp2(s_p - mx)
            p_n = jnp.exp2(s_n - mx)
            l = jnp.sum(p_p, axis=-1, keepdims=True) + jnp.sum(p_n, axis=-1, keepdims=True)
            outs.append((_dot(p_p.astype(BF16), v_past) + _dot(p_n.astype(BF16), v_new)) / l)
        o = outs[0] - lam * outs[1]
        o = o * lax.rsqrt(jnp.mean(o * o, axis=-1, keepdims=True) + EPS) * sub_ref[...]
        o_ref[0, :, h * A_VD:(h + 1) * A_VD] = (o * (1.0 - lam_init)).astype(o_ref.dtype)


def _diff_decode(qa, kn, vn, kp_rows, vp_rows, layer, lam_p, subln, P, lam_init):
    B, T, W = qa.shape
    assert P % CHUNK == 0 and kp_rows.shape[2] == P * A_HEADS * 2 and vp_rows.shape[2] == P * A_HEADS * 2
    new_spec = pl.BlockSpec((1, T, W), lambda b: (b, 0, 0))
    cache_spec = pl.BlockSpec((None, None, P * A_HEADS * 2, LANES), lambda b: (layer, b, 0, 0))
    return pl.pallas_call(
        functools.partial(_diff_decode_kernel, P=P, lam_init=lam_init),
        grid=(B,),
        in_specs=[new_spec, new_spec, new_spec, cache_spec, cache_spec,
                  pl.BlockSpec((4, A_HD), lambda b: (0, 0)),
                  pl.BlockSpec((1, A_VD), lambda b: (0, 0))],
        out_specs=new_spec,
        out_shape=jax.ShapeDtypeStruct((B, T, W), BF16),
        compiler_params=_cparams(("parallel",)),
        name="diff_decode",
    )(qa, kn, vn, kp_rows, vp_rows, lam_p, subln.reshape(1, A_VD))


def _dsa_kernel(qi_ref, wi_ref, qb_ref, ki_ref, kb_ref, vb_ref, o_ref,
                key_s, bias_s, s_s, *, P, L, tq, tk, topk):
    i = pl.program_id(1)
    q_pos0 = P + i * tq
    ncol = jnp.minimum(L, ((q_pos0 + tq - 1) // CHUNK + 1) * CHUNK)
    nt = (ncol + tk - 1) // tk
    qpos = q_pos0 + lax.broadcasted_iota(jnp.int32, (tq, 1), 0)
    grp = B_HEADS // B_KV
    lane = lax.broadcasted_iota(jnp.int32, (tq, LANES), 1)
    wi = wi_ref[0] * (I_HEADS ** -0.5)
    w_lanes = [jnp.broadcast_to(wi[:, h:h + 1], (tq, LANES)) for h in range(I_HEADS)]

    def vis_of(t):
        kpos = t * tk + lax.broadcasted_iota(jnp.int32, (1, tk), 1)
        return ((kpos >> CHUNK_SHIFT) <= (qpos >> CHUNK_SHIFT)) & (kpos < L)

    def score_tile(t, carry):
        c0 = pl.multiple_of(t * tk, tk)
        kit = ki_ref[0, pl.ds(c0, tk), :]
        sc = jnp.zeros((tq, tk), F32)
        for pr in range(I_HEADS * I_HD // LANES):
            qpair = qi_ref[0, :, pr * LANES:(pr + 1) * LANES]
            for half in range(LANES // I_HD):
                h = pr * (LANES // I_HD) + half
                qh = jnp.where((lane >> I_HD_SHIFT) == half, qpair, jnp.zeros_like(qpair))
                isc = _dot_nt(qh, kit)
                sc = sc + jnp.maximum(isc, 0.0) * pltpu.repeat(w_lanes[h], tk // LANES, axis=1)
        sc = jnp.where(vis_of(t), sc, -jnp.inf)
        key_s[t] = _float_order_key(sc)
        return carry

    lax.fori_loop(0, nt, score_tile, 0)

    def counts(pred_fns):
        def body(t, accs):
            kk = key_s[t]
            out = []
            for pred_fn, acc in zip(pred_fns, accs):
                hit = jnp.where(pred_fn(kk), 1, 0)
                part = hit[:, 0:LANES]
                for c in range(1, tk // LANES):
                    part = part + hit[:, c * LANES:(c + 1) * LANES]
                out.append(acc + part)
            return tuple(out)
        accs = _tile_loop(nt, body, tuple(jnp.zeros((tq, LANES), jnp.int32) for _ in pred_fns), 2)
        return [jnp.sum(a.astype(F32), axis=-1, keepdims=True) for a in accs]

    def select(_):
        def bit_body(bi, thr):
            cand = thr + jnp.left_shift(jnp.int32(1), 31 - bi)
            (cnt,) = counts([lambda kk: kk >= cand])
            return jnp.where(cnt >= topk, cand, thr)
        return lax.fori_loop(0, 32, bit_body, jnp.full((tq, 1), KEY_MIN, jnp.int32))

    thr = lax.cond(ncol > topk, select, lambda _: jnp.full((tq, 1), KEY_MIN, jnp.int32), 0)
    n_gt, n_eq = counts([lambda kk: kk > thr, lambda kk: kk == thr])
    need = topk - n_gt
    neg_inf_key = jnp.int32(KEY_MIN + 0x7FFFFF)
    tie_break = jnp.max(jnp.where((n_eq > need) & (thr != neg_inf_key), 1.0, 0.0)) > 0.5

    def bias_fast(_):
        def body(t, c):
            sel = (key_s[t] >= thr) & vis_of(t)
            bias_s[t] = jnp.where(sel, 0.0, NEG)
            return c
        lax.fori_loop(0, nt, body, 0)
        return 0

    def bias_ties(_):
        r = lax.broadcasted_iota(jnp.int32, (LANES, LANES), 0)
        c = lax.broadcasted_iota(jnp.int32, (LANES, LANES), 1)
        before = jnp.where(r < c, 1.0, 0.0).astype(BF16)

        def body(t, seen):
            kk = key_s[t]
            vis = vis_of(t)
            for cb in range(tk // LANES):
                sl = slice(cb * LANES, (cb + 1) * LANES)
                eq = kk[:, sl] == thr
                eqf = jnp.where(eq, 1.0, 0.0)
                rank = seen + _dot(eqf.astype(BF16), before)
                sel = ((kk[:, sl] > thr) | (eq & (rank < need.astype(F32)))) & vis[:, sl]
                bias_s[t, :, sl] = jnp.where(sel, 0.0, NEG)
                seen = seen + jnp.sum(eqf, axis=-1, keepdims=True)
            return seen
        lax.fori_loop(0, nt, body, jnp.zeros((tq, 1), F32))
        return 0

    lax.cond(tie_break, bias_ties, bias_fast, 0)

    def fold(x, op):
        f = x[:, 0:LANES]
        for c in range(1, tk // LANES):
            f = op(f, x[:, c * LANES:(c + 1) * LANES])
        return f

    for n in range(B_KV):
        kv = slice(n * B_HD, (n + 1) * B_HD)

        def scores(t, mrun, kv=kv, n=n):
            c0 = pl.multiple_of(t * tk, tk)
            bias = bias_s[t]
            k = kb_ref[0, pl.ds(c0, tk), kv]
            out = []
            for g in range(grp):
                h = n * grp + g
                s = _dot_nt(qb_ref[0, :, h * B_HD:(h + 1) * B_HD], k) + bias
                s_s[g, t] = s
                out.append(jnp.maximum(mrun[g], fold(s, jnp.maximum)))
            return tuple(out)

        mrun = _tile_loop(nt, scores, tuple(jnp.full((tq, LANES), NEG, F32) for _ in range(grp)),
                          TILE_UNROLL)
        mrow = [pltpu.repeat(jnp.broadcast_to(jnp.max(m, axis=-1, keepdims=True), (tq, LANES)),
                             tk // LANES, axis=1) for m in mrun]

        ones = jnp.ones((tk, LANES), BF16)

        def weigh(t, acc, kv=kv, mrow=mrow):
            c0 = pl.multiple_of(t * tk, tk)
            v1 = jnp.concatenate([vb_ref[0, pl.ds(c0, tk), kv], ones], axis=-1)
            return tuple(acc[g] + _dot(jnp.exp2(s_s[g, t] - mrow[g]).astype(BF16), v1)
                         for g in range(grp))

        acc = _tile_loop(nt, weigh, tuple(jnp.zeros((tq, B_HD + LANES), F32) for _ in range(grp)),
                         TILE_UNROLL)
        for g in range(grp):
            h = n * grp + g
            o_ref[0, :, h * B_HD:(h + 1) * B_HD] = (
                acc[g][:, 0:B_HD] / acc[g][:, B_HD:B_HD + LANES]).astype(o_ref.dtype)


def _dsa(qi, proj, qb, ki2, kb, vb, P, L, tk):
    B, T, _ = qb.shape
    Lp = ki2.shape[1]
    tq = min(T, 256)
    assert T % tq == 0 and Lp % tk == 0 and tk % LANES == 0
    topk = min(TOPK_MAX, L // 4)
    kern = functools.partial(_dsa_kernel, P=P, L=L, tq=tq, tk=tk, topk=topk)
    return pl.pallas_call(
        kern,
        grid=(B, T // tq),
        in_specs=[pl.BlockSpec((1, tq, I_HEADS * I_HD), lambda b, i: (b, i, 0)),
                  pl.BlockSpec((1, tq, LANES), lambda b, i: (b, i, OFF_SMALL // LANES)),
                  pl.BlockSpec((1, tq, B_HEADS * B_HD), lambda b, i: (b, i, 0)),
                  pl.BlockSpec((1, Lp, LANES), lambda b, i: (b, 0, 0)),
                  pl.BlockSpec((1, Lp, B_KV * B_HD), lambda b, i: (b, 0, 0)),
                  pl.BlockSpec((1, Lp, B_KV * B_HD), lambda b, i: (b, 0, 0))],
        out_specs=pl.BlockSpec((1, tq, B_HEADS * B_HD), lambda b, i: (b, i, 0)),
        out_shape=jax.ShapeDtypeStruct((B, T, B_HEADS * B_HD), BF16),
        scratch_shapes=[pltpu.VMEM((Lp // tk, tq, tk), jnp.int32), pltpu.VMEM((Lp // tk, tq, tk), F32),
                        pltpu.VMEM((B_HEADS // B_KV, Lp // tk, tq, tk), F32)],
        compiler_params=_cparams(("parallel", "arbitrary")),
        name="dsa",
    )(qi, proj, qb, ki2, kb, vb)


def _float_order_key(x):
    bits = pltpu.bitcast(x, jnp.int32)
    bits = jnp.where(bits == KEY_MIN, 0, bits)
    return bits ^ ((bits >> 31) & 0x7FFFFFFF)


def _dsa_decode_kernel(qi_ref, wi_ref, qb_ref, kin_ref, kbn_ref, vbn_ref, kip_ref, kp_ref, vp_ref, o_ref,
                       *, P, topk):
    T = qi_ref.shape[1]
    grp = B_HEADS // B_KV
    qpos = P + lax.broadcasted_iota(jnp.int32, (T, 1), 0)
    kpos = P + lax.broadcasted_iota(jnp.int32, (1, T), 1)
    vis_n = (kpos >> CHUNK_SHIFT) <= (qpos >> CHUNK_SHIFT)
    wi = wi_ref[0] * (I_HEADS ** -0.5)
    ki_pT = kip_ref[...].astype(BF16)
    ki_n = kin_ref[0, :, 0:I_HD]

    sc_p = jnp.zeros((T, P), F32)
    sc_n = jnp.zeros((T, T), F32)
    for h in range(I_HEADS):
        qh = qi_ref[0, :, h * I_HD:(h + 1) * I_HD]
        w = wi[:, h:h + 1]
        sc_p = sc_p + jnp.maximum(_dot(qh, ki_pT), 0.0) * w
        sc_n = sc_n + jnp.maximum(_dot_nt(qh, ki_n), 0.0) * w
    key_p = _float_order_key(sc_p)
    key_n = _float_order_key(jnp.where(vis_n, sc_n, -jnp.inf))

    def count(pred):
        return (jnp.sum(jnp.where(pred(key_p), 1.0, 0.0), axis=-1, keepdims=True)
                + jnp.sum(jnp.where(pred(key_n), 1.0, 0.0), axis=-1, keepdims=True))

    def bit_body(bi, thr):
        cand = thr + jnp.left_shift(jnp.int32(1), 31 - bi)
        return jnp.where(count(lambda kk: kk >= cand) >= topk, cand, thr)

    thr = lax.fori_loop(0, 32, bit_body, jnp.full((T, 1), KEY_MIN, jnp.int32))
    need = topk - count(lambda kk: kk > thr)

    def before(n):
        r = lax.broadcasted_iota(jnp.int32, (n, n), 0)
        c = lax.broadcasted_iota(jnp.int32, (n, n), 1)
        return jnp.where(r < c, 1.0, 0.0).astype(BF16)

    def select(kk, seen, pre):
        eq = kk == thr
        eqf = jnp.where(eq, 1.0, 0.0)
        rank = seen + _dot(eqf.astype(BF16), pre)
        sel = (kk > thr) | (eq & (rank < need))
        return sel, seen + jnp.sum(eqf, axis=-1, keepdims=True)

    pre_l = before(LANES)
    seen = jnp.zeros((T, 1), F32)
    bias_blocks = []
    for cb in range(P // LANES):
        sel, seen = select(key_p[:, cb * LANES:(cb + 1) * LANES], seen, pre_l)
        bias_blocks.append(jnp.where(sel, 0.0, NEG))
    bias_p = jnp.concatenate(bias_blocks, axis=-1)
    sel, _ = select(key_n, seen, before(T))
    bias_n = jnp.where(sel & vis_n, 0.0, NEG)

    for n in range(B_KV):
        kv = slice(n * B_HD, (n + 1) * B_HD)
        k_p = kp_ref[pl.ds(n, P, stride=B_KV), :].astype(BF16)
        v_p = vp_ref[pl.ds(n, P, stride=B_KV), :].astype(BF16)
        k_n = kbn_ref[0, :, kv]
        v_n = vbn_ref[0, :, kv]
        for g in range(grp):
            h = n * grp + g
            q = qb_ref[0, :, h * B_HD:(h + 1) * B_HD]
            s_p = _dot_nt(q, k_p) + bias_p
            s_n = _dot_nt(q, k_n) + bias_n
            mx = jnp.maximum(jnp.max(s_p, axis=-1, keepdims=True), jnp.max(s_n, axis=-1, keepdims=True))
            p_p = jnp.exp2(s_p - mx)
            p_n = jnp.exp2(s_n - mx)
            l = jnp.sum(p_p, axis=-1, keepdims=True) + jnp.sum(p_n, axis=-1, keepdims=True)
            o = _dot(p_p.astype(BF16), v_p) + _dot(p_n.astype(BF16), v_n)
            o_ref[0, :, h * B_HD:(h + 1) * B_HD] = (o / l).astype(o_ref.dtype)


def _dsa_decode(qi, proj, qb, ki2, kbb, vbb, ki_past_t, kp_rows, vp_rows, layer):
    B, T, _ = qb.shape
    P = ki_past_t.shape[3]
    assert P % CHUNK == 0 and P % LANES == 0 and kp_rows.shape[2] == P * B_KV
    topk = min(TOPK_MAX, (P + T) // 4)

    def new(width, blk=0):
        return pl.BlockSpec((1, T, width), lambda b: (b, 0, blk))

    rows = pl.BlockSpec((None, None, P * B_KV, LANES), lambda b: (layer, b, 0, 0))
    return pl.pallas_call(
        functools.partial(_dsa_decode_kernel, P=P, topk=topk),
        grid=(B,),
        in_specs=[new(I_HEADS * I_HD), new(LANES, OFF_SMALL // LANES), new(B_HEADS * B_HD),
                  new(LANES), new(B_KV * B_HD), new(B_KV * B_HD),
                  pl.BlockSpec((None, None, I_HD, P), lambda b: (layer, b, 0, 0)), rows, rows],
        out_specs=new(B_HEADS * B_HD),
        out_shape=jax.ShapeDtypeStruct((B, T, B_HEADS * B_HD), BF16),
        compiler_params=_cparams(("parallel",)),
        name="dsa_decode",
    )(qi, proj, qb, ki2, kbb, vbb, ki_past_t, kp_rows, vp_rows)


def _bmm(a, b):
    return jnp.einsum('bij,bjk->bik', a.astype(BF16), b.astype(BF16), preferred_element_type=F32)


def _bmm_nt(a, b):
    return jnp.einsum('bik,bjk->bij', a.astype(BF16), b.astype(BF16), preferred_element_type=F32)


def _gdn_intra_kernel(x_ref, halo_ref, sm_ref, cw_ref, cbuf_ref, alog_ref, dtb_ref,
                      u_o, w_o, qt_o, ktT_o, qkg_o, dec_o, buf_s, *, cc, nck):
    i = pl.program_id(1)
    R = nck * cc
    HW = C_HEADS * C_HD

    @pl.when(i == 0)
    def _():
        buf_s[0:SUBLANES, :] = jnp.zeros((SUBLANES, C_QKV), F32)
        buf_s[SUBLANES - (C_CONV - 1):SUBLANES, :] = cbuf_ref[0]

    @pl.when(i > 0)
    def _():
        buf_s[0:SUBLANES, :] = halo_ref[0]

    x = x_ref[0]
    prev = buf_s[...]
    row8 = lax.broadcasted_iota(jnp.int32, (SUBLANES, C_QKV), 0)
    y = x * cw_ref[C_CONV - 1:C_CONV, :]
    s = x
    for k in range(1, C_CONV):
        s = pltpu.roll(s, 1, 0)
        head = jnp.where(row8 < k, pltpu.roll(prev, k, 0), s[0:SUBLANES])
        y = y + jnp.concatenate([head, s[SUBLANES:]], axis=0) * cw_ref[C_CONV - 1 - k:C_CONV - k, :]
    y = y * jax.nn.sigmoid(y)

    sm = sm_ref[0]
    xg = sm + dtb_ref[...]
    softplus = jnp.maximum(xg, 0.0) + jnp.log(1.0 + jnp.exp(-jnp.abs(xg)))
    g = -jnp.exp(alog_ref[...]) * softplus
    beta = jax.nn.sigmoid(sm)

    cshift = cc.bit_length() - 1
    rr = lax.broadcasted_iota(jnp.int32, (R, R), 0)
    rc = lax.broadcasted_iota(jnp.int32, (R, R), 1)
    cum = ((rr >> cshift) == (rc >> cshift)) & (rr >= rc)
    G = jnp.dot(jnp.where(cum, 1.0, 0.0), g, preferred_element_type=F32,
                precision=lax.Precision.HIGHEST)

    qs, ks, vs, gcs, bcs = [], [], [], [], []
    for ck in range(nck):
        rows = slice(ck * cc, (ck + 1) * cc)
        for h in range(C_HEADS):
            qs.append(y[rows, h * C_HD:(h + 1) * C_HD])
            ks.append(y[rows, HW + h * C_HD:HW + (h + 1) * C_HD])
            vs.append(y[rows, 2 * HW + h * C_HD:2 * HW + (h + 1) * C_HD])
            gcs.append(G[rows, CA_LANE + h:CA_LANE + h + 1])
            bcs.append(beta[rows, CB_LANE + h:CB_LANE + h + 1])
    q3 = jnp.stack(qs)
    k3 = jnp.stack(ks)
    v3 = jnp.stack(vs)
    Gc = jnp.stack(gcs)
    bc = jnp.stack(bcs)
    q3 = q3 * lax.rsqrt(jnp.sum(q3 * q3, axis=-1, keepdims=True) + EPS) * (C_HD ** -0.5)
    k3 = k3 * lax.rsqrt(jnp.sum(k3 * k3, axis=-1, keepdims=True) + EPS)

    ri = lax.broadcasted_iota(jnp.int32, (cc, cc), 0)
    ci = lax.broadcasted_iota(jnp.int32, (cc, cc), 1)
    incl = (ri >= ci)[None]
    strict = (ri > ci)[None]
    eye = (ri == ci)[None]
    bdiag = ((ri // SOLVE_BLOCK) == (ci // SOLVE_BLOCK))[None]

    Gr = jnp.sum(jnp.where(eye, Gc, 0.0), axis=1, keepdims=True)
    gam = jnp.where(incl, jnp.exp(jnp.where(incl, Gc - Gr, 0.0)), 0.0)
    k16 = k3.astype(BF16)
    A = jnp.where(strict, bc * _bmm_nt(k16, k16) * gam, 0.0)
    Nd = jnp.where(bdiag, -A, 0.0)
    E = jnp.where(bdiag, 0.0, A)
    N2 = _bmm(Nd, Nd)
    N4 = _bmm(N2, N2)
    N8 = _bmm(N4, N4)
    Q = Nd + N2 + _bmm(Nd, N2)
    Q = Q + N4 + _bmm(Q, N4)
    Q = Q + N8 + _bmm(Q, N8)
    Fm = E + _bmm(Q, E)
    F2 = _bmm(Fm, Fm)
    Rm = F2 - Fm - _bmm(Fm, F2)
    Wm = Rm + Q + _bmm(Rm, Q)
    eG = jnp.exp(Gc)
    rhs = jnp.concatenate([bc * v3, (bc * eG) * k3], axis=-1)
    sol = rhs + _bmm(Wm, rhs)
    qkg = _bmm_nt(q3, k16) * gam
    gl = Gc[:, cc - 1:cc, :]
    qt = q3 * eG
    kt = k3 * jnp.exp(gl - Gc)
    dec = jnp.exp(gl)

    for ck in range(nck):
        rows = slice(ck * cc, (ck + 1) * cc)
        for h in range(C_HEADS):
            b = ck * C_HEADS + h
            cols = slice(h * C_HD, (h + 1) * C_HD)
            u_o[0, rows, cols] = sol[b, :, 0:C_HD]
            w_o[0, rows, cols] = sol[b, :, C_HD:2 * C_HD].astype(BF16)
            qt_o[0, rows, cols] = qt[b].astype(BF16)
            ktT_o[0, ck, h] = kt[b].T.astype(BF16)
            qkg_o[0, ck, h] = qkg[b].astype(BF16)
            dec_o[0, ck, h] = jnp.broadcast_to(dec[b], (1, C_HD))


def _gdn_inter_kernel(u_ref, w_ref, qt_ref, ktT_ref, qkg_ref, dec_ref, z_ref, s0_ref, gn_ref,
                      o_ref, sfin_ref, st_s, *, cc, nck):
    c = pl.program_id(1)

    @pl.when(c == 0)
    def _():
        st_s[...] = s0_ref[0]

    S = st_s[...]
    for ck in range(nck):
        rows = slice(ck * cc, (ck + 1) * cc)

        def heads(ref):
            return jnp.stack([ref[0, rows, h * C_HD:(h + 1) * C_HD] for h in range(C_HEADS)])

        S16 = S.astype(BF16)
        v_new = heads(u_ref) - _bmm(heads(w_ref), S16)
        vn16 = v_new.astype(BF16)
        o = _bmm(heads(qt_ref), S16) + _bmm(qkg_ref[0, ck], vn16)
        S = dec_ref[0, ck] * S + _bmm(ktT_ref[0, ck], vn16)
        on = o * lax.rsqrt(jnp.mean(o * o, axis=-1, keepdims=True) + EPS) * gn_ref[...]
        for h in range(C_HEADS):
            cols = slice(h * C_HD, (h + 1) * C_HD)
            z = z_ref[0, rows, cols]
            o_ref[0, rows, cols] = (on[h] * (z * jax.nn.sigmoid(z))).astype(o_ref.dtype)
    st_s[...] = S

    @pl.when(c == pl.num_programs(1) - 1)
    def _():
        sfin_ref[0] = S


def _gdn(proj, conv_w, cbuf, S0, a_log, dt_bias, out_norm):
    B, T, _ = proj.shape
    cc = min(CHUNK, T)
    assert T % cc == 0 and cc % SOLVE_BLOCK == 0 and cc // SOLVE_BLOCK <= 4 and cc >= SUBLANES
    assert cc & (cc - 1) == 0
    HW = C_HEADS * C_HD
    NC = T // cc
    nck_a = 2 if NC % 2 == 0 else 1
    nck_b = 8 if NC % 8 == 0 else 1

    def at_ca(p):
        return jnp.zeros((1, LANES), F32).at[0, CA_LANE:CA_LANE + C_HEADS].set(p.astype(F32))

    Ra = nck_a * cc
    hb = Ra // SUBLANES
    u, w, qt, ktT, qkg, dec = pl.pallas_call(
        functools.partial(_gdn_intra_kernel, cc=cc, nck=nck_a),
        grid=(B, NC // nck_a),
        in_specs=[pl.BlockSpec((1, Ra, C_QKV), lambda b, i: (b, i, OFF_CQKV // C_QKV)),
                  pl.BlockSpec((1, SUBLANES, C_QKV),
                               lambda b, i: (b, jnp.maximum(i * hb - 1, 0), OFF_CQKV // C_QKV)),
                  pl.BlockSpec((1, Ra, LANES), lambda b, i: (b, i, OFF_SMALL // LANES)),
                  pl.BlockSpec((C_CONV, C_QKV), lambda b, i: (0, 0)),
                  pl.BlockSpec((1, C_CONV - 1, C_QKV), lambda b, i: (b, 0, 0)),
                  pl.BlockSpec((1, LANES), lambda b, i: (0, 0)),
                  pl.BlockSpec((1, LANES), lambda b, i: (0, 0))],
        out_specs=[pl.BlockSpec((1, Ra, HW), lambda b, i: (b, i, 0)),
                   pl.BlockSpec((1, Ra, HW), lambda b, i: (b, i, 0)),
                   pl.BlockSpec((1, Ra, HW), lambda b, i: (b, i, 0)),
                   pl.BlockSpec((1, nck_a, C_HEADS, C_HD, cc), lambda b, i: (b, i, 0, 0, 0)),
                   pl.BlockSpec((1, nck_a, C_HEADS, cc, cc), lambda b, i: (b, i, 0, 0, 0)),
                   pl.BlockSpec((1, nck_a, C_HEADS, 1, C_HD), lambda b, i: (b, i, 0, 0, 0))],
        out_shape=[jax.ShapeDtypeStruct((B, T, HW), F32),
                   jax.ShapeDtypeStruct((B, T, HW), BF16),
                   jax.ShapeDtypeStruct((B, T, HW), BF16),
                   jax.ShapeDtypeStruct((B, NC, C_HEADS, C_HD, cc), BF16),
                   jax.ShapeDtypeStruct((B, NC, C_HEADS, cc, cc), BF16),
                   jax.ShapeDtypeStruct((B, NC, C_HEADS, 1, C_HD), F32)],
        scratch_shapes=[pltpu.VMEM((SUBLANES, C_QKV), F32)],
        compiler_params=_cparams(("parallel", "parallel")),
        name="gdn_intra",
    )(proj, proj, proj, conv_w, cbuf, at_ca(a_log), at_ca(dt_bias))

    Rb = nck_b * cc
    return pl.pallas_call(
        functools.partial(_gdn_inter_kernel, cc=cc, nck=nck_b),
        grid=(B, NC // nck_b),
        in_specs=[pl.BlockSpec((1, Rb, HW), lambda b, c: (b, c, 0)),
                  pl.BlockSpec((1, Rb, HW), lambda b, c: (b, c, 0)),
                  pl.BlockSpec((1, Rb, HW), lambda b, c: (b, c, 0)),
                  pl.BlockSpec((1, nck_b, C_HEADS, C_HD, cc), lambda b, c: (b, c, 0, 0, 0)),
                  pl.BlockSpec((1, nck_b, C_HEADS, cc, cc), lambda b, c: (b, c, 0, 0, 0)),
                  pl.BlockSpec((1, nck_b, C_HEADS, 1, C_HD), lambda b, c: (b, c, 0, 0, 0)),
                  pl.BlockSpec((1, Rb, HW), lambda b, c: (b, c, OFF_CZ // HW)),
                  pl.BlockSpec((1, C_HEADS, C_HD, C_HD), lambda b, c: (b, 0, 0, 0)),
                  pl.BlockSpec((1, C_HD), lambda b, c: (0, 0))],
        out_specs=[pl.BlockSpec((1, Rb, HW), lambda b, c: (b, c, 0)),
                   pl.BlockSpec((1, C_HEADS, C_HD, C_HD), lambda b, c: (b, 0, 0, 0))],
        out_shape=[jax.ShapeDtypeStruct((B, T, HW), BF16),
                   jax.ShapeDtypeStruct((B, C_HEADS, C_HD, C_HD), F32)],
        scratch_shapes=[pltpu.VMEM((C_HEADS, C_HD, C_HD), F32)],
        compiler_params=_cparams(("parallel", "arbitrary")),
        name="gdn_inter",
    )(u, w, qt, ktT, qkg, dec, proj, S0, out_norm.reshape(1, C_HD))


def _ffn_up_kernel(x_ref, g_ref, sh_ref, sc_ref, wg_ref, wu_ref, cwg_ref, cwu_ref, fg_ref, fu_ref,
                   act_o, fng_o, fnu_o, h_ref, *scr, split_rows):
    bb, tt, D = x_ref.shape
    tn = wg_ref.shape[1]
    i = pl.program_id(1)
    j = pl.program_id(2)
    lo = SUBLANES - (FFN_CONV - 1)

    @pl.when(j == 0)
    def _():
        x = x_ref[...]
        y = x * lax.rsqrt(jnp.mean(x * x, axis=-1, keepdims=True) + EPS) * g_ref[...]
        h = y * (1.0 + sc_ref[...]) + sh_ref[...]
        h_ref[...] = h.reshape(bb * tt, D).astype(BF16)

    if split_rows:
        (carry_s,) = scr
        halves = ((wg_ref, cwg_ref, fg_ref, fng_o, 0), (wu_ref, cwu_ref, fu_ref, fnu_o, 1))
        for _, _, f_ref, _, idx in halves:
            @pl.when(i == 0)
            def _(f_ref=f_ref, idx=idx):
                carry_s[idx, j, 0, lo:SUBLANES, :] = f_ref[0]
        prev = [carry_s[idx, j, 0] for idx in range(2)]
        row8 = lax.broadcasted_iota(jnp.int32, (SUBLANES, tn), 0)
        rc = min(tt, FFN_ROW_CHUNK)
        for r in range(tt // rc):
            rows = slice(r * rc, (r + 1) * rc)
            ys = []
            for w_ref, cw_ref, _, _, idx in halves:
                u = _dot(h_ref[rows, :], w_ref[...])
                y = u * cw_ref[FFN_CONV - 1:FFN_CONV, :]
                for k in range(1, FFN_CONV):
                    s = pltpu.roll(u, k, 0)
                    head = jnp.where(row8 < k, pltpu.roll(prev[idx], k, 0), s[0:SUBLANES])
                    s = jnp.concatenate([head, s[SUBLANES:]], axis=0)
                    y = y + s * cw_ref[FFN_CONV - 1 - k:FFN_CONV - k, :]
                prev[idx] = u[rc - SUBLANES:rc]
                ys.append(y)
            act_o[0, rows, :] = (ys[0] * jax.nn.sigmoid(ys[0]) * ys[1]).astype(act_o.dtype)
        for _, _, _, fn_o, idx in halves:
            carry_s[idx, j, 0] = prev[idx]
            fn_o[...] = prev[idx][lo:SUBLANES][None]
    else:
        bg_s, bu_s = scr

        def half(w_ref, cw_ref, f_ref, fn_o, buf):
            u = _dot(h_ref[...], w_ref[...]).reshape(bb, tt, tn)
            buf[:, lo:SUBLANES, :] = f_ref[...]
            buf[:, SUBLANES:SUBLANES + tt, :] = u
            fn_o[...] = u[:, tt - (FFN_CONV - 1):tt, :]
            y = None
            for jw in range(FFN_CONV):
                term = buf[:, lo + jw:lo + jw + tt, :] * cw_ref[jw:jw + 1, :]
                y = term if y is None else y + term
            return y

        yg = half(wg_ref, cwg_ref, fg_ref, fng_o, bg_s)
        yu = half(wu_ref, cwu_ref, fu_ref, fnu_o, bu_s)
        act_o[...] = (yg * jax.nn.sigmoid(yg) * yu).astype(act_o.dtype)


def _ffn_up(x, g, shift, scale, w_up, layer, conv_w, fbuf):
    B, T, D = x.shape
    FF = w_up.shape[2] // 2
    bb, tt = _row_tiles(B, T)
    tn = COL_TILE
    assert FF % tn == 0 and tt >= SUBLANES
    nj = FF // tn
    split_rows = T > tt
    assert not split_rows or bb == 1
    nf = FFN_CONV - 1

    def cols(off, rows, lead=()):
        return pl.BlockSpec((None,) * len(lead) + rows + (tn,),
                            lambda b, i, j: lead + (0,) * len(rows) + (off + j,))

    def fb(off):
        return pl.BlockSpec((bb, nf, tn), lambda b, i, j: (b, 0, off + j))

    act, fng, fnu = pl.pallas_call(
        functools.partial(_ffn_up_kernel, split_rows=split_rows),
        grid=(B // bb, T // tt, nj),
        in_specs=[pl.BlockSpec((bb, tt, D), lambda b, i, j: (b, i, 0)),
                  pl.BlockSpec((1, 1, D), lambda b, i, j: (0, 0, 0)),
                  pl.BlockSpec((bb, 1, D), lambda b, i, j: (b, 0, 0)),
                  pl.BlockSpec((bb, 1, D), lambda b, i, j: (b, 0, 0)),
                  cols(0, (D,), (layer,)), cols(nj, (D,), (layer,)),
                  cols(0, (FFN_CONV,)), cols(nj, (FFN_CONV,)),
                  fb(0), fb(nj)],
        out_specs=[pl.BlockSpec((bb, tt, tn), lambda b, i, j: (b, i, j)),
                   pl.BlockSpec((bb, None, nf, tn), lambda b, i, j: (b, i, 0, j)),
                   pl.BlockSpec((bb, None, nf, tn), lambda b, i, j: (b, i, 0, j))],
        out_shape=[jax.ShapeDtypeStruct((B, T, FF), BF16),
                   jax.ShapeDtypeStruct((B, T // tt, nf, FF), F32),
                   jax.ShapeDtypeStruct((B, T // tt, nf, FF), F32)],
        scratch_shapes=[pltpu.VMEM((bb * tt, D), BF16)] + (
            [pltpu.VMEM((2, nj, 1, SUBLANES, tn), F32)] if split_rows else
            [pltpu.VMEM((bb, tt + SUBLANES, tn), F32), pltpu.VMEM((bb, tt + SUBLANES, tn), F32)]),
        compiler_params=_cparams(("parallel", "arbitrary", "arbitrary")),
        name="ffn_up",
    )(x, g.reshape(1, 1, D), shift, scale, w_up, w_up, conv_w, conv_w, fbuf, fbuf)
    return act, jnp.concatenate([fng[:, -1], fnu[:, -1]], axis=-1)


def _prep_w_in(w_in):
    depth, D = w_in.shape[0], w_in.shape[1]
    w_in = w_in.astype(BF16)
    sizes = (1024, 1024, 1024, 1024, 256, 256, 512, 64, 8, C_QKV, 8, 8, 1024, N_BRANCH * D)
    offs = np.concatenate([[0], np.cumsum(sizes)])
    (aq, ak, av, bq, bk, bv, bqi, bki, bwi, cqkv, ca, cb, cz, gt) = [
        w_in[:, :, int(offs[k]):int(offs[k + 1])] for k in range(len(sizes))]

    def z(n):
        return jnp.zeros((depth, D, n), w_in.dtype)

    cols = [cqkv, cz, aq, ak, av, bq, bk, bv, bqi, bki, z(LANES - I_HD), bwi, ca, cb, z(LANES - 24),
            z(OFF_GT - OFF_SMALL - LANES), gt]
    w = jnp.concatenate(cols, axis=2)
    assert w.shape[2] == N_PROJ
    return w


def _layer(x, mod, past, lw, layer_idx):
    kA_p, vA_p, kB_p, vB_p, kI_p, cbuf, S0, fbuf = past
    B, T, D = x.shape
    P = 0 if kI_p is None else kI_p.shape[3]
    sh1, sc1, g1, sh2, sc2, g2 = [m.reshape(B, 1, D) for m in jnp.split(mod, 6, axis=-1)]

    proj = _mm_norm(x, lw['norm_mix'], sh1, sc1, lw['w_in'], layer_idx)
    rope = _rope_table(P + jnp.arange(T, dtype=jnp.int32))
    (qa, ka, kab, va, vab, qb, kb, kbb, vb, vbb, qi, ki, ki2) = _prep(
        proj, rope, lw['a_q_norm'], lw['a_k_norm'], lw['b_q_norm'], lw['b_k_norm'])

    lam_init = 0.8 - 0.6 * math.exp(-0.3 * layer_idx)
    if P:
        oa = _diff_decode(qa, kab, vab, kA_p, vA_p, layer_idx, lw['a_lambda'], lw['a_subln'], P, lam_init)
        ob = _dsa_decode(qi, proj, qb, ki2, kbb, vbb, kI_p, kB_p, vB_p, layer_idx)
    else:
        oa = _diff_attn(qa, kab, vab, lw['a_lambda'], lw['a_subln'], lam_init)
        ob = _dsa(qi, proj, qb, ki2, kbb, vbb, 0, T, min(T, 512))

    oc, S_fin = _gdn(proj, lw['c_conv'], cbuf, S0, lw['c_a_log'], lw['c_dt_bias'], lw['c_out_norm'])
    assert T >= C_CONV - 1 and T >= FFN_CONV - 1
    cbuf_new = proj[:, T - (C_CONV - 1):, OFF_CQKV:OFF_CQKV + C_QKV]

    mixed = _merge(oa, ob, oc, lw['w_branch'], layer_idx, proj, D)
    x = _mm_res(mixed, lw['w_out'], layer_idx, x, g1)

    act, fbuf_new = _ffn_up(x, lw['norm_ffn'], sh2, sc2, lw['w_up'], layer_idx, lw['ffn_conv'], fbuf)
    x = _mm_res(act, lw['w_down'], layer_idx, x, g2, row_target=512)

    va_out = va.reshape(B, T, A_VD // LANES, A_HEADS, LANES).transpose(0, 1, 3, 2, 4)
    new_state = (ka.reshape(B, T, A_HEADS, 2, A_HD), va_out.reshape(B, T, A_HEADS, A_VD),
                 kb.reshape(B, T, B_KV, B_HD), vb.reshape(B, T, B_KV, B_HD), ki,
                 cbuf_new, S_fin, fbuf_new)
    return x, new_state


def kernel(x_prompt, x_sample, c_prompt, c_sample, cache_diff_k, cache_diff_v, cache_dsa_k, cache_dsa_v, cache_dsa_kidx, state_gdn_conv, state_gdn, state_ffn_conv, w_ada, b_ada, norm_mix, w_in, a_q_norm, a_k_norm, a_lambda, a_subln, b_q_norm, b_k_norm, c_conv, c_a_log, c_dt_bias, c_out_norm, w_branch, w_out, norm_ffn, w_up, ffn_conv, w_down):
    depth = w_in.shape[0]
    bp, bs = x_prompt.shape[0], x_sample.shape[0]
    d_ff2 = w_up.shape[2]
    dt_ = x_prompt.dtype
    prompt_past = (None, None, None, None, None, jnp.zeros((bp, C_CONV - 1, C_QKV), dt_),
                   jnp.zeros((bp, C_HEADS, C_HD, C_HD), dt_), jnp.zeros((bp, FFN_CONV - 1, d_ff2), dt_))
    nrow = -(-(bp + bs) // SUBLANES) * SUBLANES
    c_all = jnp.concatenate([c_prompt, c_sample, jnp.zeros((nrow - bp - bs, c_prompt.shape[1]), dt_)], axis=0)
    past_len = cache_diff_k.shape[2]
    diff_k_rows = cache_diff_k.reshape(depth, bs, past_len * A_HEADS * 2, A_HD)
    diff_v_rows = cache_diff_v.reshape(depth, bs, past_len, A_HEADS, A_VD // LANES, LANES)
    diff_v_rows = diff_v_rows.transpose(0, 1, 2, 4, 3, 5).reshape(depth, bs, past_len * A_HEADS * 2, LANES)
    dsa_k_rows = cache_dsa_k.reshape(depth, bs, past_len * B_KV, B_HD)
    dsa_v_rows = cache_dsa_v.reshape(depth, bs, past_len * B_KV, B_HD)
    dsa_kidx_t = jnp.swapaxes(cache_dsa_kidx, 2, 3)
    xp, xs = x_prompt, x_sample
    prompt_states, sample_states = [], []
    big = dict(w_in=_prep_w_in(w_in), w_branch=w_branch.astype(BF16), w_out=w_out.astype(BF16),
               w_up=w_up.astype(BF16), w_down=w_down.astype(BF16))
    for l in range(depth):
        lw = dict(big, norm_mix=norm_mix[l],
                  a_q_norm=a_q_norm[l], a_k_norm=a_k_norm[l], a_lambda=a_lambda[l], a_subln=a_subln[l],
                  b_q_norm=b_q_norm[l], b_k_norm=b_k_norm[l], c_conv=c_conv[l], c_a_log=c_a_log[l],
                  c_dt_bias=c_dt_bias[l], c_out_norm=c_out_norm[l], norm_ffn=norm_ffn[l],
                  ffn_conv=ffn_conv[l])
        mod = _ada(c_all, w_ada, b_ada, l)
        xp, st_p = _layer(xp, mod[:bp], prompt_past, lw, l)
        sample_past = (diff_k_rows, diff_v_rows, dsa_k_rows, dsa_v_rows, dsa_kidx_t,
                       state_gdn_conv[l], state_gdn[l], state_ffn_conv[l])
        xs, st_s = _layer(xs, mod[bp:bp + bs], sample_past, lw, l)
        prompt_states.append(st_p)
        sample_states.append(st_s)
    p_out = [jnp.stack(s, axis=0) for s in zip(*prompt_states)]
    s_out = [jnp.stack(s, axis=0) for s in zip(*sample_states)]
    return (xp, xs, *p_out, *s_out)
```

```python
import functools
import math

import jax
import jax.numpy as jnp
import numpy as np
from jax import lax
from jax.experimental import pallas as pl
from jax.experimental.pallas import tpu as pltpu

F32 = jnp.float32
BF16 = jnp.bfloat16

CHUNK = 64
CHUNK_SHIFT = 6
ROPE_THETA = 10000.0
EPS = 1e-6
LOG2E = 1.4426950408889634
A_HEADS, A_HD, A_VD = 4, 128, 256
B_HEADS, B_KV, B_HD = 8, 2, 128
I_HEADS, I_HD = 8, 64
I_HD_SHIFT = 6
TOPK_MAX = 256
C_HEADS, C_HD, C_CONV = 8, 128, 4
C_QKV = 3 * C_HEADS * C_HD
N_BRANCH = 3
FFN_CONV = 3

LANES = 128
SUBLANES = 8
VMEM_LIMIT = 56 * 1024 * 1024
ROW_TILE = 1024
COL_TILE = 512
SOLVE_BLOCK = 16
TILE_UNROLL = 4
NORM_ROW_CHUNK = 128
FFN_ROW_CHUNK = 256
NEG = -1e30
KEY_MIN = -(2 ** 31)

OFF_CQKV = 0
OFF_CZ = 3072
OFF_AQ = 4096
OFF_AK = 5120
OFF_AV = 6144
OFF_BQ = 7168
OFF_BK = 8192
OFF_BV = 8448
OFF_BQI = 8704
OFF_BKI = 9216
OFF_SMALL = 9344
CA_LANE = 8
CB_LANE = 16
OFF_GT = 9728
N_PROJ = 15872


def _cparams(sem):
    return pltpu.CompilerParams(dimension_semantics=sem, vmem_limit_bytes=VMEM_LIMIT)


def _row_tiles(B, T, target=ROW_TILE):
    tt = min(T, target)
    assert T % tt == 0 and tt % SUBLANES == 0
    bb = max(1, min(B, target // tt))
    while B % bb:
        bb -= 1
    return bb, tt


def _dot(a, b):
    return jnp.dot(a, b, preferred_element_type=F32)


def _dot_nt(a, b):
    return lax.dot_general(a, b, (((1,), (1,)), ((), ())), preferred_element_type=F32)


def _tile_loop(n, body, init, unroll):
    assert unroll & (unroll - 1) == 0
    c, start, u = init, 0, unroll
    while u >= 1:
        def trip(tb, c, u=u, start=start):
            for k in range(u):
                c = body(start + tb * u + k, c)
            return c

        nb = (n - start) // u
        c = lax.fori_loop(0, nb, trip, c)
        start = start + nb * u
        u //= 2
    return c


def _ada_kernel(c_ref, w_ref, b_ref, o_ref):
    c = c_ref[...]
    s = (c * jax.nn.sigmoid(c)).astype(BF16)
    o_ref[...] = _dot(s, w_ref[...].astype(BF16)) + b_ref[...]


def _ada(c, w, b, layer):
    M, D = c.shape
    N = w.shape[2]
    tn = 1024
    return pl.pallas_call(
        _ada_kernel,
        grid=(N // tn,),
        in_specs=[pl.BlockSpec((M, D), lambda j: (0, 0)),
                  pl.BlockSpec((None, D, tn), lambda j: (layer, 0, j)),
                  pl.BlockSpec((None, 1, tn), lambda j: (layer, 0, j))],
        out_specs=pl.BlockSpec((M, tn), lambda j: (0, j)),
        out_shape=jax.ShapeDtypeStruct((M, N), F32),
        compiler_params=_cparams(("parallel",)),
        name="ada",
    )(c, w, b.reshape(-1, 1, N))


def _norm_mod_rows(x_ref, g_ref, sh_ref, sc_ref, h_ref):
    bb, tt, D = x_ref.shape
    ct = min(tt, NORM_ROW_CHUNK)
    for b in range(bb):
        for r in range(tt // ct):
            x = x_ref[b, r * ct:(r + 1) * ct, :]
            y = x * lax.rsqrt(jnp.mean(x * x, axis=-1, keepdims=True) + EPS) * g_ref[0]
            h = y * (1.0 + sc_ref[b]) + sh_ref[b]
            h_ref[b * tt + r * ct:b * tt + (r + 1) * ct, :] = h.astype(BF16)


def _mm_norm_kernel(x_ref, g_ref, sh_ref, sc_ref, w_ref, o_ref, h_ref):
    bb, tt, D = x_ref.shape
    tn = w_ref.shape[1]

    @pl.when(pl.program_id(2) == 0)
    def _():
        _norm_mod_rows(x_ref, g_ref, sh_ref, sc_ref, h_ref)

    acc = _dot(h_ref[...], w_ref[...])
    o_ref[...] = acc.reshape(bb, tt, tn).astype(o_ref.dtype)


def _mm_norm(x, g, shift, scale, w, layer, out_dtype=F32):
    B, T, D = x.shape
    N = w.shape[2]
    bb, tt = _row_tiles(B, T)
    tn = COL_TILE
    assert N % tn == 0
    return pl.pallas_call(
        _mm_norm_kernel,
        grid=(B // bb, T // tt, N // tn),
        in_specs=[pl.BlockSpec((bb, tt, D), lambda b, i, j: (b, i, 0)),
                  pl.BlockSpec((1, 1, D), lambda b, i, j: (0, 0, 0)),
                  pl.BlockSpec((bb, 1, D), lambda b, i, j: (b, 0, 0)),
                  pl.BlockSpec((bb, 1, D), lambda b, i, j: (b, 0, 0)),
                  pl.BlockSpec((None, D, tn), lambda b, i, j: (layer, 0, j))],
        out_specs=pl.BlockSpec((bb, tt, tn), lambda b, i, j: (b, i, j)),
        out_shape=jax.ShapeDtypeStruct((B, T, N), out_dtype),
        scratch_shapes=[pltpu.VMEM((bb * tt, D), BF16)],
        compiler_params=_cparams(("parallel", "parallel", "arbitrary")),
        name="mm_norm",
    )(x, g.reshape(1, 1, D), shift, scale, w)


def _mm_res_kernel(a_ref, w_ref, res_ref, g_ref, o_ref):
    bb, tt, K = a_ref.shape
    tn = w_ref.shape[1]
    acc = _dot(a_ref[...].reshape(bb * tt, K), w_ref[...])
    o_ref[...] = res_ref[...] + g_ref[...] * acc.reshape(bb, tt, tn)


def _mm_res(a, w, layer, res, gate, row_target=ROW_TILE):
    B, T, K = a.shape
    N = w.shape[2]
    bb, tt = _row_tiles(B, T, row_target)
    tn = COL_TILE
    assert N % tn == 0
    return pl.pallas_call(
        _mm_res_kernel,
        grid=(B // bb, T // tt, N // tn),
        in_specs=[pl.BlockSpec((bb, tt, K), lambda b, i, j: (b, i, 0)),
                  pl.BlockSpec((None, K, tn), lambda b, i, j: (layer, 0, j)),
                  pl.BlockSpec((bb, tt, tn), lambda b, i, j: (b, i, j)),
                  pl.BlockSpec((bb, 1, tn), lambda b, i, j: (b, 0, j))],
        out_specs=pl.BlockSpec((bb, tt, tn), lambda b, i, j: (b, i, j)),
        out_shape=jax.ShapeDtypeStruct((B, T, N), F32),
        compiler_params=_cparams(("parallel", "parallel", "arbitrary")),
        name="mm_res",
    )(a, w, res, gate)


def _merge_kernel(b0_ref, b1_ref, b2_ref, w_ref, g0_ref, g1_ref, g2_ref, o_ref):
    bb, tt, W = b0_ref.shape
    tn = w_ref.shape[2]
    acc = None
    for n, (br, gr) in enumerate(((b0_ref, g0_ref), (b1_ref, g1_ref), (b2_ref, g2_ref))):
        y = _dot(br[...].reshape(bb * tt, W), w_ref[n])
        t = jax.nn.sigmoid(gr[...].reshape(bb * tt, tn)) * y
        acc = t if acc is None else acc + t
    o_ref[...] = acc.reshape(bb, tt, tn).astype(o_ref.dtype)


def _merge(oa, ob, oc, w_branch, layer, proj, D):
    B, T, W = oa.shape
    bb, tt = _row_tiles(B, T)
    tn = COL_TILE
    gblk = OFF_GT // tn
    nper = D // tn
    br_spec = pl.BlockSpec((bb, tt, W), lambda b, i, j: (b, i, 0))

    def gate_spec(n):
        return pl.BlockSpec((bb, tt, tn), lambda b, i, j: (b, i, gblk + n * nper + j))

    return pl.pallas_call(
        _merge_kernel,
        grid=(B // bb, T // tt, D // tn),
        in_specs=[br_spec, br_spec, br_spec,
                  pl.BlockSpec((None, N_BRANCH, W, tn), lambda b, i, j: (layer, 0, 0, j)),
                  gate_spec(0), gate_spec(1), gate_spec(2)],
        out_specs=pl.BlockSpec((bb, tt, tn), lambda b, i, j: (b, i, j)),
        out_shape=jax.ShapeDtypeStruct((B, T, D), BF16),
        compiler_params=_cparams(("parallel", "parallel", "arbitrary")),
        name="merge",
    )(oa, ob, oc, w_branch, proj, proj, proj)


def _prep_kernel(aqk_ref, av_ref, bq_ref, bkv_ref, bqi_ref, bki_ref, rope_ref,
                 gaq_ref, gak_ref, gbq_ref, gbk_ref,
                 qa_o, ka_o, kab_o, va_o, vab_o, qb_o, kb_o, kbb_o, vb_o, vbb_o,
                 qi_o, ki_o, ki2_o):
    c1 = rope_ref[:, 0:LANES]
    s1 = rope_ref[:, LANES:2 * LANES]
    c2 = rope_ref[:, 2 * LANES:3 * LANES]
    s2 = rope_ref[:, 3 * LANES:4 * LANES]
    tt = c1.shape[0]
    lane = lax.broadcasted_iota(jnp.int32, (tt, LANES), 1)
    low_half = (lane & (I_HD - 1)) < (I_HD // 2)

    def norm_rope(x, g):
        y = x * lax.rsqrt(jnp.mean(x * x, axis=-1, keepdims=True) + EPS) * g
        return y * c1 + pltpu.roll(y, A_HD // 2, 1) * s1

    def rope64(x):
        r = jnp.where(low_half, pltpu.roll(x, LANES - I_HD // 2, 1), pltpu.roll(x, I_HD // 2, 1))
        return x * c2 + r * s2

    gaq, gak, gbq, gbk = gaq_ref[...], gak_ref[...], gbq_ref[...], gbk_ref[...]
    for hm in range(2 * A_HEADS):
        sl = slice(hm * LANES, (hm + 1) * LANES)
        q = norm_rope(aqk_ref[0, :, sl], gaq) * (A_HD ** -0.5 * LOG2E)
        qa_o[0, :, sl] = q.astype(BF16)
        k = norm_rope(aqk_ref[0, :, 2 * A_HEADS * LANES + hm * LANES:2 * A_HEADS * LANES + (hm + 1) * LANES], gak)
        ka_o[0, pl.ds(hm, tt, stride=2 * A_HEADS), :] = k
        kab_o[0, :, sl] = k.astype(BF16)
    va = av_ref[0]
    for h in range(A_HEADS):
        for c in range(A_VD // LANES):
            va_o[0, pl.ds(c * A_HEADS + h, tt, stride=2 * A_HEADS), :] = (
                va[:, h * A_VD + c * LANES:h * A_VD + (c + 1) * LANES])
    vab_o[0] = va.astype(BF16)
    for h in range(B_HEADS):
        sl = slice(h * LANES, (h + 1) * LANES)
        q = norm_rope(bq_ref[0, :, sl], gbq) * (B_HD ** -0.5 * LOG2E)
        qb_o[0, :, sl] = q.astype(BF16)
    for n in range(B_KV):
        sl = slice(n * LANES, (n + 1) * LANES)
        k = norm_rope(bkv_ref[0, :, sl], gbk)
        kb_o[0, pl.ds(n, tt, stride=B_KV), :] = k
        kbb_o[0, :, sl] = k.astype(BF16)
    vb = bkv_ref[0, :, B_KV * LANES:2 * B_KV * LANES]
    for n in range(B_KV):
        vb_o[0, pl.ds(n, tt, stride=B_KV), :] = vb[:, n * LANES:(n + 1) * LANES]
    vbb_o[0] = vb.astype(BF16)
    for p in range(I_HEADS * I_HD // LANES):
        sl = slice(p * LANES, (p + 1) * LANES)
        qi_o[0, :, sl] = (rope64(bqi_ref[0, :, sl]) * (I_HD ** -0.5)).astype(BF16)
    ki = rope64(bki_ref[0, :, 0:LANES])
    ki_o[0] = ki[:, 0:I_HD]
    ki2_o[0] = (ki + pltpu.roll(ki, I_HD, 1)).astype(BF16)


def _rope_table(pos):
    def tab(half):
        inv = jnp.power(ROPE_THETA, -jnp.arange(half, dtype=F32) / half)
        ang = pos.astype(F32)[:, None] * inv[None, :]
        return jnp.cos(ang), jnp.sin(ang)

    c, s = tab(A_HD // 2)
    ci, si = tab(I_HD // 2)
    return jnp.concatenate([c, c, -s, s, ci, ci, ci, ci, -si, si, -si, si], axis=1)


def _prep(proj, rope, gaq, gak, gbq, gbk):
    B, T, _ = proj.shape
    tt = min(T, 512)
    assert T % tt == 0

    def pspec(width, off):
        assert off % width == 0
        return pl.BlockSpec((1, tt, width), lambda b, i: (b, i, off // width))

    gspec = pl.BlockSpec((1, LANES), lambda b, i: (0, 0))
    ra, rb = 2 * A_HEADS, B_KV
    outs = [(1, 1024, BF16), (ra, LANES, F32), (1, 1024, BF16), (ra, LANES, F32), (1, 1024, BF16),
            (1, 1024, BF16), (rb, LANES, F32), (1, 256, BF16), (rb, LANES, F32), (1, 256, BF16),
            (1, 512, BF16), (1, I_HD, F32), (1, LANES, BF16)]
    return pl.pallas_call(
        _prep_kernel,
        grid=(B, T // tt),
        in_specs=[pspec(2048, OFF_AQ), pspec(1024, OFF_AV), pspec(1024, OFF_BQ), pspec(512, OFF_BK),
                  pspec(512, OFF_BQI), pspec(256, OFF_BKI),
                  pl.BlockSpec((tt, 4 * LANES), lambda b, i: (i, 0)),
                  gspec, gspec, gspec, gspec],
        out_specs=[pl.BlockSpec((1, tt * r, w), lambda b, i: (b, i, 0)) for r, w, _ in outs],
        out_shape=[jax.ShapeDtypeStruct((B, T * r, w), d) for r, w, d in outs],
        compiler_params=_cparams(("parallel", "parallel")),
        name="prep",
    )(proj, proj, proj, proj, proj, proj, rope,
      gaq.reshape(1, LANES), gak.reshape(1, LANES), gbq.reshape(1, LANES), gbk.reshape(1, LANES))


def _diff_kernel(q_ref, kn_ref, vn_ref, lam_ref, sub_ref, o_ref, acc0_s, acc1_s, sn_s, *, tq, lam_init):
    acc = (acc0_s, acc1_s)
    i = pl.program_id(2)
    q_pos0 = i * tq
    q = [q_ref[0, :, m * A_HD:(m + 1) * A_HD] for m in range(2)]
    msl = [slice(m * A_HD, (m + 1) * A_HD) for m in range(2)]
    wide = tq % LANES == 0

    def fold(x, op):
        f = x[:, 0:LANES]
        for c in range(1, x.shape[1] // LANES):
            f = op(f, x[:, c * LANES:(c + 1) * LANES])
        return f

    neg = tuple(jnp.full((tq, LANES), NEG, F32) for _ in range(2))
    zero = tuple(jnp.zeros((tq, LANES), F32) for _ in range(2))

    def scores(t, mr):
        r0 = pl.multiple_of(t * tq, tq)
        k = kn_ref[0, pl.ds(r0, tq), :]
        out = []
        for m in range(2):
            s = _dot_nt(q[m], k[:, msl[m]])
            sn_s[m, t] = s
            out.append(jnp.maximum(mr[m], fold(s, jnp.maximum)) if wide else mr[m])
        return tuple(out)
    mn = _tile_loop(i, scores, neg, TILE_UNROLL)

    r0 = pl.multiple_of(i * tq, tq)
    qpos = q_pos0 + lax.broadcasted_iota(jnp.int32, (tq, 1), 0)
    kpos = q_pos0 + lax.broadcasted_iota(jnp.int32, (1, tq), 1)
    vis = (kpos >> CHUNK_SHIFT) <= (qpos >> CHUNK_SHIFT)
    kd = kn_ref[0, pl.ds(r0, tq), :]
    mrow = []
    for m in range(2):
        s = jnp.where(vis, _dot_nt(q[m], kd[:, msl[m]]), NEG)
        sn_s[m, i] = s
        mx = jnp.max(s, axis=-1, keepdims=True)
        if wide:
            mx = jnp.maximum(mx, jnp.max(mn[m], axis=-1, keepdims=True))
        if wide:
            mx = pltpu.repeat(jnp.broadcast_to(mx, (tq, LANES)), tq // LANES, axis=1)
        mrow.append(mx)

    for m in range(2):
        acc[m][...] = jnp.zeros(acc[m].shape, F32)

    def weigh(t, lr):
        r0 = pl.multiple_of(t * tq, tq)
        v = vn_ref[0, pl.ds(r0, tq), :]
        out = []
        for m in range(2):
            p = jnp.exp2(sn_s[m, t] - mrow[m])
            if wide:
                out.append(lr[m] + fold(p, jnp.add))
            else:
                out.append(lr[m] + jnp.sum(p, axis=-1, keepdims=True))
            acc[m][...] += _dot(p.astype(BF16), v)
        return tuple(out)
    ln = _tile_loop(i + 1, weigh, zero if wide else tuple(jnp.zeros((tq, 1), F32) for _ in range(2)),
                    TILE_UNROLL)
    lrow = [jnp.sum(ln[m], axis=-1, keepdims=True) for m in range(2)]

    lp = lam_ref[...]
    lam = (jnp.exp(jnp.sum(lp[0:1] * lp[1:2], axis=-1, keepdims=True))
           - jnp.exp(jnp.sum(lp[2:3] * lp[3:4], axis=-1, keepdims=True)) + lam_init)
    o = acc[0][...] / lrow[0] - lam * (acc[1][...] / lrow[1])
    o = o * lax.rsqrt(jnp.mean(o * o, axis=-1, keepdims=True) + EPS) * sub_ref[...]
    o_ref[0] = (o * (1.0 - lam_init)).astype(o_ref.dtype)


def _diff_attn(qa, kn, vn, lam_p, subln, lam_init):
    B, T, _ = qa.shape
    tq = min(T, 512)
    assert T % tq == 0 and (tq % CHUNK == 0 or tq == T)
    W = 2 * A_HD
    return pl.pallas_call(
        functools.partial(_diff_kernel, tq=tq, lam_init=lam_init),
        grid=(B, A_HEADS, T // tq),
        in_specs=[pl.BlockSpec((1, tq, W), lambda b, h, i: (b, i, h)),
                  pl.BlockSpec((1, T, W), lambda b, h, i: (b, 0, h)),
                  pl.BlockSpec((1, T, W), lambda b, h, i: (b, 0, h)),
                  pl.BlockSpec((4, A_HD), lambda b, h, i: (0, 0)),
                  pl.BlockSpec((1, A_VD), lambda b, h, i: (0, 0))],
        out_specs=pl.BlockSpec((1, tq, W), lambda b, h, i: (b, i, h)),
        out_shape=jax.ShapeDtypeStruct((B, T, A_HEADS * A_VD), BF16),
        scratch_shapes=[pltpu.VMEM((tq, A_VD), F32), pltpu.VMEM((tq, A_VD), F32),
                        pltpu.VMEM((2, T // tq, tq, tq), F32)],
        compiler_params=_cparams(("parallel", "parallel", "arbitrary")),
        name="diff_attn",
    )(qa, kn, vn, lam_p, subln.reshape(1, A_VD))


def _diff_decode_kernel(q_ref, kn_ref, vn_ref, kp_ref, vp_ref, lam_ref, sub_ref, o_ref, *, P, lam_init):
    T = q_ref.shape[1]
    per_pos = A_HEADS * 2
    qpos = P + lax.broadcasted_iota(jnp.int32, (T, 1), 0)
    kpos = P + lax.broadcasted_iota(jnp.int32, (1, T), 1)
    vis = (kpos >> CHUNK_SHIFT) <= (qpos >> CHUNK_SHIFT)
    lp = lam_ref[...]
    lam = (jnp.exp(jnp.sum(lp[0:1] * lp[1:2], axis=-1, keepdims=True))
           - jnp.exp(jnp.sum(lp[2:3] * lp[3:4], axis=-1, keepdims=True)) + lam_init)
    for h in range(A_HEADS):
        v_past = jnp.concatenate(
            [vp_ref[pl.ds(c * A_HEADS + h, P, stride=per_pos), :] for c in range(A_VD // LANES)],
            axis=-1).astype(BF16)
        v_new = vn_ref[0, :, h * A_VD:(h + 1) * A_VD]
        outs = []
        for m in range(2):
            hm = h * 2 + m
            q = q_ref[0, :, hm * A_HD:(hm + 1) * A_HD]
            k_past = kp_ref[pl.ds(hm, P, stride=per_pos), :].astype(BF16)
            s_p = _dot_nt(q, k_past)
            s_n = jnp.where(vis, _dot_nt(q, kn_ref[0, :, hm * A_HD:(hm + 1) * A_HD]), NEG)
            mx = jnp.maximum(jnp.max(s_p, axis=-1, keepdims=True), jnp.max(s_n, axis=-1, keepdims=True))
            p_p = jnp.exp2(s_p - mx)
            p_n = jnp.exp2(s_n - mx)
            l = jnp.sum(p_p, axis=-1, keepdims=True) + jnp.sum(p_n, axis=-1, keepdims=True)
            outs.append((_dot(p_p.astype(BF16), v_past) + _dot(p_n.astype(BF16), v_new)) / l)
        o = outs[0] - lam * outs[1]
        o = o * lax.rsqrt(jnp.mean(o * o, axis=-1, keepdims=True) + EPS) * sub_ref[...]
        o_ref[0, :, h * A_VD:(h + 1) * A_VD] = (o * (1.0 - lam_init)).astype(o_ref.dtype)


def _diff_decode(qa, kn, vn, kp_rows, vp_rows, layer, lam_p, subln, P, lam_init):
    B, T, W = qa.shape
    assert P % CHUNK == 0 and kp_rows.shape[2] == P * A_HEADS * 2 and vp_rows.shape[2] == P * A_HEADS * 2
    new_spec = pl.BlockSpec((1, T, W), lambda b: (b, 0, 0))
    cache_spec = pl.BlockSpec((None, None, P * A_HEADS * 2, LANES), lambda b: (layer, b, 0, 0))
    return pl.pallas_call(
        functools.partial(_diff_decode_kernel, P=P, lam_init=lam_init),
        grid=(B,),
        in_specs=[new_spec, new_spec, new_spec, cache_spec, cache_spec,
                  pl.BlockSpec((4, A_HD), lambda b: (0, 0)),
                  pl.BlockSpec((1, A_VD), lambda b: (0, 0))],
        out_specs=new_spec,
        out_shape=jax.ShapeDtypeStruct((B, T, W), BF16),
        compiler_params=_cparams(("parallel",)),
        name="diff_decode",
    )(qa, kn, vn, kp_rows, vp_rows, lam_p, subln.reshape(1, A_VD))


def _dsa_kernel(qi_ref, wi_ref, qb_ref, ki_ref, kb_ref, vb_ref, o_ref,
                key_s, bias_s, s_s, *, P, L, tq, tk, topk):
    i = pl.program_id(1)
    q_pos0 = P + i * tq
    ncol = jnp.minimum(L, ((q_pos0 + tq - 1) // CHUNK + 1) * CHUNK)
    nt = (ncol + tk - 1) // tk
    qpos = q_pos0 + lax.broadcasted_iota(jnp.int32, (tq, 1), 0)
    grp = B_HEADS // B_KV
    lane = lax.broadcasted_iota(jnp.int32, (tq, LANES), 1)
    wi = wi_ref[0] * (I_HEADS ** -0.5)
    w_lanes = [jnp.broadcast_to(wi[:, h:h + 1], (tq, LANES)) for h in range(I_HEADS)]

    def vis_of(t):
        kpos = t * tk + lax.broadcasted_iota(jnp.int32, (1, tk), 1)
        return ((kpos >> CHUNK_SHIFT) <= (qpos >> CHUNK_SHIFT)) & (kpos < L)

    def score_tile(t, carry):
        c0 = pl.multiple_of(t * tk, tk)
        kit = ki_ref[0, pl.ds(c0, tk), :]
        sc = jnp.zeros((tq, tk), F32)
        for pr in range(I_HEADS * I_HD // LANES):
            qpair = qi_ref[0, :, pr * LANES:(pr + 1) * LANES]
            for half in range(LANES // I_HD):
                h = pr * (LANES // I_HD) + half
                qh = jnp.where((lane >> I_HD_SHIFT) == half, qpair, jnp.zeros_like(qpair))
                isc = _dot_nt(qh, kit)
                sc = sc + jnp.maximum(isc, 0.0) * pltpu.repeat(w_lanes[h], tk // LANES, axis=1)
        sc = jnp.where(vis_of(t), sc, -jnp.inf)
        key_s[t] = _float_order_key(sc)
        return carry

    lax.fori_loop(0, nt, score_tile, 0)

    def counts(pred_fns):
        def body(t, accs):
            kk = key_s[t]
            out = []
            for pred_fn, acc in zip(pred_fns, accs):
                hit = jnp.where(pred_fn(kk), 1, 0)
                part = hit[:, 0:LANES]
                for c in range(1, tk // LANES):
                    part = part + hit[:, c * LANES:(c + 1) * LANES]
                out.append(acc + part)
            return tuple(out)
        accs = _tile_loop(nt, body, tuple(jnp.zeros((tq, LANES), jnp.int32) for _ in pred_fns), 2)
        return [jnp.sum(a.astype(F32), axis=-1, keepdims=True) for a in accs]

    def select(_):
        def bit_body(bi, thr):
            cand = thr + jnp.left_shift(jnp.int32(1), 31 - bi)
            (cnt,) = counts([lambda kk: kk >= cand])
            return jnp.where(cnt >= topk, cand, thr)
        return lax.fori_loop(0, 32, bit_body, jnp.full((tq, 1), KEY_MIN, jnp.int32))

    thr = lax.cond(ncol > topk, select, lambda _: jnp.full((tq, 1), KEY_MIN, jnp.int32), 0)
    n_gt, n_eq = counts([lambda kk: kk > thr, lambda kk: kk == thr])
    need = topk - n_gt
    neg_inf_key = jnp.int32(KEY_MIN + 0x7FFFFF)
    tie_break = jnp.max(jnp.where((n_eq > need) & (thr != neg_inf_key), 1.0, 0.0)) > 0.5

    def bias_fast(_):
        def body(t, c):
            sel = (key_s[t] >= thr) & vis_of(t)
            bias_s[t] = jnp.where(sel, 0.0, NEG)
            return c
        lax.fori_loop(0, nt, body, 0)
        return 0

    def bias_ties(_):
        r = lax.broadcasted_iota(jnp.int32, (LANES, LANES), 0)
        c = lax.broadcasted_iota(jnp.int32, (LANES, LANES), 1)
        before = jnp.where(r < c, 1.0, 0.0).astype(BF16)

        def body(t, seen):
            kk = key_s[t]
            vis = vis_of(t)
            for cb in range(tk // LANES):
                sl = slice(cb * LANES, (cb + 1) * LANES)
                eq = kk[:, sl] == thr
                eqf = jnp.where(eq, 1.0, 0.0)
                rank = seen + _dot(eqf.astype(BF16), before)
                sel = ((kk[:, sl] > thr) | (eq & (rank < need.astype(F32)))) & vis[:, sl]
                bias_s[t, :, sl] = jnp.where(sel, 0.0, NEG)
                seen = seen + jnp.sum(eqf, axis=-1, keepdims=True)
            return seen
        lax.fori_loop(0, nt, body, jnp.zeros((tq, 1), F32))
        return 0

    lax.cond(tie_break, bias_ties, bias_fast, 0)

    def fold(x, op):
        f = x[:, 0:LANES]
        for c in range(1, tk // LANES):
            f = op(f, x[:, c * LANES:(c + 1) * LANES])
        return f

    for n in range(B_KV):
        kv = slice(n * B_HD, (n + 1) * B_HD)

        def scores(t, mrun, kv=kv, n=n):
            c0 = pl.multiple_of(t * tk, tk)
            bias = bias_s[t]
            k = kb_ref[0, pl.ds(c0, tk), kv]
            out = []
            for g in range(grp):
                h = n * grp + g
                s = _dot_nt(qb_ref[0, :, h * B_HD:(h + 1) * B_HD], k) + bias
                s_s[g, t] = s
                out.append(jnp.maximum(mrun[g], fold(s, jnp.maximum)))
            return tuple(out)

        mrun = _tile_loop(nt, scores, tuple(jnp.full((tq, LANES), NEG, F32) for _ in range(grp)),
                          TILE_UNROLL)
        mrow = [pltpu.repeat(jnp.broadcast_to(jnp.max(m, axis=-1, keepdims=True), (tq, LANES)),
                             tk // LANES, axis=1) for m in mrun]

        ones = jnp.ones((tk, LANES), BF16)

        def weigh(t, acc, kv=kv, mrow=mrow):
            c0 = pl.multiple_of(t * tk, tk)
            v1 = jnp.concatenate([vb_ref[0, pl.ds(c0, tk), kv], ones], axis=-1)
            return tuple(acc[g] + _dot(jnp.exp2(s_s[g, t] - mrow[g]).astype(BF16), v1)
                         for g in range(grp))

        acc = _tile_loop(nt, weigh, tuple(jnp.zeros((tq, B_HD + LANES), F32) for _ in range(grp)),
                         TILE_UNROLL)
        for g in range(grp):
            h = n * grp + g
            o_ref[0, :, h * B_HD:(h + 1) * B_HD] = (
                acc[g][:, 0:B_HD] / acc[g][:, B_HD:B_HD + LANES]).astype(o_ref.dtype)


def _dsa(qi, proj, qb, ki2, kb, vb, P, L, tk):
    B, T, _ = qb.shape
    Lp = ki2.shape[1]
    tq = min(T, 256)
    assert T % tq == 0 and Lp % tk == 0 and tk % LANES == 0
    topk = min(TOPK_MAX, L // 4)
    kern = functools.partial(_dsa_kernel, P=P, L=L, tq=tq, tk=tk, topk=topk)
    return pl.pallas_call(
        kern,
        grid=(B, T // tq),
        in_specs=[pl.BlockSpec((1, tq, I_HEADS * I_HD), lambda b, i: (b, i, 0)),
                  pl.BlockSpec((1, tq, LANES), lambda b, i: (b, i, OFF_SMALL // LANES)),
                  pl.BlockSpec((1, tq, B_HEADS * B_HD), lambda b, i: (b, i, 0)),
                  pl.BlockSpec((1, Lp, LANES), lambda b, i: (b, 0, 0)),
                  pl.BlockSpec((1, Lp, B_KV * B_HD), lambda b, i: (b, 0, 0)),
                  pl.BlockSpec((1, Lp, B_KV * B_HD), lambda b, i: (b, 0, 0))],
        out_specs=pl.BlockSpec((1, tq, B_HEADS * B_HD), lambda b, i: (b, i, 0)),
        out_shape=jax.ShapeDtypeStruct((B, T, B_HEADS * B_HD), BF16),
        scratch_shapes=[pltpu.VMEM((Lp // tk, tq, tk), jnp.int32), pltpu.VMEM((Lp // tk, tq, tk), F32),
                        pltpu.VMEM((B_HEADS // B_KV, Lp // tk, tq, tk), F32)],
        compiler_params=_cparams(("parallel", "arbitrary")),
        name="dsa",
    )(qi, proj, qb, ki2, kb, vb)


def _float_order_key(x):
    bits = pltpu.bitcast(x, jnp.int32)
    bits = jnp.where(bits == KEY_MIN, 0, bits)
    return bits ^ ((bits >> 31) & 0x7FFFFFFF)


def _dsa_decode_kernel(qi_ref, wi_ref, qb_ref, kin_ref, kbn_ref, vbn_ref, kip_ref, kp_ref, vp_ref, o_ref,
                       *, P, topk):
    T = qi_ref.shape[1]
    grp = B_HEADS // B_KV
    qpos = P + lax.broadcasted_iota(jnp.int32, (T, 1), 0)
    kpos = P + lax.broadcasted_iota(jnp.int32, (1, T), 1)
    vis_n = (kpos >> CHUNK_SHIFT) <= (qpos >> CHUNK_SHIFT)
    wi = wi_ref[0] * (I_HEADS ** -0.5)
    ki_pT = kip_ref[...].astype(BF16)
    ki_n = kin_ref[0, :, 0:I_HD]

    sc_p = jnp.zeros((T, P), F32)
    sc_n = jnp.zeros((T, T), F32)
    for h in range(I_HEADS):
        qh = qi_ref[0, :, h * I_HD:(h + 1) * I_HD]
        w = wi[:, h:h + 1]
        sc_p = sc_p + jnp.maximum(_dot(qh, ki_pT), 0.0) * w
        sc_n = sc_n + jnp.maximum(_dot_nt(qh, ki_n), 0.0) * w
    key_p = _float_order_key(sc_p)
    key_n = _float_order_key(jnp.where(vis_n, sc_n, -jnp.inf))

    def count(pred):
        return (jnp.sum(jnp.where(pred(key_p), 1.0, 0.0), axis=-1, keepdims=True)
                + jnp.sum(jnp.where(pred(key_n), 1.0, 0.0), axis=-1, keepdims=True))

    def bit_body(bi, thr):
        cand = thr + jnp.left_shift(jnp.int32(1), 31 - bi)
        return jnp.where(count(lambda kk: kk >= cand) >= topk, cand, thr)

    thr = lax.fori_loop(0, 32, bit_body, jnp.full((T, 1), KEY_MIN, jnp.int32))
    need = topk - count(lambda kk: kk > thr)

    def before(n):
        r = lax.broadcasted_iota(jnp.int32, (n, n), 0)
        c = lax.broadcasted_iota(jnp.int32, (n, n), 1)
        return jnp.where(r < c, 1.0, 0.0).astype(BF16)

    def select(kk, seen, pre):
        eq = kk == thr
        eqf = jnp.where(eq, 1.0, 0.0)
        rank = seen + _dot(eqf.astype(BF16), pre)
        sel = (kk > thr) | (eq & (rank < need))
        return sel, seen + jnp.sum(eqf, axis=-1, keepdims=True)

    pre_l = before(LANES)
    seen = jnp.zeros((T, 1), F32)
    bias_blocks = []
    for cb in range(P // LANES):
        sel, seen = select(key_p[:, cb * LANES:(cb + 1) * LANES], seen, pre_l)
        bias_blocks.append(jnp.where(sel, 0.0, NEG))
    bias_p = jnp.concatenate(bias_blocks, axis=-1)
    sel, _ = select(key_n, seen, before(T))
    bias_n = jnp.where(sel & vis_n, 0.0, NEG)

    for n in range(B_KV):
        kv = slice(n * B_HD, (n + 1) * B_HD)
        k_p = kp_ref[pl.ds(n, P, stride=B_KV), :].astype(BF16)
        v_p = vp_ref[pl.ds(n, P, stride=B_KV), :].astype(BF16)
        k_n = kbn_ref[0, :, kv]
        v_n = vbn_ref[0, :, kv]
        for g in range(grp):
            h = n * grp + g
            q = qb_ref[0, :, h * B_HD:(h + 1) * B_HD]
            s_p = _dot_nt(q, k_p) + bias_p
            s_n = _dot_nt(q, k_n) + bias_n
            mx = jnp.maximum(jnp.max(s_p, axis=-1, keepdims=True), jnp.max(s_n, axis=-1, keepdims=True))
            p_p = jnp.exp2(s_p - mx)
            p_n = jnp.exp2(s_n - mx)
            l = jnp.sum(p_p, axis=-1, keepdims=True) + jnp.sum(p_n, axis=-1, keepdims=True)
            o = _dot(p_p.astype(BF16), v_p) + _dot(p_n.astype(BF16), v_n)
            o_ref[0, :, h * B_HD:(h + 1) * B_HD] = (o / l).astype(o_ref.dtype)


def _dsa_decode(qi, proj, qb, ki2, kbb, vbb, ki_past_t, kp_rows, vp_rows, layer):
    B, T, _ = qb.shape
    P = ki_past_t.shape[3]
    assert P % CHUNK == 0 and P % LANES == 0 and kp_rows.shape[2] == P * B_KV
    topk = min(TOPK_MAX, (P + T) // 4)

    def new(width, blk=0):
        return pl.BlockSpec((1, T, width), lambda b: (b, 0, blk))

    rows = pl.BlockSpec((None, None, P * B_KV, LANES), lambda b: (layer, b, 0, 0))
    return pl.pallas_call(
        functools.partial(_dsa_decode_kernel, P=P, topk=topk),
        grid=(B,),
        in_specs=[new(I_HEADS * I_HD), new(LANES, OFF_SMALL // LANES), new(B_HEADS * B_HD),
                  new(LANES), new(B_KV * B_HD), new(B_KV * B_HD),
                  pl.BlockSpec((None, None, I_HD, P), lambda b: (layer, b, 0, 0)), rows, rows],
        out_specs=new(B_HEADS * B_HD),
        out_shape=jax.ShapeDtypeStruct((B, T, B_HEADS * B_HD), BF16),
        compiler_params=_cparams(("parallel",)),
        name="dsa_decode",
    )(qi, proj, qb, ki2, kbb, vbb, ki_past_t, kp_rows, vp_rows)


def _bmm(a, b):
    return jnp.einsum('bij,bjk->bik', a.astype(BF16), b.astype(BF16), preferred_element_type=F32)


def _bmm_nt(a, b):
    return jnp.einsum('bik,bjk->bij', a.astype(BF16), b.astype(BF16), preferred_element_type=F32)


def _gdn_intra_kernel(x_ref, halo_ref, sm_ref, cw_ref, cbuf_ref, alog_ref, dtb_ref,
                      u_o, w_o, qt_o, ktT_o, qkg_o, dec_o, buf_s, *, cc, nck):
    i = pl.program_id(1)
    R = nck * cc
    HW = C_HEADS * C_HD

    @pl.when(i == 0)
    def _():
        buf_s[0:SUBLANES, :] = jnp.zeros((SUBLANES, C_QKV), F32)
        buf_s[SUBLANES - (C_CONV - 1):SUBLANES, :] = cbuf_ref[0]

    @pl.when(i > 0)
    def _():
        buf_s[0:SUBLANES, :] = halo_ref[0]

    x = x_ref[0]
    prev = buf_s[...]
    row8 = lax.broadcasted_iota(jnp.int32, (SUBLANES, C_QKV), 0)
    y = x * cw_ref[C_CONV - 1:C_CONV, :]
    s = x
    for k in range(1, C_CONV):
        s = pltpu.roll(s, 1, 0)
        head = jnp.where(row8 < k, pltpu.roll(prev, k, 0), s[0:SUBLANES])
        y = y + jnp.concatenate([head, s[SUBLANES:]], axis=0) * cw_ref[C_CONV - 1 - k:C_CONV - k, :]
    y = y * jax.nn.sigmoid(y)

    sm = sm_ref[0]
    xg = sm + dtb_ref[...]
    softplus = jnp.maximum(xg, 0.0) + jnp.log(1.0 + jnp.exp(-jnp.abs(xg)))
    g = -jnp.exp(alog_ref[...]) * softplus
    beta = jax.nn.sigmoid(sm)

    cshift = cc.bit_length() - 1
    rr = lax.broadcasted_iota(jnp.int32, (R, R), 0)
    rc = lax.broadcasted_iota(jnp.int32, (R, R), 1)
    cum = ((rr >> cshift) == (rc >> cshift)) & (rr >= rc)
    G = jnp.dot(jnp.where(cum, 1.0, 0.0), g, preferred_element_type=F32,
                precision=lax.Precision.HIGHEST)

    qs, ks, vs, gcs, bcs = [], [], [], [], []
    for ck in range(nck):
        rows = slice(ck * cc, (ck + 1) * cc)
        for h in range(C_HEADS):
            qs.append(y[rows, h * C_HD:(h + 1) * C_HD])
            ks.append(y[rows, HW + h * C_HD:HW + (h + 1) * C_HD])
            vs.append(y[rows, 2 * HW + h * C_HD:2 * HW + (h + 1) * C_HD])
            gcs.append(G[rows, CA_LANE + h:CA_LANE + h + 1])
            bcs.append(beta[rows, CB_LANE + h:CB_LANE + h + 1])
    q3 = jnp.stack(qs)
    k3 = jnp.stack(ks)
    v3 = jnp.stack(vs)
    Gc = jnp.stack(gcs)
    bc = jnp.stack(bcs)
    q3 = q3 * lax.rsqrt(jnp.sum(q3 * q3, axis=-1, keepdims=True) + EPS) * (C_HD ** -0.5)
    k3 = k3 * lax.rsqrt(jnp.sum(k3 * k3, axis=-1, keepdims=True) + EPS)

    ri = lax.broadcasted_iota(jnp.int32, (cc, cc), 0)
    ci = lax.broadcasted_iota(jnp.int32, (cc, cc), 1)
    incl = (ri >= ci)[None]
    strict = (ri > ci)[None]
    eye = (ri == ci)[None]
    bdiag = ((ri // SOLVE_BLOCK) == (ci // SOLVE_BLOCK))[None]

    Gr = jnp.sum(jnp.where(eye, Gc, 0.0), axis=1, keepdims=True)
    gam = jnp.where(incl, jnp.exp(jnp.where(incl, Gc - Gr, 0.0)), 0.0)
    k16 = k3.astype(BF16)
    A = jnp.where(strict, bc * _bmm_nt(k16, k16) * gam, 0.0)
    Nd = jnp.where(bdiag, -A, 0.0)
    E = jnp.where(bdiag, 0.0, A)
    N2 = _bmm(Nd, Nd)
    N4 = _bmm(N2, N2)
    N8 = _bmm(N4, N4)
    Q = Nd + N2 + _bmm(Nd, N2)
    Q = Q + N4 + _bmm(Q, N4)
    Q = Q + N8 + _bmm(Q, N8)
    Fm = E + _bmm(Q, E)
    F2 = _bmm(Fm, Fm)
    Rm = F2 - Fm - _bmm(Fm, F2)
    Wm = Rm + Q + _bmm(Rm, Q)
    eG = jnp.exp(Gc)
    rhs = jnp.concatenate([bc * v3, (bc * eG) * k3], axis=-1)
    sol = rhs + _bmm(Wm, rhs)
    qkg = _bmm_nt(q3, k16) * gam
    gl = Gc[:, cc - 1:cc, :]
    qt = q3 * eG
    kt = k3 * jnp.exp(gl - Gc)
    dec = jnp.exp(gl)

    for ck in range(nck):
        rows = slice(ck * cc, (ck + 1) * cc)
        for h in range(C_HEADS):
            b = ck * C_HEADS + h
            cols = slice(h * C_HD, (h + 1) * C_HD)
            u_o[0, rows, cols] = sol[b, :, 0:C_HD]
            w_o[0, rows, cols] = sol[b, :, C_HD:2 * C_HD].astype(BF16)
            qt_o[0, rows, cols] = qt[b].astype(BF16)
            ktT_o[0, ck, h] = kt[b].T.astype(BF16)
            qkg_o[0, ck, h] = qkg[b].astype(BF16)
            dec_o[0, ck, h] = jnp.broadcast_to(dec[b], (1, C_HD))


def _gdn_inter_kernel(u_ref, w_ref, qt_ref, ktT_ref, qkg_ref, dec_ref, z_ref, s0_ref, gn_ref,
                      o_ref, sfin_ref, st_s, *, cc, nck):
    c = pl.program_id(1)

    @pl.when(c == 0)
    def _():
        st_s[...] = s0_ref[0]

    S = st_s[...]
    for ck in range(nck):
        rows = slice(ck * cc, (ck + 1) * cc)

        def heads(ref):
            return jnp.stack([ref[0, rows, h * C_HD:(h + 1) * C_HD] for h in range(C_HEADS)])

        S16 = S.astype(BF16)
        v_new = heads(u_ref) - _bmm(heads(w_ref), S16)
        vn16 = v_new.astype(BF16)
        o = _bmm(heads(qt_ref), S16) + _bmm(qkg_ref[0, ck], vn16)
        S = dec_ref[0, ck] * S + _bmm(ktT_ref[0, ck], vn16)
        on = o * lax.rsqrt(jnp.mean(o * o, axis=-1, keepdims=True) + EPS) * gn_ref[...]
        for h in range(C_HEADS):
            cols = slice(h * C_HD, (h + 1) * C_HD)
            z = z_ref[0, rows, cols]
            o_ref[0, rows, cols] = (on[h] * (z * jax.nn.sigmoid(z))).astype(o_ref.dtype)
    st_s[...] = S

    @pl.when(c == pl.num_programs(1) - 1)
    def _():
        sfin_ref[0] = S


def _gdn(proj, conv_w, cbuf, S0, a_log, dt_bias, out_norm):
    B, T, _ = proj.shape
    cc = min(CHUNK, T)
    assert T % cc == 0 and cc % SOLVE_BLOCK == 0 and cc // SOLVE_BLOCK <= 4 and cc >= SUBLANES
    assert cc & (cc - 1) == 0
    HW = C_HEADS * C_HD
    NC = T // cc
    nck_a = 2 if NC % 2 == 0 else 1
    nck_b = 8 if NC % 8 == 0 else 1

    def at_ca(p):
        return jnp.zeros((1, LANES), F32).at[0, CA_LANE:CA_LANE + C_HEADS].set(p.astype(F32))

    Ra = nck_a * cc
    hb = Ra // SUBLANES
    u, w, qt, ktT, qkg, dec = pl.pallas_call(
        functools.partial(_gdn_intra_kernel, cc=cc, nck=nck_a),
        grid=(B, NC // nck_a),
        in_specs=[pl.BlockSpec((1, Ra, C_QKV), lambda b, i: (b, i, OFF_CQKV // C_QKV)),
                  pl.BlockSpec((1, SUBLANES, C_QKV),
                               lambda b, i: (b, jnp.maximum(i * hb - 1, 0), OFF_CQKV // C_QKV)),
                  pl.BlockSpec((1, Ra, LANES), lambda b, i: (b, i, OFF_SMALL // LANES)),
                  pl.BlockSpec((C_CONV, C_QKV), lambda b, i: (0, 0)),
                  pl.BlockSpec((1, C_CONV - 1, C_QKV), lambda b, i: (b, 0, 0)),
                  pl.BlockSpec((1, LANES), lambda b, i: (0, 0)),
                  pl.BlockSpec((1, LANES), lambda b, i: (0, 0))],
        out_specs=[pl.BlockSpec((1, Ra, HW), lambda b, i: (b, i, 0)),
                   pl.BlockSpec((1, Ra, HW), lambda b, i: (b, i, 0)),
                   pl.BlockSpec((1, Ra, HW), lambda b, i: (b, i, 0)),
                   pl.BlockSpec((1, nck_a, C_HEADS, C_HD, cc), lambda b, i: (b, i, 0, 0, 0)),
                   pl.BlockSpec((1, nck_a, C_HEADS, cc, cc), lambda b, i: (b, i, 0, 0, 0)),
                   pl.BlockSpec((1, nck_a, C_HEADS, 1, C_HD), lambda b, i: (b, i, 0, 0, 0))],
        out_shape=[jax.ShapeDtypeStruct((B, T, HW), F32),
                   jax.ShapeDtypeStruct((B, T, HW), BF16),
                   jax.ShapeDtypeStruct((B, T, HW), BF16),
                   jax.ShapeDtypeStruct((B, NC, C_HEADS, C_HD, cc), BF16),
                   jax.ShapeDtypeStruct((B, NC, C_HEADS, cc, cc), BF16),
                   jax.ShapeDtypeStruct((B, NC, C_HEADS, 1, C_HD), F32)],
        scratch_shapes=[pltpu.VMEM((SUBLANES, C_QKV), F32)],
        compiler_params=_cparams(("parallel", "parallel")),
        name="gdn_intra",
    )(proj, proj, proj, conv_w, cbuf, at_ca(a_log), at_ca(dt_bias))

    Rb = nck_b * cc
    return pl.pallas_call(
        functools.partial(_gdn_inter_kernel, cc=cc, nck=nck_b),
        grid=(B, NC // nck_b),
        in_specs=[pl.BlockSpec((1, Rb, HW), lambda b, c: (b, c, 0)),
                  pl.BlockSpec((1, Rb, HW), lambda b, c: (b, c, 0)),
                  pl.BlockSpec((1, Rb, HW), lambda b, c: (b, c, 0)),
                  pl.BlockSpec((1, nck_b, C_HEADS, C_HD, cc), lambda b, c: (b, c, 0, 0, 0)),
                  pl.BlockSpec((1, nck_b, C_HEADS, cc, cc), lambda b, c: (b, c, 0, 0, 0)),
                  pl.BlockSpec((1, nck_b, C_HEADS, 1, C_HD), lambda b, c: (b, c, 0, 0, 0)),
                  pl.BlockSpec((1, Rb, HW), lambda b, c: (b, c, OFF_CZ // HW)),
                  pl.BlockSpec((1, C_HEADS, C_HD, C_HD), lambda b, c: (b, 0, 0, 0)),
                  pl.BlockSpec((1, C_HD), lambda b, c: (0, 0))],
        out_specs=[pl.BlockSpec((1, Rb, HW), lambda b, c: (b, c, 0)),
                   pl.BlockSpec((1, C_HEADS, C_HD, C_HD), lambda b, c: (b, 0, 0, 0))],
        out_shape=[jax.ShapeDtypeStruct((B, T, HW), BF16),
                   jax.ShapeDtypeStruct((B, C_HEADS, C_HD, C_HD), F32)],
        scratch_shapes=[pltpu.VMEM((C_HEADS, C_HD, C_HD), F32)],
        compiler_params=_cparams(("parallel", "arbitrary")),
        name="gdn_inter",
    )(u, w, qt, ktT, qkg, dec, proj, S0, out_norm.reshape(1, C_HD))


def _ffn_up_kernel(x_ref, g_ref, sh_ref, sc_ref, wg_ref, wu_ref, cwg_ref, cwu_ref, fg_ref, fu_ref,
                   act_o, fng_o, fnu_o, h_ref, *scr, split_rows):
    bb, tt, D = x_ref.shape
    tn = wg_ref.shape[1]
    i = pl.program_id(1)
    j = pl.program_id(2)
    lo = SUBLANES - (FFN_CONV - 1)

    @pl.when(j == 0)
    def _():
        _norm_mod_rows(x_ref, g_ref, sh_ref, sc_ref, h_ref)

    if split_rows:
        (carry_s,) = scr
        halves = ((wg_ref, cwg_ref, fg_ref, fng_o, 0), (wu_ref, cwu_ref, fu_ref, fnu_o, 1))
        for _, _, f_ref, _, idx in halves:
            @pl.when(i == 0)
            def _(f_ref=f_ref, idx=idx):
                carry_s[idx, j, 0, lo:SUBLANES, :] = f_ref[0]
        prev = [carry_s[idx, j, 0] for idx in range(2)]
        row8 = lax.broadcasted_iota(jnp.int32, (SUBLANES, tn), 0)
        rc = min(tt, FFN_ROW_CHUNK)
        for r in range(tt // rc):
            rows = slice(r * rc, (r + 1) * rc)
            ys = []
            for w_ref, cw_ref, _, _, idx in halves:
                u = _dot(h_ref[rows, :], w_ref[...])
                y = u * cw_ref[FFN_CONV - 1:FFN_CONV, :]
                for k in range(1, FFN_CONV):
                    s = pltpu.roll(u, k, 0)
                    head = jnp.where(row8 < k, pltpu.roll(prev[idx], k, 0), s[0:SUBLANES])
                    s = jnp.concatenate([head, s[SUBLANES:]], axis=0)
                    y = y + s * cw_ref[FFN_CONV - 1 - k:FFN_CONV - k, :]
                prev[idx] = u[rc - SUBLANES:rc]
                ys.append(y)
            act_o[0, rows, :] = (ys[0] * jax.nn.sigmoid(ys[0]) * ys[1]).astype(act_o.dtype)
        for _, _, _, fn_o, idx in halves:
            carry_s[idx, j, 0] = prev[idx]
            fn_o[...] = prev[idx][lo:SUBLANES][None]
    else:
        bg_s, bu_s = scr

        def half(w_ref, cw_ref, f_ref, fn_o, buf):
            u = _dot(h_ref[...], w_ref[...]).reshape(bb, tt, tn)
            buf[:, lo:SUBLANES, :] = f_ref[...]
            buf[:, SUBLANES:SUBLANES + tt, :] = u
            fn_o[...] = u[:, tt - (FFN_CONV - 1):tt, :]
            y = None
            for jw in range(FFN_CONV):
                term = buf[:, lo + jw:lo + jw + tt, :] * cw_ref[jw:jw + 1, :]
                y = term if y is None else y + term
            return y

        yg = half(wg_ref, cwg_ref, fg_ref, fng_o, bg_s)
        yu = half(wu_ref, cwu_ref, fu_ref, fnu_o, bu_s)
        act_o[...] = (yg * jax.nn.sigmoid(yg) * yu).astype(act_o.dtype)


def _ffn_up(x, g, shift, scale, w_up, layer, conv_w, fbuf):
    B, T, D = x.shape
    FF = w_up.shape[2] // 2
    bb, tt = _row_tiles(B, T)
    tn = COL_TILE
    assert FF % tn == 0 and tt >= SUBLANES
    nj = FF // tn
    split_rows = T > tt
    assert not split_rows or bb == 1
    nf = FFN_CONV - 1

    def cols(off, rows, lead=()):
        return pl.BlockSpec((None,) * len(lead) + rows + (tn,),
                            lambda b, i, j: lead + (0,) * len(rows) + (off + j,))

    def fb(off):
        return pl.BlockSpec((bb, nf, tn), lambda b, i, j: (b, 0, off + j))

    act, fng, fnu = pl.pallas_call(
        functools.partial(_ffn_up_kernel, split_rows=split_rows),
        grid=(B // bb, T // tt, nj),
        in_specs=[pl.BlockSpec((bb, tt, D), lambda b, i, j: (b, i, 0)),
                  pl.BlockSpec((1, 1, D), lambda b, i, j: (0, 0, 0)),
                  pl.BlockSpec((bb, 1, D), lambda b, i, j: (b, 0, 0)),
                  pl.BlockSpec((bb, 1, D), lambda b, i, j: (b, 0, 0)),
                  cols(0, (D,), (layer,)), cols(nj, (D,), (layer,)),
                  cols(0, (FFN_CONV,)), cols(nj, (FFN_CONV,)),
                  fb(0), fb(nj)],
        out_specs=[pl.BlockSpec((bb, tt, tn), lambda b, i, j: (b, i, j)),
                   pl.BlockSpec((bb, None, nf, tn), lambda b, i, j: (b, i, 0, j)),
                   pl.BlockSpec((bb, None, nf, tn), lambda b, i, j: (b, i, 0, j))],
        out_shape=[jax.ShapeDtypeStruct((B, T, FF), BF16),
                   jax.ShapeDtypeStruct((B, T // tt, nf, FF), F32),
                   jax.ShapeDtypeStruct((B, T // tt, nf, FF), F32)],
        scratch_shapes=[pltpu.VMEM((bb * tt, D), BF16)] + (
            [pltpu.VMEM((2, nj, 1, SUBLANES, tn), F32)] if split_rows else
            [pltpu.VMEM((bb, tt + SUBLANES, tn), F32), pltpu.VMEM((bb, tt + SUBLANES, tn), F32)]),
        compiler_params=_cparams(("parallel", "arbitrary", "arbitrary")),
        name="ffn_up",
    )(x, g.reshape(1, 1, D), shift, scale, w_up, w_up, conv_w, conv_w, fbuf, fbuf)
    return act, jnp.concatenate([fng[:, -1], fnu[:, -1]], axis=-1)


def _prep_w_in(w_in):
    depth, D = w_in.shape[0], w_in.shape[1]
    w_in = w_in.astype(BF16)
    sizes = (1024, 1024, 1024, 1024, 256, 256, 512, 64, 8, C_QKV, 8, 8, 1024, N_BRANCH * D)
    offs = np.concatenate([[0], np.cumsum(sizes)])
    (aq, ak, av, bq, bk, bv, bqi, bki, bwi, cqkv, ca, cb, cz, gt) = [
        w_in[:, :, int(offs[k]):int(offs[k + 1])] for k in range(len(sizes))]

    def z(n):
        return jnp.zeros((depth, D, n), w_in.dtype)

    cols = [cqkv, cz, aq, ak, av, bq, bk, bv, bqi, bki, z(LANES - I_HD), bwi, ca, cb, z(LANES - 24),
            z(OFF_GT - OFF_SMALL - LANES), gt]
    w = jnp.concatenate(cols, axis=2)
    assert w.shape[2] == N_PROJ
    return w


def _layer(x, mod, past, lw, layer_idx):
    kA_p, vA_p, kB_p, vB_p, kI_p, cbuf, S0, fbuf = past
    B, T, D = x.shape
    P = 0 if kI_p is None else kI_p.shape[3]
    sh1, sc1, g1, sh2, sc2, g2 = [m.reshape(B, 1, D) for m in jnp.split(mod, 6, axis=-1)]

    proj = _mm_norm(x, lw['norm_mix'], sh1, sc1, lw['w_in'], layer_idx)
    rope = _rope_table(P + jnp.arange(T, dtype=jnp.int32))
    (qa, ka, kab, va, vab, qb, kb, kbb, vb, vbb, qi, ki, ki2) = _prep(
        proj, rope, lw['a_q_norm'], lw['a_k_norm'], lw['b_q_norm'], lw['b_k_norm'])

    lam_init = 0.8 - 0.6 * math.exp(-0.3 * layer_idx)
    if P:
        oa = _diff_decode(qa, kab, vab, kA_p, vA_p, layer_idx, lw['a_lambda'], lw['a_subln'], P, lam_init)
        ob = _dsa_decode(qi, proj, qb, ki2, kbb, vbb, kI_p, kB_p, vB_p, layer_idx)
    else:
        oa = _diff_attn(qa, kab, vab, lw['a_lambda'], lw['a_subln'], lam_init)
        ob = _dsa(qi, proj, qb, ki2, kbb, vbb, 0, T, min(T, 512))

    oc, S_fin = _gdn(proj, lw['c_conv'], cbuf, S0, lw['c_a_log'], lw['c_dt_bias'], lw['c_out_norm'])
    assert T >= C_CONV - 1 and T >= FFN_CONV - 1
    cbuf_new = proj[:, T - (C_CONV - 1):, OFF_CQKV:OFF_CQKV + C_QKV]

    mixed = _merge(oa, ob, oc, lw['w_branch'], layer_idx, proj, D)
    x = _mm_res(mixed, lw['w_out'], layer_idx, x, g1)

    act, fbuf_new = _ffn_up(x, lw['norm_ffn'], sh2, sc2, lw['w_up'], layer_idx, lw['ffn_conv'], fbuf)
    x = _mm_res(act, lw['w_down'], layer_idx, x, g2, row_target=512)

    va_out = va.reshape(B, T, A_VD // LANES, A_HEADS, LANES).transpose(0, 1, 3, 2, 4)
    new_state = (ka.reshape(B, T, A_HEADS, 2, A_HD), va_out.reshape(B, T, A_HEADS, A_VD),
                 kb.reshape(B, T, B_KV, B_HD), vb.reshape(B, T, B_KV, B_HD), ki,
                 cbuf_new, S_fin, fbuf_new)
    return x, new_state


def kernel(x_prompt, x_sample, c_prompt, c_sample, cache_diff_k, cache_diff_v, cache_dsa_k, cache_dsa_v, cache_dsa_kidx, state_gdn_conv, state_gdn, state_ffn_conv, w_ada, b_ada, norm_mix, w_in, a_q_norm, a_k_norm, a_lambda, a_subln, b_q_norm, b_k_norm, c_conv, c_a_log, c_dt_bias, c_out_norm, w_branch, w_out, norm_ffn, w_up, ffn_conv, w_down):
    depth = w_in.shape[0]
    bp, bs = x_prompt.shape[0], x_sample.shape[0]
    d_ff2 = w_up.shape[2]
    dt_ = x_prompt.dtype
    prompt_past = (None, None, None, None, None, jnp.zeros((bp, C_CONV - 1, C_QKV), dt_),
                   jnp.zeros((bp, C_HEADS, C_HD, C_HD), dt_), jnp.zeros((bp, FFN_CONV - 1, d_ff2), dt_))
    nrow = -(-(bp + bs) // SUBLANES) * SUBLANES
    c_all = jnp.concatenate([c_prompt, c_sample, jnp.zeros((nrow - bp - bs, c_prompt.shape[1]), dt_)], axis=0)
    past_len = cache_diff_k.shape[2]
    diff_k_rows = cache_diff_k.reshape(depth, bs, past_len * A_HEADS * 2, A_HD)
    diff_v_rows = cache_diff_v.reshape(depth, bs, past_len, A_HEADS, A_VD // LANES, LANES)
    diff_v_rows = diff_v_rows.transpose(0, 1, 2, 4, 3, 5).reshape(depth, bs, past_len * A_HEADS * 2, LANES)
    dsa_k_rows = cache_dsa_k.reshape(depth, bs, past_len * B_KV, B_HD)
    dsa_v_rows = cache_dsa_v.reshape(depth, bs, past_len * B_KV, B_HD)
    dsa_kidx_t = jnp.swapaxes(cache_dsa_kidx, 2, 3)
    xp, xs = x_prompt, x_sample
    prompt_states, sample_states = [], []
    big = dict(w_in=_prep_w_in(w_in), w_branch=w_branch.astype(BF16), w_out=w_out.astype(BF16),
               w_up=w_up.astype(BF16), w_down=w_down.astype(BF16))
    for l in range(depth):
        lw = dict(big, norm_mix=norm_mix[l],
                  a_q_norm=a_q_norm[l], a_k_norm=a_k_norm[l], a_lambda=a_lambda[l], a_subln=a_subln[l],
                  b_q_norm=b_q_norm[l], b_k_norm=b_k_norm[l], c_conv=c_conv[l], c_a_log=c_a_log[l],
                  c_dt_bias=c_dt_bias[l], c_out_norm=c_out_norm[l], norm_ffn=norm_ffn[l],
                  ffn_conv=ffn_conv[l])
        mod = _ada(c_all, w_ada, b_ada, l)
        xp, st_p = _layer(xp, mod[:bp], prompt_past, lw, l)
        sample_past = (diff_k_rows, diff_v_rows, dsa_k_rows, dsa_v_rows, dsa_kidx_t,
                       state_gdn_conv[l], state_gdn[l], state_ffn_conv[l])
        xs, st_s = _layer(xs, mod[bp:bp + bs], sample_past, lw, l)
        prompt_states.append(st_p)
        sample_states.append(st_s)
    p_out = [jnp.stack(s, axis=0) for s in zip(*prompt_states)]
    s_out = [jnp.stack(s, axis=0) for s in zip(*sample_states)]
    return (xp, xs, *p_out, *s_out)
```
